```python
import jax
import jax.numpy as jnp
from jax import lax
import numpy as np

D_MODEL = 1024
BATCH = 8
SEQ = 2048
DEPTH = 2
DEC_BATCH = 128
DEC_SEQ = 1
PAST_LEN = 8192
PAGE_SIZE = 128

N_A_LAYERS = DEPTH // 2
N_B_LAYERS = DEPTH - N_A_LAYERS
CHUNK = 128
D_GATE = 2 * D_MODEL
N_GROUPS_A = 8
GROUP_DIM_A = D_GATE // N_GROUPS_A
HEAD_DIM = 64
N_HEADS = D_MODEL // HEAD_DIM
N_KV_HEADS = N_HEADS // 4
Q_PER_KV = N_HEADS // N_KV_HEADS
WINDOW = 128
ROPE_DIM = HEAD_DIM // 4
ROPE_THETA = 500000.0
N_EXPERTS = 32
TOP_K = 4
D_EXPERT = D_MODEL
SWIGLU_LIMIT = 7.0
SWIGLU_ALPHA = 1.702
MOE_BLOCK = 128
EPS = 1e-6

kernel_name = 'yoco_gmlp_swa_sink_moe_step'


def rms_norm(x, g):
    xf = x.astype(jnp.float32)
    y = xf * lax.rsqrt(jnp.mean(xf * xf, axis=-1, keepdims=True) + EPS)
    return (y * g.astype(jnp.float32)).astype(x.dtype)


def layer_norm(x, g, b):
    xf = x.astype(jnp.float32)
    xc = xf - jnp.mean(xf, axis=-1, keepdims=True)
    y = xc * lax.rsqrt(jnp.mean(xc * xc, axis=-1, keepdims=True) + EPS)
    return (y * g.astype(jnp.float32) + b.astype(jnp.float32)).astype(x.dtype)


def partial_rope(x, pos):
    half = ROPE_DIM // 2
    inv_freq = jnp.power(jnp.float32(ROPE_THETA), -jnp.arange(half, dtype=jnp.float32) / half)
    ang = pos.astype(jnp.float32)[:, None] * inv_freq[None, :]
    cos = jnp.cos(ang)[:, None, :]
    sin = jnp.sin(ang)[:, None, :]
    xr = x[..., :ROPE_DIM].astype(jnp.float32)
    x1, x2 = xr[..., :half], xr[..., half:]
    rot = jnp.concatenate([x1 * cos - x2 * sin, x2 * cos + x1 * sin], axis=-1).astype(x.dtype)
    return jnp.concatenate([rot, x[..., ROPE_DIM:]], axis=-1)


def spatial_gating_mixer(h, w_in, ln_g, ln_b, w_s, b_s, w_out):
    bsz, seq, _ = h.shape
    z = jax.nn.gelu(h @ w_in, approximate=False)
    u, v = jnp.split(z, 2, axis=-1)
    v = layer_norm(v, ln_g, ln_b)
    n_chunks = -(-seq // CHUNK)
    vc = jnp.pad(v, ((0, 0), (0, n_chunks * CHUNK - seq), (0, 0)))
    vc = vc.reshape(bsz, n_chunks, CHUNK, N_GROUPS_A, GROUP_DIM_A)
    causal = jnp.tril(jnp.ones((CHUNK, CHUNK), dtype=bool))
    w_causal = jnp.where(causal[None], w_s, jnp.zeros_like(w_s))
    mixed = jnp.einsum('gij,bnjgc->bnigc', w_causal, vc) + b_s.T[:, :, None]
    mixed = mixed.reshape(bsz, n_chunks * CHUNK, D_GATE)[:, :seq]
    return (u * mixed) @ w_out, v


def shared_kv(x, kv_norm_g, w_k, w_v, k_norm_g, pos):
    bsz, seq, _ = x.shape
    h = rms_norm(x, kv_norm_g)
    k = (h @ w_k).reshape(bsz, seq, N_KV_HEADS, HEAD_DIM)
    v = (h @ w_v).reshape(bsz, seq, N_KV_HEADS, HEAD_DIM)
    return partial_rope(rms_norm(k, k_norm_g), pos), v


def query_heads(h, w_q, q_norm_g, pos):
    bsz, seq, _ = h.shape
    q = (h @ w_q).reshape(bsz, seq, N_HEADS, HEAD_DIM)
    q = partial_rope(rms_norm(q, q_norm_g), pos)
    return q.reshape(bsz, seq, N_KV_HEADS, Q_PER_KV, HEAD_DIM)


def sink_softmax(s, sinks):
    sink = sinks.astype(jnp.float32).reshape(N_KV_HEADS, Q_PER_KV, 1, 1)
    m = jnp.maximum(jnp.max(s, axis=-1, keepdims=True), sink)
    p = jnp.exp(s - m)
    return p / (jnp.sum(p, axis=-1, keepdims=True) + jnp.exp(sink - m))


def window_attention_prompt(q, k, v, sinks):
    bsz, seq = q.shape[:2]
    nb = seq // WINDOW
    qb = q.reshape(bsz, nb, WINDOW, N_KV_HEADS, Q_PER_KV, HEAD_DIM)
    kb = k.reshape(bsz, nb, WINDOW, N_KV_HEADS, HEAD_DIM)
    vb = v.reshape(bsz, nb, WINDOW, N_KV_HEADS, HEAD_DIM)
    shift = ((0, 0), (1, 0), (0, 0), (0, 0), (0, 0))
    kk = jnp.concatenate([jnp.pad(kb, shift)[:, :-1], kb], axis=2)
    vv = jnp.concatenate([jnp.pad(vb, shift)[:, :-1], vb], axis=2)
    s = jnp.einsum('bnqkgd,bnjkd->bnkgqj', qb, kk, preferred_element_type=jnp.float32) * (HEAD_DIM ** -0.5)
    qi = jnp.arange(WINDOW)[:, None]
    kj = jnp.arange(2 * WINDOW)[None, :]
    diff = qi + WINDOW - kj
    band = (diff >= 0) & (diff < WINDOW)
    has_prev = (jnp.arange(nb) > 0)[:, None, None]
    mask = band[None] & (has_prev | (kj >= WINDOW)[None])
    s = jnp.where(mask[None, :, None, None], s, -jnp.inf)
    p = sink_softmax(s, sinks).astype(v.dtype)
    o = jnp.einsum('bnkgqj,bnjkd->bnqkgd', p, vv)
    return o.reshape(bsz, seq, N_HEADS * HEAD_DIM)


def window_attention_sample(q, k_all, v_all, q_pos, k_pos, sinks):
    s = jnp.einsum('btkgd,bjkd->bkgtj', q, k_all, preferred_element_type=jnp.float32) * (HEAD_DIM ** -0.5)
    diff = q_pos[:, None] - k_pos[None, :]
    mask = (diff >= 0) & (diff < WINDOW)
    s = jnp.where(mask, s, -jnp.inf)
    p = sink_softmax(s, sinks).astype(v_all.dtype)
    o = jnp.einsum('bkgtj,bjkd->btkgd', p, v_all)
    return o.reshape(q.shape[0], q.shape[1], N_HEADS * HEAD_DIM)


def moe_ffn(h, w_router, b_router, w_gu, b_gu, w_down, b_down):
    shp = h.shape
    xt = h.reshape(-1, D_MODEL)
    n_tok = xt.shape[0]
    logits = (xt @ w_router + b_router).astype(jnp.float32)
    top_val, top_idx = lax.top_k(logits, TOP_K)
    gates = jax.nn.softmax(top_val, axis=-1)
    n_assign = n_tok * TOP_K
    flat_e = top_idx.reshape(-1).astype(jnp.int32)
    order = jnp.argsort(flat_e)
    sorted_e = flat_e[order]
    sorted_tok = (order // TOP_K).astype(jnp.int32)
    counts = jnp.bincount(flat_e, length=N_EXPERTS)
    padded = (counts + MOE_BLOCK - 1) // MOE_BLOCK * MOE_BLOCK
    pad_end = jnp.cumsum(padded)
    pad_start = pad_end - padded
    start = jnp.cumsum(counts) - counts
    dest = (pad_start[sorted_e] + jnp.arange(n_assign, dtype=jnp.int32) - start[sorted_e]).astype(jnp.int32)
    n_blocks = (n_assign + N_EXPERTS * (MOE_BLOCK - 1)) // MOE_BLOCK
    buf_tok = jnp.zeros((n_blocks * MOE_BLOCK,), jnp.int32).at[dest].set(sorted_tok)
    block_e = jnp.minimum(jnp.searchsorted(pad_end, jnp.arange(n_blocks) * MOE_BLOCK, side='right'), N_EXPERTS - 1)

    def expert_block(args):
        tok, e = args
        gu = xt[tok] @ w_gu[e] + b_gu[e]
        gate, up = jnp.split(gu, 2, axis=-1)
        gate = jnp.minimum(gate, SWIGLU_LIMIT)
        up = jnp.clip(up, -SWIGLU_LIMIT, SWIGLU_LIMIT)
        act = (up + 1.0) * gate * jax.nn.sigmoid(SWIGLU_ALPHA * gate)
        return act @ w_down[e] + b_down[e]

    out_buf = lax.map(expert_block, (buf_tok.reshape(n_blocks, MOE_BLOCK), block_e))
    out_buf = out_buf.reshape(n_blocks * MOE_BLOCK, D_MODEL)
    dest_orig = jnp.zeros((n_assign,), jnp.int32).at[order].set(dest)
    picked = out_buf[dest_orig].reshape(n_tok, TOP_K, D_MODEL)
    y = jnp.einsum('tk,tkd->td', gates.astype(h.dtype), picked)
    return y.reshape(shp)


def setup_inputs(seed: int = 0) -> dict:
    key = jax.random.key(seed)
    keys = iter(jax.random.split(key, 32))

    def nrm(shape, scale):
        return jax.random.normal(next(keys), shape, dtype=jnp.float32) * scale

    def gain(shape):
        return 1.0 + nrm(shape, 0.02)

    win = min(WINDOW, PAST_LEN)
    d_q = N_HEADS * HEAD_DIM
    d_kv = N_KV_HEADS * HEAD_DIM
    return {
        'x_prompt': nrm((BATCH, SEQ, D_MODEL), 1.0),
        'x_sample': nrm((DEC_BATCH, DEC_SEQ, D_MODEL), 1.0),
        'cache_k': nrm((DEC_BATCH, win, N_KV_HEADS, HEAD_DIM), 1.0),
        'cache_v': nrm((DEC_BATCH, win, N_KV_HEADS, HEAD_DIM), 1.0),
        'a_norm_g': gain((N_A_LAYERS, D_MODEL)),
        'a_w_in': nrm((N_A_LAYERS, D_MODEL, 2 * D_GATE), D_MODEL ** -0.5),
        'a_ln_g': gain((N_A_LAYERS, D_GATE)),
        'a_ln_b': nrm((N_A_LAYERS, D_GATE), 0.02),
        'a_w_s': nrm((N_A_LAYERS, N_GROUPS_A, CHUNK, CHUNK), CHUNK ** -0.5),
        'a_b_s': 1.0 + nrm((N_A_LAYERS, N_GROUPS_A, CHUNK), 0.1),
        'a_w_out': nrm((N_A_LAYERS, D_GATE, D_MODEL), D_GATE ** -0.5),
        'kv_norm_g': gain((D_MODEL,)),
        'w_k': nrm((D_MODEL, d_kv), D_MODEL ** -0.5),
        'w_v': nrm((D_MODEL, d_kv), D_MODEL ** -0.5),
        'k_norm_g': gain((HEAD_DIM,)),
        'b_norm_g': gain((N_B_LAYERS, D_MODEL)),
        'w_q': nrm((N_B_LAYERS, D_MODEL, d_q), D_MODEL ** -0.5),
        'q_norm_g': gain((N_B_LAYERS, HEAD_DIM)),
        'sinks': nrm((N_B_LAYERS, N_HEADS), 0.5),
        'w_o': nrm((N_B_LAYERS, d_q, D_MODEL), d_q ** -0.5),
        'ffn_norm_g': gain((DEPTH, D_MODEL)),
        'w_router': nrm((DEPTH, D_MODEL, N_EXPERTS), D_MODEL ** -0.5),
        'b_router': nrm((DEPTH, N_EXPERTS), 0.01),
        'w_gu': nrm((DEPTH, N_EXPERTS, D_MODEL, 2 * D_EXPERT), D_MODEL ** -0.5),
        'b_gu': nrm((DEPTH, N_EXPERTS, 2 * D_EXPERT), 0.02),
        'w_down': nrm((DEPTH, N_EXPERTS, D_EXPERT, D_MODEL), D_EXPERT ** -0.5),
        'b_down': nrm((DEPTH, N_EXPERTS, D_MODEL), 0.02),
    }


def reference(x_prompt, x_sample, cache_k, cache_v, a_norm_g, a_w_in, a_ln_g, a_ln_b, a_w_s, a_b_s, a_w_out, kv_norm_g, w_k, w_v, k_norm_g, b_norm_g, w_q, q_norm_g, sinks, w_o, ffn_norm_g, w_router, b_router, w_gu, b_gu, w_down, b_down):
    seq, dec_seq = x_prompt.shape[1], x_sample.shape[1]
    win = cache_k.shape[1]
    pos_p = jnp.arange(seq, dtype=jnp.int32)
    pos_s = PAST_LEN + jnp.arange(dec_seq, dtype=jnp.int32)
    pos_c = PAST_LEN - win + jnp.arange(win, dtype=jnp.int32)
    k_pos_s = jnp.concatenate([pos_c, pos_s])
    xp, xs = x_prompt, x_sample
    v_rows = []
    for layer in range(DEPTH):
        if layer < N_A_LAYERS:
            a = layer
            yp, _ = spatial_gating_mixer(rms_norm(xp, a_norm_g[a]), a_w_in[a], a_ln_g[a], a_ln_b[a], a_w_s[a], a_b_s[a], a_w_out[a])
            ys, vs = spatial_gating_mixer(rms_norm(xs, a_norm_g[a]), a_w_in[a], a_ln_g[a], a_ln_b[a], a_w_s[a], a_b_s[a], a_w_out[a])
            xp = xp + yp
            xs = xs + ys
            v_rows.append(vs)
        else:
            b = layer - N_A_LAYERS
            if b == 0:
                k_p, v_p = shared_kv(xp, kv_norm_g, w_k, w_v, k_norm_g, pos_p)
                k_s, v_s = shared_kv(xs, kv_norm_g, w_k, w_v, k_norm_g, pos_s)
                k_all = jnp.concatenate([cache_k, k_s], axis=1)
                v_all = jnp.concatenate([cache_v, v_s], axis=1)
            qp = query_heads(rms_norm(xp, b_norm_g[b]), w_q[b], q_norm_g[b], pos_p)
            xp = xp + window_attention_prompt(qp, k_p, v_p, sinks[b]) @ w_o[b]
            qs = query_heads(rms_norm(xs, b_norm_g[b]), w_q[b], q_norm_g[b], pos_s)
            xs = xs + window_attention_sample(qs, k_all, v_all, pos_s, k_pos_s, sinks[b]) @ w_o[b]
        xp = xp + moe_ffn(rms_norm(xp, ffn_norm_g[layer]), w_router[layer], b_router[layer], w_gu[layer], b_gu[layer], w_down[layer], b_down[layer])
        xs = xs + moe_ffn(rms_norm(xs, ffn_norm_g[layer]), w_router[layer], b_router[layer], w_gu[layer], b_gu[layer], w_down[layer], b_down[layer])
    state_v_sample = jnp.stack(v_rows)
    return (xp, xs, state_v_sample, k_p[:, -WINDOW:], v_p[:, -WINDOW:], k_s, v_s)
```

```python
import functools

import jax
import jax.numpy as jnp
from jax import lax
from jax.experimental import pallas as pl
from jax.experimental.pallas import tpu as pltpu

F32, BF16, I32 = jnp.float32, jnp.bfloat16, jnp.int32

D_MODEL = 1024
CHUNK = 128
D_GATE = 2 * D_MODEL
N_GROUPS_A = 8
GROUP_DIM_A = D_GATE // N_GROUPS_A
HEAD_DIM = 64
N_HEADS = D_MODEL // HEAD_DIM
N_KV_HEADS = N_HEADS // 4
Q_PER_KV = N_HEADS // N_KV_HEADS
D_KV = N_KV_HEADS * HEAD_DIM
WINDOW = 128
ROPE_DIM = HEAD_DIM // 4
ROPE_THETA = 500000.0
N_EXPERTS = 32
TOP_K = 4
D_EXPERT = D_MODEL
SWIGLU_LIMIT = 7.0
SWIGLU_ALPHA = 1.702
EPS = 1e-6
PAST_LEN = 8192

V7X_LANES = 128
V7X_VMEM_BYTES = 64 * 1024 * 1024
VMEM_LIMIT_BYTES = V7X_VMEM_BYTES - 8 * 1024 * 1024

TM_DENSE = 512
TM_COMBINE = 256
MOE_ROWS = 512
SLAB = V7X_LANES
PAIR = 2 * HEAD_DIM
INV_SQRT2 = 0.7071067811865476


def _cparams():
    return pltpu.CompilerParams(dimension_semantics=("arbitrary",), vmem_limit_bytes=VMEM_LIMIT_BYTES)


def _resident(shape):
    zeros = (0,) * len(shape)
    return pl.BlockSpec(shape, lambda i, *_: zeros, pipeline_mode=pl.Buffered(1))


def _rows(tm, width, offset=0):
    return pl.BlockSpec((tm, width), lambda i, *_: (i + offset, 0))


def _rms(x, g):
    return x * lax.rsqrt(jnp.mean(x * x, axis=-1, keepdims=True) + EPS) * g


def _pair_norm_rope(x2, g2, cos2, sin2):
    lane = lax.broadcasted_iota(I32, x2.shape, 1)
    first = lane < HEAD_DIM
    sq = x2 * x2
    s0 = jnp.sum(jnp.where(first, sq, 0.0), axis=-1, keepdims=True)
    s1 = jnp.sum(jnp.where(first, 0.0, sq), axis=-1, keepdims=True)
    ms = jnp.where(first, s0, s1) * (1.0 / HEAD_DIM)
    y = x2 * lax.rsqrt(ms + EPS) * g2
    half = ROPE_DIM // 2
    up = pltpu.roll(y, PAIR - half, 1)
    dn = pltpu.roll(y, half, 1)
    partner = jnp.where((lane & (HEAD_DIM - 1)) < half, up, dn)
    return y * cos2 + partner * sin2


def _mixer_a_kernel(x_ref, ng_ref, win_ref, lng_ref, lnb_ref, mix_ref, bias_ref, wout_ref, *refs,
                    tm, single_token_chunks):
    if single_token_chunks:
        o_ref, vout_ref, u_scr, v_scr, p_scr = refs
    else:
        o_ref, u_scr, v_scr, p_scr = refs
    x = x_ref[...]
    h = _rms(x, ng_ref[...]).astype(BF16)
    nc = 512
    for c in range(2 * D_GATE // nc):
        z = jnp.dot(h, win_ref[:, c * nc:(c + 1) * nc], preferred_element_type=F32)
        z = 0.5 * z * (1.0 + lax.erf(z * INV_SQRT2))
        if c < D_GATE // nc:
            u_scr[:, c * nc:(c + 1) * nc] = z
        else:
            v_scr[:, (c - D_GATE // nc) * nc:(c - D_GATE // nc + 1) * nc] = z
    v = v_scr[...]
    vc = v - jnp.mean(v, axis=-1, keepdims=True)
    vn = vc * lax.rsqrt(jnp.mean(vc * vc, axis=-1, keepdims=True) + EPS) * lng_ref[...] + lnb_ref[...]
    if single_token_chunks:
        vout_ref[...] = vn
        p_scr[...] = (u_scr[...] * (vn * mix_ref[...] + bias_ref[...])).astype(BF16)
    else:
        v_scr[...] = vn
        row = lax.broadcasted_iota(I32, (CHUNK, CHUNK), 0)
        col = lax.broadcasted_iota(I32, (CHUNK, CHUNK), 1)
        for g in range(N_GROUPS_A):
            wc = jnp.where(row >= col, mix_ref[g], 0.0).astype(BF16)
            cols = slice(g * GROUP_DIM_A, (g + 1) * GROUP_DIM_A)
            for c in range(tm // CHUNK):
                rows = slice(c * CHUNK, (c + 1) * CHUNK)
                mixed = jnp.dot(wc, v_scr[rows, cols].astype(BF16), preferred_element_type=F32)
                mixed = mixed + bias_ref[:, cols]
                p_scr[rows, cols] = (u_scr[rows, cols] * mixed).astype(BF16)
    o_ref[...] = x + jnp.dot(p_scr[...], wout_ref[...], preferred_element_type=F32)


def _mixer_a(x, ng, win, lng, lnb, mix, bias, wout, *, tm, single_token_chunks):
    n = x.shape[0]
    out_shape = [jax.ShapeDtypeStruct((n, D_MODEL), F32)]
    out_specs = [_rows(tm, D_MODEL)]
    if single_token_chunks:
        out_shape.append(jax.ShapeDtypeStruct((n, D_GATE), F32))
        out_specs.append(_rows(tm, D_GATE))
    return pl.pallas_call(
        functools.partial(_mixer_a_kernel, tm=tm, single_token_chunks=single_token_chunks),
        grid=(n // tm,),
        in_specs=[_rows(tm, D_MODEL), _resident(ng.shape), _resident(win.shape), _resident(lng.shape),
                  _resident(lnb.shape), _resident(mix.shape), _resident(bias.shape), _resident(wout.shape)],
        out_specs=out_specs,
        out_shape=out_shape,
        scratch_shapes=[pltpu.VMEM((tm, D_GATE), F32), pltpu.VMEM((tm, D_GATE), F32),
                        pltpu.VMEM((tm, D_GATE), BF16)],
        compiler_params=_cparams(),
        name="mixer_a_sample" if single_token_chunks else "mixer_a_prompt",
    )(x, ng, win, lng, lnb, mix, bias, wout)


def _route_rows(x, g, whi, wlo, bias, run_scr):
    tm = x.shape[0]
    h = _rms(x, g)
    hh = h.astype(BF16)
    hl = (h - hh.astype(F32)).astype(BF16)
    logits = (jnp.dot(hh, whi, preferred_element_type=F32) + jnp.dot(hh, wlo, preferred_element_type=F32)
              + jnp.dot(hl, whi, preferred_element_type=F32) + bias)
    e = lax.broadcasted_iota(I32, (tm, N_EXPERTS), 1).astype(F32)
    hot, vals, idxs = [], [], []
    for _ in range(TOP_K):
        m = jnp.max(logits, axis=-1, keepdims=True)
        idx = jnp.min(jnp.where(logits == m, e, float(N_EXPERTS)), axis=-1, keepdims=True)
        oh = e == idx
        hot.append(oh)
        vals.append(m)
        idxs.append(idx)
        logits = jnp.where(oh, -jnp.inf, logits)
    ex = [jnp.exp(v - vals[0]) for v in vals]
    den = ex[0] + ex[1] + ex[2] + ex[3]
    picked = jnp.zeros((tm, N_EXPERTS), F32)
    for oh in hot:
        picked = picked + jnp.where(oh, 1.0, 0.0)
    r = lax.broadcasted_iota(I32, (tm, tm), 0)
    c = lax.broadcasted_iota(I32, (tm, tm), 1)
    before = jnp.where(c < r, 1.0, 0.0).astype(BF16)
    run = run_scr[0:1, 0:N_EXPERTS]
    pos = jnp.dot(before, picked.astype(BF16), preferred_element_type=F32) + run
    run_scr[0:1, 0:N_EXPERTS] = run + jnp.sum(picked, axis=0, keepdims=True)
    lane = lax.broadcasted_iota(I32, (tm, SLAB), 1)
    slab = jnp.zeros((tm, SLAB), F32)
    for k in range(TOP_K):
        pos_k = jnp.sum(jnp.where(hot[k], pos, 0.0), axis=-1, keepdims=True)
        slab = jnp.where(lane == k, idxs[k], slab)
        slab = jnp.where(lane == TOP_K + k, pos_k, slab)
        slab = jnp.where(lane == 2 * TOP_K + k, ex[k] / den, slab)
    return h, slab


def _router_kernel(xp_ref, xs_ref, g_ref, whi_ref, wlo_ref, b_ref, h_ref, slabp_ref, slabs_ref, cnt_ref, run_scr,
                   *, n_tiles, n_s):
    i = pl.program_id(0)

    @pl.when(i == 0)
    def _():
        run_scr[...] = jnp.zeros_like(run_scr)

    @pl.when(i < n_tiles)
    def _():
        h, slab = _route_rows(xp_ref[...], g_ref[...], whi_ref[...], wlo_ref[...], b_ref[...], run_scr)
        h_ref[...] = h
        slabp_ref[...] = slab

    @pl.when(i == n_tiles)
    def _():
        h, slab = _route_rows(xs_ref[...], g_ref[...], whi_ref[...], wlo_ref[...], b_ref[...], run_scr)
        h_ref[0:n_s, :] = h
        h_ref[n_s:, :] = jnp.zeros((h_ref.shape[0] - n_s, D_MODEL), F32)
        slabs_ref[...] = slab

    cnt_ref[...] = run_scr[...]


def _router(xp, xs, g, whi, wlo, b, *, tm):
    n_p, n_s = xp.shape[0], xs.shape[0]
    n_tiles = n_p // tm
    last = n_tiles - 1
    return pl.pallas_call(
        functools.partial(_router_kernel, n_tiles=n_tiles, n_s=n_s),
        grid=(n_tiles + 1,),
        in_specs=[pl.BlockSpec((tm, D_MODEL), lambda i: (jnp.minimum(i, last), 0)), _resident(xs.shape),
                  _resident(g.shape), _resident(whi.shape), _resident(wlo.shape), _resident(b.shape)],
        out_specs=[_rows(tm, D_MODEL), pl.BlockSpec((tm, SLAB), lambda i: (jnp.minimum(i, last), 0)),
                   pl.BlockSpec((n_s, SLAB), lambda i: (0, 0)), pl.BlockSpec((8, V7X_LANES), lambda i: (0, 0))],
        out_shape=[jax.ShapeDtypeStruct((n_p + tm, D_MODEL), F32), jax.ShapeDtypeStruct((n_p, SLAB), F32),
                   jax.ShapeDtypeStruct((n_s, SLAB), F32), jax.ShapeDtypeStruct((8, V7X_LANES), F32)],
        scratch_shapes=[pltpu.VMEM((8, V7X_LANES), F32)],
        compiler_params=_cparams(),
        name="moe_router",
    )(xp, xs, g, whi, wlo, b)


def _expert_kernel(be_ref, nact_ref, tok_first_ref, tok_next_ref, h_ref, wgu_ref, bgu_ref, wdn_ref, bdn_ref,
                   o_ref, xbuf, sem, wgu_bf, wdn_bf, act_scr, *, rows):
    b = pl.program_id(0)
    nact = nact_ref[0]
    slot = lax.rem(b, 2)

    def row_copy(tok_ref, r, s):
        t = tok_ref[0, 0, r]
        return pltpu.make_async_copy(h_ref.at[pl.ds(t, 1)], xbuf.at[s, pl.ds(r, 1)], sem.at[s])

    def gather(tok_ref, s):
        def body(r, carry):
            row_copy(tok_ref, r, s).start()
            return carry
        lax.fori_loop(0, rows, body, 0, unroll=8)

    @pl.when(b == 0)
    def _():
        gather(tok_first_ref, 0)

    @pl.when(b + 1 < nact)
    def _():
        gather(tok_next_ref, 1 - slot)

    @pl.when(b < nact)
    def _():
        pltpu.make_async_copy(h_ref.at[pl.ds(0, rows)], xbuf.at[slot], sem.at[slot]).wait()
        changed = jnp.logical_or(b == 0, be_ref[b] != be_ref[jnp.maximum(b - 1, 0)])

        @pl.when(changed)
        def _():
            step = 256
            for c in range(D_MODEL // step):
                wgu_bf[c * step:(c + 1) * step, :] = wgu_ref[c * step:(c + 1) * step, :].astype(BF16)
                wdn_bf[c * step:(c + 1) * step, :] = wdn_ref[c * step:(c + 1) * step, :].astype(BF16)

        xb = xbuf[slot].astype(BF16)
        nc = 512
        for c in range(D_EXPERT // nc):
            gcols = slice(c * nc, (c + 1) * nc)
            ucols = slice(D_EXPERT + c * nc, D_EXPERT + (c + 1) * nc)
            gate = jnp.dot(xb, wgu_bf[:, gcols], preferred_element_type=F32) + bgu_ref[:, gcols]
            up = jnp.dot(xb, wgu_bf[:, ucols], preferred_element_type=F32) + bgu_ref[:, ucols]
            gate = jnp.minimum(gate, SWIGLU_LIMIT)
            up = jnp.clip(up, -SWIGLU_LIMIT, SWIGLU_LIMIT)
            act = (up + 1.0) * gate * (1.0 / (1.0 + jnp.exp(-SWIGLU_ALPHA * gate)))
            act_scr[:, gcols] = act.astype(BF16)
        o_ref[...] = jnp.dot(act_scr[...], wdn_bf[...], preferred_element_type=F32) + bdn_ref[...]

    @pl.when(b >= nact)
    def _():
        o_ref[...] = jnp.zeros_like(o_ref)


def _experts(block_e, n_active, tok3, h_all, wgu, bgu, wdn, bdn, *, rows):
    nb = tok3.shape[0]
    grid_spec = pltpu.PrefetchScalarGridSpec(
        num_scalar_prefetch=2,
        grid=(nb,),
        in_specs=[
            pl.BlockSpec((1, 1, rows), lambda b, be, na: (0, 0, 0), memory_space=pltpu.SMEM),
            pl.BlockSpec((1, 1, rows), lambda b, be, na: (jnp.minimum(b + 1, nb - 1), 0, 0),
                         memory_space=pltpu.SMEM),
            pl.BlockSpec(memory_space=pl.ANY),
            pl.BlockSpec((None, D_MODEL, 2 * D_EXPERT), lambda b, be, na: (be[b], 0, 0)),
            pl.BlockSpec((None, 1, 2 * D_EXPERT), lambda b, be, na: (be[b], 0, 0)),
            pl.BlockSpec((None, D_EXPERT, D_MODEL), lambda b, be, na: (be[b], 0, 0)),
            pl.BlockSpec((None, 1, D_MODEL), lambda b, be, na: (be[b], 0, 0)),
        ],
        out_specs=pl.BlockSpec((rows, D_MODEL), lambda b, be, na: (b, 0)),
        scratch_shapes=[pltpu.VMEM((2, rows, D_MODEL), F32), pltpu.SemaphoreType.DMA((2,)),
                        pltpu.VMEM((D_MODEL, 2 * D_EXPERT), BF16), pltpu.VMEM((D_EXPERT, D_MODEL), BF16),
                        pltpu.VMEM((rows, D_EXPERT), BF16)],
    )
    return pl.pallas_call(
        functools.partial(_expert_kernel, rows=rows),
        grid_spec=grid_spec,
        out_shape=jax.ShapeDtypeStruct((nb * rows, D_MODEL), F32),
        compiler_params=_cparams(),
        name="moe_experts",
    )(block_e, n_active, tok3, tok3, h_all, wgu, bgu, wdn, bdn)


def _combine_kernel(dest_first_ref, dest_next_ref, x_ref, slab_ref, ebuf_ref, o_ref, rbuf, sem, *, tm, n_tiles):
    i = pl.program_id(0)
    slot = lax.rem(i, 2)

    def gather(dest_ref, s):
        def body(r, carry):
            for k in range(TOP_K):
                d = dest_ref[0, 0, k * tm + r]
                pltpu.make_async_copy(ebuf_ref.at[pl.ds(d, 1)], rbuf.at[s, k, pl.ds(r, 1)], sem.at[s]).start()
            return carry
        lax.fori_loop(0, tm, body, 0, unroll=4)

    @pl.when(i == 0)
    def _():
        gather(dest_first_ref, 0)

    @pl.when(i + 1 < n_tiles)
    def _():
        gather(dest_next_ref, 1 - slot)

    for k in range(TOP_K):
        pltpu.make_async_copy(ebuf_ref.at[pl.ds(0, tm)], rbuf.at[slot, k], sem.at[slot]).wait()
    slab = slab_ref[...]
    acc = x_ref[...]
    for k in range(TOP_K):
        acc = acc + slab[:, 2 * TOP_K + k:2 * TOP_K + k + 1] * rbuf[slot, k]
    o_ref[...] = acc


def _combine(x, slab, dest3, ebuf, *, tm):
    n = x.shape[0]
    n_tiles = n // tm
    return pl.pallas_call(
        functools.partial(_combine_kernel, tm=tm, n_tiles=n_tiles),
        grid=(n_tiles,),
        in_specs=[
            pl.BlockSpec((1, 1, TOP_K * tm), lambda i: (0, 0, 0), memory_space=pltpu.SMEM),
            pl.BlockSpec((1, 1, TOP_K * tm), lambda i: (jnp.minimum(i + 1, n_tiles - 1), 0, 0),
                         memory_space=pltpu.SMEM),
            _rows(tm, D_MODEL), _rows(tm, SLAB), pl.BlockSpec(memory_space=pl.ANY),
        ],
        out_specs=_rows(tm, D_MODEL),
        out_shape=jax.ShapeDtypeStruct((n, D_MODEL), F32),
        scratch_shapes=[pltpu.VMEM((2, TOP_K, tm, D_MODEL), F32), pltpu.SemaphoreType.DMA((2,))],
        compiler_params=_cparams(),
        name="moe_combine",
    )(dest3, dest3, x, slab, ebuf)


def _moe(xs, g, w_router, b_router, wgu, bgu, wdn, bdn):
    n_all = xs[0].shape[0] + xs[1].shape[0]
    whi = w_router.astype(BF16)
    wlo = (w_router - whi.astype(F32)).astype(BF16)
    h_all, slab_p, slab_s, cnt = _router(xs[0], xs[1], g.reshape(1, D_MODEL), whi, wlo,
                                         b_router.reshape(1, N_EXPERTS), tm=TM_DENSE)
    slabs = [slab_p, slab_s]
    slab_all = jnp.concatenate([s[:, :2 * TOP_K] for s in slabs], axis=0)
    idx = slab_all[:, :TOP_K].astype(I32)
    pos = slab_all[:, TOP_K:].astype(I32)
    counts = cnt[0, :N_EXPERTS].astype(I32)
    rows = MOE_ROWS
    nb = (n_all * TOP_K + N_EXPERTS * (rows - 1)) // rows
    padded = (counts + rows - 1) // rows * rows
    pad_end = jnp.cumsum(padded)
    pad_start = pad_end - padded
    dest = pad_start[idx] + pos
    tok = jnp.broadcast_to(jnp.arange(n_all, dtype=I32)[:, None], dest.shape)
    buf_tok = jnp.zeros((nb * rows,), I32).at[dest.reshape(-1)].set(tok.reshape(-1), unique_indices=True)
    n_active = (pad_end[-1] // rows).astype(I32)
    blk = jnp.arange(nb, dtype=I32)
    block_e = jnp.minimum(jnp.searchsorted(pad_end, blk * rows, side="right"), N_EXPERTS - 1).astype(I32)
    block_e = jnp.where(blk < n_active, block_e, block_e[jnp.maximum(n_active - 1, 0)])
    ebuf = _experts(block_e, n_active.reshape(1), buf_tok.reshape(nb, 1, rows), h_all,
                    wgu, bgu.reshape(N_EXPERTS, 1, 2 * D_EXPERT), wdn, bdn.reshape(N_EXPERTS, 1, D_MODEL), rows=rows)
    outs, off = [], 0
    for x, slab in zip(xs, slabs):
        n = x.shape[0]
        tm = min(TM_COMBINE, n)
        d = dest[off:off + n].reshape(n // tm, tm, TOP_K).transpose(0, 2, 1).reshape(n // tm, 1, TOP_K * tm)
        outs.append(_combine(x, slab, d, ebuf, tm=tm))
        off += n
    return outs


def _kv_kernel(x_ref, kvg_ref, wk_ref, wv_ref, kng_ref, cos_ref, sin_ref, *refs, with_q):
    if with_q:
        bng_ref, wq_ref, qng_ref, k_ref, v_ref, q_ref = refs
    else:
        k_ref, v_ref = refs
    x = x_ref[...]
    h = _rms(x, kvg_ref[...]).astype(BF16)
    k = jnp.dot(h, wk_ref[...], preferred_element_type=F32)
    v_ref[...] = jnp.dot(h, wv_ref[...], preferred_element_type=F32)
    cos2, sin2 = cos_ref[...], sin_ref[...]
    for p in range(D_KV // PAIR):
        cols = slice(p * PAIR, (p + 1) * PAIR)
        k_ref[:, cols] = _pair_norm_rope(k[:, cols], kng_ref[...], cos2, sin2)
    if with_q:
        hq = _rms(x, bng_ref[...]).astype(BF16)
        q = jnp.dot(hq, wq_ref[...], preferred_element_type=F32)
        for p in range(D_MODEL // PAIR):
            cols = slice(p * PAIR, (p + 1) * PAIR)
            q_ref[:, cols] = _pair_norm_rope(q[:, cols], qng_ref[...], cos2, sin2)


def _kv_proj(x, kvg, wk, wv, kng2, cos2, sin2, q_args, *, tm, table_tiles):
    n = x.shape[0]
    with_q = q_args is not None
    table_spec = pl.BlockSpec((tm, PAIR), lambda i: (lax.rem(i, table_tiles), 0))
    in_specs = [_rows(tm, D_MODEL), _resident(kvg.shape), _resident(wk.shape), _resident(wv.shape),
                _resident(kng2.shape), table_spec, table_spec]
    args = [x, kvg, wk, wv, kng2, cos2, sin2]
    out_specs = [_rows(tm, D_KV), _rows(tm, D_KV)]
    out_shape = [jax.ShapeDtypeStruct((n, D_KV), F32), jax.ShapeDtypeStruct((n, D_KV), F32)]
    if with_q:
        in_specs += [_resident(a.shape) for a in q_args]
        args += list(q_args)
        out_specs.append(_rows(tm, D_MODEL))
        out_shape.append(jax.ShapeDtypeStruct((n, D_MODEL), F32))
    return pl.pallas_call(
        functools.partial(_kv_kernel, with_q=with_q),
        grid=(n // tm,),
        in_specs=in_specs, out_specs=out_specs, out_shape=out_shape,
        compiler_params=_cparams(),
        name="kvq_proj_sample" if with_q else "kv_proj_prompt",
    )(*args)


def _attn_prompt_kernel(sinks_ref, x_ref, bng_ref, wq_ref, qng_ref, cos_ref, sin_ref, kc_ref, kp_ref, vc_ref, vp_ref,
                        wo_ref, o_ref, q_scr, a_scr, *, tm, tiles_per_seq):
    i = pl.program_id(0)
    x = x_ref[...]
    hq = _rms(x, bng_ref[...]).astype(BF16)
    q = jnp.dot(hq, wq_ref[...], preferred_element_type=F32)
    cos2, sin2 = cos_ref[...], sin_ref[...]
    for p in range(D_MODEL // PAIR):
        cols = slice(p * PAIR, (p + 1) * PAIR)
        q_scr[:, cols] = _pair_norm_rope(q[:, cols], qng_ref[...], cos2, sin2)

    nq = Q_PER_KV * WINDOW
    qi = lax.broadcasted_iota(I32, (nq, 2 * WINDOW), 0) & (WINDOW - 1)
    kj = lax.broadcasted_iota(I32, (nq, 2 * WINDOW), 1)
    diff = qi + WINDOW - kj
    band = jnp.logical_and(diff >= 0, diff < WINDOW)
    lane = lax.broadcasted_iota(I32, (WINDOW, PAIR), 1)
    first_half = lane < HEAD_DIM
    first_kj = jnp.where(lax.rem(i, tiles_per_seq) == 0, WINDOW, 0)
    scale = HEAD_DIM ** -0.5

    for blk in range(tm // WINDOW):
        rows = slice(blk * WINDOW, (blk + 1) * WINDOW)
        if blk == 0:
            kk = jnp.concatenate([kp_ref[...], kc_ref[rows, :]], axis=0).astype(BF16)
            vv = jnp.concatenate([vp_ref[...], vc_ref[rows, :]], axis=0).astype(BF16)
            valid = jnp.logical_and(band, kj >= first_kj)
        else:
            win = slice((blk - 1) * WINDOW, (blk + 1) * WINDOW)
            kk = kc_ref[win, :].astype(BF16)
            vv = vc_ref[win, :].astype(BF16)
            valid = band
        for kvh in range(N_KV_HEADS):
            kcols = slice((kvh // 2) * PAIR, (kvh // 2 + 1) * PAIR)
            k_half = kvh % 2
            pieces, sink_rows = [], []
            for g in range(Q_PER_KV):
                head = kvh * Q_PER_KV + g
                qp = q_scr[rows, (head // 2) * PAIR:(head // 2 + 1) * PAIR]
                qm = jnp.where(first_half if head % 2 == 0 else jnp.logical_not(first_half), qp, 0.0)
                if head % 2 != k_half:
                    qm = pltpu.roll(qm, HEAD_DIM, 1)
                pieces.append(qm.astype(BF16))
                sink_rows.append(jnp.full((WINDOW, 1), sinks_ref[head], F32))
            q4 = jnp.concatenate(pieces, axis=0)
            sink = jnp.concatenate(sink_rows, axis=0)
            s = lax.dot_general(q4, kk[:, kcols], (((1,), (1,)), ((), ())), preferred_element_type=F32) * scale
            s = jnp.where(valid, s, -jnp.inf)
            m = jnp.maximum(jnp.max(s, axis=-1, keepdims=True), sink)
            p = jnp.exp(s - m)
            den = jnp.sum(p, axis=-1, keepdims=True) + jnp.exp(sink - m)
            o4 = jnp.dot(p.astype(BF16), vv[:, kcols], preferred_element_type=F32) * (1.0 / den)
            for pq in range(Q_PER_KV // 2):
                halves = []
                for hq_half in range(2):
                    g = 2 * pq + hq_half
                    og = o4[g * WINDOW:(g + 1) * WINDOW, :]
                    if hq_half != k_half:
                        og = pltpu.roll(og, HEAD_DIM, 1)
                    halves.append(og)
                pair = (kvh * Q_PER_KV) // 2 + pq
                a_scr[rows, pair * PAIR:(pair + 1) * PAIR] = jnp.where(first_half, halves[0], halves[1]).astype(BF16)
    o_ref[...] = x + jnp.dot(a_scr[...], wo_ref[...], preferred_element_type=F32)


def _attn_prompt(x, k, v, sinks, bng, wq, qng2, cos2, sin2, wo, *, tm, seq):
    n = x.shape[0]
    tiles_per_seq = seq // tm
    blocks_per_tile = tm // WINDOW
    table_spec = pl.BlockSpec((tm, PAIR), lambda i: (lax.rem(i, tiles_per_seq), 0))
    prev_spec = pl.BlockSpec((WINDOW, D_KV), lambda i: (jnp.maximum(i * blocks_per_tile - 1, 0), 0))
    return pl.pallas_call(
        functools.partial(_attn_prompt_kernel, tm=tm, tiles_per_seq=tiles_per_seq),
        grid=(n // tm,),
        in_specs=[pl.BlockSpec(memory_space=pltpu.SMEM), _rows(tm, D_MODEL), _resident(bng.shape),
                  _resident(wq.shape), _resident(qng2.shape), table_spec, table_spec,
                  _rows(tm, D_KV), prev_spec, _rows(tm, D_KV), prev_spec, _resident(wo.shape)],
        out_specs=_rows(tm, D_MODEL),
        out_shape=jax.ShapeDtypeStruct((n, D_MODEL), F32),
        scratch_shapes=[pltpu.VMEM((tm, D_MODEL), F32), pltpu.VMEM((tm, D_MODEL), BF16)],
        compiler_params=_cparams(),
        name="attn_prompt",
    )(sinks, x, bng, wq, qng2, cos2, sin2, k, k, v, v, wo)


def _attn_sample_kernel(q_ref, ck_ref, cv_ref, kn_ref, vn_ref, sink_ref, o_ref, *, per_step, win):
    rowi = lax.broadcasted_iota(I32, (N_HEADS, D_KV), 0)
    lanei = lax.broadcasted_iota(I32, (N_HEADS, D_KV), 1)
    group = rowi >> 2
    own_block = group == (lanei >> 6)
    j = lax.broadcasted_iota(I32, (N_HEADS, win), 1)
    diff = win - j
    valid = jnp.logical_and(diff >= 0, diff < WINDOW)
    sink = sink_ref[...]
    scale = HEAD_DIM ** -0.5
    for b in range(per_step):
        qm = jnp.where(own_block, q_ref[b], 0.0)
        s = lax.dot_general(qm.astype(BF16), ck_ref[b].astype(BF16), (((1,), (1,)), ((), ())),
                            preferred_element_type=F32) * scale
        s = jnp.where(valid, s, -jnp.inf)
        s_new = jnp.sum(qm * kn_ref[b:b + 1, :], axis=-1, keepdims=True) * scale
        m = jnp.maximum(jnp.maximum(jnp.max(s, axis=-1, keepdims=True), s_new), sink)
        p = jnp.exp(s - m)
        p_new = jnp.exp(s_new - m)
        den = jnp.sum(p, axis=-1, keepdims=True) + p_new + jnp.exp(sink - m)
        o = jnp.dot(p.astype(BF16), cv_ref[b].astype(BF16), preferred_element_type=F32) + p_new * vn_ref[b:b + 1, :]
        o = o * (1.0 / den)
        res = o
        for sft in range(1, N_KV_HEADS):
            res = jnp.where(group == sft, pltpu.roll(o, D_KV - sft * HEAD_DIM, 1), res)
        o_ref[b] = res[:, :HEAD_DIM]


def _attn_sample(q4, ck, cv, kn, vn, sink_col, *, per_step):
    nb, win = ck.shape[0], ck.shape[1]
    return pl.pallas_call(
        functools.partial(_attn_sample_kernel, per_step=per_step, win=win),
        grid=(nb // per_step,),
        in_specs=[pl.BlockSpec((per_step, N_HEADS, D_KV), lambda i: (i, 0, 0)),
                  pl.BlockSpec((per_step, win, D_KV), lambda i: (i, 0, 0)),
                  pl.BlockSpec((per_step, win, D_KV), lambda i: (i, 0, 0)),
                  _rows(per_step, D_KV), _rows(per_step, D_KV), _resident(sink_col.shape)],
        out_specs=pl.BlockSpec((per_step, N_HEADS, HEAD_DIM), lambda i: (i, 0, 0)),
        out_shape=jax.ShapeDtypeStruct((nb, N_HEADS, HEAD_DIM), F32),
        compiler_params=_cparams(),
        name="attn_sample",
    )(q4, ck, cv, kn, vn, sink_col)


def _residual_proj_kernel(x_ref, a_ref, w_ref, o_ref):
    o_ref[...] = x_ref[...] + jnp.dot(a_ref[...].astype(BF16), w_ref[...], preferred_element_type=F32)


def _residual_proj(x, a, w):
    n = x.shape[0]
    return pl.pallas_call(
        _residual_proj_kernel,
        grid=(1,),
        in_specs=[_rows(n, D_MODEL), _rows(n, a.shape[1]), _resident(w.shape)],
        out_specs=_rows(n, D_MODEL),
        out_shape=jax.ShapeDtypeStruct((n, D_MODEL), F32),
        compiler_params=_cparams(),
        name="attn_out_sample",
    )(x, a, w)


def _rope_tables(pos):
    half = ROPE_DIM // 2
    inv_freq = jnp.power(jnp.float32(ROPE_THETA), -jnp.arange(half, dtype=F32) / half)
    ang = pos.astype(F32)[:, None] * inv_freq[None, :]
    cos, sin = jnp.cos(ang), jnp.sin(ang)
    n = pos.shape[0]
    rest = HEAD_DIM - ROPE_DIM
    cos_h = jnp.concatenate([cos, cos, jnp.ones((n, rest), F32)], axis=1)
    sin_h = jnp.concatenate([-sin, sin, jnp.zeros((n, rest), F32)], axis=1)
    return jnp.tile(cos_h, (1, 2)), jnp.tile(sin_h, (1, 2))


def kernel(x_prompt, x_sample, cache_k, cache_v, a_norm_g, a_w_in, a_ln_g, a_ln_b, a_w_s, a_b_s, a_w_out, kv_norm_g, w_k, w_v, k_norm_g, b_norm_g, w_q, q_norm_g, sinks, w_o, ffn_norm_g, w_router, b_router, w_gu, b_gu, w_down, b_down):
    bsz, seq, _ = x_prompt.shape
    dec_b, dec_seq, _ = x_sample.shape
    assert dec_seq == 1 and seq % TM_DENSE == 0 and a_norm_g.shape[0] == 1 and b_norm_g.shape[0] == 1
    win = cache_k.shape[1]
    n_p, n_s = bsz * seq, dec_b * dec_seq
    xp = x_prompt.reshape(n_p, D_MODEL)
    xs = x_sample.reshape(n_s, D_MODEL)

    ng = a_norm_g[0].reshape(1, D_MODEL)
    win_bf = a_w_in[0].astype(BF16)
    wout_bf = a_w_out[0].astype(BF16)
    lng = a_ln_g[0].reshape(1, D_GATE)
    lnb = a_ln_b[0].reshape(1, D_GATE)
    bias_full = jnp.repeat(a_b_s[0].T, GROUP_DIM_A, axis=1)
    diag_row = jnp.repeat(a_w_s[0][:, 0, 0], GROUP_DIM_A).reshape(1, D_GATE)
    (xp,) = _mixer_a(xp, ng, win_bf, lng, lnb, a_w_s[0], bias_full, wout_bf, tm=TM_DENSE, single_token_chunks=False)
    xs, v_rows = _mixer_a(xs, ng, win_bf, lng, lnb, diag_row, bias_full[0:1], wout_bf, tm=n_s,
                          single_token_chunks=True)
    xp, xs = _moe([xp, xs], ffn_norm_g[0], w_router[0], b_router[0], w_gu[0], b_gu[0], w_down[0], b_down[0])

    kvg = kv_norm_g.reshape(1, D_MODEL)
    bng = b_norm_g[0].reshape(1, D_MODEL)
    wk_bf, wv_bf, wq_bf, wo_bf = (w.astype(BF16) for w in (w_k, w_v, w_q[0], w_o[0]))
    kng2 = jnp.tile(k_norm_g, 2).reshape(1, PAIR)
    qng2 = jnp.tile(q_norm_g[0], 2).reshape(1, PAIR)
    cos_p, sin_p = _rope_tables(jnp.arange(seq, dtype=I32))
    cos_s, sin_s = _rope_tables(jnp.full((n_s,), PAST_LEN, I32))
    k_p, v_p = _kv_proj(xp, kvg, wk_bf, wv_bf, kng2, cos_p, sin_p, None, tm=TM_DENSE, table_tiles=seq // TM_DENSE)
    k_s, v_s, q_s = _kv_proj(xs, kvg, wk_bf, wv_bf, kng2, cos_s, sin_s, (bng, wq_bf, qng2), tm=n_s, table_tiles=1)
    xp = _attn_prompt(xp, k_p, v_p, sinks[0], bng, wq_bf, qng2, cos_p, sin_p, wo_bf, tm=TM_DENSE, seq=seq)
    q4 = jnp.tile(q_s.reshape(n_s, N_HEADS, HEAD_DIM), (1, 1, N_KV_HEADS))
    attn_s = _attn_sample(q4, cache_k.reshape(dec_b, win, D_KV), cache_v.reshape(dec_b, win, D_KV), k_s, v_s,
                          sinks[0].reshape(N_HEADS, 1), per_step=8)
    xs = _residual_proj(xs, attn_s.reshape(n_s, D_MODEL), wo_bf)
    xp, xs = _moe([xp, xs], ffn_norm_g[1], w_router[1], b_router[1], w_gu[1], b_gu[1], w_down[1], b_down[1])

    k_p4 = k_p.reshape(bsz, seq, N_KV_HEADS, HEAD_DIM)
    v_p4 = v_p.reshape(bsz, seq, N_KV_HEADS, HEAD_DIM)
    return (xp.reshape(bsz, seq, D_MODEL), xs.reshape(dec_b, dec_seq, D_MODEL),
            v_rows.reshape(1, dec_b, dec_seq, D_GATE), k_p4[:, -WINDOW:], v_p4[:, -WINDOW:],
            k_s.reshape(dec_b, dec_seq, N_KV_HEADS, HEAD_DIM), v_s.reshape(dec_b, dec_seq, N_KV_HEADS, HEAD_DIM))
```

```python
import functools

import jax
import jax.numpy as jnp
from jax import lax
from jax.experimental import pallas as pl
from jax.experimental.pallas import tpu as pltpu

F32, BF16, I32 = jnp.float32, jnp.bfloat16, jnp.int32

D_MODEL = 1024
CHUNK = 128
D_GATE = 2 * D_MODEL
N_GROUPS_A = 8
GROUP_DIM_A = D_GATE // N_GROUPS_A
HEAD_DIM = 64
N_HEADS = D_MODEL // HEAD_DIM
N_KV_HEADS = N_HEADS // 4
Q_PER_KV = N_HEADS // N_KV_HEADS
D_KV = N_KV_HEADS * HEAD_DIM
WINDOW = 128
ROPE_DIM = HEAD_DIM // 4
ROPE_THETA = 500000.0
N_EXPERTS = 32
TOP_K = 4
D_EXPERT = D_MODEL
SWIGLU_LIMIT = 7.0
SWIGLU_ALPHA = 1.702
EPS = 1e-6
PAST_LEN = 8192

V7X_LANES = 128
V7X_VMEM_BYTES = 64 * 1024 * 1024
VMEM_LIMIT_BYTES = V7X_VMEM_BYTES - 8 * 1024 * 1024

TM_DENSE = 512
TM_COMBINE = 256
MOE_ROWS = 512
TM_DISPATCH = 256
FILL_BITS = MOE_ROWS.bit_length() - 1
TOKEN_SUBLANES = D_MODEL // V7X_LANES
SLAB = V7X_LANES
PAIR = 2 * HEAD_DIM
INV_SQRT2 = 0.7071067811865476


def _cparams():
    return pltpu.CompilerParams(dimension_semantics=("arbitrary",), vmem_limit_bytes=VMEM_LIMIT_BYTES)


def _resident(shape):
    zeros = (0,) * len(shape)
    return pl.BlockSpec(shape, lambda i, *_: zeros, pipeline_mode=pl.Buffered(1))


def _rows(tm, width, offset=0):
    return pl.BlockSpec((tm, width), lambda i, *_: (i + offset, 0))


def _rms(x, g):
    return x * lax.rsqrt(jnp.mean(x * x, axis=-1, keepdims=True) + EPS) * g


def _pair_norm_rope(x2, g2, cos2, sin2):
    lane = lax.broadcasted_iota(I32, x2.shape, 1)
    first = lane < HEAD_DIM
    sq = x2 * x2
    s0 = jnp.sum(jnp.where(first, sq, 0.0), axis=-1, keepdims=True)
    s1 = jnp.sum(jnp.where(first, 0.0, sq), axis=-1, keepdims=True)
    ms = jnp.where(first, s0, s1) * (1.0 / HEAD_DIM)
    y = x2 * lax.rsqrt(ms + EPS) * g2
    half = ROPE_DIM // 2
    up = pltpu.roll(y, PAIR - half, 1)
    dn = pltpu.roll(y, half, 1)
    partner = jnp.where((lane & (HEAD_DIM - 1)) < half, up, dn)
    return y * cos2 + partner * sin2


def _mixer_a_kernel(x_ref, ng_ref, win_ref, lng_ref, lnb_ref, mix_ref, bias_ref, wout_ref, *refs,
                    tm, single_token_chunks):
    if single_token_chunks:
        o_ref, vout_ref, u_scr, v_scr, p_scr = refs
    else:
        o_ref, u_scr, v_scr, p_scr = refs
    x = x_ref[...]
    h = _rms(x, ng_ref[...]).astype(BF16)
    nc = 512
    for c in range(2 * D_GATE // nc):
        z = jnp.dot(h, win_ref[:, c * nc:(c + 1) * nc], preferred_element_type=F32)
        z = 0.5 * z * (1.0 + lax.erf(z * INV_SQRT2))
        if c < D_GATE // nc:
            u_scr[:, c * nc:(c + 1) * nc] = z
        else:
            v_scr[:, (c - D_GATE // nc) * nc:(c - D_GATE // nc + 1) * nc] = z
    v = v_scr[...]
    vc = v - jnp.mean(v, axis=-1, keepdims=True)
    vn = vc * lax.rsqrt(jnp.mean(vc * vc, axis=-1, keepdims=True) + EPS) * lng_ref[...] + lnb_ref[...]
    if single_token_chunks:
        vout_ref[...] = vn
        p_scr[...] = (u_scr[...] * (vn * mix_ref[...] + bias_ref[...])).astype(BF16)
    else:
        v_scr[...] = vn
        row = lax.broadcasted_iota(I32, (CHUNK, CHUNK), 0)
        col = lax.broadcasted_iota(I32, (CHUNK, CHUNK), 1)
        for g in range(N_GROUPS_A):
            wc = jnp.where(row >= col, mix_ref[g], 0.0).astype(BF16)
            cols = slice(g * GROUP_DIM_A, (g + 1) * GROUP_DIM_A)
            for c in range(tm // CHUNK):
                rows = slice(c * CHUNK, (c + 1) * CHUNK)
                mixed = jnp.dot(wc, v_scr[rows, cols].astype(BF16), preferred_element_type=F32)
                mixed = mixed + bias_ref[:, cols]
                p_scr[rows, cols] = (u_scr[rows, cols] * mixed).astype(BF16)
    o_ref[...] = x + jnp.dot(p_scr[...], wout_ref[...], preferred_element_type=F32)


def _mixer_a(x, ng, win, lng, lnb, mix, bias, wout, *, tm, single_token_chunks):
    n = x.shape[0]
    out_shape = [jax.ShapeDtypeStruct((n, D_MODEL), F32)]
    out_specs = [_rows(tm, D_MODEL)]
    if single_token_chunks:
        out_shape.append(jax.ShapeDtypeStruct((n, D_GATE), F32))
        out_specs.append(_rows(tm, D_GATE))
    return pl.pallas_call(
        functools.partial(_mixer_a_kernel, tm=tm, single_token_chunks=single_token_chunks),
        grid=(n // tm,),
        in_specs=[_rows(tm, D_MODEL), _resident(ng.shape), _resident(win.shape), _resident(lng.shape),
                  _resident(lnb.shape), _resident(mix.shape), _resident(bias.shape), _resident(wout.shape)],
        out_specs=out_specs,
        out_shape=out_shape,
        scratch_shapes=[pltpu.VMEM((tm, D_GATE), F32), pltpu.VMEM((tm, D_GATE), F32),
                        pltpu.VMEM((tm, D_GATE), BF16)],
        compiler_params=_cparams(),
        name="mixer_a_sample" if single_token_chunks else "mixer_a_prompt",
    )(x, ng, win, lng, lnb, mix, bias, wout)


def _route_rows(x, g, whi, wlo, bias, run_scr):
    tm = x.shape[0]
    h = _rms(x, g)
    hh = h.astype(BF16)
    hl = (h - hh.astype(F32)).astype(BF16)
    logits = (jnp.dot(hh, whi, preferred_element_type=F32) + jnp.dot(hh, wlo, preferred_element_type=F32)
              + jnp.dot(hl, whi, preferred_element_type=F32) + bias)
    e = lax.broadcasted_iota(I32, (tm, N_EXPERTS), 1).astype(F32)
    hot, vals, idxs = [], [], []
    for _ in range(TOP_K):
        m = jnp.max(logits, axis=-1, keepdims=True)
        idx = jnp.min(jnp.where(logits == m, e, float(N_EXPERTS)), axis=-1, keepdims=True)
        oh = e == idx
        hot.append(oh)
        vals.append(m)
        idxs.append(idx)
        logits = jnp.where(oh, -jnp.inf, logits)
    ex = [jnp.exp(v - vals[0]) for v in vals]
    den = ex[0] + ex[1] + ex[2] + ex[3]
    picked = jnp.zeros((tm, N_EXPERTS), F32)
    for oh in hot:
        picked = picked + jnp.where(oh, 1.0, 0.0)
    r = lax.broadcasted_iota(I32, (tm, tm), 0)
    c = lax.broadcasted_iota(I32, (tm, tm), 1)
    before = jnp.where(c < r, 1.0, 0.0).astype(BF16)
    run = run_scr[0:1, 0:N_EXPERTS]
    pos = jnp.dot(before, picked.astype(BF16), preferred_element_type=F32) + run
    run_scr[0:1, 0:N_EXPERTS] = run + jnp.sum(picked, axis=0, keepdims=True)
    lane = lax.broadcasted_iota(I32, (tm, SLAB), 1)
    slab = jnp.zeros((tm, SLAB), F32)
    for k in range(TOP_K):
        pos_k = jnp.sum(jnp.where(hot[k], pos, 0.0), axis=-1, keepdims=True)
        slab = jnp.where(lane == k, idxs[k], slab)
        slab = jnp.where(lane == TOP_K + k, pos_k, slab)
        slab = jnp.where(lane == 2 * TOP_K + k, ex[k] / den, slab)
    return h, slab


def _router_kernel(xp_ref, xs_ref, g_ref, whi_ref, wlo_ref, b_ref, h_ref, slabp_ref, slabs_ref, cnt_ref, run_scr,
                   *, n_tiles, n_s):
    i = pl.program_id(0)

    @pl.when(i == 0)
    def _():
        run_scr[...] = jnp.zeros_like(run_scr)

    @pl.when(i < n_tiles)
    def _():
        h, slab = _route_rows(xp_ref[...], g_ref[...], whi_ref[...], wlo_ref[...], b_ref[...], run_scr)
        _store_token_tiles(h_ref, h, h.shape[0])
        slabp_ref[...] = slab

    @pl.when(i == n_tiles)
    def _():
        h, slab = _route_rows(xs_ref[...], g_ref[...], whi_ref[...], wlo_ref[...], b_ref[...], run_scr)
        h_ref[n_s * TOKEN_SUBLANES:, :] = jnp.zeros((h_ref.shape[0] - n_s * TOKEN_SUBLANES, V7X_LANES), F32)
        _store_token_tiles(h_ref, h, n_s)
        slabs_ref[...] = slab

    cnt_ref[...] = run_scr[...]


def _store_token_tiles(ref, x, n):
    for s in range(TOKEN_SUBLANES):
        ref[pl.ds(s, n, stride=TOKEN_SUBLANES), :] = x[:, s * V7X_LANES:(s + 1) * V7X_LANES]


def _load_token_tiles(ref, n):
    return jnp.concatenate([ref[pl.ds(s, n, stride=TOKEN_SUBLANES), :] for s in range(TOKEN_SUBLANES)], axis=1)


def _router(xp, xs, g, whi, wlo, b, *, tm):
    n_p, n_s = xp.shape[0], xs.shape[0]
    n_tiles = n_p // tm
    last = n_tiles - 1
    return pl.pallas_call(
        functools.partial(_router_kernel, n_tiles=n_tiles, n_s=n_s),
        grid=(n_tiles + 1,),
        in_specs=[pl.BlockSpec((tm, D_MODEL), lambda i: (jnp.minimum(i, last), 0)), _resident(xs.shape),
                  _resident(g.shape), _resident(whi.shape), _resident(wlo.shape), _resident(b.shape)],
        out_specs=[_rows(tm * TOKEN_SUBLANES, V7X_LANES),
                   pl.BlockSpec((tm, SLAB), lambda i: (jnp.minimum(i, last), 0)),
                   pl.BlockSpec((n_s, SLAB), lambda i: (0, 0)), pl.BlockSpec((8, V7X_LANES), lambda i: (0, 0))],
        out_shape=[jax.ShapeDtypeStruct(((n_p + tm) * TOKEN_SUBLANES, V7X_LANES), F32),
                   jax.ShapeDtypeStruct((n_p, SLAB), F32),
                   jax.ShapeDtypeStruct((n_s, SLAB), F32), jax.ShapeDtypeStruct((8, V7X_LANES), F32)],
        scratch_shapes=[pltpu.VMEM((8, V7X_LANES), F32)],
        compiler_params=_cparams(),
        name="moe_router",
    )(xp, xs, g, whi, wlo, b)


def _dispatch_kernel(fill_start_ref, fill_len_ref, dest_ref, h_ref, xs_ref, zbuf, sem, fill_sem,
                     *, tm, n_tiles, n_last, rows, n_blocks):
    i = pl.program_id(0)
    ts = TOKEN_SUBLANES

    def tile_copy(r, k):
        d = dest_ref[0, 0, k * tm + r]
        src = h_ref.at[pl.ds(pl.multiple_of(r * ts, ts), ts)]
        dst = xs_ref.at[pl.ds(pl.multiple_of(d * ts, ts), ts)]
        return pltpu.make_async_copy(src, dst, sem)

    def scatter(n):
        def body(r, carry):
            for k in range(TOP_K):
                tile_copy(r, k).start(priority=k % 2)
            return carry
        lax.fori_loop(0, n, body, 0, unroll=4)
        for k in range(TOP_K):
            pltpu.make_async_copy(h_ref.at[pl.ds(0, n * ts)], xs_ref.at[pl.ds(0, n * ts)], sem).wait()

    def fill_copy(e, bit):
        size = 1 << bit
        length = fill_len_ref[e]
        done = length - (length & (2 * size - 1))
        off = pl.multiple_of((fill_start_ref[e] + done) * ts, ts)
        return (length & size) != 0, pltpu.make_async_copy(zbuf.at[pl.ds(0, size * ts)],
                                                            xs_ref.at[pl.ds(off, size * ts)], fill_sem)

    def tail_copy(blk):
        off = pl.multiple_of(blk * (rows * ts), rows * ts)
        return pltpu.make_async_copy(zbuf, xs_ref.at[pl.ds(off, rows * ts)], fill_sem)

    def for_each_fill(act):
        for e in range(N_EXPERTS):
            for bit in range(FILL_BITS):
                go, cp = fill_copy(e, bit)

                @pl.when(go)
                def _():
                    act(cp)

        def body(blk, carry):
            act(tail_copy(blk))
            return carry
        lax.fori_loop(fill_start_ref[N_EXPERTS], n_blocks, body, 0)

    @pl.when(i == 0)
    def _():
        zbuf[...] = jnp.zeros_like(zbuf)
        for_each_fill(lambda cp: cp.start())

    @pl.when(i < n_tiles - 1)
    def _():
        scatter(tm)

    @pl.when(i == n_tiles - 1)
    def _():
        scatter(n_last)
        for_each_fill(lambda cp: cp.wait())


def _dispatch(fill_start, fill_len, dest3, h_tiles, n_all, n_blocks, *, tm, rows):
    n_tiles = dest3.shape[0]
    n_last = n_all - (n_tiles - 1) * tm
    ts = TOKEN_SUBLANES
    n_slots = n_blocks * rows
    grid_spec = pltpu.PrefetchScalarGridSpec(
        num_scalar_prefetch=2,
        grid=(n_tiles,),
        in_specs=[pl.BlockSpec((1, 1, TOP_K * tm), lambda i, *_: (i, 0, 0), memory_space=pltpu.SMEM),
                  _rows(tm * ts, V7X_LANES)],
        out_specs=pl.BlockSpec(memory_space=pl.ANY),
        scratch_shapes=[pltpu.VMEM((rows * ts, V7X_LANES), F32),
                        pltpu.SemaphoreType.DMA, pltpu.SemaphoreType.DMA],
    )
    return pl.pallas_call(
        functools.partial(_dispatch_kernel, tm=tm, n_tiles=n_tiles, n_last=n_last, rows=rows, n_blocks=n_blocks),
        grid_spec=grid_spec,
        out_shape=jax.ShapeDtypeStruct((n_slots * ts, V7X_LANES), F32),
        compiler_params=_cparams(),
        name="moe_dispatch",
    )(fill_start, fill_len, dest3, h_tiles)


def _expert_kernel(be_ref, nact_ref, x_ref, wgu_ref, bgu_ref, wdn_ref, bdn_ref, o_ref, wgu_bf, wdn_bf, act_scr,
                   *, rows):
    b = pl.program_id(0)
    nact = nact_ref[0]

    @pl.when(b < nact)
    def _():
        changed = jnp.logical_or(b == 0, be_ref[b] != be_ref[jnp.maximum(b - 1, 0)])

        @pl.when(changed)
        def _():
            step = 256
            for c in range(D_MODEL // step):
                wgu_bf[c * step:(c + 1) * step, :] = wgu_ref[c * step:(c + 1) * step, :].astype(BF16)
                wdn_bf[c * step:(c + 1) * step, :] = wdn_ref[c * step:(c + 1) * step, :].astype(BF16)

        xb = _load_token_tiles(x_ref, rows).astype(BF16)
        nc = 512
        for c in range(D_EXPERT // nc):
            gcols = slice(c * nc, (c + 1) * nc)
            ucols = slice(D_EXPERT + c * nc, D_EXPERT + (c + 1) * nc)
            gate = jnp.dot(xb, wgu_bf[:, gcols], preferred_element_type=F32) + bgu_ref[:, gcols]
            up = jnp.dot(xb, wgu_bf[:, ucols], preferred_element_type=F32) + bgu_ref[:, ucols]
            gate = jnp.minimum(gate, SWIGLU_LIMIT)
            up = jnp.clip(up, -SWIGLU_LIMIT, SWIGLU_LIMIT)
            act = (up + 1.0) * gate * (1.0 / (1.0 + jnp.exp(-SWIGLU_ALPHA * gate)))
            act_scr[:, gcols] = act.astype(BF16)
        out = jnp.dot(act_scr[...], wdn_bf[...], preferred_element_type=F32) + bdn_ref[...]
        _store_token_tiles(o_ref, out, rows)

    @pl.when(b >= nact)
    def _():
        o_ref[...] = jnp.zeros_like(o_ref)


def _experts(block_e, n_active, xs_tiles, wgu, bgu, wdn, bdn, *, rows, layer):
    ts = TOKEN_SUBLANES
    nb = xs_tiles.shape[0] // (rows * ts)
    grid_spec = pltpu.PrefetchScalarGridSpec(
        num_scalar_prefetch=2,
        grid=(nb,),
        in_specs=[
            pl.BlockSpec((rows * ts, V7X_LANES), lambda b, be, na: (jnp.minimum(b, na[0] - 1), 0)),
            pl.BlockSpec((None, None, D_MODEL, 2 * D_EXPERT), lambda b, be, na: (layer, be[b], 0, 0)),
            pl.BlockSpec((None, None, 1, 2 * D_EXPERT), lambda b, be, na: (layer, be[b], 0, 0)),
            pl.BlockSpec((None, None, D_EXPERT, D_MODEL), lambda b, be, na: (layer, be[b], 0, 0)),
            pl.BlockSpec((None, None, 1, D_MODEL), lambda b, be, na: (layer, be[b], 0, 0)),
        ],
        out_specs=pl.BlockSpec((rows * ts, V7X_LANES), lambda b, be, na: (b, 0)),
        scratch_shapes=[pltpu.VMEM((D_MODEL, 2 * D_EXPERT), BF16), pltpu.VMEM((D_EXPERT, D_MODEL), BF16),
                        pltpu.VMEM((rows, D_EXPERT), BF16)],
    )
    return pl.pallas_call(
        functools.partial(_expert_kernel, rows=rows),
        grid_spec=grid_spec,
        out_shape=jax.ShapeDtypeStruct(xs_tiles.shape, F32),
        compiler_params=_cparams(),
        name="moe_experts",
    )(block_e, n_active, xs_tiles, wgu, bgu, wdn, bdn)


def _combine_kernel(dest_first_ref, dest_next_ref, x_ref, slab_ref, ebuf_ref, o_ref, rbuf, sem, *, tm, n_tiles):
    i = pl.program_id(0)
    slot = lax.rem(i, 2)
    ts = TOKEN_SUBLANES

    def gather(dest_ref, s):
        def body(r, carry):
            for k in range(TOP_K):
                d = dest_ref[0, 0, k * tm + r]
                src = ebuf_ref.at[pl.ds(pl.multiple_of(d * ts, ts), ts)]
                dst = rbuf.at[s, k, pl.ds(pl.multiple_of(r * ts, ts), ts)]
                pltpu.make_async_copy(src, dst, sem.at[s]).start(priority=k % 2)
            return carry
        lax.fori_loop(0, tm, body, 0, unroll=4)

    @pl.when(i == 0)
    def _():
        gather(dest_first_ref, 0)

    @pl.when(i + 1 < n_tiles)
    def _():
        gather(dest_next_ref, 1 - slot)

    for k in range(TOP_K):
        pltpu.make_async_copy(ebuf_ref.at[pl.ds(0, tm * ts)], rbuf.at[slot, k], sem.at[slot]).wait()
    slab = slab_ref[...]
    gates = [slab[:, 2 * TOP_K + k:2 * TOP_K + k + 1] for k in range(TOP_K)]
    for s in range(ts):
        cols = slice(s * V7X_LANES, (s + 1) * V7X_LANES)
        acc = x_ref[:, cols]
        for k in range(TOP_K):
            acc = acc + gates[k] * rbuf[slot, k, pl.ds(s, tm, stride=ts), :]
        o_ref[:, cols] = acc


def _combine(x, slab, dest3, ebuf, *, tm):
    n = x.shape[0]
    n_tiles = n // tm
    return pl.pallas_call(
        functools.partial(_combine_kernel, tm=tm, n_tiles=n_tiles),
        grid=(n_tiles,),
        in_specs=[
            pl.BlockSpec((1, 1, TOP_K * tm), lambda i: (0, 0, 0), memory_space=pltpu.SMEM),
            pl.BlockSpec((1, 1, TOP_K * tm), lambda i: (jnp.minimum(i + 1, n_tiles - 1), 0, 0),
                         memory_space=pltpu.SMEM),
            _rows(tm, D_MODEL), _rows(tm, SLAB), pl.BlockSpec(memory_space=pl.ANY),
        ],
        out_specs=_rows(tm, D_MODEL),
        out_shape=jax.ShapeDtypeStruct((n, D_MODEL), F32),
        scratch_shapes=[pltpu.VMEM((2, TOP_K, tm * TOKEN_SUBLANES, V7X_LANES), F32),
                        pltpu.SemaphoreType.DMA((2,))],
        compiler_params=_cparams(),
        name="moe_combine",
    )(dest3, dest3, x, slab, ebuf)


def _moe(xs, g, w_router, b_router, wgu, bgu, wdn, bdn, layer):
    n_all = xs[0].shape[0] + xs[1].shape[0]
    whi = w_router.astype(BF16)
    wlo = (w_router - whi.astype(F32)).astype(BF16)
    h_all, slab_p, slab_s, cnt = _router(xs[0], xs[1], g.reshape(1, D_MODEL), whi, wlo,
                                         b_router.reshape(1, N_EXPERTS), tm=TM_DENSE)
    slabs = [slab_p, slab_s]
    slab_all = jnp.concatenate([s[:, :2 * TOP_K] for s in slabs], axis=0)
    idx = slab_all[:, :TOP_K].astype(I32)
    pos = slab_all[:, TOP_K:].astype(I32)
    counts = cnt[0, :N_EXPERTS].astype(I32)
    rows = MOE_ROWS
    nb = (n_all * TOP_K + N_EXPERTS * (rows - 1)) // rows
    padded = (counts + rows - 1) // rows * rows
    pad_end = jnp.cumsum(padded)
    pad_start = pad_end - padded
    dest = pad_start[idx] + pos
    n_active = (pad_end[-1] // rows).astype(I32)
    blk = jnp.arange(nb, dtype=I32)
    block_e = jnp.sum((pad_end[None, :] <= (blk * rows)[:, None]).astype(I32), axis=1)
    block_e = jnp.minimum(block_e, N_EXPERTS - 1)
    block_e = jnp.where(blk < n_active, block_e, block_e[jnp.maximum(n_active - 1, 0)])

    def tile_table(d, tm):
        n_tiles = -(-d.shape[0] // tm)
        d = jnp.pad(d, ((0, n_tiles * tm - d.shape[0]), (0, 0)))
        return d.reshape(n_tiles, tm, TOP_K).transpose(0, 2, 1).reshape(n_tiles, 1, TOP_K * tm)

    fill_start = jnp.concatenate([pad_start + counts, n_active.reshape(1)])
    xs_tiles = _dispatch(fill_start, padded - counts, tile_table(dest, TM_DISPATCH), h_all,
                         n_all, nb, tm=TM_DISPATCH, rows=rows)
    n_l = wgu.shape[0]
    ebuf = _experts(block_e, n_active.reshape(1), xs_tiles,
                    wgu, bgu.reshape(n_l, N_EXPERTS, 1, 2 * D_EXPERT), wdn, bdn.reshape(n_l, N_EXPERTS, 1, D_MODEL),
                    rows=rows, layer=layer)
    outs, off = [], 0
    for x, slab in zip(xs, slabs):
        n = x.shape[0]
        tm = min(TM_COMBINE, n)
        outs.append(_combine(x, slab, tile_table(dest[off:off + n], tm), ebuf, tm=tm))
        off += n
    return outs


def _kv_kernel(x_ref, kvg_ref, wk_ref, wv_ref, kng_ref, cos_ref, sin_ref, *refs, with_q):
    if with_q:
        bng_ref, wq_ref, qng_ref, k_ref, v_ref, q_ref = refs
    else:
        k_ref, v_ref = refs
    x = x_ref[...]
    h = _rms(x, kvg_ref[...]).astype(BF16)
    k = jnp.dot(h, wk_ref[...], preferred_element_type=F32)
    v_ref[...] = jnp.dot(h, wv_ref[...], preferred_element_type=F32)
    cos2, sin2 = cos_ref[...], sin_ref[...]
    for p in range(D_KV // PAIR):
        cols = slice(p * PAIR, (p + 1) * PAIR)
        k_ref[:, cols] = _pair_norm_rope(k[:, cols], kng_ref[...], cos2, sin2)
    if with_q:
        hq = _rms(x, bng_ref[...]).astype(BF16)
        q = jnp.dot(hq, wq_ref[...], preferred_element_type=F32)
        for p in range(D_MODEL // PAIR):
            cols = slice(p * PAIR, (p + 1) * PAIR)
            q_ref[:, cols] = _pair_norm_rope(q[:, cols], qng_ref[...], cos2, sin2)


def _kv_proj(x, kvg, wk, wv, kng2, cos2, sin2, q_args, *, tm, table_tiles):
    n = x.shape[0]
    with_q = q_args is not None
    table_spec = pl.BlockSpec((tm, PAIR), lambda i: (lax.rem(i, table_tiles), 0))
    in_specs = [_rows(tm, D_MODEL), _resident(kvg.shape), _resident(wk.shape), _resident(wv.shape),
                _resident(kng2.shape), table_spec, table_spec]
    args = [x, kvg, wk, wv, kng2, cos2, sin2]
    out_specs = [_rows(tm, D_KV), _rows(tm, D_KV)]
    out_shape = [jax.ShapeDtypeStruct((n, D_KV), F32), jax.ShapeDtypeStruct((n, D_KV), F32)]
    if with_q:
        in_specs += [_resident(a.shape) for a in q_args]
        args += list(q_args)
        out_specs.append(_rows(tm, D_MODEL))
        out_shape.append(jax.ShapeDtypeStruct((n, D_MODEL), F32))
    return pl.pallas_call(
        functools.partial(_kv_kernel, with_q=with_q),
        grid=(n // tm,),
        in_specs=in_specs, out_specs=out_specs, out_shape=out_shape,
        compiler_params=_cparams(),
        name="kvq_proj_sample" if with_q else "kv_proj_prompt",
    )(*args)


def _attn_prompt_kernel(sinks_ref, x_ref, bng_ref, wq_ref, qng_ref, cos_ref, sin_ref, kc_ref, kp_ref, vc_ref, vp_ref,
                        wo_ref, o_ref, q_scr, a_scr, *, tm, tiles_per_seq):
    i = pl.program_id(0)
    x = x_ref[...]
    hq = _rms(x, bng_ref[...]).astype(BF16)
    q = jnp.dot(hq, wq_ref[...], preferred_element_type=F32)
    cos2, sin2 = cos_ref[...], sin_ref[...]
    for p in range(D_MODEL // PAIR):
        cols = slice(p * PAIR, (p + 1) * PAIR)
        q_scr[:, cols] = _pair_norm_rope(q[:, cols], qng_ref[...], cos2, sin2)

    nq = Q_PER_KV * WINDOW
    qi = lax.broadcasted_iota(I32, (nq, 2 * WINDOW), 0) & (WINDOW - 1)
    kj = lax.broadcasted_iota(I32, (nq, 2 * WINDOW), 1)
    diff = qi + WINDOW - kj
    band = jnp.logical_and(diff >= 0, diff < WINDOW)
    lane = lax.broadcasted_iota(I32, (WINDOW, PAIR), 1)
    first_half = lane < HEAD_DIM
    first_kj = jnp.where(lax.rem(i, tiles_per_seq) == 0, WINDOW, 0)
    scale = HEAD_DIM ** -0.5

    for blk in range(tm // WINDOW):
        rows = slice(blk * WINDOW, (blk + 1) * WINDOW)
        if blk == 0:
            kk = jnp.concatenate([kp_ref[...], kc_ref[rows, :]], axis=0).astype(BF16)
            vv = jnp.concatenate([vp_ref[...], vc_ref[rows, :]], axis=0).astype(BF16)
            valid = jnp.logical_and(band, kj >= first_kj)
        else:
            win = slice((blk - 1) * WINDOW, (blk + 1) * WINDOW)
            kk = kc_ref[win, :].astype(BF16)
            vv = vc_ref[win, :].astype(BF16)
            valid = band
        for kvh in range(N_KV_HEADS):
            kcols = slice((kvh // 2) * PAIR, (kvh // 2 + 1) * PAIR)
            k_half = kvh % 2
            pieces, sink_rows = [], []
            for g in range(Q_PER_KV):
                head = kvh * Q_PER_KV + g
                qp = q_scr[rows, (head // 2) * PAIR:(head // 2 + 1) * PAIR]
                qm = jnp.where(first_half if head % 2 == 0 else jnp.logical_not(first_half), qp, 0.0)
                if head % 2 != k_half:
                    qm = pltpu.roll(qm, HEAD_DIM, 1)
                pieces.append(qm.astype(BF16))
                sink_rows.append(jnp.full((WINDOW, 1), sinks_ref[head], F32))
            q4 = jnp.concatenate(pieces, axis=0)
            sink = jnp.concatenate(sink_rows, axis=0)
            s = lax.dot_general(q4, kk[:, kcols], (((1,), (1,)), ((), ())), preferred_element_type=F32) * scale
            s = jnp.where(valid, s, -jnp.inf)
            m = jnp.maximum(jnp.max(s, axis=-1, keepdims=True), sink)
            p = jnp.exp(s - m)
            den = jnp.sum(p, axis=-1, keepdims=True) + jnp.exp(sink - m)
            o4 = jnp.dot(p.astype(BF16), vv[:, kcols], preferred_element_type=F32) * (1.0 / den)
            for pq in range(Q_PER_KV // 2):
                halves = []
                for hq_half in range(2):
                    g = 2 * pq + hq_half
                    og = o4[g * WINDOW:(g + 1) * WINDOW, :]
                    if hq_half != k_half:
                        og = pltpu.roll(og, HEAD_DIM, 1)
                    halves.append(og)
                pair = (kvh * Q_PER_KV) // 2 + pq
                a_scr[rows, pair * PAIR:(pair + 1) * PAIR] = jnp.where(first_half, halves[0], halves[1]).astype(BF16)
    o_ref[...] = x + jnp.dot(a_scr[...], wo_ref[...], preferred_element_type=F32)


def _attn_prompt(x, k, v, sinks, bng, wq, qng2, cos2, sin2, wo, *, tm, seq):
    n = x.shape[0]
    tiles_per_seq = seq // tm
    blocks_per_tile = tm // WINDOW
    table_spec = pl.BlockSpec((tm, PAIR), lambda i: (lax.rem(i, tiles_per_seq), 0))
    prev_spec = pl.BlockSpec((WINDOW, D_KV), lambda i: (jnp.maximum(i * blocks_per_tile - 1, 0), 0))
    return pl.pallas_call(
        functools.partial(_attn_prompt_kernel, tm=tm, tiles_per_seq=tiles_per_seq),
        grid=(n // tm,),
        in_specs=[pl.BlockSpec(memory_space=pltpu.SMEM), _rows(tm, D_MODEL), _resident(bng.shape),
                  _resident(wq.shape), _resident(qng2.shape), table_spec, table_spec,
                  _rows(tm, D_KV), prev_spec, _rows(tm, D_KV), prev_spec, _resident(wo.shape)],
        out_specs=_rows(tm, D_MODEL),
        out_shape=jax.ShapeDtypeStruct((n, D_MODEL), F32),
        scratch_shapes=[pltpu.VMEM((tm, D_MODEL), F32), pltpu.VMEM((tm, D_MODEL), BF16)],
        compiler_params=_cparams(),
        name="attn_prompt",
    )(sinks, x, bng, wq, qng2, cos2, sin2, k, k, v, v, wo)


def _attn_sample_kernel(q_ref, ck_ref, cv_ref, kn_ref, vn_ref, sink_ref, o_ref, *, per_step, win):
    rowi = lax.broadcasted_iota(I32, (N_HEADS, D_KV), 0)
    lanei = lax.broadcasted_iota(I32, (N_HEADS, D_KV), 1)
    group = rowi >> 2
    own_block = group == (lanei >> 6)
    j = lax.broadcasted_iota(I32, (N_HEADS, win), 1)
    diff = win - j
    valid = jnp.logical_and(diff >= 0, diff < WINDOW)
    sink = sink_ref[...]
    scale = HEAD_DIM ** -0.5
    for b in range(per_step):
        qm = jnp.where(own_block, q_ref[b], 0.0)
        s = lax.dot_general(qm.astype(BF16), ck_ref[b].astype(BF16), (((1,), (1,)), ((), ())),
                            preferred_element_type=F32) * scale
        s = jnp.where(valid, s, -jnp.inf)
        s_new = jnp.sum(qm * kn_ref[b:b + 1, :], axis=-1, keepdims=True) * scale
        m = jnp.maximum(jnp.maximum(jnp.max(s, axis=-1, keepdims=True), s_new), sink)
        p = jnp.exp(s - m)
        p_new = jnp.exp(s_new - m)
        den = jnp.sum(p, axis=-1, keepdims=True) + p_new + jnp.exp(sink - m)
        o = jnp.dot(p.astype(BF16), cv_ref[b].astype(BF16), preferred_element_type=F32) + p_new * vn_ref[b:b + 1, :]
        o = o * (1.0 / den)
        res = o
        for sft in range(1, N_KV_HEADS):
            res = jnp.where(group == sft, pltpu.roll(o, D_KV - sft * HEAD_DIM, 1), res)
        o_ref[b] = res[:, :HEAD_DIM]


def _attn_sample(q4, ck, cv, kn, vn, sink_col, *, per_step):
    nb, win = ck.shape[0], ck.shape[1]
    return pl.pallas_call(
        functools.partial(_attn_sample_kernel, per_step=per_step, win=win),
        grid=(nb // per_step,),
        in_specs=[pl.BlockSpec((per_step, N_HEADS, D_KV), lambda i: (i, 0, 0)),
                  pl.BlockSpec((per_step, win, D_KV), lambda i: (i, 0, 0)),
                  pl.BlockSpec((per_step, win, D_KV), lambda i: (i, 0, 0)),
                  _rows(per_step, D_KV), _rows(per_step, D_KV), _resident(sink_col.shape)],
        out_specs=pl.BlockSpec((per_step, N_HEADS, HEAD_DIM), lambda i: (i, 0, 0)),
        out_shape=jax.ShapeDtypeStruct((nb, N_HEADS, HEAD_DIM), F32),
        compiler_params=_cparams(),
        name="attn_sample",
    )(q4, ck, cv, kn, vn, sink_col)


def _residual_proj_kernel(x_ref, a_ref, w_ref, o_ref):
    o_ref[...] = x_ref[...] + jnp.dot(a_ref[...].astype(BF16), w_ref[...], preferred_element_type=F32)


def _residual_proj(x, a, w):
    n = x.shape[0]
    return pl.pallas_call(
        _residual_proj_kernel,
        grid=(1,),
        in_specs=[_rows(n, D_MODEL), _rows(n, a.shape[1]), _resident(w.shape)],
        out_specs=_rows(n, D_MODEL),
        out_shape=jax.ShapeDtypeStruct((n, D_MODEL), F32),
        compiler_params=_cparams(),
        name="attn_out_sample",
    )(x, a, w)


def _rope_tables(pos):
    half = ROPE_DIM // 2
    inv_freq = jnp.power(jnp.float32(ROPE_THETA), -jnp.arange(half, dtype=F32) / half)
    ang = pos.astype(F32)[:, None] * inv_freq[None, :]
    cos, sin = jnp.cos(ang), jnp.sin(ang)
    n = pos.shape[0]
    rest = HEAD_DIM - ROPE_DIM
    cos_h = jnp.concatenate([cos, cos, jnp.ones((n, rest), F32)], axis=1)
    sin_h = jnp.concatenate([-sin, sin, jnp.zeros((n, rest), F32)], axis=1)
    return jnp.tile(cos_h, (1, 2)), jnp.tile(sin_h, (1, 2))


def kernel(x_prompt, x_sample, cache_k, cache_v, a_norm_g, a_w_in, a_ln_g, a_ln_b, a_w_s, a_b_s, a_w_out, kv_norm_g, w_k, w_v, k_norm_g, b_norm_g, w_q, q_norm_g, sinks, w_o, ffn_norm_g, w_router, b_router, w_gu, b_gu, w_down, b_down):
    bsz, seq, _ = x_prompt.shape
    dec_b, dec_seq, _ = x_sample.shape
    assert dec_seq == 1 and seq % TM_DENSE == 0 and a_norm_g.shape[0] == 1 and b_norm_g.shape[0] == 1
    win = cache_k.shape[1]
    n_p, n_s = bsz * seq, dec_b * dec_seq
    xp = x_prompt.reshape(n_p, D_MODEL)
    xs = x_sample.reshape(n_s, D_MODEL)

    ng = a_norm_g[0].reshape(1, D_MODEL)
    win_bf = a_w_in[0].astype(BF16)
    wout_bf = a_w_out[0].astype(BF16)
    lng = a_ln_g[0].reshape(1, D_GATE)
    lnb = a_ln_b[0].reshape(1, D_GATE)
    bias_full = jnp.repeat(a_b_s[0].T, GROUP_DIM_A, axis=1)
    diag_row = jnp.repeat(a_w_s[0][:, 0, 0], GROUP_DIM_A).reshape(1, D_GATE)
    (xp,) = _mixer_a(xp, ng, win_bf, lng, lnb, a_w_s[0], bias_full, wout_bf, tm=TM_DENSE, single_token_chunks=False)
    xs, v_rows = _mixer_a(xs, ng, win_bf, lng, lnb, diag_row, bias_full[0:1], wout_bf, tm=n_s,
                          single_token_chunks=True)
    xp, xs = _moe([xp, xs], ffn_norm_g[0], w_router[0], b_router[0], w_gu, b_gu, w_down, b_down, 0)

    kvg = kv_norm_g.reshape(1, D_MODEL)
    bng = b_norm_g[0].reshape(1, D_MODEL)
    wk_bf, wv_bf, wq_bf, wo_bf = (w.astype(BF16) for w in (w_k, w_v, w_q[0], w_o[0]))
    kng2 = jnp.tile(k_norm_g, 2).reshape(1, PAIR)
    qng2 = jnp.tile(q_norm_g[0], 2).reshape(1, PAIR)
    cos_p, sin_p = _rope_tables(jnp.arange(seq, dtype=I32))
    cos_s, sin_s = _rope_tables(jnp.full((n_s,), PAST_LEN, I32))
    k_p, v_p = _kv_proj(xp, kvg, wk_bf, wv_bf, kng2, cos_p, sin_p, None, tm=TM_DENSE, table_tiles=seq // TM_DENSE)
    k_s, v_s, q_s = _kv_proj(xs, kvg, wk_bf, wv_bf, kng2, cos_s, sin_s, (bng, wq_bf, qng2), tm=n_s, table_tiles=1)
    xp = _attn_prompt(xp, k_p, v_p, sinks[0], bng, wq_bf, qng2, cos_p, sin_p, wo_bf, tm=TM_DENSE, seq=seq)
    q4 = jnp.tile(q_s.reshape(n_s, N_HEADS, HEAD_DIM), (1, 1, N_KV_HEADS))
    attn_s = _attn_sample(q4, cache_k.reshape(dec_b, win, D_KV), cache_v.reshape(dec_b, win, D_KV), k_s, v_s,
                          sinks[0].reshape(N_HEADS, 1), per_step=8)
    xs = _residual_proj(xs, attn_s.reshape(n_s, D_MODEL), wo_bf)
    xp, xs = _moe([xp, xs], ffn_norm_g[1], w_router[1], b_router[1], w_gu, b_gu, w_down, b_down, 1)

    k_p4 = k_p.reshape(bsz, seq, N_KV_HEADS, HEAD_DIM)
    v_p4 = v_p.reshape(bsz, seq, N_KV_HEADS, HEAD_DIM)
    return (xp.reshape(bsz, seq, D_MODEL), xs.reshape(dec_b, dec_seq, D_MODEL),
            v_rows.reshape(1, dec_b, dec_seq, D_GATE), k_p4[:, -WINDOW:], v_p4[:, -WINDOW:],
            k_s.reshape(dec_b, dec_seq, N_KV_HEADS, HEAD_DIM), v_s.reshape(dec_b, dec_seq, N_KV_HEADS, HEAD_DIM))
```

```python
import functools

import jax
import jax.numpy as jnp
from jax import lax
from jax.experimental import pallas as pl
from jax.experimental.pallas import tpu as pltpu

F32, BF16, I32 = jnp.float32, jnp.bfloat16, jnp.int32

D_MODEL = 1024
CHUNK = 128
D_GATE = 2 * D_MODEL
N_GROUPS_A = 8
GROUP_DIM_A = D_GATE // N_GROUPS_A
HEAD_DIM = 64
N_HEADS = D_MODEL // HEAD_DIM
N_KV_HEADS = N_HEADS // 4
Q_PER_KV = N_HEADS // N_KV_HEADS
D_KV = N_KV_HEADS * HEAD_DIM
WINDOW = 128
ROPE_DIM = HEAD_DIM // 4
ROPE_THETA = 500000.0
N_EXPERTS = 32
TOP_K = 4
D_EXPERT = D_MODEL
SWIGLU_LIMIT = 7.0
SWIGLU_ALPHA = 1.702
EPS = 1e-6
PAST_LEN = 8192

V7X_LANES = 128
V7X_VMEM_BYTES = 64 * 1024 * 1024
VMEM_LIMIT_BYTES = V7X_VMEM_BYTES - 8 * 1024 * 1024

TM_DENSE = 512
TM_COMBINE = 256
MOE_ROWS = 512
TM_DISPATCH = 256
FILL_BITS = MOE_ROWS.bit_length() - 1
TOKEN_SUBLANES = D_MODEL // V7X_LANES
SLAB = V7X_LANES
PAIR = 2 * HEAD_DIM
INV_SQRT2 = 0.7071067811865476


def _cparams():
    return pltpu.CompilerParams(dimension_semantics=("arbitrary",), vmem_limit_bytes=VMEM_LIMIT_BYTES)


def _resident(shape):
    zeros = (0,) * len(shape)
    return pl.BlockSpec(shape, lambda i, *_: zeros, pipeline_mode=pl.Buffered(1))


def _rows(tm, width, offset=0):
    return pl.BlockSpec((tm, width), lambda i, *_: (i + offset, 0))


def _rms(x, g):
    return x * lax.rsqrt(jnp.mean(x * x, axis=-1, keepdims=True) + EPS) * g


def _pair_norm_rope(x2, g2, cos2, sin2):
    lane = lax.broadcasted_iota(I32, x2.shape, 1)
    first = lane < HEAD_DIM
    sq = x2 * x2
    s0 = jnp.sum(jnp.where(first, sq, 0.0), axis=-1, keepdims=True)
    s1 = jnp.sum(jnp.where(first, 0.0, sq), axis=-1, keepdims=True)
    ms = jnp.where(first, s0, s1) * (1.0 / HEAD_DIM)
    y = x2 * lax.rsqrt(ms + EPS) * g2
    half = ROPE_DIM // 2
    up = pltpu.roll(y, PAIR - half, 1)
    dn = pltpu.roll(y, half, 1)
    partner = jnp.where((lane & (HEAD_DIM - 1)) < half, up, dn)
    return y * cos2 + partner * sin2


def _mixer_a_kernel(x_ref, ng_ref, win_ref, lng_ref, lnb_ref, mix_ref, bias_ref, wout_ref, *refs,
                    tm, single_token_chunks):
    if single_token_chunks:
        o_ref, vout_ref, u_scr, v_scr, p_scr = refs
    else:
        o_ref, u_scr, v_scr, p_scr = refs
    x = x_ref[...]
    h = _rms(x, ng_ref[...]).astype(BF16)
    nc = 512
    for c in range(2 * D_GATE // nc):
        z = jnp.dot(h, win_ref[:, c * nc:(c + 1) * nc], preferred_element_type=F32)
        z = 0.5 * z * (1.0 + lax.erf(z * INV_SQRT2))
        if c < D_GATE // nc:
            u_scr[:, c * nc:(c + 1) * nc] = z
        else:
            v_scr[:, (c - D_GATE // nc) * nc:(c - D_GATE // nc + 1) * nc] = z
    v = v_scr[...]
    vc = v - jnp.mean(v, axis=-1, keepdims=True)
    vn = vc * lax.rsqrt(jnp.mean(vc * vc, axis=-1, keepdims=True) + EPS) * lng_ref[...] + lnb_ref[...]
    if single_token_chunks:
        vout_ref[...] = vn
        p_scr[...] = (u_scr[...] * (vn * mix_ref[...] + bias_ref[...])).astype(BF16)
    else:
        v_scr[...] = vn
        row = lax.broadcasted_iota(I32, (CHUNK, CHUNK), 0)
        col = lax.broadcasted_iota(I32, (CHUNK, CHUNK), 1)
        for g in range(N_GROUPS_A):
            wc = jnp.where(row >= col, mix_ref[g], 0.0).astype(BF16)
            cols = slice(g * GROUP_DIM_A, (g + 1) * GROUP_DIM_A)
            for c in range(tm // CHUNK):
                rows = slice(c * CHUNK, (c + 1) * CHUNK)
                mixed = jnp.dot(wc, v_scr[rows, cols].astype(BF16), preferred_element_type=F32)
                mixed = mixed + bias_ref[:, cols]
                p_scr[rows, cols] = (u_scr[rows, cols] * mixed).astype(BF16)
    o_ref[...] = x + jnp.dot(p_scr[...], wout_ref[...], preferred_element_type=F32)


def _mixer_a(x, ng, win, lng, lnb, mix, bias, wout, *, tm, single_token_chunks):
    n = x.shape[0]
    out_shape = [jax.ShapeDtypeStruct((n, D_MODEL), F32)]
    out_specs = [_rows(tm, D_MODEL)]
    if single_token_chunks:
        out_shape.append(jax.ShapeDtypeStruct((n, D_GATE), F32))
        out_specs.append(_rows(tm, D_GATE))
    return pl.pallas_call(
        functools.partial(_mixer_a_kernel, tm=tm, single_token_chunks=single_token_chunks),
        grid=(n // tm,),
        in_specs=[_rows(tm, D_MODEL), _resident(ng.shape), _resident(win.shape), _resident(lng.shape),
                  _resident(lnb.shape), _resident(mix.shape), _resident(bias.shape), _resident(wout.shape)],
        out_specs=out_specs,
        out_shape=out_shape,
        scratch_shapes=[pltpu.VMEM((tm, D_GATE), F32), pltpu.VMEM((tm, D_GATE), F32),
                        pltpu.VMEM((tm, D_GATE), BF16)],
        compiler_params=_cparams(),
        name="mixer_a_sample" if single_token_chunks else "mixer_a_prompt",
    )(x, ng, win, lng, lnb, mix, bias, wout)


def _route_rows(x, g, whi, wlo, bias, run_scr):
    tm = x.shape[0]
    h = _rms(x, g)
    hh = h.astype(BF16)
    hl = (h - hh.astype(F32)).astype(BF16)
    logits = (jnp.dot(hh, whi, preferred_element_type=F32) + jnp.dot(hh, wlo, preferred_element_type=F32)
              + jnp.dot(hl, whi, preferred_element_type=F32) + bias)
    e = lax.broadcasted_iota(I32, (tm, N_EXPERTS), 1).astype(F32)
    hot, vals, idxs = [], [], []
    for _ in range(TOP_K):
        m = jnp.max(logits, axis=-1, keepdims=True)
        idx = jnp.min(jnp.where(logits == m, e, float(N_EXPERTS)), axis=-1, keepdims=True)
        oh = e == idx
        hot.append(oh)
        vals.append(m)
        idxs.append(idx)
        logits = jnp.where(oh, -jnp.inf, logits)
    ex = [jnp.exp(v - vals[0]) for v in vals]
    den = ex[0] + ex[1] + ex[2] + ex[3]
    picked = jnp.zeros((tm, N_EXPERTS), F32)
    for oh in hot:
        picked = picked + jnp.where(oh, 1.0, 0.0)
    r = lax.broadcasted_iota(I32, (tm, tm), 0)
    c = lax.broadcasted_iota(I32, (tm, tm), 1)
    before = jnp.where(c < r, 1.0, 0.0).astype(BF16)
    run = run_scr[0:1, 0:N_EXPERTS]
    pos = jnp.dot(before, picked.astype(BF16), preferred_element_type=F32) + run
    run_scr[0:1, 0:N_EXPERTS] = run + jnp.sum(picked, axis=0, keepdims=True)
    lane = lax.broadcasted_iota(I32, (tm, SLAB), 1)
    slab = jnp.zeros((tm, SLAB), F32)
    for k in range(TOP_K):
        pos_k = jnp.sum(jnp.where(hot[k], pos, 0.0), axis=-1, keepdims=True)
        slab = jnp.where(lane == k, idxs[k], slab)
        slab = jnp.where(lane == TOP_K + k, pos_k, slab)
        slab = jnp.where(lane == 2 * TOP_K + k, ex[k] / den, slab)
    return h, slab


def _router_kernel(xp_ref, xs_ref, g_ref, whi_ref, wlo_ref, b_ref, h_ref, slabp_ref, slabs_ref, cnt_ref, run_scr,
                   *, n_tiles, n_s):
    i = pl.program_id(0)

    @pl.when(i == 0)
    def _():
        run_scr[...] = jnp.zeros_like(run_scr)

    @pl.when(i < n_tiles)
    def _():
        h, slab = _route_rows(xp_ref[...], g_ref[...], whi_ref[...], wlo_ref[...], b_ref[...], run_scr)
        _store_token_tiles(h_ref, h, h.shape[0])
        slabp_ref[...] = slab

    @pl.when(i == n_tiles)
    def _():
        h, slab = _route_rows(xs_ref[...], g_ref[...], whi_ref[...], wlo_ref[...], b_ref[...], run_scr)
        h_ref[n_s * TOKEN_SUBLANES:, :] = jnp.zeros((h_ref.shape[0] - n_s * TOKEN_SUBLANES, V7X_LANES), F32)
        _store_token_tiles(h_ref, h, n_s)
        slabs_ref[...] = slab

    cnt_ref[...] = run_scr[...]


def _store_token_tiles(ref, x, n):
    for s in range(TOKEN_SUBLANES):
        ref[pl.ds(s, n, stride=TOKEN_SUBLANES), :] = x[:, s * V7X_LANES:(s + 1) * V7X_LANES]


def _load_token_tiles(ref, n):
    return jnp.concatenate([ref[pl.ds(s, n, stride=TOKEN_SUBLANES), :] for s in range(TOKEN_SUBLANES)], axis=1)


def _router(xp, xs, g, whi, wlo, b, *, tm):
    n_p, n_s = xp.shape[0], xs.shape[0]
    n_tiles = n_p // tm
    last = n_tiles - 1
    return pl.pallas_call(
        functools.partial(_router_kernel, n_tiles=n_tiles, n_s=n_s),
        grid=(n_tiles + 1,),
        in_specs=[pl.BlockSpec((tm, D_MODEL), lambda i: (jnp.minimum(i, last), 0)), _resident(xs.shape),
                  _resident(g.shape), _resident(whi.shape), _resident(wlo.shape), _resident(b.shape)],
        out_specs=[_rows(tm * TOKEN_SUBLANES, V7X_LANES),
                   pl.BlockSpec((tm, SLAB), lambda i: (jnp.minimum(i, last), 0)),
                   pl.BlockSpec((n_s, SLAB), lambda i: (0, 0)), pl.BlockSpec((8, V7X_LANES), lambda i: (0, 0))],
        out_shape=[jax.ShapeDtypeStruct(((n_p + tm) * TOKEN_SUBLANES, V7X_LANES), F32),
                   jax.ShapeDtypeStruct((n_p, SLAB), F32),
                   jax.ShapeDtypeStruct((n_s, SLAB), F32), jax.ShapeDtypeStruct((8, V7X_LANES), F32)],
        scratch_shapes=[pltpu.VMEM((8, V7X_LANES), F32)],
        compiler_params=_cparams(),
        name="moe_router",
    )(xp, xs, g, whi, wlo, b)


def _dispatch_kernel(fill_start_ref, fill_len_ref, dest_ref, h_ref, xs_ref, zbuf, sem, fill_sem,
                     *, tm, n_tiles, n_last, rows, n_blocks):
    i = pl.program_id(0)
    ts = TOKEN_SUBLANES

    def tile_copy(r, k):
        d = dest_ref[0, 0, k * tm + r]
        src = h_ref.at[pl.ds(pl.multiple_of(r * ts, ts), ts)]
        dst = xs_ref.at[pl.ds(pl.multiple_of(d * ts, ts), ts)]
        return pltpu.make_async_copy(src, dst, sem)

    def scatter(n):
        def body(r, carry):
            for k in range(TOP_K):
                tile_copy(r, k).start(priority=k % 2)
            return carry
        lax.fori_loop(0, n, body, 0, unroll=4)
        for k in range(TOP_K):
            pltpu.make_async_copy(h_ref.at[pl.ds(0, n * ts)], xs_ref.at[pl.ds(0, n * ts)], sem).wait()

    def fill_copy(e, bit):
        size = 1 << bit
        length = fill_len_ref[e]
        done = length - (length & (2 * size - 1))
        off = pl.multiple_of((fill_start_ref[e] + done) * ts, ts)
        return (length & size) != 0, pltpu.make_async_copy(zbuf.at[pl.ds(0, size * ts)],
                                                            xs_ref.at[pl.ds(off, size * ts)], fill_sem)

    def tail_copy(blk):
        off = pl.multiple_of(blk * (rows * ts), rows * ts)
        return pltpu.make_async_copy(zbuf, xs_ref.at[pl.ds(off, rows * ts)], fill_sem)

    def for_each_fill(act):
        for e in range(N_EXPERTS):
            for bit in range(FILL_BITS):
                go, cp = fill_copy(e, bit)

                @pl.when(go)
                def _():
                    act(cp)

        def body(blk, carry):
            act(tail_copy(blk))
            return carry
        lax.fori_loop(fill_start_ref[N_EXPERTS], n_blocks, body, 0)

    @pl.when(i == 0)
    def _():
        zbuf[...] = jnp.zeros_like(zbuf)
        for_each_fill(lambda cp: cp.start())

    @pl.when(i < n_tiles - 1)
    def _():
        scatter(tm)

    @pl.when(i == n_tiles - 1)
    def _():
        scatter(n_last)
        for_each_fill(lambda cp: cp.wait())


def _dispatch(fill_start, fill_len, dest3, h_tiles, n_all, n_blocks, *, tm, rows):
    n_tiles = dest3.shape[0]
    n_last = n_all - (n_tiles - 1) * tm
    ts = TOKEN_SUBLANES
    n_slots = n_blocks * rows
    grid_spec = pltpu.PrefetchScalarGridSpec(
        num_scalar_prefetch=2,
        grid=(n_tiles,),
        in_specs=[pl.BlockSpec((1, 1, TOP_K * tm), lambda i, *_: (i, 0, 0), memory_space=pltpu.SMEM),
                  _rows(tm * ts, V7X_LANES)],
        out_specs=pl.BlockSpec(memory_space=pl.ANY),
        scratch_shapes=[pltpu.VMEM((rows * ts, V7X_LANES), F32),
                        pltpu.SemaphoreType.DMA, pltpu.SemaphoreType.DMA],
    )
    return pl.pallas_call(
        functools.partial(_dispatch_kernel, tm=tm, n_tiles=n_tiles, n_last=n_last, rows=rows, n_blocks=n_blocks),
        grid_spec=grid_spec,
        out_shape=jax.ShapeDtypeStruct((n_slots * ts, V7X_LANES), F32),
        compiler_params=_cparams(),
        name="moe_dispatch",
    )(fill_start, fill_len, dest3, h_tiles)


def _expert_kernel(be_ref, nact_ref, x_ref, wgu_ref, bgu_ref, wdn_ref, bdn_ref, o_ref, wgu_bf, wdn_bf, act_scr,
                   *, rows):
    b = pl.program_id(0)
    nact = nact_ref[0]

    @pl.when(b < nact)
    def _():
        changed = jnp.logical_or(b == 0, be_ref[b] != be_ref[jnp.maximum(b - 1, 0)])

        @pl.when(changed)
        def _():
            step = 256
            for c in range(D_MODEL // step):
                wgu_bf[c * step:(c + 1) * step, :] = wgu_ref[c * step:(c + 1) * step, :].astype(BF16)
                wdn_bf[c * step:(c + 1) * step, :] = wdn_ref[c * step:(c + 1) * step, :].astype(BF16)

        xb = _load_token_tiles(x_ref, rows).astype(BF16)
        nc = 512
        for c in range(D_EXPERT // nc):
            gcols = slice(c * nc, (c + 1) * nc)
            ucols = slice(D_EXPERT + c * nc, D_EXPERT + (c + 1) * nc)
            gate = jnp.dot(xb, wgu_bf[:, gcols], preferred_element_type=F32) + bgu_ref[:, gcols]
            up = jnp.dot(xb, wgu_bf[:, ucols], preferred_element_type=F32) + bgu_ref[:, ucols]
            gate = jnp.minimum(gate, SWIGLU_LIMIT)
            up = jnp.clip(up, -SWIGLU_LIMIT, SWIGLU_LIMIT)
            act = (up + 1.0) * gate * (1.0 / (1.0 + jnp.exp(-SWIGLU_ALPHA * gate)))
            act_scr[:, gcols] = act.astype(BF16)
        out = jnp.dot(act_scr[...], wdn_bf[...], preferred_element_type=F32) + bdn_ref[...]
        _store_token_tiles(o_ref, out, rows)

    @pl.when(b >= nact)
    def _():
        o_ref[...] = jnp.zeros_like(o_ref)


def _experts(block_e, n_active, xs_tiles, wgu, bgu, wdn, bdn, *, rows, layer):
    ts = TOKEN_SUBLANES
    nb = xs_tiles.shape[0] // (rows * ts)
    grid_spec = pltpu.PrefetchScalarGridSpec(
        num_scalar_prefetch=2,
        grid=(nb,),
        in_specs=[
            pl.BlockSpec((rows * ts, V7X_LANES), lambda b, be, na: (jnp.minimum(b, na[0] - 1), 0)),
            pl.BlockSpec((None, None, D_MODEL, 2 * D_EXPERT), lambda b, be, na: (layer, be[b], 0, 0)),
            pl.BlockSpec((None, None, 1, 2 * D_EXPERT), lambda b, be, na: (layer, be[b], 0, 0)),
            pl.BlockSpec((None, None, D_EXPERT, D_MODEL), lambda b, be, na: (layer, be[b], 0, 0)),
            pl.BlockSpec((None, None, 1, D_MODEL), lambda b, be, na: (layer, be[b], 0, 0)),
        ],
        out_specs=pl.BlockSpec((rows * ts, V7X_LANES), lambda b, be, na: (b, 0)),
        scratch_shapes=[pltpu.VMEM((D_MODEL, 2 * D_EXPERT), BF16), pltpu.VMEM((D_EXPERT, D_MODEL), BF16),
                        pltpu.VMEM((rows, D_EXPERT), BF16)],
    )
    return pl.pallas_call(
        functools.partial(_expert_kernel, rows=rows),
        grid_spec=grid_spec,
        out_shape=jax.ShapeDtypeStruct(xs_tiles.shape, F32),
        compiler_params=_cparams(),
        name="moe_experts",
    )(block_e, n_active, xs_tiles, wgu, bgu, wdn, bdn)


def _combine_kernel(dest_first_ref, dest_next_ref, x_ref, slab_ref, ebuf_ref, o_ref, rbuf, sem, *, tm, n_tiles):
    i = pl.program_id(0)
    slot = lax.rem(i, 2)
    ts = TOKEN_SUBLANES

    def gather(dest_ref, s):
        def body(r, carry):
            for k in range(TOP_K):
                d = dest_ref[0, 0, k * tm + r]
                src = ebuf_ref.at[pl.ds(pl.multiple_of(d * ts, ts), ts)]
                dst = rbuf.at[s, k, pl.ds(pl.multiple_of(r * ts, ts), ts)]
                pltpu.make_async_copy(src, dst, sem.at[s]).start(priority=k % 2)
            return carry
        lax.fori_loop(0, tm, body, 0, unroll=4)

    @pl.when(i == 0)
    def _():
        gather(dest_first_ref, 0)

    @pl.when(i + 1 < n_tiles)
    def _():
        gather(dest_next_ref, 1 - slot)

    for k in range(TOP_K):
        pltpu.make_async_copy(ebuf_ref.at[pl.ds(0, tm * ts)], rbuf.at[slot, k], sem.at[slot]).wait()
    slab = slab_ref[...]
    gates = [slab[:, 2 * TOP_K + k:2 * TOP_K + k + 1] for k in range(TOP_K)]
    for s in range(ts):
        cols = slice(s * V7X_LANES, (s + 1) * V7X_LANES)
        acc = x_ref[:, cols]
        for k in range(TOP_K):
            acc = acc + gates[k] * rbuf[slot, k, pl.ds(s, tm, stride=ts), :]
        o_ref[:, cols] = acc


def _combine(x, slab, dest3, ebuf, *, tm):
    n = x.shape[0]
    n_tiles = n // tm
    return pl.pallas_call(
        functools.partial(_combine_kernel, tm=tm, n_tiles=n_tiles),
        grid=(n_tiles,),
        in_specs=[
            pl.BlockSpec((1, 1, TOP_K * tm), lambda i: (0, 0, 0), memory_space=pltpu.SMEM),
            pl.BlockSpec((1, 1, TOP_K * tm), lambda i: (jnp.minimum(i + 1, n_tiles - 1), 0, 0),
                         memory_space=pltpu.SMEM),
            _rows(tm, D_MODEL), _rows(tm, SLAB), pl.BlockSpec(memory_space=pl.ANY),
        ],
        out_specs=_rows(tm, D_MODEL),
        out_shape=jax.ShapeDtypeStruct((n, D_MODEL), F32),
        scratch_shapes=[pltpu.VMEM((2, TOP_K, tm * TOKEN_SUBLANES, V7X_LANES), F32),
                        pltpu.SemaphoreType.DMA((2,))],
        compiler_params=_cparams(),
        name="moe_combine",
    )(dest3, dest3, x, slab, ebuf)


def _moe(xs, g, w_router, b_router, wgu, bgu, wdn, bdn, layer):
    n_all = xs[0].shape[0] + xs[1].shape[0]
    whi = w_router.astype(BF16)
    wlo = (w_router - whi.astype(F32)).astype(BF16)
    h_all, slab_p, slab_s, cnt = _router(xs[0], xs[1], g.reshape(1, D_MODEL), whi, wlo,
                                         b_router.reshape(1, N_EXPERTS), tm=TM_DENSE)
    slabs = [slab_p, slab_s]
    slab_all = jnp.concatenate([s[:, :2 * TOP_K] for s in slabs], axis=0)
    idx = slab_all[:, :TOP_K].astype(I32)
    pos = slab_all[:, TOP_K:].astype(I32)
    counts = cnt[0, :N_EXPERTS].astype(I32)
    rows = MOE_ROWS
    nb = (n_all * TOP_K + N_EXPERTS * (rows - 1)) // rows
    padded = (counts + rows - 1) // rows * rows
    pad_end = jnp.cumsum(padded)
    pad_start = pad_end - padded
    dest = pad_start[idx] + pos
    n_active = (pad_end[-1] // rows).astype(I32)
    blk = jnp.arange(nb, dtype=I32)
    block_e = jnp.sum((pad_end[None, :] <= (blk * rows)[:, None]).astype(I32), axis=1)
    block_e = jnp.minimum(block_e, N_EXPERTS - 1)
    block_e = jnp.where(blk < n_active, block_e, block_e[jnp.maximum(n_active - 1, 0)])

    def tile_table(d, tm):
        n_tiles = -(-d.shape[0] // tm)
        d = jnp.pad(d, ((0, n_tiles * tm - d.shape[0]), (0, 0)))
        return d.reshape(n_tiles, tm, TOP_K).transpose(0, 2, 1).reshape(n_tiles, 1, TOP_K * tm)

    fill_start = jnp.concatenate([pad_start + counts, n_active.reshape(1)])
    xs_tiles = _dispatch(fill_start, padded - counts, tile_table(dest, TM_DISPATCH), h_all,
                         n_all, nb, tm=TM_DISPATCH, rows=rows)
    n_l = wgu.shape[0]
    ebuf = _experts(block_e, n_active.reshape(1), xs_tiles,
                    wgu, bgu.reshape(n_l, N_EXPERTS, 1, 2 * D_EXPERT), wdn, bdn.reshape(n_l, N_EXPERTS, 1, D_MODEL),
                    rows=rows, layer=layer)
    outs, off = [], 0
    for x, slab in zip(xs, slabs):
        n = x.shape[0]
        tm = min(TM_COMBINE, n)
        outs.append(_combine(x, slab, tile_table(dest[off:off + n], tm), ebuf, tm=tm))
        off += n
    return outs


def _kv_kernel(x_ref, kvg_ref, wk_ref, wv_ref, kng_ref, cos_ref, sin_ref, *refs, with_q):
    if with_q:
        bng_ref, wq_ref, qng_ref, k_ref, kdup_ref, v_ref, q_ref = refs
    else:
        k_ref, kdup_ref, v_ref = refs
    x = x_ref[...]
    h = _rms(x, kvg_ref[...]).astype(BF16)
    kd = jnp.dot(h, wk_ref[...], preferred_element_type=F32)
    v_ref[...] = jnp.dot(h, wv_ref[...], preferred_element_type=F32)
    cos2, sin2 = cos_ref[...], sin_ref[...]
    first_half = lax.broadcasted_iota(I32, (x.shape[0], PAIR), 1) < HEAD_DIM
    slabs = []
    for p in range(N_KV_HEADS):
        cols = slice(p * PAIR, (p + 1) * PAIR)
        slab = _pair_norm_rope(kd[:, cols], kng_ref[...], cos2, sin2)
        kdup_ref[:, cols] = slab
        slabs.append(slab)
    for p in range(D_KV // PAIR):
        k_ref[:, p * PAIR:(p + 1) * PAIR] = jnp.where(first_half, slabs[2 * p], slabs[2 * p + 1])
    if with_q:
        hq = _rms(x, bng_ref[...]).astype(BF16)
        q = jnp.dot(hq, wq_ref[...], preferred_element_type=F32)
        for p in range(D_MODEL // PAIR):
            cols = slice(p * PAIR, (p + 1) * PAIR)
            q_ref[:, cols] = _pair_norm_rope(q[:, cols], qng_ref[...], cos2, sin2)


def _kv_proj(x, kvg, wk, wv, kng2, cos2, sin2, q_args, *, tm, table_tiles):
    n = x.shape[0]
    with_q = q_args is not None
    table_spec = pl.BlockSpec((tm, PAIR), lambda i: (lax.rem(i, table_tiles), 0))
    in_specs = [_rows(tm, D_MODEL), _resident(kvg.shape), _resident(wk.shape), _resident(wv.shape),
                _resident(kng2.shape), table_spec, table_spec]
    args = [x, kvg, wk, wv, kng2, cos2, sin2]
    out_specs = [_rows(tm, D_KV), _rows(tm, 2 * D_KV), _rows(tm, D_KV)]
    out_shape = [jax.ShapeDtypeStruct((n, D_KV), F32), jax.ShapeDtypeStruct((n, 2 * D_KV), F32),
                 jax.ShapeDtypeStruct((n, D_KV), F32)]
    if with_q:
        in_specs += [_resident(a.shape) for a in q_args]
        args += list(q_args)
        out_specs.append(_rows(tm, D_MODEL))
        out_shape.append(jax.ShapeDtypeStruct((n, D_MODEL), F32))
    return pl.pallas_call(
        functools.partial(_kv_kernel, with_q=with_q),
        grid=(n // tm,),
        in_specs=in_specs, out_specs=out_specs, out_shape=out_shape,
        compiler_params=_cparams(),
        name="kvq_proj_sample" if with_q else "kv_proj_prompt",
    )(*args)


def _attn_prompt_kernel(sinks_ref, x_ref, bng_ref, wq_ref, qng_ref, cos_ref, sin_ref, kc_ref, kp_ref, vc_ref, vp_ref,
                        wo_ref, o_ref, q_scr, a_scr, *, tm, tiles_per_seq):
    i = pl.program_id(0)
    x = x_ref[...]
    hq = _rms(x, bng_ref[...]).astype(BF16)
    q = jnp.dot(hq, wq_ref[...], preferred_element_type=F32)
    cos2, sin2 = cos_ref[...], sin_ref[...]
    for p in range(D_MODEL // PAIR):
        cols = slice(p * PAIR, (p + 1) * PAIR)
        q_scr[:, cols] = _pair_norm_rope(q[:, cols], qng_ref[...], cos2, sin2)

    kj = lax.broadcasted_iota(I32, (2 * WINDOW, WINDOW), 0)
    qi = lax.broadcasted_iota(I32, (2 * WINDOW, WINDOW), 1)
    diff = qi + WINDOW - kj
    band = jnp.logical_and(diff >= 0, diff < WINDOW)
    first_half = lax.broadcasted_iota(I32, (WINDOW, PAIR), 1) < HEAD_DIM
    first_kj = jnp.where(lax.rem(i, tiles_per_seq) == 0, WINDOW, 0)
    scale = HEAD_DIM ** -0.5

    for blk in range(tm // WINDOW):
        rows = slice(blk * WINDOW, (blk + 1) * WINDOW)
        if blk == 0:
            kk = jnp.concatenate([kp_ref[...], kc_ref[rows, :]], axis=0).astype(BF16)
            vv = jnp.concatenate([vp_ref[...], vc_ref[rows, :]], axis=0)
            valid = jnp.logical_and(band, kj >= first_kj)
        else:
            win = slice((blk - 1) * WINDOW, (blk + 1) * WINDOW)
            kk = kc_ref[win, :].astype(BF16)
            vv = vc_ref[win, :]
            valid = band
        for kvh in range(N_KV_HEADS):
            kk_h = kk[:, kvh * PAIR:(kvh + 1) * PAIR]
            vcols = slice((kvh // 2) * PAIR, (kvh // 2 + 1) * PAIR)
            vt = vv[:, vcols].T.astype(BF16)
            ch = slice((kvh % 2) * HEAD_DIM, (kvh % 2 + 1) * HEAD_DIM)
            outs = []
            for g in range(Q_PER_KV):
                head = kvh * Q_PER_KV + g
                qp = q_scr[rows, (head // 2) * PAIR:(head // 2 + 1) * PAIR]
                qm = jnp.where(first_half if head % 2 == 0 else jnp.logical_not(first_half), qp * scale, 0.0)
                sink = sinks_ref[head]
                st = lax.dot_general(kk_h, qm.astype(BF16), (((1,), (1,)), ((), ())), preferred_element_type=F32)
                st = jnp.where(valid, st, -jnp.inf)
                m = jnp.maximum(jnp.max(st, axis=0, keepdims=True), sink)
                p = jnp.exp(st - m)
                den = jnp.sum(p, axis=0, keepdims=True) + jnp.exp(sink - m)
                ot = jnp.dot(vt, p.astype(BF16), preferred_element_type=F32)
                outs.append(ot[ch, :] * (1.0 / den))
            for pq in range(Q_PER_KV // 2):
                pair = (kvh * Q_PER_KV) // 2 + pq
                both = jnp.concatenate([outs[2 * pq], outs[2 * pq + 1]], axis=0)
                a_scr[rows, pair * PAIR:(pair + 1) * PAIR] = both.T.astype(BF16)
    o_ref[...] = x + jnp.dot(a_scr[...], wo_ref[...], preferred_element_type=F32)


def _attn_prompt(x, kdup, v, sinks, bng, wq, qng2, cos2, sin2, wo, *, tm, seq):
    n = x.shape[0]
    tiles_per_seq = seq // tm
    blocks_per_tile = tm // WINDOW

    def prev_spec(width):
        return pl.BlockSpec((WINDOW, width), lambda i: (jnp.maximum(i * blocks_per_tile - 1, 0), 0))

    table_spec = pl.BlockSpec((tm, PAIR), lambda i: (lax.rem(i, tiles_per_seq), 0))
    return pl.pallas_call(
        functools.partial(_attn_prompt_kernel, tm=tm, tiles_per_seq=tiles_per_seq),
        grid=(n // tm,),
        in_specs=[pl.BlockSpec(memory_space=pltpu.SMEM), _rows(tm, D_MODEL), _resident(bng.shape),
                  _resident(wq.shape), _resident(qng2.shape), table_spec, table_spec,
                  _rows(tm, 2 * D_KV), prev_spec(2 * D_KV), _rows(tm, D_KV), prev_spec(D_KV), _resident(wo.shape)],
        out_specs=_rows(tm, D_MODEL),
        out_shape=jax.ShapeDtypeStruct((n, D_MODEL), F32),
        scratch_shapes=[pltpu.VMEM((tm, D_MODEL), F32), pltpu.VMEM((tm, D_MODEL), BF16)],
        compiler_params=_cparams(),
        name="attn_prompt",
    )(sinks, x, bng, wq, qng2, cos2, sin2, kdup, kdup, v, v, wo)


def _attn_sample_kernel(q_ref, ck_ref, cv_ref, kn_ref, vn_ref, sink_ref, o_ref, *, per_step, win):
    rowi = lax.broadcasted_iota(I32, (N_HEADS, D_KV), 0)
    lanei = lax.broadcasted_iota(I32, (N_HEADS, D_KV), 1)
    group = rowi >> 2
    own_block = group == (lanei >> 6)
    j = lax.broadcasted_iota(I32, (N_HEADS, win), 1)
    diff = win - j
    valid = jnp.logical_and(diff >= 0, diff < WINDOW)
    sink = sink_ref[...]
    scale = HEAD_DIM ** -0.5
    for b in range(per_step):
        qm = jnp.where(own_block, q_ref[b], 0.0)
        s = lax.dot_general(qm.astype(BF16), ck_ref[b].astype(BF16), (((1,), (1,)), ((), ())),
                            preferred_element_type=F32) * scale
        s = jnp.where(valid, s, -jnp.inf)
        s_new = jnp.sum(qm * kn_ref[b:b + 1, :], axis=-1, keepdims=True) * scale
        m = jnp.maximum(jnp.maximum(jnp.max(s, axis=-1, keepdims=True), s_new), sink)
        p = jnp.exp(s - m)
        p_new = jnp.exp(s_new - m)
        den = jnp.sum(p, axis=-1, keepdims=True) + p_new + jnp.exp(sink - m)
        o = jnp.dot(p.astype(BF16), cv_ref[b].astype(BF16), preferred_element_type=F32) + p_new * vn_ref[b:b + 1, :]
        o = o * (1.0 / den)
        res = o
        for sft in range(1, N_KV_HEADS):
            res = jnp.where(group == sft, pltpu.roll(o, D_KV - sft * HEAD_DIM, 1), res)
        o_ref[b] = res[:, :HEAD_DIM]


def _attn_sample(q4, ck, cv, kn, vn, sink_col, *, per_step):
    nb, win = ck.shape[0], ck.shape[1]
    return pl.pallas_call(
        functools.partial(_attn_sample_kernel, per_step=per_step, win=win),
        grid=(nb // per_step,),
        in_specs=[pl.BlockSpec((per_step, N_HEADS, D_KV), lambda i: (i, 0, 0)),
                  pl.BlockSpec((per_step, win, D_KV), lambda i: (i, 0, 0)),
                  pl.BlockSpec((per_step, win, D_KV), lambda i: (i, 0, 0)),
                  _rows(per_step, D_KV), _rows(per_step, D_KV), _resident(sink_col.shape)],
        out_specs=pl.BlockSpec((per_step, N_HEADS, HEAD_DIM), lambda i: (i, 0, 0)),
        out_shape=jax.ShapeDtypeStruct((nb, N_HEADS, HEAD_DIM), F32),
        compiler_params=_cparams(),
        name="attn_sample",
    )(q4, ck, cv, kn, vn, sink_col)


def _residual_proj_kernel(x_ref, a_ref, w_ref, o_ref):
    o_ref[...] = x_ref[...] + jnp.dot(a_ref[...].astype(BF16), w_ref[...], preferred_element_type=F32)


def _residual_proj(x, a, w):
    n = x.shape[0]
    return pl.pallas_call(
        _residual_proj_kernel,
        grid=(1,),
        in_specs=[_rows(n, D_MODEL), _rows(n, a.shape[1]), _resident(w.shape)],
        out_specs=_rows(n, D_MODEL),
        out_shape=jax.ShapeDtypeStruct((n, D_MODEL), F32),
        compiler_params=_cparams(),
        name="attn_out_sample",
    )(x, a, w)


def _rope_tables(pos):
    half = ROPE_DIM // 2
    inv_freq = jnp.power(jnp.float32(ROPE_THETA), -jnp.arange(half, dtype=F32) / half)
    ang = pos.astype(F32)[:, None] * inv_freq[None, :]
    cos, sin = jnp.cos(ang), jnp.sin(ang)
    n = pos.shape[0]
    rest = HEAD_DIM - ROPE_DIM
    cos_h = jnp.concatenate([cos, cos, jnp.ones((n, rest), F32)], axis=1)
    sin_h = jnp.concatenate([-sin, sin, jnp.zeros((n, rest), F32)], axis=1)
    return jnp.tile(cos_h, (1, 2)), jnp.tile(sin_h, (1, 2))


def kernel(x_prompt, x_sample, cache_k, cache_v, a_norm_g, a_w_in, a_ln_g, a_ln_b, a_w_s, a_b_s, a_w_out, kv_norm_g, w_k, w_v, k_norm_g, b_norm_g, w_q, q_norm_g, sinks, w_o, ffn_norm_g, w_router, b_router, w_gu, b_gu, w_down, b_down):
    bsz, seq, _ = x_prompt.shape
    dec_b, dec_seq, _ = x_sample.shape
    assert dec_seq == 1 and seq % TM_DENSE == 0 and a_norm_g.shape[0] == 1 and b_norm_g.shape[0] == 1
    win = cache_k.shape[1]
    n_p, n_s = bsz * seq, dec_b * dec_seq
    xp = x_prompt.reshape(n_p, D_MODEL)
    xs = x_sample.reshape(n_s, D_MODEL)

    ng = a_norm_g[0].reshape(1, D_MODEL)
    win_bf = a_w_in[0].astype(BF16)
    wout_bf = a_w_out[0].astype(BF16)
    lng = a_ln_g[0].reshape(1, D_GATE)
    lnb = a_ln_b[0].reshape(1, D_GATE)
    bias_full = jnp.repeat(a_b_s[0].T, GROUP_DIM_A, axis=1)
    diag_row = jnp.repeat(a_w_s[0][:, 0, 0], GROUP_DIM_A).reshape(1, D_GATE)
    (xp,) = _mixer_a(xp, ng, win_bf, lng, lnb, a_w_s[0], bias_full, wout_bf, tm=TM_DENSE, single_token_chunks=False)
    xs, v_rows = _mixer_a(xs, ng, win_bf, lng, lnb, diag_row, bias_full[0:1], wout_bf, tm=n_s,
                          single_token_chunks=True)
    xp, xs = _moe([xp, xs], ffn_norm_g[0], w_router[0], b_router[0], w_gu, b_gu, w_down, b_down, 0)

    kvg = kv_norm_g.reshape(1, D_MODEL)
    bng = b_norm_g[0].reshape(1, D_MODEL)
    w_k_dup = jnp.repeat(w_k.reshape(D_MODEL, N_KV_HEADS, 1, HEAD_DIM), 2, axis=2).reshape(D_MODEL, 2 * D_KV)
    wk_bf, wv_bf, wq_bf, wo_bf = (w.astype(BF16) for w in (w_k_dup, w_v, w_q[0], w_o[0]))
    kng2 = jnp.tile(k_norm_g, 2).reshape(1, PAIR)
    qng2 = jnp.tile(q_norm_g[0], 2).reshape(1, PAIR)
    cos_p, sin_p = _rope_tables(jnp.arange(seq, dtype=I32))
    cos_s, sin_s = _rope_tables(jnp.full((n_s,), PAST_LEN, I32))
    k_p, kdup_p, v_p = _kv_proj(xp, kvg, wk_bf, wv_bf, kng2, cos_p, sin_p, None, tm=TM_DENSE,
                                table_tiles=seq // TM_DENSE)
    k_s, _, v_s, q_s = _kv_proj(xs, kvg, wk_bf, wv_bf, kng2, cos_s, sin_s, (bng, wq_bf, qng2), tm=n_s,
                                table_tiles=1)
    xp = _attn_prompt(xp, kdup_p, v_p, sinks[0], bng, wq_bf, qng2, cos_p, sin_p, wo_bf, tm=TM_DENSE, seq=seq)
    q4 = jnp.tile(q_s.reshape(n_s, N_HEADS, HEAD_DIM), (1, 1, N_KV_HEADS))
    attn_s = _attn_sample(q4, cache_k.reshape(dec_b, win, D_KV), cache_v.reshape(dec_b, win, D_KV), k_s, v_s,
                          sinks[0].reshape(N_HEADS, 1), per_step=8)
    xs = _residual_proj(xs, attn_s.reshape(n_s, D_MODEL), wo_bf)
    xp, xs = _moe([xp, xs], ffn_norm_g[1], w_router[1], b_router[1], w_gu, b_gu, w_down, b_down, 1)

    k_p4 = k_p.reshape(bsz, seq, N_KV_HEADS, HEAD_DIM)
    v_p4 = v_p.reshape(bsz, seq, N_KV_HEADS, HEAD_DIM)
    return (xp.reshape(bsz, seq, D_MODEL), xs.reshape(dec_b, dec_seq, D_MODEL),
            v_rows.reshape(1, dec_b, dec_seq, D_GATE), k_p4[:, -WINDOW:], v_p4[:, -WINDOW:],
            k_s.reshape(dec_b, dec_seq, N_KV_HEADS, HEAD_DIM), v_s.reshape(dec_b, dec_seq, N_KV_HEADS, HEAD_DIM))
```

```python
import functools

import jax
import jax.numpy as jnp
from jax import lax
from jax.experimental import pallas as pl
from jax.experimental.pallas import tpu as pltpu

F32, BF16, I32 = jnp.float32, jnp.bfloat16, jnp.int32

D_MODEL = 1024
CHUNK = 128
D_GATE = 2 * D_MODEL
N_GROUPS_A = 8
GROUP_DIM_A = D_GATE // N_GROUPS_A
HEAD_DIM = 64
N_HEADS = D_MODEL // HEAD_DIM
N_KV_HEADS = N_HEADS // 4
Q_PER_KV = N_HEADS // N_KV_HEADS
D_KV = N_KV_HEADS * HEAD_DIM
WINDOW = 128
ROPE_DIM = HEAD_DIM // 4
ROPE_THETA = 500000.0
N_EXPERTS = 32
TOP_K = 4
D_EXPERT = D_MODEL
SWIGLU_LIMIT = 7.0
SWIGLU_ALPHA = 1.702
EPS = 1e-6
PAST_LEN = 8192

V7X_LANES = 128
V7X_VMEM_BYTES = 64 * 1024 * 1024
VMEM_LIMIT_BYTES = V7X_VMEM_BYTES - 8 * 1024 * 1024

TM_DENSE = 512
TM_COMBINE = 256
MOE_ROWS = 512
TM_DISPATCH = 256
FILL_BITS = MOE_ROWS.bit_length() - 1
TOKEN_SUBLANES = D_MODEL // V7X_LANES
SLAB = V7X_LANES
PAIR = 2 * HEAD_DIM
INV_SQRT2 = 0.7071067811865476


def _cparams():
    return pltpu.CompilerParams(dimension_semantics=("arbitrary",), vmem_limit_bytes=VMEM_LIMIT_BYTES)


def _resident(shape):
    zeros = (0,) * len(shape)
    return pl.BlockSpec(shape, lambda i, *_: zeros, pipeline_mode=pl.Buffered(1))


def _rows(tm, width, offset=0):
    return pl.BlockSpec((tm, width), lambda i, *_: (i + offset, 0))


def _rms(x, g):
    return x * lax.rsqrt(jnp.mean(x * x, axis=-1, keepdims=True) + EPS) * g


def _pair_norm_rope(x2, g2, cos2, sin2):
    lane = lax.broadcasted_iota(I32, x2.shape, 1)
    first = lane < HEAD_DIM
    sq = x2 * x2
    s0 = jnp.sum(jnp.where(first, sq, 0.0), axis=-1, keepdims=True)
    s1 = jnp.sum(jnp.where(first, 0.0, sq), axis=-1, keepdims=True)
    ms = jnp.where(first, s0, s1) * (1.0 / HEAD_DIM)
    y = x2 * lax.rsqrt(ms + EPS) * g2
    half = ROPE_DIM // 2
    up = pltpu.roll(y, PAIR - half, 1)
    dn = pltpu.roll(y, half, 1)
    partner = jnp.where((lane & (HEAD_DIM - 1)) < half, up, dn)
    return y * cos2 + partner * sin2


def _mixer_a_kernel(x_ref, ng_ref, win_ref, lng_ref, lnb_ref, mix_ref, bias_ref, wout_ref, *refs,
                    tm, single_token_chunks):
    if single_token_chunks:
        o_ref, vout_ref, u_scr, v_scr, p_scr = refs
    else:
        o_ref, u_scr, v_scr, p_scr = refs
    x = x_ref[...]
    h = _rms(x, ng_ref[...]).astype(BF16)
    nc = 512
    for c in range(2 * D_GATE // nc):
        z = jnp.dot(h, win_ref[:, c * nc:(c + 1) * nc], preferred_element_type=F32)
        z = 0.5 * z * (1.0 + lax.erf(z * INV_SQRT2))
        if c < D_GATE // nc:
            u_scr[:, c * nc:(c + 1) * nc] = z
        else:
            v_scr[:, (c - D_GATE // nc) * nc:(c - D_GATE // nc + 1) * nc] = z
    v = v_scr[...]
    vc = v - jnp.mean(v, axis=-1, keepdims=True)
    vn = vc * lax.rsqrt(jnp.mean(vc * vc, axis=-1, keepdims=True) + EPS) * lng_ref[...] + lnb_ref[...]
    if single_token_chunks:
        vout_ref[...] = vn
        p_scr[...] = (u_scr[...] * (vn * mix_ref[...] + bias_ref[...])).astype(BF16)
    else:
        v_scr[...] = vn
        row = lax.broadcasted_iota(I32, (CHUNK, CHUNK), 0)
        col = lax.broadcasted_iota(I32, (CHUNK, CHUNK), 1)
        for g in range(N_GROUPS_A):
            wc = jnp.where(row >= col, mix_ref[g], 0.0).astype(BF16)
            cols = slice(g * GROUP_DIM_A, (g + 1) * GROUP_DIM_A)
            for c in range(tm // CHUNK):
                rows = slice(c * CHUNK, (c + 1) * CHUNK)
                mixed = jnp.dot(wc, v_scr[rows, cols].astype(BF16), preferred_element_type=F32)
                mixed = mixed + bias_ref[:, cols]
                p_scr[rows, cols] = (u_scr[rows, cols] * mixed).astype(BF16)
    o_ref[...] = x + jnp.dot(p_scr[...], wout_ref[...], preferred_element_type=F32)


def _mixer_a(x, ng, win, lng, lnb, mix, bias, wout, *, tm, single_token_chunks):
    n = x.shape[0]
    out_shape = [jax.ShapeDtypeStruct((n, D_MODEL), F32)]
    out_specs = [_rows(tm, D_MODEL)]
    if single_token_chunks:
        out_shape.append(jax.ShapeDtypeStruct((n, D_GATE), F32))
        out_specs.append(_rows(tm, D_GATE))
    return pl.pallas_call(
        functools.partial(_mixer_a_kernel, tm=tm, single_token_chunks=single_token_chunks),
        grid=(n // tm,),
        in_specs=[_rows(tm, D_MODEL), _resident(ng.shape), _resident(win.shape), _resident(lng.shape),
                  _resident(lnb.shape), _resident(mix.shape), _resident(bias.shape), _resident(wout.shape)],
        out_specs=out_specs,
        out_shape=out_shape,
        scratch_shapes=[pltpu.VMEM((tm, D_GATE), F32), pltpu.VMEM((tm, D_GATE), F32),
                        pltpu.VMEM((tm, D_GATE), BF16)],
        compiler_params=_cparams(),
        name="mixer_a_sample" if single_token_chunks else "mixer_a_prompt",
    )(x, ng, win, lng, lnb, mix, bias, wout)


def _route_rows(x, g, whi, wlo, bias, run_scr):
    tm = x.shape[0]
    h = _rms(x, g)
    hh = h.astype(BF16)
    hl = (h - hh.astype(F32)).astype(BF16)
    logits = (jnp.dot(hh, whi, preferred_element_type=F32) + jnp.dot(hh, wlo, preferred_element_type=F32)
              + jnp.dot(hl, whi, preferred_element_type=F32) + bias)
    e = lax.broadcasted_iota(I32, (tm, N_EXPERTS), 1).astype(F32)
    hot, vals, idxs = [], [], []
    for _ in range(TOP_K):
        m = jnp.max(logits, axis=-1, keepdims=True)
        idx = jnp.min(jnp.where(logits == m, e, float(N_EXPERTS)), axis=-1, keepdims=True)
        oh = e == idx
        hot.append(oh)
        vals.append(m)
        idxs.append(idx)
        logits = jnp.where(oh, -jnp.inf, logits)
    ex = [jnp.exp(v - vals[0]) for v in vals]
    den = ex[0] + ex[1] + ex[2] + ex[3]
    picked = jnp.zeros((tm, N_EXPERTS), F32)
    for oh in hot:
        picked = picked + jnp.where(oh, 1.0, 0.0)
    r = lax.broadcasted_iota(I32, (tm, tm), 0)
    c = lax.broadcasted_iota(I32, (tm, tm), 1)
    before = jnp.where(c < r, 1.0, 0.0).astype(BF16)
    run = run_scr[0:1, 0:N_EXPERTS]
    pos = jnp.dot(before, picked.astype(BF16), preferred_element_type=F32) + run
    run_scr[0:1, 0:N_EXPERTS] = run + jnp.sum(picked, axis=0, keepdims=True)
    lane = lax.broadcasted_iota(I32, (tm, SLAB), 1)
    slab = jnp.zeros((tm, SLAB), F32)
    for k in range(TOP_K):
        pos_k = jnp.sum(jnp.where(hot[k], pos, 0.0), axis=-1, keepdims=True)
        slab = jnp.where(lane == k, idxs[k], slab)
        slab = jnp.where(lane == TOP_K + k, pos_k, slab)
        slab = jnp.where(lane == 2 * TOP_K + k, ex[k] / den, slab)
    return h, slab


def _router_kernel(xp_ref, xs_ref, g_ref, whi_ref, wlo_ref, b_ref, h_ref, slabp_ref, slabs_ref, cnt_ref, run_scr,
                   *, n_tiles, n_s):
    i = pl.program_id(0)

    @pl.when(i == 0)
    def _():
        run_scr[...] = jnp.zeros_like(run_scr)

    @pl.when(i < n_tiles)
    def _():
        h, slab = _route_rows(xp_ref[...], g_ref[...], whi_ref[...], wlo_ref[...], b_ref[...], run_scr)
        _store_token_tiles(h_ref, h, h.shape[0])
        slabp_ref[...] = slab

    @pl.when(i == n_tiles)
    def _():
        h, slab = _route_rows(xs_ref[...], g_ref[...], whi_ref[...], wlo_ref[...], b_ref[...], run_scr)
        h_ref[n_s * TOKEN_SUBLANES:, :] = jnp.zeros((h_ref.shape[0] - n_s * TOKEN_SUBLANES, V7X_LANES), F32)
        _store_token_tiles(h_ref, h, n_s)
        slabs_ref[...] = slab

    cnt_ref[...] = run_scr[...]


def _store_token_tiles(ref, x, n):
    for s in range(TOKEN_SUBLANES):
        ref[pl.ds(s, n, stride=TOKEN_SUBLANES), :] = x[:, s * V7X_LANES:(s + 1) * V7X_LANES]


def _load_token_tiles(ref, n):
    return jnp.concatenate([ref[pl.ds(s, n, stride=TOKEN_SUBLANES), :] for s in range(TOKEN_SUBLANES)], axis=1)


def _router(xp, xs, g, whi, wlo, b, *, tm):
    n_p, n_s = xp.shape[0], xs.shape[0]
    n_tiles = n_p // tm
    last = n_tiles - 1
    return pl.pallas_call(
        functools.partial(_router_kernel, n_tiles=n_tiles, n_s=n_s),
        grid=(n_tiles + 1,),
        in_specs=[pl.BlockSpec((tm, D_MODEL), lambda i: (jnp.minimum(i, last), 0)), _resident(xs.shape),
                  _resident(g.shape), _resident(whi.shape), _resident(wlo.shape), _resident(b.shape)],
        out_specs=[_rows(tm * TOKEN_SUBLANES, V7X_LANES),
                   pl.BlockSpec((tm, SLAB), lambda i: (jnp.minimum(i, last), 0)),
                   pl.BlockSpec((n_s, SLAB), lambda i: (0, 0)), pl.BlockSpec((8, V7X_LANES), lambda i: (0, 0))],
        out_shape=[jax.ShapeDtypeStruct(((n_p + tm) * TOKEN_SUBLANES, V7X_LANES), F32),
                   jax.ShapeDtypeStruct((n_p, SLAB), F32),
                   jax.ShapeDtypeStruct((n_s, SLAB), F32), jax.ShapeDtypeStruct((8, V7X_LANES), F32)],
        scratch_shapes=[pltpu.VMEM((8, V7X_LANES), F32)],
        compiler_params=_cparams(),
        name="moe_router",
    )(xp, xs, g, whi, wlo, b)


def _dispatch_kernel(fill_start_ref, fill_len_ref, dest_ref, h_ref, xs_ref, zbuf, sem, fill_sem,
                     *, tm, n_tiles, n_last, rows, n_blocks):
    i = pl.program_id(0)
    ts = TOKEN_SUBLANES

    def tile_copy(r, k):
        d = dest_ref[0, 0, k * tm + r]
        src = h_ref.at[pl.ds(pl.multiple_of(r * ts, ts), ts)]
        dst = xs_ref.at[pl.ds(pl.multiple_of(d * ts, ts), ts)]
        return pltpu.make_async_copy(src, dst, sem)

    def scatter(n):
        def body(r, carry):
            for k in range(TOP_K):
                tile_copy(r, k).start(priority=k % 2)
            return carry
        lax.fori_loop(0, n, body, 0, unroll=4)
        for k in range(TOP_K):
            pltpu.make_async_copy(h_ref.at[pl.ds(0, n * ts)], xs_ref.at[pl.ds(0, n * ts)], sem).wait()

    def fill_copy(e, bit):
        size = 1 << bit
        length = fill_len_ref[e]
        done = length - (length & (2 * size - 1))
        off = pl.multiple_of((fill_start_ref[e] + done) * ts, ts)
        return (length & size) != 0, pltpu.make_async_copy(zbuf.at[pl.ds(0, size * ts)],
                                                            xs_ref.at[pl.ds(off, size * ts)], fill_sem)

    def tail_copy(blk):
        off = pl.multiple_of(blk * (rows * ts), rows * ts)
        return pltpu.make_async_copy(zbuf, xs_ref.at[pl.ds(off, rows * ts)], fill_sem)

    def for_each_fill(act):
        for e in range(N_EXPERTS):
            for bit in range(FILL_BITS):
                go, cp = fill_copy(e, bit)

                @pl.when(go)
                def _():
                    act(cp)

        def body(blk, carry):
            act(tail_copy(blk))
            return carry
        lax.fori_loop(fill_start_ref[N_EXPERTS], n_blocks, body, 0)

    @pl.when(i == 0)
    def _():
        zbuf[...] = jnp.zeros_like(zbuf)
        for_each_fill(lambda cp: cp.start())

    @pl.when(i < n_tiles - 1)
    def _():
        scatter(tm)

    @pl.when(i == n_tiles - 1)
    def _():
        scatter(n_last)
        for_each_fill(lambda cp: cp.wait())


def _dispatch(fill_start, fill_len, dest3, h_tiles, n_all, n_blocks, *, tm, rows):
    n_tiles = dest3.shape[0]
    n_last = n_all - (n_tiles - 1) * tm
    ts = TOKEN_SUBLANES
    n_slots = n_blocks * rows
    grid_spec = pltpu.PrefetchScalarGridSpec(
        num_scalar_prefetch=2,
        grid=(n_tiles,),
        in_specs=[pl.BlockSpec((1, 1, TOP_K * tm), lambda i, *_: (i, 0, 0), memory_space=pltpu.SMEM),
                  _rows(tm * ts, V7X_LANES)],
        out_specs=pl.BlockSpec(memory_space=pl.ANY),
        scratch_shapes=[pltpu.VMEM((rows * ts, V7X_LANES), F32),
                        pltpu.SemaphoreType.DMA, pltpu.SemaphoreType.DMA],
    )
    return pl.pallas_call(
        functools.partial(_dispatch_kernel, tm=tm, n_tiles=n_tiles, n_last=n_last, rows=rows, n_blocks=n_blocks),
        grid_spec=grid_spec,
        out_shape=jax.ShapeDtypeStruct((n_slots * ts, V7X_LANES), F32),
        compiler_params=_cparams(),
        name="moe_dispatch",
    )(fill_start, fill_len, dest3, h_tiles)


def _expert_kernel(be_ref, nact_ref, run_ref, rune_ref, x_ref, wgu_hbm, bgu_ref, wdn_hbm, bdn_ref, o_ref,
                   wgu_f32, wdn_f32, wsem, wgu_bf, wdn_bf, act_scr, *, rows, layer):
    b = pl.program_id(0)
    nact = nact_ref[0]

    def weight_copies(r, slot):
        e = rune_ref[r]
        return (pltpu.make_async_copy(wgu_hbm.at[layer, e], wgu_f32.at[slot], wsem.at[0, slot]),
                pltpu.make_async_copy(wdn_hbm.at[layer, e], wdn_f32.at[slot], wsem.at[1, slot]))

    @pl.when(b == 0)
    def _():
        for cp in weight_copies(0, 0):
            cp.start()

    @pl.when(b < nact)
    def _():
        r = run_ref[b]
        slot = lax.rem(r, 2)
        changed = jnp.logical_or(b == 0, r != run_ref[jnp.maximum(b - 1, 0)])

        @pl.when(changed)
        def _():
            for cp in weight_copies(r, slot):
                cp.wait()

            @pl.when(r + 1 < rune_ref[N_EXPERTS])
            def _():
                for cp in weight_copies(r + 1, 1 - slot):
                    cp.start()

            step = 256
            for c in range(D_MODEL // step):
                wgu_bf[c * step:(c + 1) * step, :] = wgu_f32[slot, c * step:(c + 1) * step, :].astype(BF16)
                wdn_bf[c * step:(c + 1) * step, :] = wdn_f32[slot, c * step:(c + 1) * step, :].astype(BF16)

        xb = _load_token_tiles(x_ref, rows).astype(BF16)
        nc = 512
        for c in range(D_EXPERT // nc):
            gcols = slice(c * nc, (c + 1) * nc)
            ucols = slice(D_EXPERT + c * nc, D_EXPERT + (c + 1) * nc)
            gate = jnp.dot(xb, wgu_bf[:, gcols], preferred_element_type=F32) + bgu_ref[:, gcols]
            up = jnp.dot(xb, wgu_bf[:, ucols], preferred_element_type=F32) + bgu_ref[:, ucols]
            gate = jnp.minimum(gate, SWIGLU_LIMIT)
            up = jnp.clip(up, -SWIGLU_LIMIT, SWIGLU_LIMIT)
            act = (up + 1.0) * gate * (1.0 / (1.0 + jnp.exp(-SWIGLU_ALPHA * gate)))
            act_scr[:, gcols] = act.astype(BF16)
        out = jnp.dot(act_scr[...], wdn_bf[...], preferred_element_type=F32) + bdn_ref[...]
        _store_token_tiles(o_ref, out, rows)

    @pl.when(b >= nact)
    def _():
        o_ref[...] = jnp.zeros_like(o_ref)


def _experts(block_e, n_active, block_run, run_e, xs_tiles, wgu, bgu, wdn, bdn, *, rows, layer):
    ts = TOKEN_SUBLANES
    nb = xs_tiles.shape[0] // (rows * ts)
    grid_spec = pltpu.PrefetchScalarGridSpec(
        num_scalar_prefetch=4,
        grid=(nb,),
        in_specs=[
            pl.BlockSpec((rows * ts, V7X_LANES), lambda b, be, na, *_: (jnp.minimum(b, na[0] - 1), 0)),
            pl.BlockSpec(memory_space=pl.ANY),
            pl.BlockSpec((None, None, 1, 2 * D_EXPERT), lambda b, be, *_: (layer, be[b], 0, 0)),
            pl.BlockSpec(memory_space=pl.ANY),
            pl.BlockSpec((None, None, 1, D_MODEL), lambda b, be, *_: (layer, be[b], 0, 0)),
        ],
        out_specs=pl.BlockSpec((rows * ts, V7X_LANES), lambda b, *_: (b, 0)),
        scratch_shapes=[pltpu.VMEM((2, D_MODEL, 2 * D_EXPERT), F32), pltpu.VMEM((2, D_EXPERT, D_MODEL), F32),
                        pltpu.SemaphoreType.DMA((2, 2)),
                        pltpu.VMEM((D_MODEL, 2 * D_EXPERT), BF16), pltpu.VMEM((D_EXPERT, D_MODEL), BF16),
                        pltpu.VMEM((rows, D_EXPERT), BF16)],
    )
    return pl.pallas_call(
        functools.partial(_expert_kernel, rows=rows, layer=layer),
        grid_spec=grid_spec,
        out_shape=jax.ShapeDtypeStruct(xs_tiles.shape, F32),
        compiler_params=_cparams(),
        name="moe_experts",
    )(block_e, n_active, block_run, run_e, xs_tiles, wgu, bgu, wdn, bdn)


def _combine_kernel(dest_first_ref, dest_next_ref, x_ref, slab_ref, ebuf_ref, *refs, tm, n_tiles, attn_pre):
    if attn_pre:
        (kvg_ref, wk_ref, wv_ref, kng_ref, bng_ref, wq_ref, qng_ref, cos_ref, sin_ref,
         o_ref, k_ref, kdup_ref, v_ref, q_ref, rbuf, sem) = refs
    else:
        o_ref, rbuf, sem = refs
    i = pl.program_id(0)
    slot = lax.rem(i, 2)
    ts = TOKEN_SUBLANES

    def gather(dest_ref, s):
        def body(r, carry):
            for k in range(TOP_K):
                d = dest_ref[0, 0, k * tm + r]
                src = ebuf_ref.at[pl.ds(pl.multiple_of(d * ts, ts), ts)]
                dst = rbuf.at[s, k, pl.ds(pl.multiple_of(r * ts, ts), ts)]
                pltpu.make_async_copy(src, dst, sem.at[s]).start(priority=k % 2)
            return carry
        lax.fori_loop(0, tm, body, 0, unroll=4)

    @pl.when(i == 0)
    def _():
        gather(dest_first_ref, 0)

    @pl.when(i + 1 < n_tiles)
    def _():
        gather(dest_next_ref, 1 - slot)

    for k in range(TOP_K):
        pltpu.make_async_copy(ebuf_ref.at[pl.ds(0, tm * ts)], rbuf.at[slot, k], sem.at[slot]).wait()
    slab = slab_ref[...]
    gates = [slab[:, 2 * TOP_K + k:2 * TOP_K + k + 1] for k in range(TOP_K)]
    for s in range(ts):
        cols = slice(s * V7X_LANES, (s + 1) * V7X_LANES)
        acc = x_ref[:, cols]
        for k in range(TOP_K):
            acc = acc + gates[k] * rbuf[slot, k, pl.ds(s, tm, stride=ts), :]
        o_ref[:, cols] = acc
    if attn_pre:
        x2 = o_ref[...]
        cos2, sin2 = cos_ref[...], sin_ref[...]
        _shared_kv(x2, kvg_ref[...], wk_ref, wv_ref, kng_ref[...], cos2, sin2, k_ref, kdup_ref, v_ref)
        hq = _rms(x2, bng_ref[...]).astype(BF16)
        q = jnp.dot(hq, wq_ref[...], preferred_element_type=F32)
        for p in range(D_MODEL // PAIR):
            cols = slice(p * PAIR, (p + 1) * PAIR)
            q_ref[:, cols] = (_pair_norm_rope(q[:, cols], qng_ref[...], cos2, sin2) * HEAD_DIM ** -0.5).astype(BF16)


def _shared_kv(x, kvg, wk_ref, wv_ref, kng2, cos2, sin2, k_ref, kdup_ref, v_ref):
    h = _rms(x, kvg).astype(BF16)
    kd = jnp.dot(h, wk_ref[...], preferred_element_type=F32)
    v_ref[...] = jnp.dot(h, wv_ref[...], preferred_element_type=F32)
    first_half = lax.broadcasted_iota(I32, (x.shape[0], PAIR), 1) < HEAD_DIM
    slabs = []
    for p in range(N_KV_HEADS):
        cols = slice(p * PAIR, (p + 1) * PAIR)
        slab = _pair_norm_rope(kd[:, cols], kng2, cos2, sin2)
        kdup_ref[:, cols] = slab
        slabs.append(slab)
    for p in range(D_KV // PAIR):
        k_ref[:, p * PAIR:(p + 1) * PAIR] = jnp.where(first_half, slabs[2 * p], slabs[2 * p + 1])


def _combine(x, slab, dest3, ebuf, *, tm, attn_pre=None):
    n = x.shape[0]
    n_tiles = n // tm
    in_specs = [
        pl.BlockSpec((1, 1, TOP_K * tm), lambda i: (0, 0, 0), memory_space=pltpu.SMEM),
        pl.BlockSpec((1, 1, TOP_K * tm), lambda i: (jnp.minimum(i + 1, n_tiles - 1), 0, 0),
                     memory_space=pltpu.SMEM),
        _rows(tm, D_MODEL), _rows(tm, SLAB), pl.BlockSpec(memory_space=pl.ANY),
    ]
    args = [dest3, dest3, x, slab, ebuf]
    out_specs = [_rows(tm, D_MODEL)]
    out_shape = [jax.ShapeDtypeStruct((n, D_MODEL), F32)]
    if attn_pre is not None:
        *weights, cos2, sin2, table_tiles = attn_pre
        table_spec = pl.BlockSpec((tm, PAIR), lambda i: (lax.rem(i, table_tiles), 0))
        in_specs += [_resident(w.shape) for w in weights] + [table_spec, table_spec]
        args += list(weights) + [cos2, sin2]
        out_specs += [_rows(tm, D_KV), _rows(tm, 2 * D_KV), _rows(tm, D_KV), _rows(tm, D_MODEL)]
        out_shape += [jax.ShapeDtypeStruct((n, D_KV), F32), jax.ShapeDtypeStruct((n, 2 * D_KV), F32),
                      jax.ShapeDtypeStruct((n, D_KV), F32), jax.ShapeDtypeStruct((n, D_MODEL), BF16)]
    return pl.pallas_call(
        functools.partial(_combine_kernel, tm=tm, n_tiles=n_tiles, attn_pre=attn_pre is not None),
        grid=(n_tiles,),
        in_specs=in_specs, out_specs=out_specs, out_shape=out_shape,
        scratch_shapes=[pltpu.VMEM((2, TOP_K, tm * TOKEN_SUBLANES, V7X_LANES), F32),
                        pltpu.SemaphoreType.DMA((2,))],
        compiler_params=_cparams(),
        name="moe_combine_attn_pre" if attn_pre is not None else "moe_combine",
    )(*args)


def _moe(xs, g, w_router, b_router, wgu, bgu, wdn, bdn, layer, prompt_attn_pre=None):
    n_all = xs[0].shape[0] + xs[1].shape[0]
    whi = w_router.astype(BF16)
    wlo = (w_router - whi.astype(F32)).astype(BF16)
    h_all, slab_p, slab_s, cnt = _router(xs[0], xs[1], g.reshape(1, D_MODEL), whi, wlo,
                                         b_router.reshape(1, N_EXPERTS), tm=TM_DENSE)
    slabs = [slab_p, slab_s]
    slab_all = jnp.concatenate([s[:, :2 * TOP_K] for s in slabs], axis=0)
    idx = slab_all[:, :TOP_K].astype(I32)
    pos = slab_all[:, TOP_K:].astype(I32)
    counts = cnt[0, :N_EXPERTS].astype(I32)
    rows = MOE_ROWS
    nb = (n_all * TOP_K + N_EXPERTS * (rows - 1)) // rows
    padded = (counts + rows - 1) // rows * rows
    pad_end = jnp.cumsum(padded)
    pad_start = pad_end - padded
    dest = pad_start[idx] + pos
    n_active = (pad_end[-1] // rows).astype(I32)
    blk = jnp.arange(nb, dtype=I32)
    block_e = jnp.sum((pad_end[None, :] <= (blk * rows)[:, None]).astype(I32), axis=1)
    block_e = jnp.minimum(block_e, N_EXPERTS - 1)
    block_e = jnp.where(blk < n_active, block_e, block_e[jnp.maximum(n_active - 1, 0)])

    def tile_table(d, tm):
        n_tiles = -(-d.shape[0] // tm)
        d = jnp.pad(d, ((0, n_tiles * tm - d.shape[0]), (0, 0)))
        return d.reshape(n_tiles, tm, TOP_K).transpose(0, 2, 1).reshape(n_tiles, 1, TOP_K * tm)

    fill_start = jnp.concatenate([pad_start + counts, n_active.reshape(1)])
    xs_tiles = _dispatch(fill_start, padded - counts, tile_table(dest, TM_DISPATCH), h_all,
                         n_all, nb, tm=TM_DISPATCH, rows=rows)
    has_rows = counts > 0
    rank = jnp.cumsum(has_rows.astype(I32)) - 1
    eid = jnp.arange(N_EXPERTS, dtype=I32)
    run_e = jnp.sum(jnp.where(has_rows[None, :] & (rank[None, :] == eid[:, None]), eid[None, :], 0), axis=1)
    run_e = jnp.concatenate([run_e, jnp.sum(has_rows.astype(I32)).reshape(1)]).astype(I32)
    block_run = rank[block_e].astype(I32)
    n_l = wgu.shape[0]
    ebuf = _experts(block_e, n_active.reshape(1), block_run, run_e, xs_tiles,
                    wgu, bgu.reshape(n_l, N_EXPERTS, 1, 2 * D_EXPERT), wdn, bdn.reshape(n_l, N_EXPERTS, 1, D_MODEL),
                    rows=rows, layer=layer)
    outs, off = [], 0
    for x, slab, pre in zip(xs, slabs, (prompt_attn_pre, None)):
        n = x.shape[0]
        tm = min(TM_COMBINE, n)
        outs.append(_combine(x, slab, tile_table(dest[off:off + n], tm), ebuf, tm=tm, attn_pre=pre))
        off += n
    return outs


def _kv_kernel(x_ref, kvg_ref, wk_ref, wv_ref, kng_ref, cos_ref, sin_ref, *refs, with_q):
    if with_q:
        bng_ref, wq_ref, qng_ref, k_ref, kdup_ref, v_ref, q_ref = refs
    else:
        k_ref, kdup_ref, v_ref = refs
    x = x_ref[...]
    cos2, sin2 = cos_ref[...], sin_ref[...]
    _shared_kv(x, kvg_ref[...], wk_ref, wv_ref, kng_ref[...], cos2, sin2, k_ref, kdup_ref, v_ref)
    if with_q:
        hq = _rms(x, bng_ref[...]).astype(BF16)
        q = jnp.dot(hq, wq_ref[...], preferred_element_type=F32)
        for p in range(D_MODEL // PAIR):
            cols = slice(p * PAIR, (p + 1) * PAIR)
            q_ref[:, cols] = _pair_norm_rope(q[:, cols], qng_ref[...], cos2, sin2)


def _kv_proj(x, kvg, wk, wv, kng2, cos2, sin2, q_args, *, tm, table_tiles):
    n = x.shape[0]
    with_q = q_args is not None
    table_spec = pl.BlockSpec((tm, PAIR), lambda i: (lax.rem(i, table_tiles), 0))
    in_specs = [_rows(tm, D_MODEL), _resident(kvg.shape), _resident(wk.shape), _resident(wv.shape),
                _resident(kng2.shape), table_spec, table_spec]
    args = [x, kvg, wk, wv, kng2, cos2, sin2]
    out_specs = [_rows(tm, D_KV), _rows(tm, 2 * D_KV), _rows(tm, D_KV)]
    out_shape = [jax.ShapeDtypeStruct((n, D_KV), F32), jax.ShapeDtypeStruct((n, 2 * D_KV), F32),
                 jax.ShapeDtypeStruct((n, D_KV), F32)]
    if with_q:
        in_specs += [_resident(a.shape) for a in q_args]
        args += list(q_args)
        out_specs.append(_rows(tm, D_MODEL))
        out_shape.append(jax.ShapeDtypeStruct((n, D_MODEL), F32))
    return pl.pallas_call(
        functools.partial(_kv_kernel, with_q=with_q),
        grid=(n // tm,),
        in_specs=in_specs, out_specs=out_specs, out_shape=out_shape,
        compiler_params=_cparams(),
        name="kvq_proj_sample" if with_q else "kv_proj_prompt",
    )(*args)


def _attn_prompt_kernel(sinks_ref, x_ref, q_ref, kc_ref, kp_ref, vc_ref, vp_ref, wo_ref, o_ref, a_scr,
                        *, tm, tiles_per_seq):
    i = pl.program_id(0)

    kj = lax.broadcasted_iota(I32, (2 * WINDOW, WINDOW), 0)
    qi = lax.broadcasted_iota(I32, (2 * WINDOW, WINDOW), 1)
    diff = qi + WINDOW - kj
    band = jnp.logical_and(diff >= 0, diff < WINDOW)
    first_half = lax.broadcasted_iota(I32, (WINDOW, PAIR), 1) < HEAD_DIM
    keep = (jnp.where(first_half, 1.0, 0.0).astype(BF16), jnp.where(first_half, 0.0, 1.0).astype(BF16))
    first_kj = jnp.where(lax.rem(i, tiles_per_seq) == 0, WINDOW, 0)

    for blk in range(tm // WINDOW):
        rows = slice(blk * WINDOW, (blk + 1) * WINDOW)
        if blk == 0:
            kk = jnp.concatenate([kp_ref[...], kc_ref[rows, :]], axis=0).astype(BF16)
            vv = jnp.concatenate([vp_ref[...], vc_ref[rows, :]], axis=0)
            valid = jnp.logical_and(band, kj >= first_kj)
        else:
            win = slice((blk - 1) * WINDOW, (blk + 1) * WINDOW)
            kk = kc_ref[win, :].astype(BF16)
            vv = vc_ref[win, :]
            valid = band
        for kvh in range(N_KV_HEADS):
            kk_h = kk[:, kvh * PAIR:(kvh + 1) * PAIR]
            vcols = slice((kvh // 2) * PAIR, (kvh // 2 + 1) * PAIR)
            vt = vv[:, vcols].T.astype(BF16)
            ch = slice((kvh % 2) * HEAD_DIM, (kvh % 2 + 1) * HEAD_DIM)
            outs = []
            for g in range(Q_PER_KV):
                head = kvh * Q_PER_KV + g
                qm = q_ref[rows, (head // 2) * PAIR:(head // 2 + 1) * PAIR] * keep[head % 2]
                sink = sinks_ref[head]
                st = lax.dot_general(kk_h, qm, (((1,), (1,)), ((), ())), preferred_element_type=F32)
                st = jnp.where(valid, st, -jnp.inf)
                m = jnp.maximum(jnp.max(st, axis=0, keepdims=True), sink)
                p = jnp.exp(st - m)
                den = jnp.sum(p, axis=0, keepdims=True) + jnp.exp(sink - m)
                ot = jnp.dot(vt, p.astype(BF16), preferred_element_type=F32)
                outs.append(ot[ch, :] * (1.0 / den))
            for pq in range(Q_PER_KV // 2):
                pair = (kvh * Q_PER_KV) // 2 + pq
                both = jnp.concatenate([outs[2 * pq], outs[2 * pq + 1]], axis=0)
                a_scr[rows, pair * PAIR:(pair + 1) * PAIR] = both.T.astype(BF16)
    o_ref[...] = x_ref[...] + jnp.dot(a_scr[...], wo_ref[...], preferred_element_type=F32)


def _attn_prompt(x, q, kdup, v, sinks, wo, *, tm, seq):
    n = x.shape[0]
    tiles_per_seq = seq // tm
    blocks_per_tile = tm // WINDOW

    def prev_spec(width):
        return pl.BlockSpec((WINDOW, width), lambda i: (jnp.maximum(i * blocks_per_tile - 1, 0), 0))

    return pl.pallas_call(
        functools.partial(_attn_prompt_kernel, tm=tm, tiles_per_seq=tiles_per_seq),
        grid=(n // tm,),
        in_specs=[pl.BlockSpec(memory_space=pltpu.SMEM), _rows(tm, D_MODEL), _rows(tm, D_MODEL),
                  _rows(tm, 2 * D_KV), prev_spec(2 * D_KV), _rows(tm, D_KV), prev_spec(D_KV), _resident(wo.shape)],
        out_specs=_rows(tm, D_MODEL),
        out_shape=jax.ShapeDtypeStruct((n, D_MODEL), F32),
        scratch_shapes=[pltpu.VMEM((tm, D_MODEL), BF16)],
        compiler_params=_cparams(),
        name="attn_prompt",
    )(sinks, x, q, kdup, kdup, v, v, wo)


def _attn_sample_kernel(q_ref, ck_ref, cv_ref, kn_ref, vn_ref, sink_ref, o_ref, *, per_step, win):
    rowi = lax.broadcasted_iota(I32, (N_HEADS, D_KV), 0)
    lanei = lax.broadcasted_iota(I32, (N_HEADS, D_KV), 1)
    group = rowi >> 2
    own_block = group == (lanei >> 6)
    j = lax.broadcasted_iota(I32, (N_HEADS, win), 1)
    diff = win - j
    valid = jnp.logical_and(diff >= 0, diff < WINDOW)
    sink = sink_ref[...]
    scale = HEAD_DIM ** -0.5
    for b in range(per_step):
        qm = jnp.where(own_block, q_ref[b], 0.0)
        s = lax.dot_general(qm.astype(BF16), ck_ref[b].astype(BF16), (((1,), (1,)), ((), ())),
                            preferred_element_type=F32) * scale
        s = jnp.where(valid, s, -jnp.inf)
        s_new = jnp.sum(qm * kn_ref[b:b + 1, :], axis=-1, keepdims=True) * scale
        m = jnp.maximum(jnp.maximum(jnp.max(s, axis=-1, keepdims=True), s_new), sink)
        p = jnp.exp(s - m)
        p_new = jnp.exp(s_new - m)
        den = jnp.sum(p, axis=-1, keepdims=True) + p_new + jnp.exp(sink - m)
        o = jnp.dot(p.astype(BF16), cv_ref[b].astype(BF16), preferred_element_type=F32) + p_new * vn_ref[b:b + 1, :]
        o = o * (1.0 / den)
        res = o
        for sft in range(1, N_KV_HEADS):
            res = jnp.where(group == sft, pltpu.roll(o, D_KV - sft * HEAD_DIM, 1), res)
        o_ref[b] = res[:, :HEAD_DIM]


def _attn_sample(q4, ck, cv, kn, vn, sink_col, *, per_step):
    nb, win = ck.shape[0], ck.shape[1]
    return pl.pallas_call(
        functools.partial(_attn_sample_kernel, per_step=per_step, win=win),
        grid=(nb // per_step,),
        in_specs=[pl.BlockSpec((per_step, N_HEADS, D_KV), lambda i: (i, 0, 0)),
                  pl.BlockSpec((per_step, win, D_KV), lambda i: (i, 0, 0)),
                  pl.BlockSpec((per_step, win, D_KV), lambda i: (i, 0, 0)),
                  _rows(per_step, D_KV), _rows(per_step, D_KV), _resident(sink_col.shape)],
        out_specs=pl.BlockSpec((per_step, N_HEADS, HEAD_DIM), lambda i: (i, 0, 0)),
        out_shape=jax.ShapeDtypeStruct((nb, N_HEADS, HEAD_DIM), F32),
        compiler_params=_cparams(),
        name="attn_sample",
    )(q4, ck, cv, kn, vn, sink_col)


def _residual_proj_kernel(x_ref, a_ref, w_ref, o_ref):
    o_ref[...] = x_ref[...] + jnp.dot(a_ref[...].astype(BF16), w_ref[...], preferred_element_type=F32)


def _residual_proj(x, a, w):
    n = x.shape[0]
    return pl.pallas_call(
        _residual_proj_kernel,
        grid=(1,),
        in_specs=[_rows(n, D_MODEL), _rows(n, a.shape[1]), _resident(w.shape)],
        out_specs=_rows(n, D_MODEL),
        out_shape=jax.ShapeDtypeStruct((n, D_MODEL), F32),
        compiler_params=_cparams(),
        name="attn_out_sample",
    )(x, a, w)


def _rope_tables(pos):
    half = ROPE_DIM // 2
    inv_freq = jnp.power(jnp.float32(ROPE_THETA), -jnp.arange(half, dtype=F32) / half)
    ang = pos.astype(F32)[:, None] * inv_freq[None, :]
    cos, sin = jnp.cos(ang), jnp.sin(ang)
    n = pos.shape[0]
    rest = HEAD_DIM - ROPE_DIM
    cos_h = jnp.concatenate([cos, cos, jnp.ones((n, rest), F32)], axis=1)
    sin_h = jnp.concatenate([-sin, sin, jnp.zeros((n, rest), F32)], axis=1)
    return jnp.tile(cos_h, (1, 2)), jnp.tile(sin_h, (1, 2))


def kernel(x_prompt, x_sample, cache_k, cache_v, a_norm_g, a_w_in, a_ln_g, a_ln_b, a_w_s, a_b_s, a_w_out, kv_norm_g, w_k, w_v, k_norm_g, b_norm_g, w_q, q_norm_g, sinks, w_o, ffn_norm_g, w_router, b_router, w_gu, b_gu, w_down, b_down):
    bsz, seq, _ = x_prompt.shape
    dec_b, dec_seq, _ = x_sample.shape
    assert dec_seq == 1 and seq % TM_DENSE == 0 and a_norm_g.shape[0] == 1 and b_norm_g.shape[0] == 1
    win = cache_k.shape[1]
    n_p, n_s = bsz * seq, dec_b * dec_seq
    xp = x_prompt.reshape(n_p, D_MODEL)
    xs = x_sample.reshape(n_s, D_MODEL)

    ng = a_norm_g[0].reshape(1, D_MODEL)
    win_bf = a_w_in[0].astype(BF16)
    wout_bf = a_w_out[0].astype(BF16)
    lng = a_ln_g[0].reshape(1, D_GATE)
    lnb = a_ln_b[0].reshape(1, D_GATE)
    bias_full = jnp.repeat(a_b_s[0].T, GROUP_DIM_A, axis=1)
    diag_row = jnp.repeat(a_w_s[0][:, 0, 0], GROUP_DIM_A).reshape(1, D_GATE)
    (xp,) = _mixer_a(xp, ng, win_bf, lng, lnb, a_w_s[0], bias_full, wout_bf, tm=TM_DENSE, single_token_chunks=False)
    xs, v_rows = _mixer_a(xs, ng, win_bf, lng, lnb, diag_row, bias_full[0:1], wout_bf, tm=n_s,
                          single_token_chunks=True)

    kvg = kv_norm_g.reshape(1, D_MODEL)
    bng = b_norm_g[0].reshape(1, D_MODEL)
    w_k_dup = jnp.repeat(w_k.reshape(D_MODEL, N_KV_HEADS, 1, HEAD_DIM), 2, axis=2).reshape(D_MODEL, 2 * D_KV)
    wk_bf, wv_bf, wq_bf, wo_bf = (w.astype(BF16) for w in (w_k_dup, w_v, w_q[0], w_o[0]))
    kng2 = jnp.tile(k_norm_g, 2).reshape(1, PAIR)
    qng2 = jnp.tile(q_norm_g[0], 2).reshape(1, PAIR)
    cos_p, sin_p = _rope_tables(jnp.arange(seq, dtype=I32))
    cos_s, sin_s = _rope_tables(jnp.full((n_s,), PAST_LEN, I32))
    attn_pre = (kvg, wk_bf, wv_bf, kng2, bng, wq_bf, qng2, cos_p, sin_p, seq // TM_COMBINE)
    (xp, k_p, kdup_p, v_p, q_p), (xs,) = _moe([xp, xs], ffn_norm_g[0], w_router[0], b_router[0], w_gu, b_gu,
                                              w_down, b_down, 0, prompt_attn_pre=attn_pre)

    k_s, _, v_s, q_s = _kv_proj(xs, kvg, wk_bf, wv_bf, kng2, cos_s, sin_s, (bng, wq_bf, qng2), tm=n_s,
                                table_tiles=1)
    xp = _attn_prompt(xp, q_p, kdup_p, v_p, sinks[0], wo_bf, tm=TM_DENSE, seq=seq)
    q4 = jnp.tile(q_s.reshape(n_s, N_HEADS, HEAD_DIM), (1, 1, N_KV_HEADS))
    attn_s = _attn_sample(q4, cache_k.reshape(dec_b, win, D_KV), cache_v.reshape(dec_b, win, D_KV), k_s, v_s,
                          sinks[0].reshape(N_HEADS, 1), per_step=8)
    xs = _residual_proj(xs, attn_s.reshape(n_s, D_MODEL), wo_bf)
    (xp,), (xs,) = _moe([xp, xs], ffn_norm_g[1], w_router[1], b_router[1], w_gu, b_gu, w_down, b_down, 1)

    k_p4 = k_p.reshape(bsz, seq, N_KV_HEADS, HEAD_DIM)
    v_p4 = v_p.reshape(bsz, seq, N_KV_HEADS, HEAD_DIM)
    return (xp.reshape(bsz, seq, D_MODEL), xs.reshape(dec_b, dec_seq, D_MODEL),
            v_rows.reshape(1, dec_b, dec_seq, D_GATE), k_p4[:, -WINDOW:], v_p4[:, -WINDOW:],
            k_s.reshape(dec_b, dec_seq, N_KV_HEADS, HEAD_DIM), v_s.reshape(dec_b, dec_seq, N_KV_HEADS, HEAD_DIM))
```

```python
import functools

import jax
import jax.numpy as jnp
from jax import lax
from jax.experimental import pallas as pl
from jax.experimental.pallas import tpu as pltpu

F32, BF16, I32 = jnp.float32, jnp.bfloat16, jnp.int32

D_MODEL = 1024
CHUNK = 128
D_GATE = 2 * D_MODEL
N_GROUPS_A = 8
GROUP_DIM_A = D_GATE // N_GROUPS_A
HEAD_DIM = 64
N_HEADS = D_MODEL // HEAD_DIM
N_KV_HEADS = N_HEADS // 4
Q_PER_KV = N_HEADS // N_KV_HEADS
D_KV = N_KV_HEADS * HEAD_DIM
WINDOW = 128
ROPE_DIM = HEAD_DIM // 4
ROPE_THETA = 500000.0
N_EXPERTS = 32
TOP_K = 4
D_EXPERT = D_MODEL
SWIGLU_LIMIT = 7.0
SWIGLU_ALPHA = 1.702
EPS = 1e-6
PAST_LEN = 8192

V7X_LANES = 128
V7X_VMEM_BYTES = 64 * 1024 * 1024
VMEM_LIMIT_BYTES = V7X_VMEM_BYTES - 8 * 1024 * 1024

TM_DENSE = 512
TM_COMBINE = 256
MOE_ROWS = 512
TM_DISPATCH = 256
FILL_BITS = MOE_ROWS.bit_length() - 1
TOKEN_SUBLANES = D_MODEL // V7X_LANES
SLAB = V7X_LANES
PAIR = 2 * HEAD_DIM
INV_SQRT2 = 0.7071067811865476


def _cparams():
    return pltpu.CompilerParams(dimension_semantics=("arbitrary",), vmem_limit_bytes=VMEM_LIMIT_BYTES)


def _resident(shape):
    zeros = (0,) * len(shape)
    return pl.BlockSpec(shape, lambda i, *_: zeros, pipeline_mode=pl.Buffered(1))


def _rows(tm, width, offset=0):
    return pl.BlockSpec((tm, width), lambda i, *_: (i + offset, 0))


def _rms(x, g):
    return x * lax.rsqrt(jnp.mean(x * x, axis=-1, keepdims=True) + EPS) * g


def _pair_norm_rope(x2, g2, cos2, sin2):
    lane = lax.broadcasted_iota(I32, x2.shape, 1)
    first = lane < HEAD_DIM
    sq = x2 * x2
    s0 = jnp.sum(jnp.where(first, sq, 0.0), axis=-1, keepdims=True)
    s1 = jnp.sum(jnp.where(first, 0.0, sq), axis=-1, keepdims=True)
    ms = jnp.where(first, s0, s1) * (1.0 / HEAD_DIM)
    y = x2 * lax.rsqrt(ms + EPS) * g2
    half = ROPE_DIM // 2
    up = pltpu.roll(y, PAIR - half, 1)
    dn = pltpu.roll(y, half, 1)
    partner = jnp.where((lane & (HEAD_DIM - 1)) < half, up, dn)
    return y * cos2 + partner * sin2


def _mixer_a_kernel(x_ref, ng_ref, win_ref, lng_ref, lnb_ref, mix_ref, bias_ref, wout_ref, *refs,
                    tm, single_token_chunks):
    if single_token_chunks:
        o_ref, vout_ref, u_scr, v_scr, p_scr = refs
    else:
        o_ref, u_scr, v_scr, p_scr = refs
    x = x_ref[...]
    h = _rms(x, ng_ref[...]).astype(BF16)
    nc = 512
    for c in range(2 * D_GATE // nc):
        z = jnp.dot(h, win_ref[:, c * nc:(c + 1) * nc], preferred_element_type=F32)
        z = 0.5 * z * (1.0 + lax.erf(z * INV_SQRT2))
        if c < D_GATE // nc:
            u_scr[:, c * nc:(c + 1) * nc] = z
        else:
            v_scr[:, (c - D_GATE // nc) * nc:(c - D_GATE // nc + 1) * nc] = z
    v = v_scr[...]
    vc = v - jnp.mean(v, axis=-1, keepdims=True)
    vn = vc * lax.rsqrt(jnp.mean(vc * vc, axis=-1, keepdims=True) + EPS) * lng_ref[...] + lnb_ref[...]
    if single_token_chunks:
        vout_ref[...] = vn
        p_scr[...] = (u_scr[...] * (vn * mix_ref[...] + bias_ref[...])).astype(BF16)
    else:
        v_scr[...] = vn
        row = lax.broadcasted_iota(I32, (CHUNK, CHUNK), 0)
        col = lax.broadcasted_iota(I32, (CHUNK, CHUNK), 1)
        for g in range(N_GROUPS_A):
            wc = jnp.where(row >= col, mix_ref[g], 0.0).astype(BF16)
            cols = slice(g * GROUP_DIM_A, (g + 1) * GROUP_DIM_A)
            for c in range(tm // CHUNK):
                rows = slice(c * CHUNK, (c + 1) * CHUNK)
                mixed = jnp.dot(wc, v_scr[rows, cols].astype(BF16), preferred_element_type=F32)
                mixed = mixed + bias_ref[:, cols]
                p_scr[rows, cols] = (u_scr[rows, cols] * mixed).astype(BF16)
    o_ref[...] = x + jnp.dot(p_scr[...], wout_ref[...], preferred_element_type=F32)


def _mixer_a(x, ng, win, lng, lnb, mix, bias, wout, *, tm, single_token_chunks):
    n = x.shape[0]
    out_shape = [jax.ShapeDtypeStruct((n, D_MODEL), F32)]
    out_specs = [_rows(tm, D_MODEL)]
    if single_token_chunks:
        out_shape.append(jax.ShapeDtypeStruct((n, D_GATE), F32))
        out_specs.append(_rows(tm, D_GATE))
    return pl.pallas_call(
        functools.partial(_mixer_a_kernel, tm=tm, single_token_chunks=single_token_chunks),
        grid=(n // tm,),
        in_specs=[_rows(tm, D_MODEL), _resident(ng.shape), _resident(win.shape), _resident(lng.shape),
                  _resident(lnb.shape), _resident(mix.shape), _resident(bias.shape), _resident(wout.shape)],
        out_specs=out_specs,
        out_shape=out_shape,
        scratch_shapes=[pltpu.VMEM((tm, D_GATE), F32), pltpu.VMEM((tm, D_GATE), F32),
                        pltpu.VMEM((tm, D_GATE), BF16)],
        compiler_params=_cparams(),
        name="mixer_a_sample" if single_token_chunks else "mixer_a_prompt",
    )(x, ng, win, lng, lnb, mix, bias, wout)


def _route_rows(x, g, whi, wlo, bias, run_scr):
    tm = x.shape[0]
    h = _rms(x, g)
    hh = h.astype(BF16)
    hl = (h - hh.astype(F32)).astype(BF16)
    logits = (jnp.dot(hh, whi, preferred_element_type=F32) + jnp.dot(hh, wlo, preferred_element_type=F32)
              + jnp.dot(hl, whi, preferred_element_type=F32) + bias)
    e = lax.broadcasted_iota(I32, (tm, N_EXPERTS), 1).astype(F32)
    hot, vals, idxs = [], [], []
    for _ in range(TOP_K):
        m = jnp.max(logits, axis=-1, keepdims=True)
        idx = jnp.min(jnp.where(logits == m, e, float(N_EXPERTS)), axis=-1, keepdims=True)
        oh = e == idx
        hot.append(oh)
        vals.append(m)
        idxs.append(idx)
        logits = jnp.where(oh, -jnp.inf, logits)
    ex = [jnp.exp(v - vals[0]) for v in vals]
    den = ex[0] + ex[1] + ex[2] + ex[3]
    picked = jnp.zeros((tm, N_EXPERTS), F32)
    for oh in hot:
        picked = picked + jnp.where(oh, 1.0, 0.0)
    r = lax.broadcasted_iota(I32, (tm, tm), 0)
    c = lax.broadcasted_iota(I32, (tm, tm), 1)
    before = jnp.where(c < r, 1.0, 0.0).astype(BF16)
    run = run_scr[0:1, 0:N_EXPERTS]
    pos = jnp.dot(before, picked.astype(BF16), preferred_element_type=F32) + run
    run_scr[0:1, 0:N_EXPERTS] = run + jnp.sum(picked, axis=0, keepdims=True)
    lane = lax.broadcasted_iota(I32, (tm, SLAB), 1)
    slab = jnp.zeros((tm, SLAB), F32)
    for k in range(TOP_K):
        pos_k = jnp.sum(jnp.where(hot[k], pos, 0.0), axis=-1, keepdims=True)
        slab = jnp.where(lane == k, idxs[k], slab)
        slab = jnp.where(lane == TOP_K + k, pos_k, slab)
        slab = jnp.where(lane == 2 * TOP_K + k, ex[k] / den, slab)
    return h, slab


def _router_kernel(xp_ref, xs_ref, g_ref, whi_ref, wlo_ref, b_ref, h_ref, slabp_ref, slabs_ref, cnt_ref, run_scr,
                   *, n_tiles, n_s):
    i = pl.program_id(0)

    @pl.when(i == 0)
    def _():
        run_scr[...] = jnp.zeros_like(run_scr)

    @pl.when(i < n_tiles)
    def _():
        h, slab = _route_rows(xp_ref[...], g_ref[...], whi_ref[...], wlo_ref[...], b_ref[...], run_scr)
        _store_token_tiles(h_ref, h, h.shape[0])
        slabp_ref[...] = slab

    @pl.when(i == n_tiles)
    def _():
        h, slab = _route_rows(xs_ref[...], g_ref[...], whi_ref[...], wlo_ref[...], b_ref[...], run_scr)
        h_ref[n_s * TOKEN_SUBLANES:, :] = jnp.zeros((h_ref.shape[0] - n_s * TOKEN_SUBLANES, V7X_LANES), F32)
        _store_token_tiles(h_ref, h, n_s)
        slabs_ref[...] = slab

    cnt_ref[...] = run_scr[...]


def _store_token_tiles(ref, x, n):
    for s in range(TOKEN_SUBLANES):
        ref[pl.ds(s, n, stride=TOKEN_SUBLANES), :] = x[:, s * V7X_LANES:(s + 1) * V7X_LANES]


def _load_token_tiles(ref, n):
    return jnp.concatenate([ref[pl.ds(s, n, stride=TOKEN_SUBLANES), :] for s in range(TOKEN_SUBLANES)], axis=1)


def _router(xp, xs, g, whi, wlo, b, *, tm):
    n_p, n_s = xp.shape[0], xs.shape[0]
    n_tiles = n_p // tm
    last = n_tiles - 1
    return pl.pallas_call(
        functools.partial(_router_kernel, n_tiles=n_tiles, n_s=n_s),
        grid=(n_tiles + 1,),
        in_specs=[pl.BlockSpec((tm, D_MODEL), lambda i: (jnp.minimum(i, last), 0)), _resident(xs.shape),
                  _resident(g.shape), _resident(whi.shape), _resident(wlo.shape), _resident(b.shape)],
        out_specs=[_rows(tm * TOKEN_SUBLANES, V7X_LANES),
                   pl.BlockSpec((tm, SLAB), lambda i: (jnp.minimum(i, last), 0)),
                   pl.BlockSpec((n_s, SLAB), lambda i: (0, 0)), pl.BlockSpec((8, V7X_LANES), lambda i: (0, 0))],
        out_shape=[jax.ShapeDtypeStruct(((n_p + tm) * TOKEN_SUBLANES, V7X_LANES), F32),
                   jax.ShapeDtypeStruct((n_p, SLAB), F32),
                   jax.ShapeDtypeStruct((n_s, SLAB), F32), jax.ShapeDtypeStruct((8, V7X_LANES), F32)],
        scratch_shapes=[pltpu.VMEM((8, V7X_LANES), F32)],
        compiler_params=_cparams(),
        name="moe_router",
    )(xp, xs, g, whi, wlo, b)


def _dispatch_kernel(fill_start_ref, fill_len_ref, dest_ref, h_ref, h_hbm, xs_ref, zbuf, sem, fill_sem,
                     *, tm, n_tiles, n_last, rows, n_blocks):
    i = pl.program_id(0)
    ts = TOKEN_SUBLANES

    def tile_copy(r, k):
        d = dest_ref[0, 0, k * tm + r]
        if k < TOP_K // 2:
            src = h_ref.at[pl.ds(pl.multiple_of(r * ts, ts), ts)]
        else:
            src = h_hbm.at[pl.ds(pl.multiple_of((i * tm + r) * ts, ts), ts)]
        dst = xs_ref.at[pl.ds(pl.multiple_of(d * ts, ts), ts)]
        return pltpu.make_async_copy(src, dst, sem)

    def scatter(n):
        def body(r, carry):
            for k in range(TOP_K):
                tile_copy(r, k).start(priority=k % 2)
            return carry
        lax.fori_loop(0, n, body, 0, unroll=4)
        for k in range(TOP_K):
            pltpu.make_async_copy(h_ref.at[pl.ds(0, n * ts)], xs_ref.at[pl.ds(0, n * ts)], sem).wait()

    def fill_copy(e, bit):
        size = 1 << bit
        length = fill_len_ref[e]
        done = length - (length & (2 * size - 1))
        off = pl.multiple_of((fill_start_ref[e] + done) * ts, ts)
        return (length & size) != 0, pltpu.make_async_copy(zbuf.at[pl.ds(0, size * ts)],
                                                            xs_ref.at[pl.ds(off, size * ts)], fill_sem)

    def tail_copy(blk):
        off = pl.multiple_of(blk * (rows * ts), rows * ts)
        return pltpu.make_async_copy(zbuf, xs_ref.at[pl.ds(off, rows * ts)], fill_sem)

    def for_each_fill(act):
        for e in range(N_EXPERTS):
            for bit in range(FILL_BITS):
                go, cp = fill_copy(e, bit)

                @pl.when(go)
                def _():
                    act(cp)

        def body(blk, carry):
            act(tail_copy(blk))
            return carry
        lax.fori_loop(fill_start_ref[N_EXPERTS], n_blocks, body, 0)

    @pl.when(i == 0)
    def _():
        zbuf[...] = jnp.zeros_like(zbuf)
        for_each_fill(lambda cp: cp.start())

    @pl.when(i < n_tiles - 1)
    def _():
        scatter(tm)

    @pl.when(i == n_tiles - 1)
    def _():
        scatter(n_last)
        for_each_fill(lambda cp: cp.wait())


def _dispatch(fill_start, fill_len, dest3, h_tiles, n_all, n_blocks, *, tm, rows):
    n_tiles = dest3.shape[0]
    n_last = n_all - (n_tiles - 1) * tm
    ts = TOKEN_SUBLANES
    n_slots = n_blocks * rows
    grid_spec = pltpu.PrefetchScalarGridSpec(
        num_scalar_prefetch=2,
        grid=(n_tiles,),
        in_specs=[pl.BlockSpec((1, 1, TOP_K * tm), lambda i, *_: (i, 0, 0), memory_space=pltpu.SMEM),
                  _rows(tm * ts, V7X_LANES), pl.BlockSpec(memory_space=pl.ANY)],
        out_specs=pl.BlockSpec(memory_space=pl.ANY),
        scratch_shapes=[pltpu.VMEM((rows * ts, V7X_LANES), F32),
                        pltpu.SemaphoreType.DMA, pltpu.SemaphoreType.DMA],
    )
    return pl.pallas_call(
        functools.partial(_dispatch_kernel, tm=tm, n_tiles=n_tiles, n_last=n_last, rows=rows, n_blocks=n_blocks),
        grid_spec=grid_spec,
        out_shape=jax.ShapeDtypeStruct((n_slots * ts, V7X_LANES), F32),
        compiler_params=_cparams(),
        name="moe_dispatch",
    )(fill_start, fill_len, dest3, h_tiles, h_tiles)


def _expert_kernel(be_ref, nact_ref, run_ref, rune_ref, x_ref, wgu_hbm, bgu_ref, wdn_hbm, bdn_ref, o_ref,
                   wgu_f32, wdn_f32, wsem, wgu_bf, wdn_bf, *, rows, layer):
    b = pl.program_id(0)
    nact = nact_ref[0]

    def weight_copies(r, slot):
        e = rune_ref[r]
        return (pltpu.make_async_copy(wgu_hbm.at[layer, e], wgu_f32.at[slot], wsem.at[0, slot]),
                pltpu.make_async_copy(wdn_hbm.at[layer, e], wdn_f32.at[slot], wsem.at[1, slot]))

    @pl.when(b == 0)
    def _():
        for cp in weight_copies(0, 0):
            cp.start()

    @pl.when(b < nact)
    def _():
        r = run_ref[b]
        slot = lax.rem(r, 2)
        changed = jnp.logical_or(b == 0, r != run_ref[jnp.maximum(b - 1, 0)])

        @pl.when(changed)
        def _():
            for cp in weight_copies(r, slot):
                cp.wait()

            @pl.when(r + 1 < rune_ref[N_EXPERTS])
            def _():
                for cp in weight_copies(r + 1, 1 - slot):
                    cp.start()

            step = 256
            for c in range(D_MODEL // step):
                wgu_bf[c * step:(c + 1) * step, :] = wgu_f32[slot, c * step:(c + 1) * step, :].astype(BF16)
                wdn_bf[c * step:(c + 1) * step, :] = wdn_f32[slot, c * step:(c + 1) * step, :].astype(BF16)

        xb = _load_token_tiles(x_ref, rows).astype(BF16)
        nc = 512
        out = None
        for c in range(D_EXPERT // nc):
            gcols = slice(c * nc, (c + 1) * nc)
            ucols = slice(D_EXPERT + c * nc, D_EXPERT + (c + 1) * nc)
            gate = jnp.dot(xb, wgu_bf[:, gcols], preferred_element_type=F32) + bgu_ref[:, gcols]
            up = jnp.dot(xb, wgu_bf[:, ucols], preferred_element_type=F32) + bgu_ref[:, ucols]
            gate = jnp.minimum(gate, SWIGLU_LIMIT)
            up = jnp.clip(up, -SWIGLU_LIMIT, SWIGLU_LIMIT)
            act = ((up + 1.0) * gate * (1.0 / (1.0 + jnp.exp(-SWIGLU_ALPHA * gate)))).astype(BF16)
            part = jnp.dot(act, wdn_bf[gcols, :], preferred_element_type=F32)
            out = part if out is None else out + part
        _store_token_tiles(o_ref, out + bdn_ref[...], rows)

    @pl.when(b >= nact)
    def _():
        o_ref[...] = jnp.zeros_like(o_ref)


def _experts(block_e, n_active, block_run, run_e, xs_tiles, wgu, bgu, wdn, bdn, *, rows, layer):
    ts = TOKEN_SUBLANES
    nb = xs_tiles.shape[0] // (rows * ts)
    grid_spec = pltpu.PrefetchScalarGridSpec(
        num_scalar_prefetch=4,
        grid=(nb,),
        in_specs=[
            pl.BlockSpec((rows * ts, V7X_LANES), lambda b, be, na, *_: (jnp.minimum(b, na[0] - 1), 0)),
            pl.BlockSpec(memory_space=pl.ANY),
            pl.BlockSpec((None, None, 1, 2 * D_EXPERT), lambda b, be, *_: (layer, be[b], 0, 0)),
            pl.BlockSpec(memory_space=pl.ANY),
            pl.BlockSpec((None, None, 1, D_MODEL), lambda b, be, *_: (layer, be[b], 0, 0)),
        ],
        out_specs=pl.BlockSpec((rows * ts, V7X_LANES), lambda b, *_: (b, 0)),
        scratch_shapes=[pltpu.VMEM((2, D_MODEL, 2 * D_EXPERT), F32), pltpu.VMEM((2, D_EXPERT, D_MODEL), F32),
                        pltpu.SemaphoreType.DMA((2, 2)),
                        pltpu.VMEM((D_MODEL, 2 * D_EXPERT), BF16), pltpu.VMEM((D_EXPERT, D_MODEL), BF16)],
    )
    return pl.pallas_call(
        functools.partial(_expert_kernel, rows=rows, layer=layer),
        grid_spec=grid_spec,
        out_shape=jax.ShapeDtypeStruct(xs_tiles.shape, F32),
        compiler_params=_cparams(),
        name="moe_experts",
    )(block_e, n_active, block_run, run_e, xs_tiles, wgu, bgu, wdn, bdn)


def _combine_kernel(dest_first_ref, dest_next_ref, x_ref, slab_ref, ebuf_ref, *refs, tm, n_tiles, attn_pre):
    if attn_pre:
        (kvg_ref, wk_ref, wv_ref, kng_ref, bng_ref, wq_ref, qng_ref, cos_ref, sin_ref,
         o_ref, k_ref, kdup_ref, v_ref, q_ref, rbuf, sem) = refs
    else:
        o_ref, rbuf, sem = refs
    i = pl.program_id(0)
    slot = lax.rem(i, 2)
    ts = TOKEN_SUBLANES

    def gather(dest_ref, s):
        def body(r, carry):
            for k in range(TOP_K):
                d = dest_ref[0, 0, k * tm + r]
                src = ebuf_ref.at[pl.ds(pl.multiple_of(d * ts, ts), ts)]
                dst = rbuf.at[s, k, pl.ds(pl.multiple_of(r * ts, ts), ts)]
                pltpu.make_async_copy(src, dst, sem.at[s]).start(priority=k % 2)
            return carry
        lax.fori_loop(0, tm, body, 0, unroll=4)

    @pl.when(i == 0)
    def _():
        gather(dest_first_ref, 0)

    @pl.when(i + 1 < n_tiles)
    def _():
        gather(dest_next_ref, 1 - slot)

    for k in range(TOP_K):
        pltpu.make_async_copy(ebuf_ref.at[pl.ds(0, tm * ts)], rbuf.at[slot, k], sem.at[slot]).wait()
    slab = slab_ref[...]
    gates = [slab[:, 2 * TOP_K + k:2 * TOP_K + k + 1] for k in range(TOP_K)]
    for s in range(ts):
        cols = slice(s * V7X_LANES, (s + 1) * V7X_LANES)
        acc = x_ref[:, cols]
        for k in range(TOP_K):
            acc = acc + gates[k] * rbuf[slot, k, pl.ds(s, tm, stride=ts), :]
        o_ref[:, cols] = acc
    if attn_pre:
        x2 = o_ref[...]
        cos2, sin2 = cos_ref[...], sin_ref[...]
        _shared_kv(x2, kvg_ref[...], wk_ref, wv_ref, kng_ref[...], cos2, sin2, k_ref, kdup_ref, v_ref)
        hq = _rms(x2, bng_ref[...]).astype(BF16)
        q = jnp.dot(hq, wq_ref[...], preferred_element_type=F32)
        for p in range(D_MODEL // PAIR):
            cols = slice(p * PAIR, (p + 1) * PAIR)
            q_ref[:, cols] = (_pair_norm_rope(q[:, cols], qng_ref[...], cos2, sin2) * HEAD_DIM ** -0.5).astype(BF16)


def _shared_kv(x, kvg, wk_ref, wv_ref, kng2, cos2, sin2, k_ref, kdup_ref, v_ref):
    h = _rms(x, kvg).astype(BF16)
    kd = jnp.dot(h, wk_ref[...], preferred_element_type=F32)
    v_ref[...] = jnp.dot(h, wv_ref[...], preferred_element_type=F32)
    first_half = lax.broadcasted_iota(I32, (x.shape[0], PAIR), 1) < HEAD_DIM
    slabs = []
    for p in range(N_KV_HEADS):
        cols = slice(p * PAIR, (p + 1) * PAIR)
        slab = _pair_norm_rope(kd[:, cols], kng2, cos2, sin2)
        kdup_ref[:, cols] = slab
        slabs.append(slab)
    for p in range(D_KV // PAIR):
        k_ref[:, p * PAIR:(p + 1) * PAIR] = jnp.where(first_half, slabs[2 * p], slabs[2 * p + 1])


def _combine(x, slab, dest3, ebuf, *, tm, attn_pre=None):
    n = x.shape[0]
    n_tiles = n // tm
    in_specs = [
        pl.BlockSpec((1, 1, TOP_K * tm), lambda i: (0, 0, 0), memory_space=pltpu.SMEM),
        pl.BlockSpec((1, 1, TOP_K * tm), lambda i: (jnp.minimum(i + 1, n_tiles - 1), 0, 0),
                     memory_space=pltpu.SMEM),
        _rows(tm, D_MODEL), _rows(tm, SLAB), pl.BlockSpec(memory_space=pl.ANY),
    ]
    args = [dest3, dest3, x, slab, ebuf]
    out_specs = [_rows(tm, D_MODEL)]
    out_shape = [jax.ShapeDtypeStruct((n, D_MODEL), F32)]
    if attn_pre is not None:
        *weights, cos2, sin2, table_tiles = attn_pre
        table_spec = pl.BlockSpec((tm, PAIR), lambda i: (lax.rem(i, table_tiles), 0))
        in_specs += [_resident(w.shape) for w in weights] + [table_spec, table_spec]
        args += list(weights) + [cos2, sin2]
        out_specs += [_rows(tm, D_KV), _rows(tm, 2 * D_KV), _rows(tm, D_KV), _rows(tm, D_MODEL)]
        out_shape += [jax.ShapeDtypeStruct((n, D_KV), F32), jax.ShapeDtypeStruct((n, 2 * D_KV), F32),
                      jax.ShapeDtypeStruct((n, D_KV), F32), jax.ShapeDtypeStruct((n, D_MODEL), BF16)]
    return pl.pallas_call(
        functools.partial(_combine_kernel, tm=tm, n_tiles=n_tiles, attn_pre=attn_pre is not None),
        grid=(n_tiles,),
        in_specs=in_specs, out_specs=out_specs, out_shape=out_shape,
        scratch_shapes=[pltpu.VMEM((2, TOP_K, tm * TOKEN_SUBLANES, V7X_LANES), F32),
                        pltpu.SemaphoreType.DMA((2,))],
        compiler_params=_cparams(),
        name="moe_combine_attn_pre" if attn_pre is not None else "moe_combine",
    )(*args)


def _moe(xs, g, w_router, b_router, wgu, bgu, wdn, bdn, layer, prompt_attn_pre=None):
    n_all = xs[0].shape[0] + xs[1].shape[0]
    whi = w_router.astype(BF16)
    wlo = (w_router - whi.astype(F32)).astype(BF16)
    h_all, slab_p, slab_s, cnt = _router(xs[0], xs[1], g.reshape(1, D_MODEL), whi, wlo,
                                         b_router.reshape(1, N_EXPERTS), tm=TM_DENSE)
    slabs = [slab_p, slab_s]
    slab_all = jnp.concatenate([s[:, :2 * TOP_K] for s in slabs], axis=0)
    idx = slab_all[:, :TOP_K].astype(I32)
    pos = slab_all[:, TOP_K:].astype(I32)
    counts = cnt[0, :N_EXPERTS].astype(I32)
    rows = MOE_ROWS
    nb = (n_all * TOP_K + N_EXPERTS * (rows - 1)) // rows
    padded = (counts + rows - 1) // rows * rows
    pad_end = jnp.cumsum(padded)
    pad_start = pad_end - padded
    eid = jnp.arange(N_EXPERTS, dtype=I32)
    has_rows = counts > 0
    dest = pos + jnp.sum(jnp.where(idx[..., None] == eid, pad_start, 0), axis=-1)
    n_active = (pad_end[-1] // rows).astype(I32)
    blk = jnp.arange(nb, dtype=I32)
    block_e = jnp.sum((pad_end[None, :] <= (blk * rows)[:, None]).astype(I32), axis=1)
    last_e = jnp.max(jnp.where(has_rows, eid, 0))
    block_e = jnp.where(blk < n_active, jnp.minimum(block_e, N_EXPERTS - 1), last_e)

    def tile_table(d, tm):
        n_tiles = -(-d.shape[0] // tm)
        d = jnp.pad(d, ((0, n_tiles * tm - d.shape[0]), (0, 0)))
        return d.reshape(n_tiles, tm, TOP_K).transpose(0, 2, 1).reshape(n_tiles, 1, TOP_K * tm)

    fill_start = jnp.concatenate([pad_start + counts, n_active.reshape(1)])
    xs_tiles = _dispatch(fill_start, padded - counts, tile_table(dest, TM_DISPATCH), h_all,
                         n_all, nb, tm=TM_DISPATCH, rows=rows)
    rank = jnp.cumsum(has_rows.astype(I32)) - 1
    run_e = jnp.sum(jnp.where(has_rows[None, :] & (rank[None, :] == eid[:, None]), eid[None, :], 0), axis=1)
    run_e = jnp.concatenate([run_e, jnp.sum(has_rows.astype(I32)).reshape(1)]).astype(I32)
    block_run = jnp.sum(jnp.where(block_e[:, None] == eid, rank, 0), axis=1).astype(I32)
    n_l = wgu.shape[0]
    ebuf = _experts(block_e, n_active.reshape(1), block_run, run_e, xs_tiles,
                    wgu, bgu.reshape(n_l, N_EXPERTS, 1, 2 * D_EXPERT), wdn, bdn.reshape(n_l, N_EXPERTS, 1, D_MODEL),
                    rows=rows, layer=layer)
    outs, off = [], 0
    for x, slab, pre in zip(xs, slabs, (prompt_attn_pre, None)):
        n = x.shape[0]
        tm = min(TM_COMBINE, n)
        outs.append(_combine(x, slab, tile_table(dest[off:off + n], tm), ebuf, tm=tm, attn_pre=pre))
        off += n
    return outs


def _kv_kernel(x_ref, kvg_ref, wk_ref, wv_ref, kng_ref, cos_ref, sin_ref, *refs, with_q):
    if with_q:
        bng_ref, wq_ref, qng_ref, k_ref, kdup_ref, v_ref, q_ref = refs
    else:
        k_ref, kdup_ref, v_ref = refs
    x = x_ref[...]
    cos2, sin2 = cos_ref[...], sin_ref[...]
    _shared_kv(x, kvg_ref[...], wk_ref, wv_ref, kng_ref[...], cos2, sin2, k_ref, kdup_ref, v_ref)
    if with_q:
        hq = _rms(x, bng_ref[...]).astype(BF16)
        q = jnp.dot(hq, wq_ref[...], preferred_element_type=F32)
        for p in range(D_MODEL // PAIR):
            cols = slice(p * PAIR, (p + 1) * PAIR)
            q_ref[:, cols] = _pair_norm_rope(q[:, cols], qng_ref[...], cos2, sin2)


def _kv_proj(x, kvg, wk, wv, kng2, cos2, sin2, q_args, *, tm, table_tiles):
    n = x.shape[0]
    with_q = q_args is not None
    table_spec = pl.BlockSpec((tm, PAIR), lambda i: (lax.rem(i, table_tiles), 0))
    in_specs = [_rows(tm, D_MODEL), _resident(kvg.shape), _resident(wk.shape), _resident(wv.shape),
                _resident(kng2.shape), table_spec, table_spec]
    args = [x, kvg, wk, wv, kng2, cos2, sin2]
    out_specs = [_rows(tm, D_KV), _rows(tm, 2 * D_KV), _rows(tm, D_KV)]
    out_shape = [jax.ShapeDtypeStruct((n, D_KV), F32), jax.ShapeDtypeStruct((n, 2 * D_KV), F32),
                 jax.ShapeDtypeStruct((n, D_KV), F32)]
    if with_q:
        in_specs += [_resident(a.shape) for a in q_args]
        args += list(q_args)
        out_specs.append(_rows(tm, D_MODEL))
        out_shape.append(jax.ShapeDtypeStruct((n, D_MODEL), F32))
    return pl.pallas_call(
        functools.partial(_kv_kernel, with_q=with_q),
        grid=(n // tm,),
        in_specs=in_specs, out_specs=out_specs, out_shape=out_shape,
        compiler_params=_cparams(),
        name="kvq_proj_sample" if with_q else "kv_proj_prompt",
    )(*args)


def _attn_prompt_kernel(sinks_ref, x_ref, q_ref, kc_ref, kp_ref, vc_ref, vp_ref, wo_ref, o_ref, a_scr,
                        *, tm, tiles_per_seq):
    i = pl.program_id(0)

    kj = lax.broadcasted_iota(I32, (2 * WINDOW, WINDOW), 0)
    qi = lax.broadcasted_iota(I32, (2 * WINDOW, WINDOW), 1)
    diff = qi + WINDOW - kj
    band = jnp.logical_and(diff >= 0, diff < WINDOW)
    first_half = lax.broadcasted_iota(I32, (WINDOW, PAIR), 1) < HEAD_DIM
    keep = (jnp.where(first_half, 1.0, 0.0).astype(BF16), jnp.where(first_half, 0.0, 1.0).astype(BF16))
    first_kj = jnp.where(lax.rem(i, tiles_per_seq) == 0, WINDOW, 0)

    for blk in range(tm // WINDOW):
        rows = slice(blk * WINDOW, (blk + 1) * WINDOW)
        if blk == 0:
            kk = jnp.concatenate([kp_ref[...], kc_ref[rows, :]], axis=0).astype(BF16)
            vv = jnp.concatenate([vp_ref[...], vc_ref[rows, :]], axis=0)
            valid = jnp.logical_and(band, kj >= first_kj)
        else:
            win = slice((blk - 1) * WINDOW, (blk + 1) * WINDOW)
            kk = kc_ref[win, :].astype(BF16)
            vv = vc_ref[win, :]
            valid = band
        for kvh in range(N_KV_HEADS):
            kk_h = kk[:, kvh * PAIR:(kvh + 1) * PAIR]
            vcols = slice((kvh // 2) * PAIR, (kvh // 2 + 1) * PAIR)
            vt = vv[:, vcols].T.astype(BF16)
            ch = slice((kvh % 2) * HEAD_DIM, (kvh % 2 + 1) * HEAD_DIM)
            outs = []
            for g in range(Q_PER_KV):
                head = kvh * Q_PER_KV + g
                qm = q_ref[rows, (head // 2) * PAIR:(head // 2 + 1) * PAIR] * keep[head % 2]
                sink = sinks_ref[head]
                st = lax.dot_general(kk_h, qm, (((1,), (1,)), ((), ())), preferred_element_type=F32)
                st = jnp.where(valid, st, -jnp.inf)
                m = jnp.maximum(jnp.max(st, axis=0, keepdims=True), sink)
                p = jnp.exp(st - m)
                den = jnp.sum(p, axis=0, keepdims=True) + jnp.exp(sink - m)
                ot = jnp.dot(vt, p.astype(BF16), preferred_element_type=F32)
                outs.append(ot[ch, :] * (1.0 / den))
            for pq in range(Q_PER_KV // 2):
                pair = (kvh * Q_PER_KV) // 2 + pq
                both = jnp.concatenate([outs[2 * pq], outs[2 * pq + 1]], axis=0)
                a_scr[rows, pair * PAIR:(pair + 1) * PAIR] = both.T.astype(BF16)
    o_ref[...] = x_ref[...] + jnp.dot(a_scr[...], wo_ref[...], preferred_element_type=F32)


def _attn_prompt(x, q, kdup, v, sinks, wo, *, tm, seq):
    n = x.shape[0]
    tiles_per_seq = seq // tm
    blocks_per_tile = tm // WINDOW

    def prev_spec(width):
        return pl.BlockSpec((WINDOW, width), lambda i: (jnp.maximum(i * blocks_per_tile - 1, 0), 0))

    return pl.pallas_call(
        functools.partial(_attn_prompt_kernel, tm=tm, tiles_per_seq=tiles_per_seq),
        grid=(n // tm,),
        in_specs=[pl.BlockSpec(memory_space=pltpu.SMEM), _rows(tm, D_MODEL), _rows(tm, D_MODEL),
                  _rows(tm, 2 * D_KV), prev_spec(2 * D_KV), _rows(tm, D_KV), prev_spec(D_KV), _resident(wo.shape)],
        out_specs=_rows(tm, D_MODEL),
        out_shape=jax.ShapeDtypeStruct((n, D_MODEL), F32),
        scratch_shapes=[pltpu.VMEM((tm, D_MODEL), BF16)],
        compiler_params=_cparams(),
        name="attn_prompt",
    )(sinks, x, q, kdup, kdup, v, v, wo)


def _attn_sample_kernel(q_ref, ck_ref, cv_ref, kn_ref, vn_ref, sink_ref, o_ref, *, per_step, win):
    rowi = lax.broadcasted_iota(I32, (N_HEADS, D_KV), 0)
    lanei = lax.broadcasted_iota(I32, (N_HEADS, D_KV), 1)
    group = rowi >> 2
    own_block = group == (lanei >> 6)
    j = lax.broadcasted_iota(I32, (N_HEADS, win), 1)
    diff = win - j
    valid = jnp.logical_and(diff >= 0, diff < WINDOW)
    sink = sink_ref[...]
    scale = HEAD_DIM ** -0.5
    for b in range(per_step):
        qm = jnp.where(own_block, q_ref[b], 0.0)
        s = lax.dot_general(qm.astype(BF16), ck_ref[b].astype(BF16), (((1,), (1,)), ((), ())),
                            preferred_element_type=F32) * scale
        s = jnp.where(valid, s, -jnp.inf)
        s_new = jnp.sum(qm * kn_ref[b:b + 1, :], axis=-1, keepdims=True) * scale
        m = jnp.maximum(jnp.maximum(jnp.max(s, axis=-1, keepdims=True), s_new), sink)
        p = jnp.exp(s - m)
        p_new = jnp.exp(s_new - m)
        den = jnp.sum(p, axis=-1, keepdims=True) + p_new + jnp.exp(sink - m)
        o = jnp.dot(p.astype(BF16), cv_ref[b].astype(BF16), preferred_element_type=F32) + p_new * vn_ref[b:b + 1, :]
        o = o * (1.0 / den)
        res = o
        for sft in range(1, N_KV_HEADS):
            res = jnp.where(group == sft, pltpu.roll(o, D_KV - sft * HEAD_DIM, 1), res)
        o_ref[b] = res[:, :HEAD_DIM]


def _attn_sample(q4, ck, cv, kn, vn, sink_col, *, per_step):
    nb, win = ck.shape[0], ck.shape[1]
    return pl.pallas_call(
        functools.partial(_attn_sample_kernel, per_step=per_step, win=win),
        grid=(nb // per_step,),
        in_specs=[pl.BlockSpec((per_step, N_HEADS, D_KV), lambda i: (i, 0, 0)),
                  pl.BlockSpec((per_step, win, D_KV), lambda i: (i, 0, 0)),
                  pl.BlockSpec((per_step, win, D_KV), lambda i: (i, 0, 0)),
                  _rows(per_step, D_KV), _rows(per_step, D_KV), _resident(sink_col.shape)],
        out_specs=pl.BlockSpec((per_step, N_HEADS, HEAD_DIM), lambda i: (i, 0, 0)),
        out_shape=jax.ShapeDtypeStruct((nb, N_HEADS, HEAD_DIM), F32),
        compiler_params=_cparams(),
        name="attn_sample",
    )(q4, ck, cv, kn, vn, sink_col)


def _residual_proj_kernel(x_ref, a_ref, w_ref, o_ref):
    o_ref[...] = x_ref[...] + jnp.dot(a_ref[...].astype(BF16), w_ref[...], preferred_element_type=F32)


def _residual_proj(x, a, w):
    n = x.shape[0]
    return pl.pallas_call(
        _residual_proj_kernel,
        grid=(1,),
        in_specs=[_rows(n, D_MODEL), _rows(n, a.shape[1]), _resident(w.shape)],
        out_specs=_rows(n, D_MODEL),
        out_shape=jax.ShapeDtypeStruct((n, D_MODEL), F32),
        compiler_params=_cparams(),
        name="attn_out_sample",
    )(x, a, w)


def _rope_tables(pos):
    half = ROPE_DIM // 2
    inv_freq = jnp.power(jnp.float32(ROPE_THETA), -jnp.arange(half, dtype=F32) / half)
    ang = pos.astype(F32)[:, None] * inv_freq[None, :]
    cos, sin = jnp.cos(ang), jnp.sin(ang)
    n = pos.shape[0]
    rest = HEAD_DIM - ROPE_DIM
    cos_h = jnp.concatenate([cos, cos, jnp.ones((n, rest), F32)], axis=1)
    sin_h = jnp.concatenate([-sin, sin, jnp.zeros((n, rest), F32)], axis=1)
    return jnp.tile(cos_h, (1, 2)), jnp.tile(sin_h, (1, 2))


def kernel(x_prompt, x_sample, cache_k, cache_v, a_norm_g, a_w_in, a_ln_g, a_ln_b, a_w_s, a_b_s, a_w_out, kv_norm_g, w_k, w_v, k_norm_g, b_norm_g, w_q, q_norm_g, sinks, w_o, ffn_norm_g, w_router, b_router, w_gu, b_gu, w_down, b_down):
    bsz, seq, _ = x_prompt.shape
    dec_b, dec_seq, _ = x_sample.shape
    assert dec_seq == 1 and seq % TM_DENSE == 0 and a_norm_g.shape[0] == 1 and b_norm_g.shape[0] == 1
    win = cache_k.shape[1]
    n_p, n_s = bsz * seq, dec_b * dec_seq
    xp = x_prompt.reshape(n_p, D_MODEL)
    xs = x_sample.reshape(n_s, D_MODEL)

    ng = a_norm_g[0].reshape(1, D_MODEL)
    win_bf = a_w_in[0].astype(BF16)
    wout_bf = a_w_out[0].astype(BF16)
    lng = a_ln_g[0].reshape(1, D_GATE)
    lnb = a_ln_b[0].reshape(1, D_GATE)
    bias_full = jnp.repeat(a_b_s[0].T, GROUP_DIM_A, axis=1)
    diag_row = jnp.repeat(a_w_s[0][:, 0, 0], GROUP_DIM_A).reshape(1, D_GATE)
    (xp,) = _mixer_a(xp, ng, win_bf, lng, lnb, a_w_s[0], bias_full, wout_bf, tm=TM_DENSE, single_token_chunks=False)
    xs, v_rows = _mixer_a(xs, ng, win_bf, lng, lnb, diag_row, bias_full[0:1], wout_bf, tm=n_s,
                          single_token_chunks=True)

    kvg = kv_norm_g.reshape(1, D_MODEL)
    bng = b_norm_g[0].reshape(1, D_MODEL)
    w_k_dup = jnp.repeat(w_k.reshape(D_MODEL, N_KV_HEADS, 1, HEAD_DIM), 2, axis=2).reshape(D_MODEL, 2 * D_KV)
    wk_bf, wv_bf, wq_bf, wo_bf = (w.astype(BF16) for w in (w_k_dup, w_v, w_q[0], w_o[0]))
    kng2 = jnp.tile(k_norm_g, 2).reshape(1, PAIR)
    qng2 = jnp.tile(q_norm_g[0], 2).reshape(1, PAIR)
    cos_p, sin_p = _rope_tables(jnp.arange(seq, dtype=I32))
    cos_s, sin_s = _rope_tables(jnp.full((n_s,), PAST_LEN, I32))
    attn_pre = (kvg, wk_bf, wv_bf, kng2, bng, wq_bf, qng2, cos_p, sin_p, seq // TM_COMBINE)
    (xp, k_p, kdup_p, v_p, q_p), (xs,) = _moe([xp, xs], ffn_norm_g[0], w_router[0], b_router[0], w_gu, b_gu,
                                              w_down, b_down, 0, prompt_attn_pre=attn_pre)

    k_s, _, v_s, q_s = _kv_proj(xs, kvg, wk_bf, wv_bf, kng2, cos_s, sin_s, (bng, wq_bf, qng2), tm=n_s,
                                table_tiles=1)
    xp = _attn_prompt(xp, q_p, kdup_p, v_p, sinks[0], wo_bf, tm=TM_DENSE, seq=seq)
    q4 = jnp.tile(q_s.reshape(n_s, N_HEADS, HEAD_DIM), (1, 1, N_KV_HEADS))
    attn_s = _attn_sample(q4, cache_k.reshape(dec_b, win, D_KV), cache_v.reshape(dec_b, win, D_KV), k_s, v_s,
                          sinks[0].reshape(N_HEADS, 1), per_step=8)
    xs = _residual_proj(xs, attn_s.reshape(n_s, D_MODEL), wo_bf)
    (xp,), (xs,) = _moe([xp, xs], ffn_norm_g[1], w_router[1], b_router[1], w_gu, b_gu, w_down, b_down, 1)

    def last_window(a):
        a = a.reshape(bsz, seq, D_KV)[:, -WINDOW:]
        return a.reshape(bsz, min(WINDOW, seq), N_KV_HEADS, HEAD_DIM)

    return (xp.reshape(bsz, seq, D_MODEL), xs.reshape(dec_b, dec_seq, D_MODEL),
            v_rows.reshape(1, dec_b, dec_seq, D_GATE), last_window(k_p), last_window(v_p),
            k_s.reshape(dec_b, dec_seq, N_KV_HEADS, HEAD_DIM), v_s.reshape(dec_b, dec_seq, N_KV_HEADS, HEAD_DIM))
```

```python
import functools

import jax
import jax.numpy as jnp
from jax import lax
from jax.experimental import pallas as pl
from jax.experimental.pallas import tpu as pltpu

F32, BF16, I32 = jnp.float32, jnp.bfloat16, jnp.int32

D_MODEL = 1024
CHUNK = 128
D_GATE = 2 * D_MODEL
N_GROUPS_A = 8
GROUP_DIM_A = D_GATE // N_GROUPS_A
HEAD_DIM = 64
N_HEADS = D_MODEL // HEAD_DIM
N_KV_HEADS = N_HEADS // 4
Q_PER_KV = N_HEADS // N_KV_HEADS
D_KV = N_KV_HEADS * HEAD_DIM
WINDOW = 128
ROPE_DIM = HEAD_DIM // 4
ROPE_THETA = 500000.0
N_EXPERTS = 32
TOP_K = 4
D_EXPERT = D_MODEL
SWIGLU_LIMIT = 7.0
SWIGLU_ALPHA = 1.702
EPS = 1e-6
PAST_LEN = 8192

V7X_LANES = 128
V7X_VMEM_BYTES = 64 * 1024 * 1024
VMEM_LIMIT_BYTES = V7X_VMEM_BYTES - 8 * 1024 * 1024

TM_DENSE = 512
TM_COMBINE = 256
MOE_ROWS = 512
TM_DISPATCH = 256
FILL_BITS = MOE_ROWS.bit_length() - 1
TOKEN_SUBLANES = D_MODEL // V7X_LANES
SLAB = V7X_LANES
REC_ROWS = 16
PAIR = 2 * HEAD_DIM
INV_SQRT2 = 0.7071067811865476


def _cparams():
    return pltpu.CompilerParams(dimension_semantics=("arbitrary",), vmem_limit_bytes=VMEM_LIMIT_BYTES)


def _resident(shape):
    zeros = (0,) * len(shape)
    return pl.BlockSpec(shape, lambda i, *_: zeros, pipeline_mode=pl.Buffered(1))


def _rows(tm, width, offset=0):
    return pl.BlockSpec((tm, width), lambda i, *_: (i + offset, 0))


def _rms(x, g):
    return x * lax.rsqrt(jnp.mean(x * x, axis=-1, keepdims=True) + EPS) * g


def _pair_norm_rope(x2, g2, cos2, sin2):
    lane = lax.broadcasted_iota(I32, x2.shape, 1)
    first = lane < HEAD_DIM
    sq = x2 * x2
    s0 = jnp.sum(jnp.where(first, sq, 0.0), axis=-1, keepdims=True)
    s1 = jnp.sum(jnp.where(first, 0.0, sq), axis=-1, keepdims=True)
    ms = jnp.where(first, s0, s1) * (1.0 / HEAD_DIM)
    y = x2 * lax.rsqrt(ms + EPS) * g2
    half = ROPE_DIM // 2
    up = pltpu.roll(y, PAIR - half, 1)
    dn = pltpu.roll(y, half, 1)
    partner = jnp.where((lane & (HEAD_DIM - 1)) < half, up, dn)
    return y * cos2 + partner * sin2


def _mixer_a_kernel(x_ref, ng_ref, win_ref, lng_ref, lnb_ref, mix_ref, bias_ref, wout_ref, *refs,
                    tm, single_token_chunks):
    if single_token_chunks:
        o_ref, vout_ref, u_scr, v_scr, p_scr = refs
    else:
        o_ref, u_scr, v_scr, p_scr = refs
    x = x_ref[...]
    h = _rms(x, ng_ref[...]).astype(BF16)
    nc = 512
    for c in range(2 * D_GATE // nc):
        z = jnp.dot(h, win_ref[:, c * nc:(c + 1) * nc], preferred_element_type=F32)
        z = 0.5 * z * (1.0 + lax.erf(z * INV_SQRT2))
        if c < D_GATE // nc:
            u_scr[:, c * nc:(c + 1) * nc] = z
        else:
            v_scr[:, (c - D_GATE // nc) * nc:(c - D_GATE // nc + 1) * nc] = z
    v = v_scr[...]
    vc = v - jnp.mean(v, axis=-1, keepdims=True)
    vn = vc * lax.rsqrt(jnp.mean(vc * vc, axis=-1, keepdims=True) + EPS) * lng_ref[...] + lnb_ref[...]
    if single_token_chunks:
        vout_ref[...] = vn
        p_scr[...] = (u_scr[...] * (vn * mix_ref[...] + bias_ref[...])).astype(BF16)
    else:
        v_scr[...] = vn
        row = lax.broadcasted_iota(I32, (CHUNK, CHUNK), 0)
        col = lax.broadcasted_iota(I32, (CHUNK, CHUNK), 1)
        for g in range(N_GROUPS_A):
            wc = jnp.where(row >= col, mix_ref[g], 0.0).astype(BF16)
            cols = slice(g * GROUP_DIM_A, (g + 1) * GROUP_DIM_A)
            for c in range(tm // CHUNK):
                rows = slice(c * CHUNK, (c + 1) * CHUNK)
                mixed = jnp.dot(wc, v_scr[rows, cols].astype(BF16), preferred_element_type=F32)
                mixed = mixed + bias_ref[:, cols]
                p_scr[rows, cols] = (u_scr[rows, cols] * mixed).astype(BF16)
    o_ref[...] = x + jnp.dot(p_scr[...], wout_ref[...], preferred_element_type=F32)


def _mixer_a(x, ng, win, lng, lnb, mix, bias, wout, *, tm, single_token_chunks):
    n = x.shape[0]
    out_shape = [jax.ShapeDtypeStruct((n, D_MODEL), F32)]
    out_specs = [_rows(tm, D_MODEL)]
    if single_token_chunks:
        out_shape.append(jax.ShapeDtypeStruct((n, D_GATE), F32))
        out_specs.append(_rows(tm, D_GATE))
    return pl.pallas_call(
        functools.partial(_mixer_a_kernel, tm=tm, single_token_chunks=single_token_chunks),
        grid=(n // tm,),
        in_specs=[_rows(tm, D_MODEL), _resident(ng.shape), _resident(win.shape), _resident(lng.shape),
                  _resident(lnb.shape), _resident(mix.shape), _resident(bias.shape), _resident(wout.shape)],
        out_specs=out_specs,
        out_shape=out_shape,
        scratch_shapes=[pltpu.VMEM((tm, D_GATE), F32), pltpu.VMEM((tm, D_GATE), F32),
                        pltpu.VMEM((tm, D_GATE), BF16)],
        compiler_params=_cparams(),
        name="mixer_a_sample" if single_token_chunks else "mixer_a_prompt",
    )(x, ng, win, lng, lnb, mix, bias, wout)


def _route_rows(x, g, whi_t, wlo_t, bias_col, run_scr):
    tm = x.shape[0]
    h = _rms(x, g)
    hh = h.astype(BF16)
    hl = (h - hh.astype(F32)).astype(BF16)
    nt = (((1,), (1,)), ((), ()))
    logits = (lax.dot_general(whi_t, hh, nt, preferred_element_type=F32)
              + lax.dot_general(wlo_t, hh, nt, preferred_element_type=F32)
              + lax.dot_general(whi_t, hl, nt, preferred_element_type=F32) + bias_col)
    e = lax.broadcasted_iota(I32, (N_EXPERTS, tm), 0).astype(F32)
    hot, vals, idxs = [], [], []
    for _ in range(TOP_K):
        m = jnp.max(logits, axis=0, keepdims=True)
        idx = jnp.min(jnp.where(logits == m, e, float(N_EXPERTS)), axis=0, keepdims=True)
        oh = e == idx
        hot.append(oh)
        vals.append(m)
        idxs.append(idx)
        logits = jnp.where(oh, -jnp.inf, logits)
    ex = [jnp.exp(v - vals[0]) for v in vals]
    den = ex[0] + ex[1] + ex[2] + ex[3]
    picked = jnp.zeros((N_EXPERTS, tm), F32)
    for oh in hot:
        picked = picked + jnp.where(oh, 1.0, 0.0)
    r = lax.broadcasted_iota(I32, (tm, tm), 0)
    c = lax.broadcasted_iota(I32, (tm, tm), 1)
    earlier = jnp.where(r < c, 1.0, 0.0).astype(BF16)
    run = run_scr[:, 0:1]
    pos = jnp.dot(picked.astype(BF16), earlier, preferred_element_type=F32) + run
    run_scr[:, 0:1] = run + jnp.sum(picked, axis=1, keepdims=True)
    row = lax.broadcasted_iota(I32, (REC_ROWS, tm), 0)
    rec = jnp.zeros((REC_ROWS, tm), F32)
    for k in range(TOP_K):
        pos_k = jnp.sum(jnp.where(hot[k], pos, 0.0), axis=0, keepdims=True)
        rec = jnp.where(row == k, idxs[k], rec)
        rec = jnp.where(row == TOP_K + k, pos_k, rec)
        rec = jnp.where(row == 2 * TOP_K + k, ex[k] / den, rec)
    return h, rec


def _router_kernel(xp_ref, xs_ref, g_ref, whi_ref, wlo_ref, b_ref, h_ref, slabp_ref, slabs_ref, rec_ref, cnt_ref,
                   run_scr, *, n_tiles, n_s):
    i = pl.program_id(0)

    @pl.when(i == 0)
    def _():
        run_scr[...] = jnp.zeros_like(run_scr)

    def pad_rows(rec):
        return jnp.concatenate([rec, jnp.zeros((SLAB - REC_ROWS, rec.shape[1]), F32)], axis=0)

    @pl.when(i < n_tiles)
    def _():
        h, rec = _route_rows(xp_ref[...], g_ref[...], whi_ref[...], wlo_ref[...], b_ref[...], run_scr)
        _store_token_tiles(h_ref, h, h.shape[0])
        slabp_ref[...] = pad_rows(rec).T
        rec_ref[...] = rec[0:2 * TOP_K, :]

    @pl.when(i == n_tiles)
    def _():
        h, rec = _route_rows(xs_ref[...], g_ref[...], whi_ref[...], wlo_ref[...], b_ref[...], run_scr)
        h_ref[n_s * TOKEN_SUBLANES:, :] = jnp.zeros((h_ref.shape[0] - n_s * TOKEN_SUBLANES, V7X_LANES), F32)
        _store_token_tiles(h_ref, h, n_s)
        slabs_ref[...] = pad_rows(rec).T
        rec_ref[...] = jnp.zeros_like(rec_ref)
        rec_ref[:, 0:n_s] = rec[0:2 * TOP_K, :]

    cnt_ref[...] = run_scr[...]


def _store_token_tiles(ref, x, n):
    for s in range(TOKEN_SUBLANES):
        ref[pl.ds(s, n, stride=TOKEN_SUBLANES), :] = x[:, s * V7X_LANES:(s + 1) * V7X_LANES]


def _load_token_tiles(ref, n):
    return jnp.concatenate([ref[pl.ds(s, n, stride=TOKEN_SUBLANES), :] for s in range(TOKEN_SUBLANES)], axis=1)


def _router(xp, xs, g, whi, wlo, b, *, tm):
    n_p, n_s = xp.shape[0], xs.shape[0]
    n_tiles = n_p // tm
    last = n_tiles - 1
    return pl.pallas_call(
        functools.partial(_router_kernel, n_tiles=n_tiles, n_s=n_s),
        grid=(n_tiles + 1,),
        in_specs=[pl.BlockSpec((tm, D_MODEL), lambda i: (jnp.minimum(i, last), 0)), _resident(xs.shape),
                  _resident(g.shape), _resident(whi.shape), _resident(wlo.shape), _resident(b.shape)],
        out_specs=[_rows(tm * TOKEN_SUBLANES, V7X_LANES),
                   pl.BlockSpec((tm, SLAB), lambda i: (jnp.minimum(i, last), 0)),
                   pl.BlockSpec((n_s, SLAB), lambda i: (0, 0)),
                   pl.BlockSpec((2 * TOP_K, tm), lambda i: (0, i)),
                   pl.BlockSpec((N_EXPERTS, V7X_LANES), lambda i: (0, 0))],
        out_shape=[jax.ShapeDtypeStruct(((n_p + tm) * TOKEN_SUBLANES, V7X_LANES), F32),
                   jax.ShapeDtypeStruct((n_p, SLAB), F32),
                   jax.ShapeDtypeStruct((n_s, SLAB), F32),
                   jax.ShapeDtypeStruct((2 * TOP_K, n_p + tm), F32),
                   jax.ShapeDtypeStruct((N_EXPERTS, V7X_LANES), F32)],
        scratch_shapes=[pltpu.VMEM((N_EXPERTS, V7X_LANES), F32)],
        compiler_params=_cparams(),
        name="moe_router",
    )(xp, xs, g, whi, wlo, b)


def _dispatch_kernel(fill_start_ref, fill_len_ref, dest_ref, h_ref, xs_ref, zbuf, sem, fill_sem,
                     *, tm, n_tiles, n_last, rows, n_blocks):
    i = pl.program_id(0)
    ts = TOKEN_SUBLANES

    def tile_copy(r, k):
        d = dest_ref[0, 0, k * tm + r]
        src = h_ref.at[pl.ds(pl.multiple_of(r * ts, ts), ts)]
        dst = xs_ref.at[pl.ds(pl.multiple_of(d * ts, ts), ts)]
        return pltpu.make_async_copy(src, dst, sem)

    def scatter(n):
        def body(r, carry):
            for k in range(TOP_K):
                tile_copy(r, k).start(priority=k % 2)
            return carry
        lax.fori_loop(0, n, body, 0, unroll=4)
        for k in range(TOP_K):
            pltpu.make_async_copy(h_ref.at[pl.ds(0, n * ts)], xs_ref.at[pl.ds(0, n * ts)], sem).wait()

    def fill_copy(e, bit):
        size = 1 << bit
        length = fill_len_ref[e]
        done = length - (length & (2 * size - 1))
        off = pl.multiple_of((fill_start_ref[e] + done) * ts, ts)
        return (length & size) != 0, pltpu.make_async_copy(zbuf.at[pl.ds(0, size * ts)],
                                                            xs_ref.at[pl.ds(off, size * ts)], fill_sem)

    def tail_copy(blk):
        off = pl.multiple_of(blk * (rows * ts), rows * ts)
        return pltpu.make_async_copy(zbuf, xs_ref.at[pl.ds(off, rows * ts)], fill_sem)

    def for_each_fill(act):
        for e in range(N_EXPERTS):
            for bit in range(FILL_BITS):
                go, cp = fill_copy(e, bit)

                @pl.when(go)
                def _():
                    act(cp)

        def body(blk, carry):
            act(tail_copy(blk))
            return carry
        lax.fori_loop(fill_start_ref[N_EXPERTS], n_blocks, body, 0)

    @pl.when(i == 0)
    def _():
        zbuf[...] = jnp.zeros_like(zbuf)
        for_each_fill(lambda cp: cp.start())

    @pl.when(i < n_tiles - 1)
    def _():
        scatter(tm)

    @pl.when(i == n_tiles - 1)
    def _():
        scatter(n_last)
        for_each_fill(lambda cp: cp.wait())


def _dispatch(fill_start, fill_len, dest3, h_tiles, n_all, n_blocks, *, tm, rows):
    n_tiles = dest3.shape[0]
    n_last = n_all - (n_tiles - 1) * tm
    ts = TOKEN_SUBLANES
    n_slots = n_blocks * rows
    grid_spec = pltpu.PrefetchScalarGridSpec(
        num_scalar_prefetch=2,
        grid=(n_tiles,),
        in_specs=[pl.BlockSpec((1, 1, TOP_K * tm), lambda i, *_: (i, 0, 0), memory_space=pltpu.SMEM),
                  _rows(tm * ts, V7X_LANES)],
        out_specs=pl.BlockSpec(memory_space=pl.ANY),
        scratch_shapes=[pltpu.VMEM((rows * ts, V7X_LANES), F32),
                        pltpu.SemaphoreType.DMA, pltpu.SemaphoreType.DMA],
    )
    return pl.pallas_call(
        functools.partial(_dispatch_kernel, tm=tm, n_tiles=n_tiles, n_last=n_last, rows=rows, n_blocks=n_blocks),
        grid_spec=grid_spec,
        out_shape=jax.ShapeDtypeStruct((n_slots * ts, V7X_LANES), F32),
        compiler_params=_cparams(),
        name="moe_dispatch",
    )(fill_start, fill_len, dest3, h_tiles)


def _expert_kernel(be_ref, nact_ref, run_ref, rune_ref, x_ref, wgu_hbm, bgu_ref, wdn_hbm, bdn_ref, o_ref,
                   wgu_f32, wdn_f32, wsem, wgu_bf, wdn_bf, *, rows, layer):
    b = pl.program_id(0)
    nact = nact_ref[0]

    def weight_copies(r, slot):
        e = rune_ref[r]
        return (pltpu.make_async_copy(wgu_hbm.at[layer, e], wgu_f32.at[slot], wsem.at[0, slot]),
                pltpu.make_async_copy(wdn_hbm.at[layer, e], wdn_f32.at[slot], wsem.at[1, slot]))

    @pl.when(b == 0)
    def _():
        for cp in weight_copies(0, 0):
            cp.start()

    @pl.when(b < nact)
    def _():
        r = run_ref[b]
        slot = lax.rem(r, 2)
        changed = jnp.logical_or(b == 0, r != run_ref[jnp.maximum(b - 1, 0)])

        @pl.when(changed)
        def _():
            for cp in weight_copies(r, slot):
                cp.wait()

            @pl.when(r + 1 < rune_ref[N_EXPERTS])
            def _():
                for cp in weight_copies(r + 1, 1 - slot):
                    cp.start()

            step = 256
            for c in range(D_MODEL // step):
                wgu_bf[c * step:(c + 1) * step, :] = wgu_f32[slot, c * step:(c + 1) * step, :].astype(BF16)
                wdn_bf[c * step:(c + 1) * step, :] = wdn_f32[slot, c * step:(c + 1) * step, :].astype(BF16)

        xb = _load_token_tiles(x_ref, rows).astype(BF16)
        nc = 512
        out = None
        for c in range(D_EXPERT // nc):
            gcols = slice(c * nc, (c + 1) * nc)
            ucols = slice(D_EXPERT + c * nc, D_EXPERT + (c + 1) * nc)
            gate = jnp.dot(xb, wgu_bf[:, gcols], preferred_element_type=F32) + bgu_ref[:, gcols]
            up = jnp.dot(xb, wgu_bf[:, ucols], preferred_element_type=F32) + bgu_ref[:, ucols]
            gate = jnp.minimum(gate, SWIGLU_LIMIT)
            up = jnp.clip(up, -SWIGLU_LIMIT, SWIGLU_LIMIT)
            act = ((up + 1.0) * gate * (1.0 / (1.0 + jnp.exp(-SWIGLU_ALPHA * gate)))).astype(BF16)
            part = jnp.dot(act, wdn_bf[gcols, :], preferred_element_type=F32)
            out = part if out is None else out + part
        _store_token_tiles(o_ref, out + bdn_ref[...], rows)

    @pl.when(b >= nact)
    def _():
        o_ref[...] = jnp.zeros_like(o_ref)


def _experts(block_e, n_active, block_run, run_e, xs_tiles, wgu, bgu, wdn, bdn, *, rows, layer):
    ts = TOKEN_SUBLANES
    nb = xs_tiles.shape[0] // (rows * ts)
    grid_spec = pltpu.PrefetchScalarGridSpec(
        num_scalar_prefetch=4,
        grid=(nb,),
        in_specs=[
            pl.BlockSpec((rows * ts, V7X_LANES), lambda b, be, na, *_: (jnp.minimum(b, na[0] - 1), 0)),
            pl.BlockSpec(memory_space=pl.ANY),
            pl.BlockSpec((None, None, 1, 2 * D_EXPERT), lambda b, be, *_: (layer, be[b], 0, 0)),
            pl.BlockSpec(memory_space=pl.ANY),
            pl.BlockSpec((None, None, 1, D_MODEL), lambda b, be, *_: (layer, be[b], 0, 0)),
        ],
        out_specs=pl.BlockSpec((rows * ts, V7X_LANES), lambda b, *_: (b, 0)),
        scratch_shapes=[pltpu.VMEM((2, D_MODEL, 2 * D_EXPERT), F32), pltpu.VMEM((2, D_EXPERT, D_MODEL), F32),
                        pltpu.SemaphoreType.DMA((2, 2)),
                        pltpu.VMEM((D_MODEL, 2 * D_EXPERT), BF16), pltpu.VMEM((D_EXPERT, D_MODEL), BF16)],
    )
    return pl.pallas_call(
        functools.partial(_expert_kernel, rows=rows, layer=layer),
        grid_spec=grid_spec,
        out_shape=jax.ShapeDtypeStruct(xs_tiles.shape, F32),
        compiler_params=_cparams(),
        name="moe_experts",
    )(block_e, n_active, block_run, run_e, xs_tiles, wgu, bgu, wdn, bdn)


def _combine_kernel(dest_first_ref, dest_next_ref, x_ref, slab_ref, ebuf_ref, *refs, tm, n_tiles, attn_pre):
    if attn_pre:
        (kvg_ref, wk_ref, wv_ref, kng_ref, bng_ref, wq_ref, qng_ref, cos_ref, sin_ref,
         o_ref, k_ref, kdup_ref, v_ref, q_ref, rbuf, sem) = refs
    else:
        o_ref, rbuf, sem = refs
    i = pl.program_id(0)
    slot = lax.rem(i, 2)
    ts = TOKEN_SUBLANES

    def gather(dest_ref, s):
        def body(r, carry):
            for k in range(TOP_K):
                d = dest_ref[0, 0, k * tm + r]
                src = ebuf_ref.at[pl.ds(pl.multiple_of(d * ts, ts), ts)]
                dst = rbuf.at[s, k, pl.ds(pl.multiple_of(r * ts, ts), ts)]
                pltpu.make_async_copy(src, dst, sem.at[s]).start(priority=k % 2)
            return carry
        lax.fori_loop(0, tm, body, 0, unroll=4)

    @pl.when(i == 0)
    def _():
        gather(dest_first_ref, 0)

    @pl.when(i + 1 < n_tiles)
    def _():
        gather(dest_next_ref, 1 - slot)

    for k in range(TOP_K):
        pltpu.make_async_copy(ebuf_ref.at[pl.ds(0, tm * ts)], rbuf.at[slot, k], sem.at[slot]).wait()
    slab = slab_ref[...]
    gates = [slab[:, 2 * TOP_K + k:2 * TOP_K + k + 1] for k in range(TOP_K)]
    for s in range(ts):
        cols = slice(s * V7X_LANES, (s + 1) * V7X_LANES)
        acc = x_ref[:, cols]
        for k in range(TOP_K):
            acc = acc + gates[k] * rbuf[slot, k, pl.ds(s, tm, stride=ts), :]
        o_ref[:, cols] = acc
    if attn_pre:
        x2 = o_ref[...]
        cos2, sin2 = cos_ref[...], sin_ref[...]
        _shared_kv(x2, kvg_ref[...], wk_ref, wv_ref, kng_ref[...], cos2, sin2, k_ref, kdup_ref, v_ref)
        hq = _rms(x2, bng_ref[...]).astype(BF16)
        q = jnp.dot(hq, wq_ref[...], preferred_element_type=F32)
        for p in range(D_MODEL // PAIR):
            cols = slice(p * PAIR, (p + 1) * PAIR)
            q_ref[:, cols] = (_pair_norm_rope(q[:, cols], qng_ref[...], cos2, sin2) * HEAD_DIM ** -0.5).astype(BF16)


def _shared_kv(x, kvg, wk_ref, wv_ref, kng2, cos2, sin2, k_ref, kdup_ref, v_ref):
    h = _rms(x, kvg).astype(BF16)
    kd = jnp.dot(h, wk_ref[...], preferred_element_type=F32)
    v_ref[...] = jnp.dot(h, wv_ref[...], preferred_element_type=F32)
    first_half = lax.broadcasted_iota(I32, (x.shape[0], PAIR), 1) < HEAD_DIM
    slabs = []
    for p in range(N_KV_HEADS):
        cols = slice(p * PAIR, (p + 1) * PAIR)
        slab = _pair_norm_rope(kd[:, cols], kng2, cos2, sin2)
        kdup_ref[:, cols] = slab
        slabs.append(slab)
    for p in range(D_KV // PAIR):
        k_ref[:, p * PAIR:(p + 1) * PAIR] = jnp.where(first_half, slabs[2 * p], slabs[2 * p + 1])


def _combine(x, slab, dest3, ebuf, *, tm, attn_pre=None):
    n = x.shape[0]
    n_tiles = n // tm
    in_specs = [
        pl.BlockSpec((1, 1, TOP_K * tm), lambda i: (0, 0, 0), memory_space=pltpu.SMEM),
        pl.BlockSpec((1, 1, TOP_K * tm), lambda i: (jnp.minimum(i + 1, n_tiles - 1), 0, 0),
                     memory_space=pltpu.SMEM),
        _rows(tm, D_MODEL), _rows(tm, SLAB), pl.BlockSpec(memory_space=pl.ANY),
    ]
    args = [dest3, dest3, x, slab, ebuf]
    out_specs = [_rows(tm, D_MODEL)]
    out_shape = [jax.ShapeDtypeStruct((n, D_MODEL), F32)]
    if attn_pre is not None:
        *weights, cos2, sin2, table_tiles = attn_pre
        table_spec = pl.BlockSpec((tm, PAIR), lambda i: (lax.rem(i, table_tiles), 0))
        in_specs += [_resident(w.shape) for w in weights] + [table_spec, table_spec]
        args += list(weights) + [cos2, sin2]
        out_specs += [_rows(tm, D_KV), _rows(tm, 2 * D_KV), _rows(tm, D_KV), _rows(tm, D_MODEL)]
        out_shape += [jax.ShapeDtypeStruct((n, D_KV), F32), jax.ShapeDtypeStruct((n, 2 * D_KV), F32),
                      jax.ShapeDtypeStruct((n, D_KV), F32), jax.ShapeDtypeStruct((n, D_MODEL), BF16)]
    return pl.pallas_call(
        functools.partial(_combine_kernel, tm=tm, n_tiles=n_tiles, attn_pre=attn_pre is not None),
        grid=(n_tiles,),
        in_specs=in_specs, out_specs=out_specs, out_shape=out_shape,
        scratch_shapes=[pltpu.VMEM((2, TOP_K, tm * TOKEN_SUBLANES, V7X_LANES), F32),
                        pltpu.SemaphoreType.DMA((2,))],
        compiler_params=_cparams(),
        name="moe_combine_attn_pre" if attn_pre is not None else "moe_combine",
    )(*args)


def _moe(xs, g, w_router, b_router, wgu, bgu, wdn, bdn, layer, prompt_attn_pre=None):
    n_all = xs[0].shape[0] + xs[1].shape[0]
    w_t = w_router.T
    whi = w_t.astype(BF16)
    wlo = (w_t - whi.astype(F32)).astype(BF16)
    h_all, slab_p, slab_s, rec, cnt = _router(xs[0], xs[1], g.reshape(1, D_MODEL), whi, wlo,
                                              b_router.reshape(N_EXPERTS, 1), tm=TM_DENSE)
    slabs = [slab_p, slab_s]
    idx = rec[:TOP_K, :n_all].astype(I32)
    pos = rec[TOP_K:, :n_all].astype(I32)
    counts = cnt[:, 0].astype(I32)
    rows = MOE_ROWS
    nb = (n_all * TOP_K + N_EXPERTS * (rows - 1)) // rows
    padded = (counts + rows - 1) // rows * rows
    pad_end = jnp.cumsum(padded)
    pad_start = pad_end - padded
    eid = jnp.arange(N_EXPERTS, dtype=I32)
    has_rows = counts > 0
    dest = pos + jnp.sum(jnp.where(idx[None] == eid[:, None, None], pad_start[:, None, None], 0), axis=0)
    n_active = (pad_end[-1] // rows).astype(I32)
    blk = jnp.arange(nb, dtype=I32)
    block_e = jnp.sum((pad_end[None, :] <= (blk * rows)[:, None]).astype(I32), axis=1)
    last_e = jnp.max(jnp.where(has_rows, eid, 0))
    block_e = jnp.where(blk < n_active, jnp.minimum(block_e, N_EXPERTS - 1), last_e)

    def tile_table(d, tm):
        n_tiles = -(-d.shape[1] // tm)
        d = jnp.pad(d, ((0, 0), (0, n_tiles * tm - d.shape[1])))
        return d.reshape(TOP_K, n_tiles, tm).transpose(1, 0, 2).reshape(n_tiles, 1, TOP_K * tm)

    fill_start = jnp.concatenate([pad_start + counts, n_active.reshape(1)])
    xs_tiles = _dispatch(fill_start, padded - counts, tile_table(dest, TM_DISPATCH), h_all,
                         n_all, nb, tm=TM_DISPATCH, rows=rows)
    rank = jnp.cumsum(has_rows.astype(I32)) - 1
    run_e = jnp.sum(jnp.where(has_rows[None, :] & (rank[None, :] == eid[:, None]), eid[None, :], 0), axis=1)
    run_e = jnp.concatenate([run_e, jnp.sum(has_rows.astype(I32)).reshape(1)]).astype(I32)
    block_run = jnp.sum(jnp.where(block_e[:, None] == eid, rank, 0), axis=1).astype(I32)
    n_l = wgu.shape[0]
    ebuf = _experts(block_e, n_active.reshape(1), block_run, run_e, xs_tiles,
                    wgu, bgu.reshape(n_l, N_EXPERTS, 1, 2 * D_EXPERT), wdn, bdn.reshape(n_l, N_EXPERTS, 1, D_MODEL),
                    rows=rows, layer=layer)
    outs, off = [], 0
    for x, slab, pre in zip(xs, slabs, (prompt_attn_pre, None)):
        n = x.shape[0]
        tm = min(TM_COMBINE, n)
        outs.append(_combine(x, slab, tile_table(dest[:, off:off + n], tm), ebuf, tm=tm, attn_pre=pre))
        off += n
    return outs


def _kv_kernel(x_ref, kvg_ref, wk_ref, wv_ref, kng_ref, cos_ref, sin_ref, *refs, with_q):
    if with_q:
        bng_ref, wq_ref, qng_ref, k_ref, kdup_ref, v_ref, q_ref = refs
    else:
        k_ref, kdup_ref, v_ref = refs
    x = x_ref[...]
    cos2, sin2 = cos_ref[...], sin_ref[...]
    _shared_kv(x, kvg_ref[...], wk_ref, wv_ref, kng_ref[...], cos2, sin2, k_ref, kdup_ref, v_ref)
    if with_q:
        hq = _rms(x, bng_ref[...]).astype(BF16)
        q = jnp.dot(hq, wq_ref[...], preferred_element_type=F32)
        for p in range(D_MODEL // PAIR):
            cols = slice(p * PAIR, (p + 1) * PAIR)
            q_ref[:, cols] = _pair_norm_rope(q[:, cols], qng_ref[...], cos2, sin2)


def _kv_proj(x, kvg, wk, wv, kng2, cos2, sin2, q_args, *, tm, table_tiles):
    n = x.shape[0]
    with_q = q_args is not None
    table_spec = pl.BlockSpec((tm, PAIR), lambda i: (lax.rem(i, table_tiles), 0))
    in_specs = [_rows(tm, D_MODEL), _resident(kvg.shape), _resident(wk.shape), _resident(wv.shape),
                _resident(kng2.shape), table_spec, table_spec]
    args = [x, kvg, wk, wv, kng2, cos2, sin2]
    out_specs = [_rows(tm, D_KV), _rows(tm, 2 * D_KV), _rows(tm, D_KV)]
    out_shape = [jax.ShapeDtypeStruct((n, D_KV), F32), jax.ShapeDtypeStruct((n, 2 * D_KV), F32),
                 jax.ShapeDtypeStruct((n, D_KV), F32)]
    if with_q:
        in_specs += [_resident(a.shape) for a in q_args]
        args += list(q_args)
        out_specs.append(_rows(tm, D_MODEL))
        out_shape.append(jax.ShapeDtypeStruct((n, D_MODEL), F32))
    return pl.pallas_call(
        functools.partial(_kv_kernel, with_q=with_q),
        grid=(n // tm,),
        in_specs=in_specs, out_specs=out_specs, out_shape=out_shape,
        compiler_params=_cparams(),
        name="kvq_proj_sample" if with_q else "kv_proj_prompt",
    )(*args)


def _attn_prompt_kernel(sinks_ref, x_ref, q_ref, kc_ref, kp_ref, vc_ref, vp_ref, wo_ref, o_ref, a_scr,
                        *, tm, tiles_per_seq):
    i = pl.program_id(0)

    kj = lax.broadcasted_iota(I32, (2 * WINDOW, WINDOW), 0)
    qi = lax.broadcasted_iota(I32, (2 * WINDOW, WINDOW), 1)
    diff = qi + WINDOW - kj
    band = jnp.logical_and(diff >= 0, diff < WINDOW)
    first_half = lax.broadcasted_iota(I32, (WINDOW, PAIR), 1) < HEAD_DIM
    keep = (jnp.where(first_half, 1.0, 0.0).astype(BF16), jnp.where(first_half, 0.0, 1.0).astype(BF16))
    first_kj = jnp.where(lax.rem(i, tiles_per_seq) == 0, WINDOW, 0)

    for blk in range(tm // WINDOW):
        rows = slice(blk * WINDOW, (blk + 1) * WINDOW)
        if blk == 0:
            kk = jnp.concatenate([kp_ref[...], kc_ref[rows, :]], axis=0).astype(BF16)
            vv = jnp.concatenate([vp_ref[...], vc_ref[rows, :]], axis=0)
            valid = jnp.logical_and(band, kj >= first_kj)
        else:
            win = slice((blk - 1) * WINDOW, (blk + 1) * WINDOW)
            kk = kc_ref[win, :].astype(BF16)
            vv = vc_ref[win, :]
            valid = band
        for kvh in range(N_KV_HEADS):
            kk_h = kk[:, kvh * PAIR:(kvh + 1) * PAIR]
            vcols = slice((kvh // 2) * PAIR, (kvh // 2 + 1) * PAIR)
            vt = vv[:, vcols].T.astype(BF16)
            ch = slice((kvh % 2) * HEAD_DIM, (kvh % 2 + 1) * HEAD_DIM)
            outs = []
            for g in range(Q_PER_KV):
                head = kvh * Q_PER_KV + g
                qm = q_ref[rows, (head // 2) * PAIR:(head // 2 + 1) * PAIR] * keep[head % 2]
                sink = sinks_ref[head]
                st = lax.dot_general(kk_h, qm, (((1,), (1,)), ((), ())), preferred_element_type=F32)
                st = jnp.where(valid, st, -jnp.inf)
                m = jnp.maximum(jnp.max(st, axis=0, keepdims=True), sink)
                p = jnp.exp(st - m)
                den = jnp.sum(p, axis=0, keepdims=True) + jnp.exp(sink - m)
                ot = jnp.dot(vt, p.astype(BF16), preferred_element_type=F32)
                outs.append(ot[ch, :] * (1.0 / den))
            for pq in range(Q_PER_KV // 2):
                pair = (kvh * Q_PER_KV) // 2 + pq
                both = jnp.concatenate([outs[2 * pq], outs[2 * pq + 1]], axis=0)
                a_scr[rows, pair * PAIR:(pair + 1) * PAIR] = both.T.astype(BF16)
    o_ref[...] = x_ref[...] + jnp.dot(a_scr[...], wo_ref[...], preferred_element_type=F32)


def _attn_prompt(x, q, kdup, v, sinks, wo, *, tm, seq):
    n = x.shape[0]
    tiles_per_seq = seq // tm
    blocks_per_tile = tm // WINDOW

    def prev_spec(width):
        return pl.BlockSpec((WINDOW, width), lambda i: (jnp.maximum(i * blocks_per_tile - 1, 0), 0))

    return pl.pallas_call(
        functools.partial(_attn_prompt_kernel, tm=tm, tiles_per_seq=tiles_per_seq),
        grid=(n // tm,),
        in_specs=[pl.BlockSpec(memory_space=pltpu.SMEM), _rows(tm, D_MODEL), _rows(tm, D_MODEL),
                  _rows(tm, 2 * D_KV), prev_spec(2 * D_KV), _rows(tm, D_KV), prev_spec(D_KV), _resident(wo.shape)],
        out_specs=_rows(tm, D_MODEL),
        out_shape=jax.ShapeDtypeStruct((n, D_MODEL), F32),
        scratch_shapes=[pltpu.VMEM((tm, D_MODEL), BF16)],
        compiler_params=_cparams(),
        name="attn_prompt",
    )(sinks, x, q, kdup, kdup, v, v, wo)


def _attn_sample_kernel(q_ref, ck_ref, cv_ref, kn_ref, vn_ref, sink_ref, o_ref, *, per_step, win):
    rowi = lax.broadcasted_iota(I32, (N_HEADS, D_KV), 0)
    lanei = lax.broadcasted_iota(I32, (N_HEADS, D_KV), 1)
    group = rowi >> 2
    own_block = group == (lanei >> 6)
    j = lax.broadcasted_iota(I32, (N_HEADS, win), 1)
    diff = win - j
    valid = jnp.logical_and(diff >= 0, diff < WINDOW)
    sink = sink_ref[...]
    scale = HEAD_DIM ** -0.5
    for b in range(per_step):
        qm = jnp.where(own_block, q_ref[b], 0.0)
        s = lax.dot_general(qm.astype(BF16), ck_ref[b].astype(BF16), (((1,), (1,)), ((), ())),
                            preferred_element_type=F32) * scale
        s = jnp.where(valid, s, -jnp.inf)
        s_new = jnp.sum(qm * kn_ref[b:b + 1, :], axis=-1, keepdims=True) * scale
        m = jnp.maximum(jnp.maximum(jnp.max(s, axis=-1, keepdims=True), s_new), sink)
        p = jnp.exp(s - m)
        p_new = jnp.exp(s_new - m)
        den = jnp.sum(p, axis=-1, keepdims=True) + p_new + jnp.exp(sink - m)
        o = jnp.dot(p.astype(BF16), cv_ref[b].astype(BF16), preferred_element_type=F32) + p_new * vn_ref[b:b + 1, :]
        o = o * (1.0 / den)
        res = o
        for sft in range(1, N_KV_HEADS):
            res = jnp.where(group == sft, pltpu.roll(o, D_KV - sft * HEAD_DIM, 1), res)
        o_ref[b] = res[:, :HEAD_DIM]


def _attn_sample(q4, ck, cv, kn, vn, sink_col, *, per_step):
    nb, win = ck.shape[0], ck.shape[1]
    return pl.pallas_call(
        functools.partial(_attn_sample_kernel, per_step=per_step, win=win),
        grid=(nb // per_step,),
        in_specs=[pl.BlockSpec((per_step, N_HEADS, D_KV), lambda i: (i, 0, 0)),
                  pl.BlockSpec((per_step, win, D_KV), lambda i: (i, 0, 0)),
                  pl.BlockSpec((per_step, win, D_KV), lambda i: (i, 0, 0)),
                  _rows(per_step, D_KV), _rows(per_step, D_KV), _resident(sink_col.shape)],
        out_specs=pl.BlockSpec((per_step, N_HEADS, HEAD_DIM), lambda i: (i, 0, 0)),
        out_shape=jax.ShapeDtypeStruct((nb, N_HEADS, HEAD_DIM), F32),
        compiler_params=_cparams(),
        name="attn_sample",
    )(q4, ck, cv, kn, vn, sink_col)


def _residual_proj_kernel(x_ref, a_ref, w_ref, o_ref):
    o_ref[...] = x_ref[...] + jnp.dot(a_ref[...].astype(BF16), w_ref[...], preferred_element_type=F32)


def _residual_proj(x, a, w):
    n = x.shape[0]
    return pl.pallas_call(
        _residual_proj_kernel,
        grid=(1,),
        in_specs=[_rows(n, D_MODEL), _rows(n, a.shape[1]), _resident(w.shape)],
        out_specs=_rows(n, D_MODEL),
        out_shape=jax.ShapeDtypeStruct((n, D_MODEL), F32),
        compiler_params=_cparams(),
        name="attn_out_sample",
    )(x, a, w)


def _rope_tables(pos):
    half = ROPE_DIM // 2
    inv_freq = jnp.power(jnp.float32(ROPE_THETA), -jnp.arange(half, dtype=F32) / half)
    ang = pos.astype(F32)[:, None] * inv_freq[None, :]
    cos, sin = jnp.cos(ang), jnp.sin(ang)
    n = pos.shape[0]
    rest = HEAD_DIM - ROPE_DIM
    cos_h = jnp.concatenate([cos, cos, jnp.ones((n, rest), F32)], axis=1)
    sin_h = jnp.concatenate([-sin, sin, jnp.zeros((n, rest), F32)], axis=1)
    return jnp.tile(cos_h, (1, 2)), jnp.tile(sin_h, (1, 2))


def kernel(x_prompt, x_sample, cache_k, cache_v, a_norm_g, a_w_in, a_ln_g, a_ln_b, a_w_s, a_b_s, a_w_out, kv_norm_g, w_k, w_v, k_norm_g, b_norm_g, w_q, q_norm_g, sinks, w_o, ffn_norm_g, w_router, b_router, w_gu, b_gu, w_down, b_down):
    bsz, seq, _ = x_prompt.shape
    dec_b, dec_seq, _ = x_sample.shape
    assert dec_seq == 1 and seq % TM_DENSE == 0 and a_norm_g.shape[0] == 1 and b_norm_g.shape[0] == 1
    win = cache_k.shape[1]
    n_p, n_s = bsz * seq, dec_b * dec_seq
    xp = x_prompt.reshape(n_p, D_MODEL)
    xs = x_sample.reshape(n_s, D_MODEL)

    ng = a_norm_g[0].reshape(1, D_MODEL)
    win_bf = a_w_in[0].astype(BF16)
    wout_bf = a_w_out[0].astype(BF16)
    lng = a_ln_g[0].reshape(1, D_GATE)
    lnb = a_ln_b[0].reshape(1, D_GATE)
    bias_full = jnp.repeat(a_b_s[0].T, GROUP_DIM_A, axis=1)
    diag_row = jnp.repeat(a_w_s[0][:, 0, 0], GROUP_DIM_A).reshape(1, D_GATE)
    (xp,) = _mixer_a(xp, ng, win_bf, lng, lnb, a_w_s[0], bias_full, wout_bf, tm=TM_DENSE, single_token_chunks=False)
    xs, v_rows = _mixer_a(xs, ng, win_bf, lng, lnb, diag_row, bias_full[0:1], wout_bf, tm=n_s,
                          single_token_chunks=True)

    kvg = kv_norm_g.reshape(1, D_MODEL)
    bng = b_norm_g[0].reshape(1, D_MODEL)
    w_k_dup = jnp.repeat(w_k.reshape(D_MODEL, N_KV_HEADS, 1, HEAD_DIM), 2, axis=2).reshape(D_MODEL, 2 * D_KV)
    wk_bf, wv_bf, wq_bf, wo_bf = (w.astype(BF16) for w in (w_k_dup, w_v, w_q[0], w_o[0]))
    kng2 = jnp.tile(k_norm_g, 2).reshape(1, PAIR)
    qng2 = jnp.tile(q_norm_g[0], 2).reshape(1, PAIR)
    cos_p, sin_p = _rope_tables(jnp.arange(seq, dtype=I32))
    cos_s, sin_s = _rope_tables(jnp.full((n_s,), PAST_LEN, I32))
    attn_pre = (kvg, wk_bf, wv_bf, kng2, bng, wq_bf, qng2, cos_p, sin_p, seq // TM_COMBINE)
    (xp, k_p, kdup_p, v_p, q_p), (xs,) = _moe([xp, xs], ffn_norm_g[0], w_router[0], b_router[0], w_gu, b_gu,
                                              w_down, b_down, 0, prompt_attn_pre=attn_pre)

    k_s, _, v_s, q_s = _kv_proj(xs, kvg, wk_bf, wv_bf, kng2, cos_s, sin_s, (bng, wq_bf, qng2), tm=n_s,
                                table_tiles=1)
    xp = _attn_prompt(xp, q_p, kdup_p, v_p, sinks[0], wo_bf, tm=TM_DENSE, seq=seq)
    q4 = jnp.tile(q_s.reshape(n_s, N_HEADS, HEAD_DIM), (1, 1, N_KV_HEADS))
    attn_s = _attn_sample(q4, cache_k.reshape(dec_b, win, D_KV), cache_v.reshape(dec_b, win, D_KV), k_s, v_s,
                          sinks[0].reshape(N_HEADS, 1), per_step=8)
    xs = _residual_proj(xs, attn_s.reshape(n_s, D_MODEL), wo_bf)
    (xp,), (xs,) = _moe([xp, xs], ffn_norm_g[1], w_router[1], b_router[1], w_gu, b_gu, w_down, b_down, 1)

    def last_window(a):
        a = a.reshape(bsz, seq, D_KV)[:, -WINDOW:]
        return a.reshape(bsz, min(WINDOW, seq), N_KV_HEADS, HEAD_DIM)

    return (xp.reshape(bsz, seq, D_MODEL), xs.reshape(dec_b, dec_seq, D_MODEL),
            v_rows.reshape(1, dec_b, dec_seq, D_GATE), last_window(k_p), last_window(v_p),
            k_s.reshape(dec_b, dec_seq, N_KV_HEADS, HEAD_DIM), v_s.reshape(dec_b, dec_seq, N_KV_HEADS, HEAD_DIM))
```

```python
import functools

import jax
import jax.numpy as jnp
from jax import lax
from jax.experimental import pallas as pl
from jax.experimental.pallas import tpu as pltpu

F32, BF16, I32 = jnp.float32, jnp.bfloat16, jnp.int32

D_MODEL = 1024
CHUNK = 128
D_GATE = 2 * D_MODEL
N_GROUPS_A = 8
GROUP_DIM_A = D_GATE // N_GROUPS_A
HEAD_DIM = 64
N_HEADS = D_MODEL // HEAD_DIM
N_KV_HEADS = N_HEADS // 4
Q_PER_KV = N_HEADS // N_KV_HEADS
D_KV = N_KV_HEADS * HEAD_DIM
WINDOW = 128
ROPE_DIM = HEAD_DIM // 4
ROPE_THETA = 500000.0
N_EXPERTS = 32
TOP_K = 4
D_EXPERT = D_MODEL
SWIGLU_LIMIT = 7.0
SWIGLU_ALPHA = 1.702
EPS = 1e-6
PAST_LEN = 8192

V7X_LANES = 128
V7X_VMEM_BYTES = 64 * 1024 * 1024
VMEM_LIMIT_BYTES = V7X_VMEM_BYTES - 8 * 1024 * 1024

TM_DENSE = 512
TM_COMBINE = 256
MOE_ROWS = 512
TM_DISPATCH = 256
FILL_BITS = MOE_ROWS.bit_length() - 1
TOKEN_SUBLANES = D_MODEL // V7X_LANES
PACKED_SUBLANES = TOKEN_SUBLANES // 2
U32 = jnp.uint32
SLAB = V7X_LANES
REC_ROWS = 16
PAIR = 2 * HEAD_DIM
INV_SQRT2 = 0.7071067811865476


def _cparams():
    return pltpu.CompilerParams(dimension_semantics=("arbitrary",), vmem_limit_bytes=VMEM_LIMIT_BYTES)


def _resident(shape):
    zeros = (0,) * len(shape)
    return pl.BlockSpec(shape, lambda i, *_: zeros, pipeline_mode=pl.Buffered(1))


def _rows(tm, width, offset=0):
    return pl.BlockSpec((tm, width), lambda i, *_: (i + offset, 0))


def _rms(x, g):
    return x * lax.rsqrt(jnp.mean(x * x, axis=-1, keepdims=True) + EPS) * g


def _pair_norm_rope(x2, g2, cos2, sin2):
    lane = lax.broadcasted_iota(I32, x2.shape, 1)
    first = lane < HEAD_DIM
    sq = x2 * x2
    s0 = jnp.sum(jnp.where(first, sq, 0.0), axis=-1, keepdims=True)
    s1 = jnp.sum(jnp.where(first, 0.0, sq), axis=-1, keepdims=True)
    ms = jnp.where(first, s0, s1) * (1.0 / HEAD_DIM)
    y = x2 * lax.rsqrt(ms + EPS) * g2
    half = ROPE_DIM // 2
    up = pltpu.roll(y, PAIR - half, 1)
    dn = pltpu.roll(y, half, 1)
    partner = jnp.where((lane & (HEAD_DIM - 1)) < half, up, dn)
    return y * cos2 + partner * sin2


def _mixer_a_kernel(x_ref, ng_ref, win_ref, lng_ref, lnb_ref, mix_ref, bias_ref, wout_ref, *refs,
                    tm, single_token_chunks):
    if single_token_chunks:
        o_ref, vout_ref, u_scr, v_scr, p_scr = refs
    else:
        o_ref, u_scr, v_scr, p_scr = refs
    x = x_ref[...]
    h = _rms(x, ng_ref[...]).astype(BF16)
    nc = 512
    for c in range(2 * D_GATE // nc):
        z = jnp.dot(h, win_ref[:, c * nc:(c + 1) * nc], preferred_element_type=F32)
        z = 0.5 * z * (1.0 + lax.erf(z * INV_SQRT2))
        if c < D_GATE // nc:
            u_scr[:, c * nc:(c + 1) * nc] = z
        else:
            v_scr[:, (c - D_GATE // nc) * nc:(c - D_GATE // nc + 1) * nc] = z
    v = v_scr[...]
    vc = v - jnp.mean(v, axis=-1, keepdims=True)
    vn = vc * lax.rsqrt(jnp.mean(vc * vc, axis=-1, keepdims=True) + EPS) * lng_ref[...] + lnb_ref[...]
    if single_token_chunks:
        vout_ref[...] = vn
        p_scr[...] = (u_scr[...] * (vn * mix_ref[...] + bias_ref[...])).astype(BF16)
    else:
        v_scr[...] = vn
        row = lax.broadcasted_iota(I32, (CHUNK, CHUNK), 0)
        col = lax.broadcasted_iota(I32, (CHUNK, CHUNK), 1)
        for g in range(N_GROUPS_A):
            wc = jnp.where(row >= col, mix_ref[g], 0.0).astype(BF16)
            cols = slice(g * GROUP_DIM_A, (g + 1) * GROUP_DIM_A)
            for c in range(tm // CHUNK):
                rows = slice(c * CHUNK, (c + 1) * CHUNK)
                mixed = jnp.dot(wc, v_scr[rows, cols].astype(BF16), preferred_element_type=F32)
                mixed = mixed + bias_ref[:, cols]
                p_scr[rows, cols] = (u_scr[rows, cols] * mixed).astype(BF16)
    o_ref[...] = x + jnp.dot(p_scr[...], wout_ref[...], preferred_element_type=F32)


def _mixer_a(x, ng, win, lng, lnb, mix, bias, wout, *, tm, single_token_chunks):
    n = x.shape[0]
    out_shape = [jax.ShapeDtypeStruct((n, D_MODEL), F32)]
    out_specs = [_rows(tm, D_MODEL)]
    if single_token_chunks:
        out_shape.append(jax.ShapeDtypeStruct((n, D_GATE), F32))
        out_specs.append(_rows(tm, D_GATE))
    return pl.pallas_call(
        functools.partial(_mixer_a_kernel, tm=tm, single_token_chunks=single_token_chunks),
        grid=(n // tm,),
        in_specs=[_rows(tm, D_MODEL), _resident(ng.shape), _resident(win.shape), _resident(lng.shape),
                  _resident(lnb.shape), _resident(mix.shape), _resident(bias.shape), _resident(wout.shape)],
        out_specs=out_specs,
        out_shape=out_shape,
        scratch_shapes=[pltpu.VMEM((tm, D_GATE), F32), pltpu.VMEM((tm, D_GATE), F32),
                        pltpu.VMEM((tm, D_GATE), BF16)],
        compiler_params=_cparams(),
        name="mixer_a_sample" if single_token_chunks else "mixer_a_prompt",
    )(x, ng, win, lng, lnb, mix, bias, wout)


def _route_rows(x, g, whi_t, wlo_t, bias_col, run_scr):
    tm = x.shape[0]
    h = _rms(x, g)
    hh = h.astype(BF16)
    hl = (h - hh.astype(F32)).astype(BF16)
    nt = (((1,), (1,)), ((), ()))
    logits = (lax.dot_general(whi_t, hh, nt, preferred_element_type=F32)
              + lax.dot_general(wlo_t, hh, nt, preferred_element_type=F32)
              + lax.dot_general(whi_t, hl, nt, preferred_element_type=F32) + bias_col)
    e = lax.broadcasted_iota(I32, (N_EXPERTS, tm), 0).astype(F32)
    hot, vals, idxs = [], [], []
    for _ in range(TOP_K):
        m = jnp.max(logits, axis=0, keepdims=True)
        idx = jnp.min(jnp.where(logits == m, e, float(N_EXPERTS)), axis=0, keepdims=True)
        oh = e == idx
        hot.append(oh)
        vals.append(m)
        idxs.append(idx)
        logits = jnp.where(oh, -jnp.inf, logits)
    ex = [jnp.exp(v - vals[0]) for v in vals]
    den = ex[0] + ex[1] + ex[2] + ex[3]
    picked = jnp.zeros((N_EXPERTS, tm), F32)
    for oh in hot:
        picked = picked + jnp.where(oh, 1.0, 0.0)
    r = lax.broadcasted_iota(I32, (tm, tm), 0)
    c = lax.broadcasted_iota(I32, (tm, tm), 1)
    earlier = jnp.where(r < c, 1.0, 0.0).astype(BF16)
    run = run_scr[:, 0:1]
    pos = jnp.dot(picked.astype(BF16), earlier, preferred_element_type=F32) + run
    run_scr[:, 0:1] = run + jnp.sum(picked, axis=1, keepdims=True)
    row = lax.broadcasted_iota(I32, (REC_ROWS, tm), 0)
    rec = jnp.zeros((REC_ROWS, tm), F32)
    for k in range(TOP_K):
        pos_k = jnp.sum(jnp.where(hot[k], pos, 0.0), axis=0, keepdims=True)
        rec = jnp.where(row == k, idxs[k], rec)
        rec = jnp.where(row == TOP_K + k, pos_k, rec)
        rec = jnp.where(row == 2 * TOP_K + k, ex[k] / den, rec)
    return h, rec


def _router_kernel(xp_ref, xs_ref, g_ref, whi_ref, wlo_ref, b_ref, h_ref, slabp_ref, slabs_ref, rec_ref, cnt_ref,
                   run_scr, *, n_tiles, n_s):
    i = pl.program_id(0)

    @pl.when(i == 0)
    def _():
        run_scr[...] = jnp.zeros_like(run_scr)

    def pad_rows(rec):
        return jnp.concatenate([rec, jnp.zeros((SLAB - REC_ROWS, rec.shape[1]), F32)], axis=0)

    @pl.when(i < n_tiles)
    def _():
        h, rec = _route_rows(xp_ref[...], g_ref[...], whi_ref[...], wlo_ref[...], b_ref[...], run_scr)
        _store_packed_tiles(h_ref, h, h.shape[0])
        slabp_ref[...] = pad_rows(rec).T
        rec_ref[...] = rec[0:2 * TOP_K, :]

    @pl.when(i == n_tiles)
    def _():
        h, rec = _route_rows(xs_ref[...], g_ref[...], whi_ref[...], wlo_ref[...], b_ref[...], run_scr)
        h_ref[n_s * PACKED_SUBLANES:, :] = jnp.zeros((h_ref.shape[0] - n_s * PACKED_SUBLANES, V7X_LANES), U32)
        _store_packed_tiles(h_ref, h, n_s)
        slabs_ref[...] = pad_rows(rec).T
        rec_ref[...] = jnp.zeros_like(rec_ref)
        rec_ref[:, 0:n_s] = rec[0:2 * TOP_K, :]

    cnt_ref[...] = run_scr[...]


def _store_packed_tiles(ref, x, n):
    half = D_MODEL // 2
    lo = pltpu.bitcast(x[:, :half].astype(BF16).astype(F32), U32)
    hi = pltpu.bitcast(x[:, half:].astype(BF16).astype(F32), U32)
    word = lax.shift_right_logical(lo, jnp.uint32(16)) | (hi & jnp.uint32(0xFFFF0000))
    for s in range(PACKED_SUBLANES):
        ref[pl.ds(s, n, stride=PACKED_SUBLANES), :] = word[:, s * V7X_LANES:(s + 1) * V7X_LANES]


def _load_packed_tiles(ref, n):
    words = [ref[pl.ds(s, n, stride=PACKED_SUBLANES), :] for s in range(PACKED_SUBLANES)]
    lo = [pltpu.bitcast(w << jnp.uint32(16), F32) for w in words]
    hi = [pltpu.bitcast(w & jnp.uint32(0xFFFF0000), F32) for w in words]
    return jnp.concatenate(lo + hi, axis=1).astype(BF16)


def _store_token_tiles(ref, x, n):
    for s in range(TOKEN_SUBLANES):
        ref[pl.ds(s, n, stride=TOKEN_SUBLANES), :] = x[:, s * V7X_LANES:(s + 1) * V7X_LANES]


def _load_token_tiles(ref, n):
    return jnp.concatenate([ref[pl.ds(s, n, stride=TOKEN_SUBLANES), :] for s in range(TOKEN_SUBLANES)], axis=1)


def _router(xp, xs, g, whi, wlo, b, *, tm):
    n_p, n_s = xp.shape[0], xs.shape[0]
    n_tiles = n_p // tm
    last = n_tiles - 1
    return pl.pallas_call(
        functools.partial(_router_kernel, n_tiles=n_tiles, n_s=n_s),
        grid=(n_tiles + 1,),
        in_specs=[pl.BlockSpec((tm, D_MODEL), lambda i: (jnp.minimum(i, last), 0)), _resident(xs.shape),
                  _resident(g.shape), _resident(whi.shape), _resident(wlo.shape), _resident(b.shape)],
        out_specs=[_rows(tm * PACKED_SUBLANES, V7X_LANES),
                   pl.BlockSpec((tm, SLAB), lambda i: (jnp.minimum(i, last), 0)),
                   pl.BlockSpec((n_s, SLAB), lambda i: (0, 0)),
                   pl.BlockSpec((2 * TOP_K, tm), lambda i: (0, i)),
                   pl.BlockSpec((N_EXPERTS, V7X_LANES), lambda i: (0, 0))],
        out_shape=[jax.ShapeDtypeStruct(((n_p + tm) * PACKED_SUBLANES, V7X_LANES), U32),
                   jax.ShapeDtypeStruct((n_p, SLAB), F32),
                   jax.ShapeDtypeStruct((n_s, SLAB), F32),
                   jax.ShapeDtypeStruct((2 * TOP_K, n_p + tm), F32),
                   jax.ShapeDtypeStruct((N_EXPERTS, V7X_LANES), F32)],
        scratch_shapes=[pltpu.VMEM((N_EXPERTS, V7X_LANES), F32)],
        compiler_params=_cparams(),
        name="moe_router",
    )(xp, xs, g, whi, wlo, b)


def _dispatch_kernel(fill_start_ref, fill_len_ref, dest_ref, h_ref, xs_ref, zbuf, sem, fill_sem,
                     *, tm, n_tiles, n_last, rows, n_blocks):
    i = pl.program_id(0)
    ts = PACKED_SUBLANES

    def tile_copy(r, k):
        d = dest_ref[0, 0, k * tm + r]
        src = h_ref.at[pl.ds(pl.multiple_of(r * ts, ts), ts)]
        dst = xs_ref.at[pl.ds(pl.multiple_of(d * ts, ts), ts)]
        return pltpu.make_async_copy(src, dst, sem)

    def scatter(n):
        def body(r, carry):
            for k in range(TOP_K):
                tile_copy(r, k).start(priority=k % 2)
            return carry
        lax.fori_loop(0, n, body, 0, unroll=4)
        for k in range(TOP_K):
            pltpu.make_async_copy(h_ref.at[pl.ds(0, n * ts)], xs_ref.at[pl.ds(0, n * ts)], sem).wait()

    def fill_copy(e, bit):
        size = 1 << bit
        length = fill_len_ref[e]
        done = length - (length & (2 * size - 1))
        off = pl.multiple_of((fill_start_ref[e] + done) * ts, ts)
        return (length & size) != 0, pltpu.make_async_copy(zbuf.at[pl.ds(0, size * ts)],
                                                            xs_ref.at[pl.ds(off, size * ts)], fill_sem)

    def tail_copy(blk):
        off = pl.multiple_of(blk * (rows * ts), rows * ts)
        return pltpu.make_async_copy(zbuf, xs_ref.at[pl.ds(off, rows * ts)], fill_sem)

    def for_each_fill(act):
        for e in range(N_EXPERTS):
            for bit in range(FILL_BITS):
                go, cp = fill_copy(e, bit)

                @pl.when(go)
                def _():
                    act(cp)

        def body(blk, carry):
            act(tail_copy(blk))
            return carry
        lax.fori_loop(fill_start_ref[N_EXPERTS], n_blocks, body, 0)

    @pl.when(i == 0)
    def _():
        zbuf[...] = jnp.zeros_like(zbuf)
        for_each_fill(lambda cp: cp.start())

    @pl.when(i < n_tiles - 1)
    def _():
        scatter(tm)

    @pl.when(i == n_tiles - 1)
    def _():
        scatter(n_last)
        for_each_fill(lambda cp: cp.wait())


def _dispatch(fill_start, fill_len, dest3, h_tiles, n_all, n_blocks, *, tm, rows):
    n_tiles = dest3.shape[0]
    n_last = n_all - (n_tiles - 1) * tm
    ts = PACKED_SUBLANES
    n_slots = n_blocks * rows
    grid_spec = pltpu.PrefetchScalarGridSpec(
        num_scalar_prefetch=2,
        grid=(n_tiles,),
        in_specs=[pl.BlockSpec((1, 1, TOP_K * tm), lambda i, *_: (i, 0, 0), memory_space=pltpu.SMEM),
                  _rows(tm * ts, V7X_LANES)],
        out_specs=pl.BlockSpec(memory_space=pl.ANY),
        scratch_shapes=[pltpu.VMEM((rows * ts, V7X_LANES), U32),
                        pltpu.SemaphoreType.DMA, pltpu.SemaphoreType.DMA],
    )
    return pl.pallas_call(
        functools.partial(_dispatch_kernel, tm=tm, n_tiles=n_tiles, n_last=n_last, rows=rows, n_blocks=n_blocks),
        grid_spec=grid_spec,
        out_shape=jax.ShapeDtypeStruct((n_slots * ts, V7X_LANES), U32),
        compiler_params=_cparams(),
        name="moe_dispatch",
    )(fill_start, fill_len, dest3, h_tiles)


def _expert_kernel(be_ref, nact_ref, run_ref, rune_ref, x_ref, wgu_hbm, bgu_ref, wdn_hbm, bdn_ref, o_ref,
                   wgu_f32, wdn_f32, wsem, wgu_bf, wdn_bf, *, rows, layer):
    b = pl.program_id(0)
    nact = nact_ref[0]

    def weight_copies(r, slot):
        e = rune_ref[r]
        return (pltpu.make_async_copy(wgu_hbm.at[layer, e], wgu_f32.at[slot], wsem.at[0, slot]),
                pltpu.make_async_copy(wdn_hbm.at[layer, e], wdn_f32.at[slot], wsem.at[1, slot]))

    @pl.when(b == 0)
    def _():
        for cp in weight_copies(0, 0):
            cp.start()

    @pl.when(b < nact)
    def _():
        r = run_ref[b]
        slot = lax.rem(r, 2)
        changed = jnp.logical_or(b == 0, r != run_ref[jnp.maximum(b - 1, 0)])

        @pl.when(changed)
        def _():
            for cp in weight_copies(r, slot):
                cp.wait()

            @pl.when(r + 1 < rune_ref[N_EXPERTS])
            def _():
                for cp in weight_copies(r + 1, 1 - slot):
                    cp.start()

            step = 256
            for c in range(D_MODEL // step):
                wgu_bf[c * step:(c + 1) * step, :] = wgu_f32[slot, c * step:(c + 1) * step, :].astype(BF16)
                wdn_bf[c * step:(c + 1) * step, :] = wdn_f32[slot, c * step:(c + 1) * step, :].astype(BF16)

        xb = _load_packed_tiles(x_ref, rows)
        nc = 512
        out = None
        for c in range(D_EXPERT // nc):
            gcols = slice(c * nc, (c + 1) * nc)
            ucols = slice(D_EXPERT + c * nc, D_EXPERT + (c + 1) * nc)
            gate = jnp.dot(xb, wgu_bf[:, gcols], preferred_element_type=F32) + bgu_ref[:, gcols]
            up = jnp.dot(xb, wgu_bf[:, ucols], preferred_element_type=F32) + bgu_ref[:, ucols]
            gate = jnp.minimum(gate, SWIGLU_LIMIT)
            up = jnp.clip(up, -SWIGLU_LIMIT, SWIGLU_LIMIT)
            act = ((up + 1.0) * gate * (1.0 / (1.0 + jnp.exp(-SWIGLU_ALPHA * gate)))).astype(BF16)
            part = jnp.dot(act, wdn_bf[gcols, :], preferred_element_type=F32)
            out = part if out is None else out + part
        _store_token_tiles(o_ref, out + bdn_ref[...], rows)

    @pl.when(b >= nact)
    def _():
        o_ref[...] = jnp.zeros_like(o_ref)


def _experts(block_e, n_active, block_run, run_e, xs_tiles, wgu, bgu, wdn, bdn, *, rows, layer):
    ts = TOKEN_SUBLANES
    nb = xs_tiles.shape[0] // (rows * PACKED_SUBLANES)
    grid_spec = pltpu.PrefetchScalarGridSpec(
        num_scalar_prefetch=4,
        grid=(nb,),
        in_specs=[
            pl.BlockSpec((rows * PACKED_SUBLANES, V7X_LANES), lambda b, be, na, *_: (jnp.minimum(b, na[0] - 1), 0)),
            pl.BlockSpec(memory_space=pl.ANY),
            pl.BlockSpec((None, None, 1, 2 * D_EXPERT), lambda b, be, *_: (layer, be[b], 0, 0)),
            pl.BlockSpec(memory_space=pl.ANY),
            pl.BlockSpec((None, None, 1, D_MODEL), lambda b, be, *_: (layer, be[b], 0, 0)),
        ],
        out_specs=pl.BlockSpec((rows * ts, V7X_LANES), lambda b, *_: (b, 0)),
        scratch_shapes=[pltpu.VMEM((2, D_MODEL, 2 * D_EXPERT), F32), pltpu.VMEM((2, D_EXPERT, D_MODEL), F32),
                        pltpu.SemaphoreType.DMA((2, 2)),
                        pltpu.VMEM((D_MODEL, 2 * D_EXPERT), BF16), pltpu.VMEM((D_EXPERT, D_MODEL), BF16)],
    )
    return pl.pallas_call(
        functools.partial(_expert_kernel, rows=rows, layer=layer),
        grid_spec=grid_spec,
        out_shape=jax.ShapeDtypeStruct((nb * rows * ts, V7X_LANES), F32),
        compiler_params=_cparams(),
        name="moe_experts",
    )(block_e, n_active, block_run, run_e, xs_tiles, wgu, bgu, wdn, bdn)


def _combine_kernel(dest_first_ref, dest_next_ref, x_ref, slab_ref, ebuf_ref, *refs, tm, n_tiles, attn_pre):
    if attn_pre:
        (kvg_ref, wk_ref, wv_ref, kng_ref, bng_ref, wq_ref, qng_ref, cos_ref, sin_ref,
         o_ref, k_ref, kdup_ref, v_ref, q_ref, rbuf, sem) = refs
    else:
        o_ref, rbuf, sem = refs
    i = pl.program_id(0)
    slot = lax.rem(i, 2)
    ts = TOKEN_SUBLANES

    def gather(dest_ref, s):
        def body(r, carry):
            for k in range(TOP_K):
                d = dest_ref[0, 0, k * tm + r]
                src = ebuf_ref.at[pl.ds(pl.multiple_of(d * ts, ts), ts)]
                dst = rbuf.at[s, k, pl.ds(pl.multiple_of(r * ts, ts), ts)]
                pltpu.make_async_copy(src, dst, sem.at[s]).start(priority=k % 2)
            return carry
        lax.fori_loop(0, tm, body, 0, unroll=4)

    @pl.when(i == 0)
    def _():
        gather(dest_first_ref, 0)

    @pl.when(i + 1 < n_tiles)
    def _():
        gather(dest_next_ref, 1 - slot)

    for k in range(TOP_K):
        pltpu.make_async_copy(ebuf_ref.at[pl.ds(0, tm * ts)], rbuf.at[slot, k], sem.at[slot]).wait()
    slab = slab_ref[...]
    gates = [slab[:, 2 * TOP_K + k:2 * TOP_K + k + 1] for k in range(TOP_K)]
    for s in range(ts):
        cols = slice(s * V7X_LANES, (s + 1) * V7X_LANES)
        acc = x_ref[:, cols]
        for k in range(TOP_K):
            acc = acc + gates[k] * rbuf[slot, k, pl.ds(s, tm, stride=ts), :]
        o_ref[:, cols] = acc
    if attn_pre:
        x2 = o_ref[...]
        cos2, sin2 = cos_ref[...], sin_ref[...]
        _shared_kv(x2, kvg_ref[...], wk_ref, wv_ref, kng_ref[...], cos2, sin2, k_ref, kdup_ref, v_ref)
        hq = _rms(x2, bng_ref[...]).astype(BF16)
        q = jnp.dot(hq, wq_ref[...], preferred_element_type=F32)
        for p in range(D_MODEL // PAIR):
            cols = slice(p * PAIR, (p + 1) * PAIR)
            q_ref[:, cols] = (_pair_norm_rope(q[:, cols], qng_ref[...], cos2, sin2) * HEAD_DIM ** -0.5).astype(BF16)


def _shared_kv(x, kvg, wk_ref, wv_ref, kng2, cos2, sin2, k_ref, kdup_ref, v_ref):
    h = _rms(x, kvg).astype(BF16)
    kd = jnp.dot(h, wk_ref[...], preferred_element_type=F32)
    v_ref[...] = jnp.dot(h, wv_ref[...], preferred_element_type=F32)
    first_half = lax.broadcasted_iota(I32, (x.shape[0], PAIR), 1) < HEAD_DIM
    slabs = []
    for p in range(N_KV_HEADS):
        cols = slice(p * PAIR, (p + 1) * PAIR)
        slab = _pair_norm_rope(kd[:, cols], kng2, cos2, sin2)
        kdup_ref[:, cols] = slab
        slabs.append(slab)
    for p in range(D_KV // PAIR):
        k_ref[:, p * PAIR:(p + 1) * PAIR] = jnp.where(first_half, slabs[2 * p], slabs[2 * p + 1])


def _combine(x, slab, dest3, ebuf, *, tm, attn_pre=None):
    n = x.shape[0]
    n_tiles = n // tm
    in_specs = [
        pl.BlockSpec((1, 1, TOP_K * tm), lambda i: (0, 0, 0), memory_space=pltpu.SMEM),
        pl.BlockSpec((1, 1, TOP_K * tm), lambda i: (jnp.minimum(i + 1, n_tiles - 1), 0, 0),
                     memory_space=pltpu.SMEM),
        _rows(tm, D_MODEL), _rows(tm, SLAB), pl.BlockSpec(memory_space=pl.ANY),
    ]
    args = [dest3, dest3, x, slab, ebuf]
    out_specs = [_rows(tm, D_MODEL)]
    out_shape = [jax.ShapeDtypeStruct((n, D_MODEL), F32)]
    if attn_pre is not None:
        *weights, cos2, sin2, table_tiles = attn_pre
        table_spec = pl.BlockSpec((tm, PAIR), lambda i: (lax.rem(i, table_tiles), 0))
        in_specs += [_resident(w.shape) for w in weights] + [table_spec, table_spec]
        args += list(weights) + [cos2, sin2]
        out_specs += [_rows(tm, D_KV), _rows(tm, 2 * D_KV), _rows(tm, D_KV), _rows(tm, D_MODEL)]
        out_shape += [jax.ShapeDtypeStruct((n, D_KV), F32), jax.ShapeDtypeStruct((n, 2 * D_KV), F32),
                      jax.ShapeDtypeStruct((n, D_KV), F32), jax.ShapeDtypeStruct((n, D_MODEL), BF16)]
    return pl.pallas_call(
        functools.partial(_combine_kernel, tm=tm, n_tiles=n_tiles, attn_pre=attn_pre is not None),
        grid=(n_tiles,),
        in_specs=in_specs, out_specs=out_specs, out_shape=out_shape,
        scratch_shapes=[pltpu.VMEM((2, TOP_K, tm * TOKEN_SUBLANES, V7X_LANES), F32),
                        pltpu.SemaphoreType.DMA((2,))],
        compiler_params=_cparams(),
        name="moe_combine_attn_pre" if attn_pre is not None else "moe_combine",
    )(*args)


def _moe(xs, g, w_router, b_router, wgu, bgu, wdn, bdn, layer, prompt_attn_pre=None):
    n_all = xs[0].shape[0] + xs[1].shape[0]
    w_t = w_router.T
    whi = w_t.astype(BF16)
    wlo = (w_t - whi.astype(F32)).astype(BF16)
    h_all, slab_p, slab_s, rec, cnt = _router(xs[0], xs[1], g.reshape(1, D_MODEL), whi, wlo,
                                              b_router.reshape(N_EXPERTS, 1), tm=TM_DENSE)
    slabs = [slab_p, slab_s]
    idx = rec[:TOP_K, :n_all].astype(I32)
    pos = rec[TOP_K:, :n_all].astype(I32)
    counts = cnt[:, 0].astype(I32)
    rows = MOE_ROWS
    nb = (n_all * TOP_K + N_EXPERTS * (rows - 1)) // rows
    padded = (counts + rows - 1) // rows * rows
    pad_end = jnp.cumsum(padded)
    pad_start = pad_end - padded
    eid = jnp.arange(N_EXPERTS, dtype=I32)
    has_rows = counts > 0
    dest = pos + jnp.sum(jnp.where(idx[None] == eid[:, None, None], pad_start[:, None, None], 0), axis=0)
    n_active = (pad_end[-1] // rows).astype(I32)
    blk = jnp.arange(nb, dtype=I32)
    block_e = jnp.sum((pad_end[None, :] <= (blk * rows)[:, None]).astype(I32), axis=1)
    last_e = jnp.max(jnp.where(has_rows, eid, 0))
    block_e = jnp.where(blk < n_active, jnp.minimum(block_e, N_EXPERTS - 1), last_e)

    def tile_table(d, tm):
        n_tiles = -(-d.shape[1] // tm)
        d = jnp.pad(d, ((0, 0), (0, n_tiles * tm - d.shape[1])))
        return d.reshape(TOP_K, n_tiles, tm).transpose(1, 0, 2).reshape(n_tiles, 1, TOP_K * tm)

    fill_start = jnp.concatenate([pad_start + counts, n_active.reshape(1)])
    xs_tiles = _dispatch(fill_start, padded - counts, tile_table(dest, TM_DISPATCH), h_all,
                         n_all, nb, tm=TM_DISPATCH, rows=rows)
    rank = jnp.cumsum(has_rows.astype(I32)) - 1
    run_e = jnp.sum(jnp.where(has_rows[None, :] & (rank[None, :] == eid[:, None]), eid[None, :], 0), axis=1)
    run_e = jnp.concatenate([run_e, jnp.sum(has_rows.astype(I32)).reshape(1)]).astype(I32)
    block_run = jnp.sum(jnp.where(block_e[:, None] == eid, rank, 0), axis=1).astype(I32)
    n_l = wgu.shape[0]
    ebuf = _experts(block_e, n_active.reshape(1), block_run, run_e, xs_tiles,
                    wgu, bgu.reshape(n_l, N_EXPERTS, 1, 2 * D_EXPERT), wdn, bdn.reshape(n_l, N_EXPERTS, 1, D_MODEL),
                    rows=rows, layer=layer)
    outs, off = [], 0
    for x, slab, pre in zip(xs, slabs, (prompt_attn_pre, None)):
        n = x.shape[0]
        tm = min(TM_COMBINE, n)
        outs.append(_combine(x, slab, tile_table(dest[:, off:off + n], tm), ebuf, tm=tm, attn_pre=pre))
        off += n
    return outs


def _kv_kernel(x_ref, kvg_ref, wk_ref, wv_ref, kng_ref, cos_ref, sin_ref, *refs, with_q):
    if with_q:
        bng_ref, wq_ref, qng_ref, k_ref, kdup_ref, v_ref, q_ref = refs
    else:
        k_ref, kdup_ref, v_ref = refs
    x = x_ref[...]
    cos2, sin2 = cos_ref[...], sin_ref[...]
    _shared_kv(x, kvg_ref[...], wk_ref, wv_ref, kng_ref[...], cos2, sin2, k_ref, kdup_ref, v_ref)
    if with_q:
        hq = _rms(x, bng_ref[...]).astype(BF16)
        q = jnp.dot(hq, wq_ref[...], preferred_element_type=F32)
        for p in range(D_MODEL // PAIR):
            cols = slice(p * PAIR, (p + 1) * PAIR)
            q_ref[:, cols] = _pair_norm_rope(q[:, cols], qng_ref[...], cos2, sin2)


def _kv_proj(x, kvg, wk, wv, kng2, cos2, sin2, q_args, *, tm, table_tiles):
    n = x.shape[0]
    with_q = q_args is not None
    table_spec = pl.BlockSpec((tm, PAIR), lambda i: (lax.rem(i, table_tiles), 0))
    in_specs = [_rows(tm, D_MODEL), _resident(kvg.shape), _resident(wk.shape), _resident(wv.shape),
                _resident(kng2.shape), table_spec, table_spec]
    args = [x, kvg, wk, wv, kng2, cos2, sin2]
    out_specs = [_rows(tm, D_KV), _rows(tm, 2 * D_KV), _rows(tm, D_KV)]
    out_shape = [jax.ShapeDtypeStruct((n, D_KV), F32), jax.ShapeDtypeStruct((n, 2 * D_KV), F32),
                 jax.ShapeDtypeStruct((n, D_KV), F32)]
    if with_q:
        in_specs += [_resident(a.shape) for a in q_args]
        args += list(q_args)
        out_specs.append(_rows(tm, D_MODEL))
        out_shape.append(jax.ShapeDtypeStruct((n, D_MODEL), F32))
    return pl.pallas_call(
        functools.partial(_kv_kernel, with_q=with_q),
        grid=(n // tm,),
        in_specs=in_specs, out_specs=out_specs, out_shape=out_shape,
        compiler_params=_cparams(),
        name="kvq_proj_sample" if with_q else "kv_proj_prompt",
    )(*args)


def _attn_prompt_kernel(sinks_ref, x_ref, q_ref, kc_ref, kp_ref, vc_ref, vp_ref, wo_ref, o_ref, a_scr,
                        *, tm, tiles_per_seq):
    i = pl.program_id(0)

    kj = lax.broadcasted_iota(I32, (2 * WINDOW, WINDOW), 0)
    qi = lax.broadcasted_iota(I32, (2 * WINDOW, WINDOW), 1)
    diff = qi + WINDOW - kj
    band = jnp.logical_and(diff >= 0, diff < WINDOW)
    first_half = lax.broadcasted_iota(I32, (WINDOW, PAIR), 1) < HEAD_DIM
    keep = (jnp.where(first_half, 1.0, 0.0).astype(BF16), jnp.where(first_half, 0.0, 1.0).astype(BF16))
    first_kj = jnp.where(lax.rem(i, tiles_per_seq) == 0, WINDOW, 0)

    for blk in range(tm // WINDOW):
        rows = slice(blk * WINDOW, (blk + 1) * WINDOW)
        if blk == 0:
            kk = jnp.concatenate([kp_ref[...], kc_ref[rows, :]], axis=0).astype(BF16)
            vv = jnp.concatenate([vp_ref[...], vc_ref[rows, :]], axis=0)
            valid = jnp.logical_and(band, kj >= first_kj)
        else:
            win = slice((blk - 1) * WINDOW, (blk + 1) * WINDOW)
            kk = kc_ref[win, :].astype(BF16)
            vv = vc_ref[win, :]
            valid = band
        for kvh in range(N_KV_HEADS):
            kk_h = kk[:, kvh * PAIR:(kvh + 1) * PAIR]
            vcols = slice((kvh // 2) * PAIR, (kvh // 2 + 1) * PAIR)
            vt = vv[:, vcols].T.astype(BF16)
            ch = slice((kvh % 2) * HEAD_DIM, (kvh % 2 + 1) * HEAD_DIM)
            outs = []
            for g in range(Q_PER_KV):
                head = kvh * Q_PER_KV + g
                qm = q_ref[rows, (head // 2) * PAIR:(head // 2 + 1) * PAIR] * keep[head % 2]
                sink = sinks_ref[head]
                st = lax.dot_general(kk_h, qm, (((1,), (1,)), ((), ())), preferred_element_type=F32)
                st = jnp.where(valid, st, -jnp.inf)
                m = jnp.maximum(jnp.max(st, axis=0, keepdims=True), sink)
                p = jnp.exp(st - m)
                den = jnp.sum(p, axis=0, keepdims=True) + jnp.exp(sink - m)
                ot = jnp.dot(vt, p.astype(BF16), preferred_element_type=F32)
                outs.append(ot[ch, :] * (1.0 / den))
            for pq in range(Q_PER_KV // 2):
                pair = (kvh * Q_PER_KV) // 2 + pq
                both = jnp.concatenate([outs[2 * pq], outs[2 * pq + 1]], axis=0)
                a_scr[rows, pair * PAIR:(pair + 1) * PAIR] = both.T.astype(BF16)
    o_ref[...] = x_ref[...] + jnp.dot(a_scr[...], wo_ref[...], preferred_element_type=F32)


def _attn_prompt(x, q, kdup, v, sinks, wo, *, tm, seq):
    n = x.shape[0]
    tiles_per_seq = seq // tm
    blocks_per_tile = tm // WINDOW

    def prev_spec(width):
        return pl.BlockSpec((WINDOW, width), lambda i: (jnp.maximum(i * blocks_per_tile - 1, 0), 0))

    return pl.pallas_call(
        functools.partial(_attn_prompt_kernel, tm=tm, tiles_per_seq=tiles_per_seq),
        grid=(n // tm,),
        in_specs=[pl.BlockSpec(memory_space=pltpu.SMEM), _rows(tm, D_MODEL), _rows(tm, D_MODEL),
                  _rows(tm, 2 * D_KV), prev_spec(2 * D_KV), _rows(tm, D_KV), prev_spec(D_KV), _resident(wo.shape)],
        out_specs=_rows(tm, D_MODEL),
        out_shape=jax.ShapeDtypeStruct((n, D_MODEL), F32),
        scratch_shapes=[pltpu.VMEM((tm, D_MODEL), BF16)],
        compiler_params=_cparams(),
        name="attn_prompt",
    )(sinks, x, q, kdup, kdup, v, v, wo)


def _attn_sample_kernel(q_ref, ck_ref, cv_ref, kn_ref, vn_ref, sink_ref, o_ref, *, per_step, win):
    rowi = lax.broadcasted_iota(I32, (N_HEADS, D_KV), 0)
    lanei = lax.broadcasted_iota(I32, (N_HEADS, D_KV), 1)
    group = rowi >> 2
    own_block = group == (lanei >> 6)
    j = lax.broadcasted_iota(I32, (N_HEADS, win), 1)
    diff = win - j
    valid = jnp.logical_and(diff >= 0, diff < WINDOW)
    sink = sink_ref[...]
    scale = HEAD_DIM ** -0.5
    for b in range(per_step):
        qm = jnp.where(own_block, q_ref[b], 0.0)
        s = lax.dot_general(qm.astype(BF16), ck_ref[b].astype(BF16), (((1,), (1,)), ((), ())),
                            preferred_element_type=F32) * scale
        s = jnp.where(valid, s, -jnp.inf)
        s_new = jnp.sum(qm * kn_ref[b:b + 1, :], axis=-1, keepdims=True) * scale
        m = jnp.maximum(jnp.maximum(jnp.max(s, axis=-1, keepdims=True), s_new), sink)
        p = jnp.exp(s - m)
        p_new = jnp.exp(s_new - m)
        den = jnp.sum(p, axis=-1, keepdims=True) + p_new + jnp.exp(sink - m)
        o = jnp.dot(p.astype(BF16), cv_ref[b].astype(BF16), preferred_element_type=F32) + p_new * vn_ref[b:b + 1, :]
        o = o * (1.0 / den)
        res = o
        for sft in range(1, N_KV_HEADS):
            res = jnp.where(group == sft, pltpu.roll(o, D_KV - sft * HEAD_DIM, 1), res)
        o_ref[b] = res[:, :HEAD_DIM]


def _attn_sample(q4, ck, cv, kn, vn, sink_col, *, per_step):
    nb, win = ck.shape[0], ck.shape[1]
    return pl.pallas_call(
        functools.partial(_attn_sample_kernel, per_step=per_step, win=win),
        grid=(nb // per_step,),
        in_specs=[pl.BlockSpec((per_step, N_HEADS, D_KV), lambda i: (i, 0, 0)),
                  pl.BlockSpec((per_step, win, D_KV), lambda i: (i, 0, 0)),
                  pl.BlockSpec((per_step, win, D_KV), lambda i: (i, 0, 0)),
                  _rows(per_step, D_KV), _rows(per_step, D_KV), _resident(sink_col.shape)],
        out_specs=pl.BlockSpec((per_step, N_HEADS, HEAD_DIM), lambda i: (i, 0, 0)),
        out_shape=jax.ShapeDtypeStruct((nb, N_HEADS, HEAD_DIM), F32),
        compiler_params=_cparams(),
        name="attn_sample",
    )(q4, ck, cv, kn, vn, sink_col)


def _residual_proj_kernel(x_ref, a_ref, w_ref, o_ref):
    o_ref[...] = x_ref[...] + jnp.dot(a_ref[...].astype(BF16), w_ref[...], preferred_element_type=F32)


def _residual_proj(x, a, w):
    n = x.shape[0]
    return pl.pallas_call(
        _residual_proj_kernel,
        grid=(1,),
        in_specs=[_rows(n, D_MODEL), _rows(n, a.shape[1]), _resident(w.shape)],
        out_specs=_rows(n, D_MODEL),
        out_shape=jax.ShapeDtypeStruct((n, D_MODEL), F32),
        compiler_params=_cparams(),
        name="attn_out_sample",
    )(x, a, w)


def _rope_tables(pos):
    half = ROPE_DIM // 2
    inv_freq = jnp.power(jnp.float32(ROPE_THETA), -jnp.arange(half, dtype=F32) / half)
    ang = pos.astype(F32)[:, None] * inv_freq[None, :]
    cos, sin = jnp.cos(ang), jnp.sin(ang)
    n = pos.shape[0]
    rest = HEAD_DIM - ROPE_DIM
    cos_h = jnp.concatenate([cos, cos, jnp.ones((n, rest), F32)], axis=1)
    sin_h = jnp.concatenate([-sin, sin, jnp.zeros((n, rest), F32)], axis=1)
    return jnp.tile(cos_h, (1, 2)), jnp.tile(sin_h, (1, 2))


def kernel(x_prompt, x_sample, cache_k, cache_v, a_norm_g, a_w_in, a_ln_g, a_ln_b, a_w_s, a_b_s, a_w_out, kv_norm_g, w_k, w_v, k_norm_g, b_norm_g, w_q, q_norm_g, sinks, w_o, ffn_norm_g, w_router, b_router, w_gu, b_gu, w_down, b_down):
    bsz, seq, _ = x_prompt.shape
    dec_b, dec_seq, _ = x_sample.shape
    assert dec_seq == 1 and seq % TM_DENSE == 0 and a_norm_g.shape[0] == 1 and b_norm_g.shape[0] == 1
    win = cache_k.shape[1]
    n_p, n_s = bsz * seq, dec_b * dec_seq
    xp = x_prompt.reshape(n_p, D_MODEL)
    xs = x_sample.reshape(n_s, D_MODEL)

    ng = a_norm_g[0].reshape(1, D_MODEL)
    win_bf = a_w_in[0].astype(BF16)
    wout_bf = a_w_out[0].astype(BF16)
    lng = a_ln_g[0].reshape(1, D_GATE)
    lnb = a_ln_b[0].reshape(1, D_GATE)
    bias_full = jnp.repeat(a_b_s[0].T, GROUP_DIM_A, axis=1)
    diag_row = jnp.repeat(a_w_s[0][:, 0, 0], GROUP_DIM_A).reshape(1, D_GATE)
    (xp,) = _mixer_a(xp, ng, win_bf, lng, lnb, a_w_s[0], bias_full, wout_bf, tm=TM_DENSE, single_token_chunks=False)
    xs, v_rows = _mixer_a(xs, ng, win_bf, lng, lnb, diag_row, bias_full[0:1], wout_bf, tm=n_s,
                          single_token_chunks=True)

    kvg = kv_norm_g.reshape(1, D_MODEL)
    bng = b_norm_g[0].reshape(1, D_MODEL)
    w_k_dup = jnp.repeat(w_k.reshape(D_MODEL, N_KV_HEADS, 1, HEAD_DIM), 2, axis=2).reshape(D_MODEL, 2 * D_KV)
    wk_bf, wv_bf, wq_bf, wo_bf = (w.astype(BF16) for w in (w_k_dup, w_v, w_q[0], w_o[0]))
    kng2 = jnp.tile(k_norm_g, 2).reshape(1, PAIR)
    qng2 = jnp.tile(q_norm_g[0], 2).reshape(1, PAIR)
    cos_p, sin_p = _rope_tables(jnp.arange(seq, dtype=I32))
    cos_s, sin_s = _rope_tables(jnp.full((n_s,), PAST_LEN, I32))
    attn_pre = (kvg, wk_bf, wv_bf, kng2, bng, wq_bf, qng2, cos_p, sin_p, seq // TM_COMBINE)
    (xp, k_p, kdup_p, v_p, q_p), (xs,) = _moe([xp, xs], ffn_norm_g[0], w_router[0], b_router[0], w_gu, b_gu,
                                              w_down, b_down, 0, prompt_attn_pre=attn_pre)

    k_s, _, v_s, q_s = _kv_proj(xs, kvg, wk_bf, wv_bf, kng2, cos_s, sin_s, (bng, wq_bf, qng2), tm=n_s,
                                table_tiles=1)
    xp = _attn_prompt(xp, q_p, kdup_p, v_p, sinks[0], wo_bf, tm=TM_DENSE, seq=seq)
    q4 = jnp.tile(q_s.reshape(n_s, N_HEADS, HEAD_DIM), (1, 1, N_KV_HEADS))
    attn_s = _attn_sample(q4, cache_k.reshape(dec_b, win, D_KV), cache_v.reshape(dec_b, win, D_KV), k_s, v_s,
                          sinks[0].reshape(N_HEADS, 1), per_step=8)
    xs = _residual_proj(xs, attn_s.reshape(n_s, D_MODEL), wo_bf)
    (xp,), (xs,) = _moe([xp, xs], ffn_norm_g[1], w_router[1], b_router[1], w_gu, b_gu, w_down, b_down, 1)

    def last_window(a):
        a = a.reshape(bsz, seq, D_KV)[:, -WINDOW:]
        return a.reshape(bsz, min(WINDOW, seq), N_KV_HEADS, HEAD_DIM)

    return (xp.reshape(bsz, seq, D_MODEL), xs.reshape(dec_b, dec_seq, D_MODEL),
            v_rows.reshape(1, dec_b, dec_seq, D_GATE), last_window(k_p), last_window(v_p),
            k_s.reshape(dec_b, dec_seq, N_KV_HEADS, HEAD_DIM), v_s.reshape(dec_b, dec_seq, N_KV_HEADS, HEAD_DIM))
```

```python
import functools

import jax
import jax.numpy as jnp
from jax import lax
from jax.experimental import pallas as pl
from jax.experimental.pallas import tpu as pltpu

F32, BF16, I32 = jnp.float32, jnp.bfloat16, jnp.int32

D_MODEL = 1024
CHUNK = 128
D_GATE = 2 * D_MODEL
N_GROUPS_A = 8
GROUP_DIM_A = D_GATE // N_GROUPS_A
HEAD_DIM = 64
N_HEADS = D_MODEL // HEAD_DIM
N_KV_HEADS = N_HEADS // 4
Q_PER_KV = N_HEADS // N_KV_HEADS
D_KV = N_KV_HEADS * HEAD_DIM
WINDOW = 128
ROPE_DIM = HEAD_DIM // 4
ROPE_THETA = 500000.0
N_EXPERTS = 32
TOP_K = 4
D_EXPERT = D_MODEL
SWIGLU_LIMIT = 7.0
SWIGLU_ALPHA = 1.702
EPS = 1e-6
PAST_LEN = 8192

V7X_LANES = 128
V7X_VMEM_BYTES = 64 * 1024 * 1024
VMEM_LIMIT_BYTES = V7X_VMEM_BYTES - 8 * 1024 * 1024

TM_DENSE = 512
TM_COMBINE = 256
MOE_ROWS = 512
TM_DISPATCH = 256
FILL_BITS = MOE_ROWS.bit_length() - 1
TOKEN_SUBLANES = D_MODEL // V7X_LANES
PACKED_SUBLANES = TOKEN_SUBLANES // 2
U32 = jnp.uint32
SLAB = V7X_LANES
REC_ROWS = 16
PAIR = 2 * HEAD_DIM
INV_SQRT2 = 0.7071067811865476


def _cparams():
    return pltpu.CompilerParams(dimension_semantics=("arbitrary",), vmem_limit_bytes=VMEM_LIMIT_BYTES)


def _resident(shape):
    zeros = (0,) * len(shape)
    return pl.BlockSpec(shape, lambda i, *_: zeros, pipeline_mode=pl.Buffered(1))


def _rows(tm, width, offset=0):
    return pl.BlockSpec((tm, width), lambda i, *_: (i + offset, 0))


def _rms(x, g):
    return x * lax.rsqrt(jnp.mean(x * x, axis=-1, keepdims=True) + EPS) * g


def _pair_norm_rope(x2, g2, cos2, sin2):
    lane = lax.broadcasted_iota(I32, x2.shape, 1)
    r = lax.broadcasted_iota(I32, (PAIR, PAIR), 0)
    c = lax.broadcasted_iota(I32, (PAIR, PAIR), 1)
    head_mean = jnp.where((r < HEAD_DIM) == (c < HEAD_DIM), 1.0 / HEAD_DIM, 0.0).astype(BF16)
    sq = x2 * x2
    sq_hi = sq.astype(BF16)
    sq_lo = (sq - sq_hi.astype(F32)).astype(BF16)
    ms = (jnp.dot(sq_hi, head_mean, preferred_element_type=F32)
          + jnp.dot(sq_lo, head_mean, preferred_element_type=F32))
    y = x2 * lax.rsqrt(ms + EPS) * g2
    half = ROPE_DIM // 2
    up = pltpu.roll(y, PAIR - half, 1)
    dn = pltpu.roll(y, half, 1)
    partner = jnp.where((lane & (HEAD_DIM - 1)) < half, up, dn)
    return y * cos2 + partner * sin2


def _mixer_a_kernel(x_ref, ng_ref, win_ref, lng_ref, lnb_ref, mix_ref, bias_ref, wout_ref, *refs,
                    tm, single_token_chunks):
    if single_token_chunks:
        o_ref, vout_ref, u_scr, v_scr, p_scr = refs
    else:
        o_ref, u_scr, v_scr, p_scr = refs
    x = x_ref[...]
    h = _rms(x, ng_ref[...]).astype(BF16)
    nc = 512
    for c in range(2 * D_GATE // nc):
        z = jnp.dot(h, win_ref[:, c * nc:(c + 1) * nc], preferred_element_type=F32)
        z = 0.5 * z * (1.0 + lax.erf(z * INV_SQRT2))
        if c < D_GATE // nc:
            u_scr[:, c * nc:(c + 1) * nc] = z
        else:
            v_scr[:, (c - D_GATE // nc) * nc:(c - D_GATE // nc + 1) * nc] = z
    v = v_scr[...]
    vc = v - jnp.mean(v, axis=-1, keepdims=True)
    vn = vc * lax.rsqrt(jnp.mean(vc * vc, axis=-1, keepdims=True) + EPS) * lng_ref[...] + lnb_ref[...]
    if single_token_chunks:
        vout_ref[...] = vn
        p_scr[...] = (u_scr[...] * (vn * mix_ref[...] + bias_ref[...])).astype(BF16)
    else:
        v_scr[...] = vn
        row = lax.broadcasted_iota(I32, (CHUNK, CHUNK), 0)
        col = lax.broadcasted_iota(I32, (CHUNK, CHUNK), 1)
        for g in range(N_GROUPS_A):
            wc = jnp.where(row >= col, mix_ref[g], 0.0).astype(BF16)
            cols = slice(g * GROUP_DIM_A, (g + 1) * GROUP_DIM_A)
            for c in range(tm // CHUNK):
                rows = slice(c * CHUNK, (c + 1) * CHUNK)
                mixed = jnp.dot(wc, v_scr[rows, cols].astype(BF16), preferred_element_type=F32)
                mixed = mixed + bias_ref[:, cols]
                p_scr[rows, cols] = (u_scr[rows, cols] * mixed).astype(BF16)
    o_ref[...] = x + jnp.dot(p_scr[...], wout_ref[...], preferred_element_type=F32)


def _mixer_a(x, ng, win, lng, lnb, mix, bias, wout, *, tm, single_token_chunks):
    n = x.shape[0]
    out_shape = [jax.ShapeDtypeStruct((n, D_MODEL), F32)]
    out_specs = [_rows(tm, D_MODEL)]
    if single_token_chunks:
        out_shape.append(jax.ShapeDtypeStruct((n, D_GATE), F32))
        out_specs.append(_rows(tm, D_GATE))
    return pl.pallas_call(
        functools.partial(_mixer_a_kernel, tm=tm, single_token_chunks=single_token_chunks),
        grid=(n // tm,),
        in_specs=[_rows(tm, D_MODEL), _resident(ng.shape), _resident(win.shape), _resident(lng.shape),
                  _resident(lnb.shape), _resident(mix.shape), _resident(bias.shape), _resident(wout.shape)],
        out_specs=out_specs,
        out_shape=out_shape,
        scratch_shapes=[pltpu.VMEM((tm, D_GATE), F32), pltpu.VMEM((tm, D_GATE), F32),
                        pltpu.VMEM((tm, D_GATE), BF16)],
        compiler_params=_cparams(),
        name="mixer_a_sample" if single_token_chunks else "mixer_a_prompt",
    )(x, ng, win, lng, lnb, mix, bias, wout)


def _route_rows(x, g, whi_t, wlo_t, bias_col, run_scr):
    tm = x.shape[0]
    h = _rms(x, g)
    hh = h.astype(BF16)
    hl = (h - hh.astype(F32)).astype(BF16)
    nt = (((1,), (1,)), ((), ()))
    logits = (lax.dot_general(whi_t, hh, nt, preferred_element_type=F32)
              + lax.dot_general(wlo_t, hh, nt, preferred_element_type=F32)
              + lax.dot_general(whi_t, hl, nt, preferred_element_type=F32) + bias_col)
    e = lax.broadcasted_iota(I32, (N_EXPERTS, tm), 0).astype(F32)
    hot, vals, idxs = [], [], []
    for _ in range(TOP_K):
        m = jnp.max(logits, axis=0, keepdims=True)
        idx = jnp.min(jnp.where(logits == m, e, float(N_EXPERTS)), axis=0, keepdims=True)
        oh = e == idx
        hot.append(oh)
        vals.append(m)
        idxs.append(idx)
        logits = jnp.where(oh, -jnp.inf, logits)
    ex = [jnp.exp(v - vals[0]) for v in vals]
    den = ex[0] + ex[1] + ex[2] + ex[3]
    picked = jnp.zeros((N_EXPERTS, tm), F32)
    for oh in hot:
        picked = picked + jnp.where(oh, 1.0, 0.0)
    r = lax.broadcasted_iota(I32, (tm, tm), 0)
    c = lax.broadcasted_iota(I32, (tm, tm), 1)
    earlier = jnp.where(r < c, 1.0, 0.0).astype(BF16)
    run = run_scr[:, 0:1]
    pos = jnp.dot(picked.astype(BF16), earlier, preferred_element_type=F32) + run
    run_scr[:, 0:1] = run + jnp.sum(picked, axis=1, keepdims=True)
    row = lax.broadcasted_iota(I32, (REC_ROWS, tm), 0)
    rec = jnp.zeros((REC_ROWS, tm), F32)
    for k in range(TOP_K):
        pos_k = jnp.sum(jnp.where(hot[k], pos, 0.0), axis=0, keepdims=True)
        rec = jnp.where(row == k, idxs[k], rec)
        rec = jnp.where(row == TOP_K + k, pos_k, rec)
        rec = jnp.where(row == 2 * TOP_K + k, ex[k] / den, rec)
    return h, rec


def _router_kernel(xp_ref, xs_ref, g_ref, whi_ref, wlo_ref, b_ref, h_ref, slabp_ref, slabs_ref, rec_ref, cnt_ref,
                   run_scr, *, n_tiles, n_s):
    i = pl.program_id(0)

    @pl.when(i == 0)
    def _():
        run_scr[...] = jnp.zeros_like(run_scr)

    def pad_rows(rec):
        return jnp.concatenate([rec, jnp.zeros((SLAB - REC_ROWS, rec.shape[1]), F32)], axis=0)

    @pl.when(i < n_tiles)
    def _():
        h, rec = _route_rows(xp_ref[...], g_ref[...], whi_ref[...], wlo_ref[...], b_ref[...], run_scr)
        _store_packed_tiles(h_ref, h, h.shape[0])
        slabp_ref[...] = pad_rows(rec).T
        rec_ref[...] = rec[0:2 * TOP_K, :]

    @pl.when(i == n_tiles)
    def _():
        h, rec = _route_rows(xs_ref[...], g_ref[...], whi_ref[...], wlo_ref[...], b_ref[...], run_scr)
        h_ref[n_s * PACKED_SUBLANES:, :] = jnp.zeros((h_ref.shape[0] - n_s * PACKED_SUBLANES, V7X_LANES), U32)
        _store_packed_tiles(h_ref, h, n_s)
        slabs_ref[...] = pad_rows(rec).T
        rec_ref[...] = jnp.zeros_like(rec_ref)
        rec_ref[:, 0:n_s] = rec[0:2 * TOP_K, :]

    cnt_ref[...] = run_scr[...]


def _store_packed_tiles(ref, x, n):
    half = D_MODEL // 2
    lo = pltpu.bitcast(x[:, :half].astype(BF16).astype(F32), U32)
    hi = pltpu.bitcast(x[:, half:].astype(BF16).astype(F32), U32)
    word = lax.shift_right_logical(lo, jnp.uint32(16)) | (hi & jnp.uint32(0xFFFF0000))
    for s in range(PACKED_SUBLANES):
        ref[pl.ds(s, n, stride=PACKED_SUBLANES), :] = word[:, s * V7X_LANES:(s + 1) * V7X_LANES]


def _load_packed_tiles(ref, n):
    words = [ref[pl.ds(s, n, stride=PACKED_SUBLANES), :] for s in range(PACKED_SUBLANES)]
    lo = [pltpu.bitcast(w << jnp.uint32(16), F32) for w in words]
    hi = [pltpu.bitcast(w & jnp.uint32(0xFFFF0000), F32) for w in words]
    return jnp.concatenate(lo + hi, axis=1).astype(BF16)


def _store_token_tiles(ref, x, n):
    for s in range(TOKEN_SUBLANES):
        ref[pl.ds(s, n, stride=TOKEN_SUBLANES), :] = x[:, s * V7X_LANES:(s + 1) * V7X_LANES]


def _load_token_tiles(ref, n):
    return jnp.concatenate([ref[pl.ds(s, n, stride=TOKEN_SUBLANES), :] for s in range(TOKEN_SUBLANES)], axis=1)


def _router(xp, xs, g, whi, wlo, b, *, tm):
    n_p, n_s = xp.shape[0], xs.shape[0]
    n_tiles = n_p // tm
    last = n_tiles - 1
    return pl.pallas_call(
        functools.partial(_router_kernel, n_tiles=n_tiles, n_s=n_s),
        grid=(n_tiles + 1,),
        in_specs=[pl.BlockSpec((tm, D_MODEL), lambda i: (jnp.minimum(i, last), 0)), _resident(xs.shape),
                  _resident(g.shape), _resident(whi.shape), _resident(wlo.shape), _resident(b.shape)],
        out_specs=[_rows(tm * PACKED_SUBLANES, V7X_LANES),
                   pl.BlockSpec((tm, SLAB), lambda i: (jnp.minimum(i, last), 0)),
                   pl.BlockSpec((n_s, SLAB), lambda i: (0, 0)),
                   pl.BlockSpec((2 * TOP_K, tm), lambda i: (0, i)),
                   pl.BlockSpec((N_EXPERTS, V7X_LANES), lambda i: (0, 0))],
        out_shape=[jax.ShapeDtypeStruct(((n_p + tm) * PACKED_SUBLANES, V7X_LANES), U32),
                   jax.ShapeDtypeStruct((n_p, SLAB), F32),
                   jax.ShapeDtypeStruct((n_s, SLAB), F32),
                   jax.ShapeDtypeStruct((2 * TOP_K, n_p + tm), F32),
                   jax.ShapeDtypeStruct((N_EXPERTS, V7X_LANES), F32)],
        scratch_shapes=[pltpu.VMEM((N_EXPERTS, V7X_LANES), F32)],
        compiler_params=_cparams(),
        name="moe_router",
    )(xp, xs, g, whi, wlo, b)


def _dispatch_kernel(fill_start_ref, fill_len_ref, dest_ref, h_ref, xs_ref, zbuf, sem, fill_sem,
                     *, tm, n_tiles, n_last, rows, n_blocks):
    i = pl.program_id(0)
    ts = PACKED_SUBLANES

    def tile_copy(r, k):
        d = dest_ref[0, 0, k * tm + r]
        src = h_ref.at[pl.ds(pl.multiple_of(r * ts, ts), ts)]
        dst = xs_ref.at[pl.ds(pl.multiple_of(d * ts, ts), ts)]
        return pltpu.make_async_copy(src, dst, sem)

    def scatter(n):
        def body(r, carry):
            for k in range(TOP_K):
                tile_copy(r, k).start(priority=k % 2)
            return carry
        lax.fori_loop(0, n, body, 0, unroll=4)
        for k in range(TOP_K):
            pltpu.make_async_copy(h_ref.at[pl.ds(0, n * ts)], xs_ref.at[pl.ds(0, n * ts)], sem).wait()

    def fill_copy(e, bit):
        size = 1 << bit
        length = fill_len_ref[e]
        done = length - (length & (2 * size - 1))
        off = pl.multiple_of((fill_start_ref[e] + done) * ts, ts)
        return (length & size) != 0, pltpu.make_async_copy(zbuf.at[pl.ds(0, size * ts)],
                                                            xs_ref.at[pl.ds(off, size * ts)], fill_sem)

    def tail_copy(blk):
        off = pl.multiple_of(blk * (rows * ts), rows * ts)
        return pltpu.make_async_copy(zbuf, xs_ref.at[pl.ds(off, rows * ts)], fill_sem)

    def for_each_fill(act):
        for e in range(N_EXPERTS):
            for bit in range(FILL_BITS):
                go, cp = fill_copy(e, bit)

                @pl.when(go)
                def _():
                    act(cp)

        def body(blk, carry):
            act(tail_copy(blk))
            return carry
        lax.fori_loop(fill_start_ref[N_EXPERTS], n_blocks, body, 0)

    @pl.when(i == 0)
    def _():
        zbuf[...] = jnp.zeros_like(zbuf)
        for_each_fill(lambda cp: cp.start())

    @pl.when(i < n_tiles - 1)
    def _():
        scatter(tm)

    @pl.when(i == n_tiles - 1)
    def _():
        scatter(n_last)
        for_each_fill(lambda cp: cp.wait())


def _dispatch(fill_start, fill_len, dest3, h_tiles, n_all, n_blocks, *, tm, rows):
    n_tiles = dest3.shape[0]
    n_last = n_all - (n_tiles - 1) * tm
    ts = PACKED_SUBLANES
    n_slots = n_blocks * rows
    grid_spec = pltpu.PrefetchScalarGridSpec(
        num_scalar_prefetch=2,
        grid=(n_tiles,),
        in_specs=[pl.BlockSpec((1, 1, TOP_K * tm), lambda i, *_: (i, 0, 0), memory_space=pltpu.SMEM),
                  _rows(tm * ts, V7X_LANES)],
        out_specs=pl.BlockSpec(memory_space=pl.ANY),
        scratch_shapes=[pltpu.VMEM((rows * ts, V7X_LANES), U32),
                        pltpu.SemaphoreType.DMA, pltpu.SemaphoreType.DMA],
    )
    return pl.pallas_call(
        functools.partial(_dispatch_kernel, tm=tm, n_tiles=n_tiles, n_last=n_last, rows=rows, n_blocks=n_blocks),
        grid_spec=grid_spec,
        out_shape=jax.ShapeDtypeStruct((n_slots * ts, V7X_LANES), U32),
        compiler_params=_cparams(),
        name="moe_dispatch",
    )(fill_start, fill_len, dest3, h_tiles)


def _expert_kernel(be_ref, nact_ref, run_ref, rune_ref, x_ref, wgu_hbm, bgu_ref, wdn_hbm, bdn_ref, o_ref,
                   wgu_f32, wdn_f32, wsem, wgu_bf, wdn_bf, *, rows, layer):
    b = pl.program_id(0)
    nact = nact_ref[0]

    def weight_copies(r, slot):
        e = rune_ref[r]
        return (pltpu.make_async_copy(wgu_hbm.at[layer, e], wgu_f32.at[slot], wsem.at[0, slot]),
                pltpu.make_async_copy(wdn_hbm.at[layer, e], wdn_f32.at[slot], wsem.at[1, slot]))

    @pl.when(b == 0)
    def _():
        for cp in weight_copies(0, 0):
            cp.start()

    @pl.when(b < nact)
    def _():
        r = run_ref[b]
        slot = lax.rem(r, 2)
        changed = jnp.logical_or(b == 0, r != run_ref[jnp.maximum(b - 1, 0)])

        @pl.when(changed)
        def _():
            for cp in weight_copies(r, slot):
                cp.wait()

            @pl.when(r + 1 < rune_ref[N_EXPERTS])
            def _():
                for cp in weight_copies(r + 1, 1 - slot):
                    cp.start()

            step = 256
            for c in range(D_MODEL // step):
                wgu_bf[c * step:(c + 1) * step, :] = wgu_f32[slot, c * step:(c + 1) * step, :].astype(BF16)
                wdn_bf[c * step:(c + 1) * step, :] = wdn_f32[slot, c * step:(c + 1) * step, :].astype(BF16)

        xb = _load_packed_tiles(x_ref, rows)
        nc = 512
        out = None
        for c in range(D_EXPERT // nc):
            gcols = slice(c * nc, (c + 1) * nc)
            ucols = slice(D_EXPERT + c * nc, D_EXPERT + (c + 1) * nc)
            gate = jnp.dot(xb, wgu_bf[:, gcols], preferred_element_type=F32) + bgu_ref[:, gcols]
            up = jnp.dot(xb, wgu_bf[:, ucols], preferred_element_type=F32) + bgu_ref[:, ucols]
            gate = jnp.minimum(gate, SWIGLU_LIMIT)
            up = jnp.clip(up, -SWIGLU_LIMIT, SWIGLU_LIMIT)
            act = ((up + 1.0) * gate * (1.0 / (1.0 + jnp.exp(-SWIGLU_ALPHA * gate)))).astype(BF16)
            part = jnp.dot(act, wdn_bf[gcols, :], preferred_element_type=F32)
            out = part if out is None else out + part
        _store_token_tiles(o_ref, out + bdn_ref[...], rows)

    @pl.when(b >= nact)
    def _():
        o_ref[...] = jnp.zeros_like(o_ref)


def _experts(block_e, n_active, block_run, run_e, xs_tiles, wgu, bgu, wdn, bdn, *, rows, layer):
    ts = TOKEN_SUBLANES
    nb = xs_tiles.shape[0] // (rows * PACKED_SUBLANES)
    grid_spec = pltpu.PrefetchScalarGridSpec(
        num_scalar_prefetch=4,
        grid=(nb,),
        in_specs=[
            pl.BlockSpec((rows * PACKED_SUBLANES, V7X_LANES), lambda b, be, na, *_: (jnp.minimum(b, na[0] - 1), 0)),
            pl.BlockSpec(memory_space=pl.ANY),
            pl.BlockSpec((None, None, 1, 2 * D_EXPERT), lambda b, be, *_: (layer, be[b], 0, 0)),
            pl.BlockSpec(memory_space=pl.ANY),
            pl.BlockSpec((None, None, 1, D_MODEL), lambda b, be, *_: (layer, be[b], 0, 0)),
        ],
        out_specs=pl.BlockSpec((rows * ts, V7X_LANES), lambda b, *_: (b, 0)),
        scratch_shapes=[pltpu.VMEM((2, D_MODEL, 2 * D_EXPERT), F32), pltpu.VMEM((2, D_EXPERT, D_MODEL), F32),
                        pltpu.SemaphoreType.DMA((2, 2)),
                        pltpu.VMEM((D_MODEL, 2 * D_EXPERT), BF16), pltpu.VMEM((D_EXPERT, D_MODEL), BF16)],
    )
    return pl.pallas_call(
        functools.partial(_expert_kernel, rows=rows, layer=layer),
        grid_spec=grid_spec,
        out_shape=jax.ShapeDtypeStruct((nb * rows * ts, V7X_LANES), F32),
        compiler_params=_cparams(),
        name="moe_experts",
    )(block_e, n_active, block_run, run_e, xs_tiles, wgu, bgu, wdn, bdn)


def _combine_kernel(dest_first_ref, dest_next_ref, x_ref, slab_ref, ebuf_ref, *refs, tm, n_tiles, attn_pre):
    if attn_pre:
        (kvg_ref, wk_ref, wv_ref, kng_ref, bng_ref, wq_ref, qng_ref, cos_ref, sin_ref,
         o_ref, k_ref, kdup_ref, v_ref, q_ref, rbuf, sem) = refs
    else:
        o_ref, rbuf, sem = refs
    i = pl.program_id(0)
    slot = lax.rem(i, 2)
    ts = TOKEN_SUBLANES

    def gather(dest_ref, s):
        def body(r, carry):
            for k in range(TOP_K):
                d = dest_ref[0, 0, k * tm + r]
                src = ebuf_ref.at[pl.ds(pl.multiple_of(d * ts, ts), ts)]
                dst = rbuf.at[s, k, pl.ds(pl.multiple_of(r * ts, ts), ts)]
                pltpu.make_async_copy(src, dst, sem.at[s]).start(priority=k % 2)
            return carry
        lax.fori_loop(0, tm, body, 0, unroll=4)

    @pl.when(i == 0)
    def _():
        gather(dest_first_ref, 0)

    if attn_pre:
        todo = [(r, k) for r in range(tm) for k in range(TOP_K)]

        def tick(n):
            for r, k in todo[:n]:
                d = dest_next_ref[0, 0, k * tm + r]
                src = ebuf_ref.at[pl.ds(pl.multiple_of(d * ts, ts), ts)]
                pltpu.make_async_copy(src, rbuf.at[1 - slot, k, pl.ds(r * ts, ts)],
                                      sem.at[1 - slot]).start(priority=k % 2)
            del todo[:n]
    else:
        @pl.when(i + 1 < n_tiles)
        def _():
            gather(dest_next_ref, 1 - slot)

        def tick(n):
            pass

    def wait_slot(s):
        for k in range(TOP_K):
            pltpu.make_async_copy(ebuf_ref.at[pl.ds(0, tm * ts)], rbuf.at[s, k], sem.at[s]).wait()

    wait_slot(slot)
    slab = slab_ref[...]
    gates = [slab[:, 2 * TOP_K + k:2 * TOP_K + k + 1] for k in range(TOP_K)]
    per_phase = TOP_K * tm // (4 * ts)
    for s in range(ts):
        cols = slice(s * V7X_LANES, (s + 1) * V7X_LANES)
        acc = x_ref[:, cols]
        for k in range(TOP_K):
            acc = acc + gates[k] * rbuf[slot, k, pl.ds(s, tm, stride=ts), :]
        o_ref[:, cols] = acc
        tick(per_phase)
    if attn_pre:
        x2 = o_ref[...]
        cos2, sin2 = cos_ref[...], sin_ref[...]
        kv_phases = 1 + N_KV_HEADS
        q_phases = 1 + D_MODEL // PAIR
        per_phase = len(todo) // (kv_phases + q_phases)
        _shared_kv(x2, kvg_ref[...], wk_ref, wv_ref, kng_ref[...], cos2, sin2, k_ref, kdup_ref, v_ref,
                   tick=lambda: tick(per_phase))
        hq = _rms(x2, bng_ref[...]).astype(BF16)
        q = jnp.dot(hq, wq_ref[...], preferred_element_type=F32)
        tick(per_phase)
        for p in range(D_MODEL // PAIR):
            cols = slice(p * PAIR, (p + 1) * PAIR)
            q_ref[:, cols] = (_pair_norm_rope(q[:, cols], qng_ref[...], cos2, sin2) * HEAD_DIM ** -0.5).astype(BF16)
            tick(per_phase)
        tick(len(todo))

        @pl.when(i == n_tiles - 1)
        def _():
            wait_slot(1 - slot)


def _shared_kv(x, kvg, wk_ref, wv_ref, kng2, cos2, sin2, k_ref, kdup_ref, v_ref, tick=lambda: None):
    h = _rms(x, kvg).astype(BF16)
    kd = jnp.dot(h, wk_ref[...], preferred_element_type=F32)
    v_ref[...] = jnp.dot(h, wv_ref[...], preferred_element_type=F32)
    tick()
    first_half = lax.broadcasted_iota(I32, (x.shape[0], PAIR), 1) < HEAD_DIM
    slabs = []
    for p in range(N_KV_HEADS):
        cols = slice(p * PAIR, (p + 1) * PAIR)
        slab = _pair_norm_rope(kd[:, cols], kng2, cos2, sin2)
        kdup_ref[:, cols] = slab
        slabs.append(slab)
        tick()
    for p in range(D_KV // PAIR):
        k_ref[:, p * PAIR:(p + 1) * PAIR] = jnp.where(first_half, slabs[2 * p], slabs[2 * p + 1])


def _combine(x, slab, dest3, ebuf, *, tm, attn_pre=None):
    n = x.shape[0]
    n_tiles = n // tm
    in_specs = [
        pl.BlockSpec((1, 1, TOP_K * tm), lambda i: (0, 0, 0), memory_space=pltpu.SMEM),
        pl.BlockSpec((1, 1, TOP_K * tm), lambda i: (jnp.minimum(i + 1, n_tiles - 1), 0, 0),
                     memory_space=pltpu.SMEM),
        _rows(tm, D_MODEL), _rows(tm, SLAB), pl.BlockSpec(memory_space=pl.ANY),
    ]
    args = [dest3, dest3, x, slab, ebuf]
    out_specs = [_rows(tm, D_MODEL)]
    out_shape = [jax.ShapeDtypeStruct((n, D_MODEL), F32)]
    if attn_pre is not None:
        *weights, cos2, sin2, table_tiles = attn_pre
        table_spec = pl.BlockSpec((tm, PAIR), lambda i: (lax.rem(i, table_tiles), 0))
        in_specs += [_resident(w.shape) for w in weights] + [table_spec, table_spec]
        args += list(weights) + [cos2, sin2]
        out_specs += [_rows(tm, D_KV), _rows(tm, 2 * D_KV), _rows(tm, D_KV), _rows(tm, D_MODEL)]
        out_shape += [jax.ShapeDtypeStruct((n, D_KV), F32), jax.ShapeDtypeStruct((n, 2 * D_KV), F32),
                      jax.ShapeDtypeStruct((n, D_KV), F32), jax.ShapeDtypeStruct((n, D_MODEL), BF16)]
    return pl.pallas_call(
        functools.partial(_combine_kernel, tm=tm, n_tiles=n_tiles, attn_pre=attn_pre is not None),
        grid=(n_tiles,),
        in_specs=in_specs, out_specs=out_specs, out_shape=out_shape,
        scratch_shapes=[pltpu.VMEM((2, TOP_K, tm * TOKEN_SUBLANES, V7X_LANES), F32),
                        pltpu.SemaphoreType.DMA((2,))],
        compiler_params=_cparams(),
        name="moe_combine_attn_pre" if attn_pre is not None else "moe_combine",
    )(*args)


def _moe(xs, g, w_router, b_router, wgu, bgu, wdn, bdn, layer, prompt_attn_pre=None):
    n_all = xs[0].shape[0] + xs[1].shape[0]
    w_t = w_router.T
    whi = w_t.astype(BF16)
    wlo = (w_t - whi.astype(F32)).astype(BF16)
    h_all, slab_p, slab_s, rec, cnt = _router(xs[0], xs[1], g.reshape(1, D_MODEL), whi, wlo,
                                              b_router.reshape(N_EXPERTS, 1), tm=TM_DENSE)
    slabs = [slab_p, slab_s]
    idx = rec[:TOP_K, :n_all].astype(I32)
    pos = rec[TOP_K:, :n_all].astype(I32)
    counts = cnt[:, 0].astype(I32)
    rows = MOE_ROWS
    nb = (n_all * TOP_K + N_EXPERTS * (rows - 1)) // rows
    padded = (counts + rows - 1) // rows * rows
    pad_end = jnp.cumsum(padded)
    pad_start = pad_end - padded
    eid = jnp.arange(N_EXPERTS, dtype=I32)
    has_rows = counts > 0
    dest = pos + jnp.sum(jnp.where(idx[None] == eid[:, None, None], pad_start[:, None, None], 0), axis=0)
    n_active = (pad_end[-1] // rows).astype(I32)
    blk = jnp.arange(nb, dtype=I32)
    block_e = jnp.sum((pad_end[None, :] <= (blk * rows)[:, None]).astype(I32), axis=1)
    last_e = jnp.max(jnp.where(has_rows, eid, 0))
    block_e = jnp.where(blk < n_active, jnp.minimum(block_e, N_EXPERTS - 1), last_e)

    def tile_table(d, tm):
        n_tiles = -(-d.shape[1] // tm)
        d = jnp.pad(d, ((0, 0), (0, n_tiles * tm - d.shape[1])))
        return d.reshape(TOP_K, n_tiles, tm).transpose(1, 0, 2).reshape(n_tiles, 1, TOP_K * tm)

    fill_start = jnp.concatenate([pad_start + counts, n_active.reshape(1)])
    xs_tiles = _dispatch(fill_start, padded - counts, tile_table(dest, TM_DISPATCH), h_all,
                         n_all, nb, tm=TM_DISPATCH, rows=rows)
    rank = jnp.cumsum(has_rows.astype(I32)) - 1
    run_e = jnp.sum(jnp.where(has_rows[None, :] & (rank[None, :] == eid[:, None]), eid[None, :], 0), axis=1)
    run_e = jnp.concatenate([run_e, jnp.sum(has_rows.astype(I32)).reshape(1)]).astype(I32)
    block_run = jnp.sum(jnp.where(block_e[:, None] == eid, rank, 0), axis=1).astype(I32)
    n_l = wgu.shape[0]
    ebuf = _experts(block_e, n_active.reshape(1), block_run, run_e, xs_tiles,
                    wgu, bgu.reshape(n_l, N_EXPERTS, 1, 2 * D_EXPERT), wdn, bdn.reshape(n_l, N_EXPERTS, 1, D_MODEL),
                    rows=rows, layer=layer)
    outs, off = [], 0
    for x, slab, pre in zip(xs, slabs, (prompt_attn_pre, None)):
        n = x.shape[0]
        tm = min(TM_COMBINE, n)
        outs.append(_combine(x, slab, tile_table(dest[:, off:off + n], tm), ebuf, tm=tm, attn_pre=pre))
        off += n
    return outs


def _kv_kernel(x_ref, kvg_ref, wk_ref, wv_ref, kng_ref, cos_ref, sin_ref, *refs, with_q):
    if with_q:
        bng_ref, wq_ref, qng_ref, k_ref, kdup_ref, v_ref, q_ref = refs
    else:
        k_ref, kdup_ref, v_ref = refs
    x = x_ref[...]
    cos2, sin2 = cos_ref[...], sin_ref[...]
    _shared_kv(x, kvg_ref[...], wk_ref, wv_ref, kng_ref[...], cos2, sin2, k_ref, kdup_ref, v_ref)
    if with_q:
        hq = _rms(x, bng_ref[...]).astype(BF16)
        q = jnp.dot(hq, wq_ref[...], preferred_element_type=F32)
        for p in range(D_MODEL // PAIR):
            cols = slice(p * PAIR, (p + 1) * PAIR)
            q_ref[:, cols] = _pair_norm_rope(q[:, cols], qng_ref[...], cos2, sin2)


def _kv_proj(x, kvg, wk, wv, kng2, cos2, sin2, q_args, *, tm, table_tiles):
    n = x.shape[0]
    with_q = q_args is not None
    table_spec = pl.BlockSpec((tm, PAIR), lambda i: (lax.rem(i, table_tiles), 0))
    in_specs = [_rows(tm, D_MODEL), _resident(kvg.shape), _resident(wk.shape), _resident(wv.shape),
                _resident(kng2.shape), table_spec, table_spec]
    args = [x, kvg, wk, wv, kng2, cos2, sin2]
    out_specs = [_rows(tm, D_KV), _rows(tm, 2 * D_KV), _rows(tm, D_KV)]
    out_shape = [jax.ShapeDtypeStruct((n, D_KV), F32), jax.ShapeDtypeStruct((n, 2 * D_KV), F32),
                 jax.ShapeDtypeStruct((n, D_KV), F32)]
    if with_q:
        in_specs += [_resident(a.shape) for a in q_args]
        args += list(q_args)
        out_specs.append(_rows(tm, D_MODEL))
        out_shape.append(jax.ShapeDtypeStruct((n, D_MODEL), F32))
    return pl.pallas_call(
        functools.partial(_kv_kernel, with_q=with_q),
        grid=(n // tm,),
        in_specs=in_specs, out_specs=out_specs, out_shape=out_shape,
        compiler_params=_cparams(),
        name="kvq_proj_sample" if with_q else "kv_proj_prompt",
    )(*args)


def _attn_prompt_kernel(sinks_ref, x_ref, q_ref, kc_ref, kp_ref, vc_ref, vp_ref, wo_ref, o_ref, a_scr,
                        *, tm, tiles_per_seq):
    i = pl.program_id(0)

    kj = lax.broadcasted_iota(I32, (2 * WINDOW, WINDOW), 0)
    qi = lax.broadcasted_iota(I32, (2 * WINDOW, WINDOW), 1)
    diff = qi + WINDOW - kj
    band = jnp.logical_and(diff >= 0, diff < WINDOW)
    first_half = lax.broadcasted_iota(I32, (WINDOW, PAIR), 1) < HEAD_DIM
    keep = (jnp.where(first_half, 1.0, 0.0).astype(BF16), jnp.where(first_half, 0.0, 1.0).astype(BF16))
    first_kj = jnp.where(lax.rem(i, tiles_per_seq) == 0, WINDOW, 0)

    for blk in range(tm // WINDOW):
        rows = slice(blk * WINDOW, (blk + 1) * WINDOW)
        if blk == 0:
            kk = jnp.concatenate([kp_ref[...], kc_ref[rows, :]], axis=0).astype(BF16)
            vv = jnp.concatenate([vp_ref[...], vc_ref[rows, :]], axis=0)
            valid = jnp.logical_and(band, kj >= first_kj)
        else:
            win = slice((blk - 1) * WINDOW, (blk + 1) * WINDOW)
            kk = kc_ref[win, :].astype(BF16)
            vv = vc_ref[win, :]
            valid = band
        for kvh in range(N_KV_HEADS):
            kk_h = kk[:, kvh * PAIR:(kvh + 1) * PAIR]
            vcols = slice((kvh // 2) * PAIR, (kvh // 2 + 1) * PAIR)
            vt = vv[:, vcols].T.astype(BF16)
            ch = slice((kvh % 2) * HEAD_DIM, (kvh % 2 + 1) * HEAD_DIM)
            outs = []
            for g in range(Q_PER_KV):
                head = kvh * Q_PER_KV + g
                qm = q_ref[rows, (head // 2) * PAIR:(head // 2 + 1) * PAIR] * keep[head % 2]
                sink = sinks_ref[head]
                st = lax.dot_general(kk_h, qm, (((1,), (1,)), ((), ())), preferred_element_type=F32)
                st = jnp.where(valid, st, -jnp.inf)
                m = jnp.maximum(jnp.max(st, axis=0, keepdims=True), sink)
                p = jnp.exp(st - m)
                den = jnp.sum(p, axis=0, keepdims=True) + jnp.exp(sink - m)
                ot = jnp.dot(vt, p.astype(BF16), preferred_element_type=F32)
                outs.append(ot[ch, :] * (1.0 / den))
            for pq in range(Q_PER_KV // 2):
                pair = (kvh * Q_PER_KV) // 2 + pq
                both = jnp.concatenate([outs[2 * pq], outs[2 * pq + 1]], axis=0)
                a_scr[rows, pair * PAIR:(pair + 1) * PAIR] = both.T.astype(BF16)
    o_ref[...] = x_ref[...] + jnp.dot(a_scr[...], wo_ref[...], preferred_element_type=F32)


def _attn_prompt(x, q, kdup, v, sinks, wo, *, tm, seq):
    n = x.shape[0]
    tiles_per_seq = seq // tm
    blocks_per_tile = tm // WINDOW

    def prev_spec(width):
        return pl.BlockSpec((WINDOW, width), lambda i: (jnp.maximum(i * blocks_per_tile - 1, 0), 0))

    return pl.pallas_call(
        functools.partial(_attn_prompt_kernel, tm=tm, tiles_per_seq=tiles_per_seq),
        grid=(n // tm,),
        in_specs=[pl.BlockSpec(memory_space=pltpu.SMEM), _rows(tm, D_MODEL), _rows(tm, D_MODEL),
                  _rows(tm, 2 * D_KV), prev_spec(2 * D_KV), _rows(tm, D_KV), prev_spec(D_KV), _resident(wo.shape)],
        out_specs=_rows(tm, D_MODEL),
        out_shape=jax.ShapeDtypeStruct((n, D_MODEL), F32),
        scratch_shapes=[pltpu.VMEM((tm, D_MODEL), BF16)],
        compiler_params=_cparams(),
        name="attn_prompt",
    )(sinks, x, q, kdup, kdup, v, v, wo)


def _attn_sample_kernel(q_ref, ck_ref, cv_ref, kn_ref, vn_ref, sink_ref, o_ref, *, per_step, win):
    rowi = lax.broadcasted_iota(I32, (N_HEADS, D_KV), 0)
    lanei = lax.broadcasted_iota(I32, (N_HEADS, D_KV), 1)
    group = rowi >> 2
    own_block = group == (lanei >> 6)
    j = lax.broadcasted_iota(I32, (N_HEADS, win), 1)
    diff = win - j
    valid = jnp.logical_and(diff >= 0, diff < WINDOW)
    sink = sink_ref[...]
    scale = HEAD_DIM ** -0.5
    for b in range(per_step):
        qm = jnp.where(own_block, q_ref[b], 0.0)
        s = lax.dot_general(qm.astype(BF16), ck_ref[b].astype(BF16), (((1,), (1,)), ((), ())),
                            preferred_element_type=F32) * scale
        s = jnp.where(valid, s, -jnp.inf)
        s_new = jnp.sum(qm * kn_ref[b:b + 1, :], axis=-1, keepdims=True) * scale
        m = jnp.maximum(jnp.maximum(jnp.max(s, axis=-1, keepdims=True), s_new), sink)
        p = jnp.exp(s - m)
        p_new = jnp.exp(s_new - m)
        den = jnp.sum(p, axis=-1, keepdims=True) + p_new + jnp.exp(sink - m)
        o = jnp.dot(p.astype(BF16), cv_ref[b].astype(BF16), preferred_element_type=F32) + p_new * vn_ref[b:b + 1, :]
        o = o * (1.0 / den)
        res = o
        for sft in range(1, N_KV_HEADS):
            res = jnp.where(group == sft, pltpu.roll(o, D_KV - sft * HEAD_DIM, 1), res)
        o_ref[b] = res[:, :HEAD_DIM]


def _attn_sample(q4, ck, cv, kn, vn, sink_col, *, per_step):
    nb, win = ck.shape[0], ck.shape[1]
    return pl.pallas_call(
        functools.partial(_attn_sample_kernel, per_step=per_step, win=win),
        grid=(nb // per_step,),
        in_specs=[pl.BlockSpec((per_step, N_HEADS, D_KV), lambda i: (i, 0, 0)),
                  pl.BlockSpec((per_step, win, D_KV), lambda i: (i, 0, 0)),
                  pl.BlockSpec((per_step, win, D_KV), lambda i: (i, 0, 0)),
                  _rows(per_step, D_KV), _rows(per_step, D_KV), _resident(sink_col.shape)],
        out_specs=pl.BlockSpec((per_step, N_HEADS, HEAD_DIM), lambda i: (i, 0, 0)),
        out_shape=jax.ShapeDtypeStruct((nb, N_HEADS, HEAD_DIM), F32),
        compiler_params=_cparams(),
        name="attn_sample",
    )(q4, ck, cv, kn, vn, sink_col)


def _residual_proj_kernel(x_ref, a_ref, w_ref, o_ref):
    o_ref[...] = x_ref[...] + jnp.dot(a_ref[...].astype(BF16), w_ref[...], preferred_element_type=F32)


def _residual_proj(x, a, w):
    n = x.shape[0]
    return pl.pallas_call(
        _residual_proj_kernel,
        grid=(1,),
        in_specs=[_rows(n, D_MODEL), _rows(n, a.shape[1]), _resident(w.shape)],
        out_specs=_rows(n, D_MODEL),
        out_shape=jax.ShapeDtypeStruct((n, D_MODEL), F32),
        compiler_params=_cparams(),
        name="attn_out_sample",
    )(x, a, w)


def _rope_tables(pos):
    half = ROPE_DIM // 2
    inv_freq = jnp.power(jnp.float32(ROPE_THETA), -jnp.arange(half, dtype=F32) / half)
    ang = pos.astype(F32)[:, None] * inv_freq[None, :]
    cos, sin = jnp.cos(ang), jnp.sin(ang)
    n = pos.shape[0]
    rest = HEAD_DIM - ROPE_DIM
    cos_h = jnp.concatenate([cos, cos, jnp.ones((n, rest), F32)], axis=1)
    sin_h = jnp.concatenate([-sin, sin, jnp.zeros((n, rest), F32)], axis=1)
    return jnp.tile(cos_h, (1, 2)), jnp.tile(sin_h, (1, 2))


def kernel(x_prompt, x_sample, cache_k, cache_v, a_norm_g, a_w_in, a_ln_g, a_ln_b, a_w_s, a_b_s, a_w_out, kv_norm_g, w_k, w_v, k_norm_g, b_norm_g, w_q, q_norm_g, sinks, w_o, ffn_norm_g, w_router, b_router, w_gu, b_gu, w_down, b_down):
    bsz, seq, _ = x_prompt.shape
    dec_b, dec_seq, _ = x_sample.shape
    assert dec_seq == 1 and seq % TM_DENSE == 0 and a_norm_g.shape[0] == 1 and b_norm_g.shape[0] == 1
    win = cache_k.shape[1]
    n_p, n_s = bsz * seq, dec_b * dec_seq
    xp = x_prompt.reshape(n_p, D_MODEL)
    xs = x_sample.reshape(n_s, D_MODEL)

    ng = a_norm_g[0].reshape(1, D_MODEL)
    win_bf = a_w_in[0].astype(BF16)
    wout_bf = a_w_out[0].astype(BF16)
    lng = a_ln_g[0].reshape(1, D_GATE)
    lnb = a_ln_b[0].reshape(1, D_GATE)
    bias_full = jnp.repeat(a_b_s[0].T, GROUP_DIM_A, axis=1)
    diag_row = jnp.repeat(a_w_s[0][:, 0, 0], GROUP_DIM_A).reshape(1, D_GATE)
    (xp,) = _mixer_a(xp, ng, win_bf, lng, lnb, a_w_s[0], bias_full, wout_bf, tm=TM_DENSE, single_token_chunks=False)
    xs, v_rows = _mixer_a(xs, ng, win_bf, lng, lnb, diag_row, bias_full[0:1], wout_bf, tm=n_s,
                          single_token_chunks=True)

    kvg = kv_norm_g.reshape(1, D_MODEL)
    bng = b_norm_g[0].reshape(1, D_MODEL)
    w_k_dup = jnp.repeat(w_k.reshape(D_MODEL, N_KV_HEADS, 1, HEAD_DIM), 2, axis=2).reshape(D_MODEL, 2 * D_KV)
    wk_bf, wv_bf, wq_bf, wo_bf = (w.astype(BF16) for w in (w_k_dup, w_v, w_q[0], w_o[0]))
    kng2 = jnp.tile(k_norm_g, 2).reshape(1, PAIR)
    qng2 = jnp.tile(q_norm_g[0], 2).reshape(1, PAIR)
    cos_p, sin_p = _rope_tables(jnp.arange(seq, dtype=I32))
    cos_s, sin_s = _rope_tables(jnp.full((n_s,), PAST_LEN, I32))
    attn_pre = (kvg, wk_bf, wv_bf, kng2, bng, wq_bf, qng2, cos_p, sin_p, seq // TM_COMBINE)
    (xp, k_p, kdup_p, v_p, q_p), (xs,) = _moe([xp, xs], ffn_norm_g[0], w_router[0], b_router[0], w_gu, b_gu,
                                              w_down, b_down, 0, prompt_attn_pre=attn_pre)

    k_s, _, v_s, q_s = _kv_proj(xs, kvg, wk_bf, wv_bf, kng2, cos_s, sin_s, (bng, wq_bf, qng2), tm=n_s,
                                table_tiles=1)
    xp = _attn_prompt(xp, q_p, kdup_p, v_p, sinks[0], wo_bf, tm=TM_DENSE, seq=seq)
    q4 = jnp.tile(q_s.reshape(n_s, N_HEADS, HEAD_DIM), (1, 1, N_KV_HEADS))
    attn_s = _attn_sample(q4, cache_k.reshape(dec_b, win, D_KV), cache_v.reshape(dec_b, win, D_KV), k_s, v_s,
                          sinks[0].reshape(N_HEADS, 1), per_step=8)
    xs = _residual_proj(xs, attn_s.reshape(n_s, D_MODEL), wo_bf)
    (xp,), (xs,) = _moe([xp, xs], ffn_norm_g[1], w_router[1], b_router[1], w_gu, b_gu, w_down, b_down, 1)

    def last_window(a):
        a = a.reshape(bsz, seq, D_KV)[:, -WINDOW:]
        return a.reshape(bsz, min(WINDOW, seq), N_KV_HEADS, HEAD_DIM)

    return (xp.reshape(bsz, seq, D_MODEL), xs.reshape(dec_b, dec_seq, D_MODEL),
            v_rows.reshape(1, dec_b, dec_seq, D_GATE), last_window(k_p), last_window(v_p),
            k_s.reshape(dec_b, dec_seq, N_KV_HEADS, HEAD_DIM), v_s.reshape(dec_b, dec_seq, N_KV_HEADS, HEAD_DIM))
```

```python
import functools

import jax
import jax.numpy as jnp
from jax import lax
from jax.experimental import pallas as pl
from jax.experimental.pallas import tpu as pltpu

F32, BF16, I32 = jnp.float32, jnp.bfloat16, jnp.int32

D_MODEL = 1024
CHUNK = 128
D_GATE = 2 * D_MODEL
N_GROUPS_A = 8
GROUP_DIM_A = D_GATE // N_GROUPS_A
HEAD_DIM = 64
N_HEADS = D_MODEL // HEAD_DIM
N_KV_HEADS = N_HEADS // 4
Q_PER_KV = N_HEADS // N_KV_HEADS
D_KV = N_KV_HEADS * HEAD_DIM
WINDOW = 128
ROPE_DIM = HEAD_DIM // 4
ROPE_THETA = 500000.0
N_EXPERTS = 32
TOP_K = 4
D_EXPERT = D_MODEL
SWIGLU_LIMIT = 7.0
SWIGLU_ALPHA = 1.702
EPS = 1e-6
PAST_LEN = 8192

V7X_LANES = 128
V7X_VMEM_BYTES = 64 * 1024 * 1024
VMEM_LIMIT_BYTES = V7X_VMEM_BYTES - 8 * 1024 * 1024

TM_DENSE = 512
TM_COMBINE = 256
MOE_ROWS = 512
TM_DISPATCH = 256
FILL_BITS = MOE_ROWS.bit_length() - 1
TOKEN_SUBLANES = D_MODEL // V7X_LANES
PACKED_SUBLANES = TOKEN_SUBLANES // 2
U32 = jnp.uint32
SLAB = V7X_LANES
REC_ROWS = 16
PAIR = 2 * HEAD_DIM
INV_SQRT2 = 0.7071067811865476


def _cparams():
    return pltpu.CompilerParams(dimension_semantics=("arbitrary",), vmem_limit_bytes=VMEM_LIMIT_BYTES)


def _resident(shape):
    zeros = (0,) * len(shape)
    return pl.BlockSpec(shape, lambda i, *_: zeros, pipeline_mode=pl.Buffered(1))


def _rows(tm, width, offset=0):
    return pl.BlockSpec((tm, width), lambda i, *_: (i + offset, 0))


def _rms(x, g):
    return x * lax.rsqrt(jnp.mean(x * x, axis=-1, keepdims=True) + EPS) * g


def _pair_norm_rope(x2, g2, cos2, sin2):
    lane = lax.broadcasted_iota(I32, x2.shape, 1)
    r = lax.broadcasted_iota(I32, (PAIR, PAIR), 0)
    c = lax.broadcasted_iota(I32, (PAIR, PAIR), 1)
    head_mean = jnp.where((r < HEAD_DIM) == (c < HEAD_DIM), 1.0 / HEAD_DIM, 0.0).astype(BF16)
    sq = x2 * x2
    sq_hi = sq.astype(BF16)
    sq_lo = (sq - sq_hi.astype(F32)).astype(BF16)
    ms = (jnp.dot(sq_hi, head_mean, preferred_element_type=F32)
          + jnp.dot(sq_lo, head_mean, preferred_element_type=F32))
    y = x2 * lax.rsqrt(ms + EPS) * g2
    half = ROPE_DIM // 2
    up = pltpu.roll(y, PAIR - half, 1)
    dn = pltpu.roll(y, half, 1)
    partner = jnp.where((lane & (HEAD_DIM - 1)) < half, up, dn)
    return y * cos2 + partner * sin2


def _mixer_a_kernel(x_ref, ng_ref, win_ref, lng_ref, lnb_ref, mix_ref, bias_ref, wout_ref, *refs,
                    tm, single_token_chunks):
    if single_token_chunks:
        o_ref, vout_ref, u_scr, v_scr, p_scr = refs
    else:
        o_ref, u_scr, v_scr, p_scr = refs
    x = x_ref[...]
    h = _rms(x, ng_ref[...]).astype(BF16)
    nc = 512
    for c in range(2 * D_GATE // nc):
        z = jnp.dot(h, win_ref[:, c * nc:(c + 1) * nc], preferred_element_type=F32)
        z = 0.5 * z * (1.0 + lax.erf(z * INV_SQRT2))
        if c < D_GATE // nc:
            u_scr[:, c * nc:(c + 1) * nc] = z
        else:
            v_scr[:, (c - D_GATE // nc) * nc:(c - D_GATE // nc + 1) * nc] = z
    v = v_scr[...]
    vc = v - jnp.mean(v, axis=-1, keepdims=True)
    vn = vc * lax.rsqrt(jnp.mean(vc * vc, axis=-1, keepdims=True) + EPS) * lng_ref[...] + lnb_ref[...]
    if single_token_chunks:
        vout_ref[...] = vn
        p_scr[...] = (u_scr[...] * (vn * mix_ref[...] + bias_ref[...])).astype(BF16)
    else:
        v_scr[...] = vn
        row = lax.broadcasted_iota(I32, (CHUNK, CHUNK), 0)
        col = lax.broadcasted_iota(I32, (CHUNK, CHUNK), 1)
        for g in range(N_GROUPS_A):
            wc = jnp.where(row >= col, mix_ref[g], 0.0).astype(BF16)
            cols = slice(g * GROUP_DIM_A, (g + 1) * GROUP_DIM_A)
            for c in range(tm // CHUNK):
                rows = slice(c * CHUNK, (c + 1) * CHUNK)
                mixed = jnp.dot(wc, v_scr[rows, cols].astype(BF16), preferred_element_type=F32)
                mixed = mixed + bias_ref[:, cols]
                p_scr[rows, cols] = (u_scr[rows, cols] * mixed).astype(BF16)
    o_ref[...] = x + jnp.dot(p_scr[...], wout_ref[...], preferred_element_type=F32)


def _mixer_a(x, ng, win, lng, lnb, mix, bias, wout, *, tm, single_token_chunks):
    n = x.shape[0]
    out_shape = [jax.ShapeDtypeStruct((n, D_MODEL), F32)]
    out_specs = [_rows(tm, D_MODEL)]
    if single_token_chunks:
        out_shape.append(jax.ShapeDtypeStruct((n, D_GATE), F32))
        out_specs.append(_rows(tm, D_GATE))
    return pl.pallas_call(
        functools.partial(_mixer_a_kernel, tm=tm, single_token_chunks=single_token_chunks),
        grid=(n // tm,),
        in_specs=[_rows(tm, D_MODEL), _resident(ng.shape), _resident(win.shape), _resident(lng.shape),
                  _resident(lnb.shape), _resident(mix.shape), _resident(bias.shape), _resident(wout.shape)],
        out_specs=out_specs,
        out_shape=out_shape,
        scratch_shapes=[pltpu.VMEM((tm, D_GATE), F32), pltpu.VMEM((tm, D_GATE), F32),
                        pltpu.VMEM((tm, D_GATE), BF16)],
        compiler_params=_cparams(),
        name="mixer_a_sample" if single_token_chunks else "mixer_a_prompt",
    )(x, ng, win, lng, lnb, mix, bias, wout)


def _route_rows(x, g, whi_t, wlo_t, bias_col, run_scr):
    tm = x.shape[0]
    h = _rms(x, g)
    hh = h.astype(BF16)
    hl = (h - hh.astype(F32)).astype(BF16)
    nt = (((1,), (1,)), ((), ()))
    logits = (lax.dot_general(whi_t, hh, nt, preferred_element_type=F32)
              + lax.dot_general(wlo_t, hh, nt, preferred_element_type=F32)
              + lax.dot_general(whi_t, hl, nt, preferred_element_type=F32) + bias_col)
    e = lax.broadcasted_iota(I32, (N_EXPERTS, tm), 0).astype(F32)
    hot, vals, idxs = [], [], []
    for _ in range(TOP_K):
        m = jnp.max(logits, axis=0, keepdims=True)
        idx = jnp.min(jnp.where(logits == m, e, float(N_EXPERTS)), axis=0, keepdims=True)
        oh = e == idx
        hot.append(oh)
        vals.append(m)
        idxs.append(idx)
        logits = jnp.where(oh, -jnp.inf, logits)
    ex = [jnp.exp(v - vals[0]) for v in vals]
    den = ex[0] + ex[1] + ex[2] + ex[3]
    picked = jnp.zeros((N_EXPERTS, tm), F32)
    for oh in hot:
        picked = picked + jnp.where(oh, 1.0, 0.0)
    r = lax.broadcasted_iota(I32, (tm, tm), 0)
    c = lax.broadcasted_iota(I32, (tm, tm), 1)
    earlier = jnp.where(r < c, 1.0, 0.0).astype(BF16)
    run = run_scr[:, 0:1]
    pos = jnp.dot(picked.astype(BF16), earlier, preferred_element_type=F32) + run
    run_scr[:, 0:1] = run + jnp.sum(picked, axis=1, keepdims=True)
    row = lax.broadcasted_iota(I32, (REC_ROWS, tm), 0)
    rec = jnp.zeros((REC_ROWS, tm), F32)
    for k in range(TOP_K):
        pos_k = jnp.sum(jnp.where(hot[k], pos, 0.0), axis=0, keepdims=True)
        rec = jnp.where(row == k, idxs[k], rec)
        rec = jnp.where(row == TOP_K + k, pos_k, rec)
        rec = jnp.where(row == 2 * TOP_K + k, ex[k] / den, rec)
    return h, rec


def _router_kernel(xp_ref, xs_ref, g_ref, whi_ref, wlo_ref, b_ref, h_ref, slabp_ref, slabs_ref, rec_ref, cnt_ref,
                   run_scr, *, n_tiles, n_s):
    i = pl.program_id(0)

    @pl.when(i == 0)
    def _():
        run_scr[...] = jnp.zeros_like(run_scr)

    def pad_rows(rec):
        return jnp.concatenate([rec, jnp.zeros((SLAB - REC_ROWS, rec.shape[1]), F32)], axis=0)

    @pl.when(i < n_tiles)
    def _():
        h, rec = _route_rows(xp_ref[...], g_ref[...], whi_ref[...], wlo_ref[...], b_ref[...], run_scr)
        _store_packed_tiles(h_ref, h, h.shape[0])
        slabp_ref[...] = pad_rows(rec).T
        rec_ref[...] = rec[0:2 * TOP_K, :]

    @pl.when(i == n_tiles)
    def _():
        h, rec = _route_rows(xs_ref[...], g_ref[...], whi_ref[...], wlo_ref[...], b_ref[...], run_scr)
        h_ref[n_s * PACKED_SUBLANES:, :] = jnp.zeros((h_ref.shape[0] - n_s * PACKED_SUBLANES, V7X_LANES), U32)
        _store_packed_tiles(h_ref, h, n_s)
        slabs_ref[...] = pad_rows(rec).T
        rec_ref[...] = jnp.zeros_like(rec_ref)
        rec_ref[:, 0:n_s] = rec[0:2 * TOP_K, :]

    cnt_ref[...] = run_scr[...]


def _store_packed_tiles(ref, x, n):
    half = D_MODEL // 2
    lo = pltpu.bitcast(x[:, :half].astype(BF16).astype(F32), U32)
    hi = pltpu.bitcast(x[:, half:].astype(BF16).astype(F32), U32)
    word = lax.shift_right_logical(lo, jnp.uint32(16)) | (hi & jnp.uint32(0xFFFF0000))
    for s in range(PACKED_SUBLANES):
        ref[pl.ds(s, n, stride=PACKED_SUBLANES), :] = word[:, s * V7X_LANES:(s + 1) * V7X_LANES]


def _load_packed_tiles(ref, n):
    words = [ref[pl.ds(s, n, stride=PACKED_SUBLANES), :] for s in range(PACKED_SUBLANES)]
    lo = [pltpu.bitcast(w << jnp.uint32(16), F32) for w in words]
    hi = [pltpu.bitcast(w & jnp.uint32(0xFFFF0000), F32) for w in words]
    return jnp.concatenate(lo + hi, axis=1).astype(BF16)


def _store_token_tiles(ref, x, n):
    for s in range(TOKEN_SUBLANES):
        ref[pl.ds(s, n, stride=TOKEN_SUBLANES), :] = x[:, s * V7X_LANES:(s + 1) * V7X_LANES]


def _load_token_tiles(ref, n):
    return jnp.concatenate([ref[pl.ds(s, n, stride=TOKEN_SUBLANES), :] for s in range(TOKEN_SUBLANES)], axis=1)


def _router(xp, xs, g, whi, wlo, b, *, tm):
    n_p, n_s = xp.shape[0], xs.shape[0]
    n_tiles = n_p // tm
    last = n_tiles - 1
    return pl.pallas_call(
        functools.partial(_router_kernel, n_tiles=n_tiles, n_s=n_s),
        grid=(n_tiles + 1,),
        in_specs=[pl.BlockSpec((tm, D_MODEL), lambda i: (jnp.minimum(i, last), 0)), _resident(xs.shape),
                  _resident(g.shape), _resident(whi.shape), _resident(wlo.shape), _resident(b.shape)],
        out_specs=[_rows(tm * PACKED_SUBLANES, V7X_LANES),
                   pl.BlockSpec((tm, SLAB), lambda i: (jnp.minimum(i, last), 0)),
                   pl.BlockSpec((n_s, SLAB), lambda i: (0, 0)),
                   pl.BlockSpec((2 * TOP_K, tm), lambda i: (0, i)),
                   pl.BlockSpec((N_EXPERTS, V7X_LANES), lambda i: (0, 0))],
        out_shape=[jax.ShapeDtypeStruct(((n_p + tm) * PACKED_SUBLANES, V7X_LANES), U32),
                   jax.ShapeDtypeStruct((n_p, SLAB), F32),
                   jax.ShapeDtypeStruct((n_s, SLAB), F32),
                   jax.ShapeDtypeStruct((2 * TOP_K, n_p + tm), F32),
                   jax.ShapeDtypeStruct((N_EXPERTS, V7X_LANES), F32)],
        scratch_shapes=[pltpu.VMEM((N_EXPERTS, V7X_LANES), F32)],
        compiler_params=_cparams(),
        name="moe_router",
    )(xp, xs, g, whi, wlo, b)


def _dispatch_kernel(fill_start_ref, fill_len_ref, dest_ref, h_ref, wgu_ref, wdn_ref, xs_ref, wgu_bf_ref, wdn_bf_ref,
                     zbuf, sem, fill_sem, *, tm, n_tiles, n_last, rows, n_blocks):
    i = pl.program_id(0)
    ts = PACKED_SUBLANES

    def tile_copy(r, k):
        d = dest_ref[0, 0, k * tm + r]
        src = h_ref.at[pl.ds(pl.multiple_of(r * ts, ts), ts)]
        dst = xs_ref.at[pl.ds(pl.multiple_of(d * ts, ts), ts)]
        return pltpu.make_async_copy(src, dst, sem)

    def scatter(n):
        def body(r, carry):
            for k in range(TOP_K):
                tile_copy(r, k).start(priority=k % 2)
            return carry
        lax.fori_loop(0, n, body, 0, unroll=4)
        wgu_bf_ref[...] = wgu_ref[...].astype(BF16)
        wdn_bf_ref[...] = wdn_ref[...].astype(BF16)
        for k in range(TOP_K):
            pltpu.make_async_copy(h_ref.at[pl.ds(0, n * ts)], xs_ref.at[pl.ds(0, n * ts)], sem).wait()

    def fill_copy(e, bit):
        size = 1 << bit
        length = fill_len_ref[e]
        done = length - (length & (2 * size - 1))
        off = pl.multiple_of((fill_start_ref[e] + done) * ts, ts)
        return (length & size) != 0, pltpu.make_async_copy(zbuf.at[pl.ds(0, size * ts)],
                                                            xs_ref.at[pl.ds(off, size * ts)], fill_sem)

    def tail_copy(blk):
        off = pl.multiple_of(blk * (rows * ts), rows * ts)
        return pltpu.make_async_copy(zbuf, xs_ref.at[pl.ds(off, rows * ts)], fill_sem)

    def for_each_fill(act):
        for e in range(N_EXPERTS):
            for bit in range(FILL_BITS):
                go, cp = fill_copy(e, bit)

                @pl.when(go)
                def _():
                    act(cp)

        def body(blk, carry):
            act(tail_copy(blk))
            return carry
        lax.fori_loop(fill_start_ref[N_EXPERTS], n_blocks, body, 0)

    @pl.when(i == 0)
    def _():
        zbuf[...] = jnp.zeros_like(zbuf)
        for_each_fill(lambda cp: cp.start())

    @pl.when(i < n_tiles - 1)
    def _():
        scatter(tm)

    @pl.when(i == n_tiles - 1)
    def _():
        scatter(n_last)
        for_each_fill(lambda cp: cp.wait())


def _dispatch(fill_start, fill_len, dest3, h_tiles, wgu, wdn, n_all, n_blocks, *, tm, rows, layer):
    n_tiles = dest3.shape[0]
    n_last = n_all - (n_tiles - 1) * tm
    ts = PACKED_SUBLANES
    n_slots = n_blocks * rows
    cast_steps = n_tiles - 1
    per_expert = cast_steps // N_EXPERTS
    wrows = D_MODEL // per_expert

    def wspec(width, out):
        def index(i, *_):
            c = jnp.minimum(i, cast_steps - 1)
            return (c // per_expert, lax.rem(c, per_expert), 0) if out else (layer, c // per_expert,
                                                                              lax.rem(c, per_expert), 0)
        return pl.BlockSpec(((None,) if out else (None, None)) + (wrows, width), index)

    grid_spec = pltpu.PrefetchScalarGridSpec(
        num_scalar_prefetch=2,
        grid=(n_tiles,),
        in_specs=[pl.BlockSpec((1, 1, TOP_K * tm), lambda i, *_: (i, 0, 0), memory_space=pltpu.SMEM),
                  _rows(tm * ts, V7X_LANES), wspec(2 * D_EXPERT, False), wspec(D_MODEL, False)],
        out_specs=[pl.BlockSpec(memory_space=pl.ANY), wspec(2 * D_EXPERT, True), wspec(D_MODEL, True)],
        scratch_shapes=[pltpu.VMEM((rows * ts, V7X_LANES), U32),
                        pltpu.SemaphoreType.DMA, pltpu.SemaphoreType.DMA],
    )
    return pl.pallas_call(
        functools.partial(_dispatch_kernel, tm=tm, n_tiles=n_tiles, n_last=n_last, rows=rows, n_blocks=n_blocks),
        grid_spec=grid_spec,
        out_shape=[jax.ShapeDtypeStruct((n_slots * ts, V7X_LANES), U32),
                   jax.ShapeDtypeStruct((N_EXPERTS, D_MODEL, 2 * D_EXPERT), BF16),
                   jax.ShapeDtypeStruct((N_EXPERTS, D_EXPERT, D_MODEL), BF16)],
        compiler_params=_cparams(),
        name="moe_dispatch",
    )(fill_start, fill_len, dest3, h_tiles, wgu, wdn)


def _expert_kernel(be_ref, nact_ref, run_ref, rune_ref, x_ref, wgu_hbm, bgu_ref, wdn_hbm, bdn_ref, o_ref,
                   wgu_buf, wdn_buf, wsem, *, rows):
    b = pl.program_id(0)
    nact = nact_ref[0]

    def weight_copies(r, slot):
        e = rune_ref[r]
        return (pltpu.make_async_copy(wgu_hbm.at[e], wgu_buf.at[slot], wsem.at[0, slot]),
                pltpu.make_async_copy(wdn_hbm.at[e], wdn_buf.at[slot], wsem.at[1, slot]))

    @pl.when(b == 0)
    def _():
        for cp in weight_copies(0, 0):
            cp.start()

    @pl.when(b < nact)
    def _():
        r = run_ref[b]
        slot = lax.rem(r, 2)
        changed = jnp.logical_or(b == 0, r != run_ref[jnp.maximum(b - 1, 0)])

        @pl.when(changed)
        def _():
            for cp in weight_copies(r, slot):
                cp.wait()

            @pl.when(r + 1 < rune_ref[N_EXPERTS])
            def _():
                for cp in weight_copies(r + 1, 1 - slot):
                    cp.start()

        wgu_bf, wdn_bf = wgu_buf.at[slot], wdn_buf.at[slot]
        xb = _load_packed_tiles(x_ref, rows)
        nc = 512
        out = None
        for c in range(D_EXPERT // nc):
            gcols = slice(c * nc, (c + 1) * nc)
            ucols = slice(D_EXPERT + c * nc, D_EXPERT + (c + 1) * nc)
            gate = jnp.dot(xb, wgu_bf[:, gcols], preferred_element_type=F32) + bgu_ref[:, gcols]
            up = jnp.dot(xb, wgu_bf[:, ucols], preferred_element_type=F32) + bgu_ref[:, ucols]
            gate = jnp.minimum(gate, SWIGLU_LIMIT)
            up = jnp.clip(up, -SWIGLU_LIMIT, SWIGLU_LIMIT)
            act = ((up + 1.0) * gate * (1.0 / (1.0 + jnp.exp(-SWIGLU_ALPHA * gate)))).astype(BF16)
            part = jnp.dot(act, wdn_bf[gcols, :], preferred_element_type=F32)
            out = part if out is None else out + part
        _store_token_tiles(o_ref, out + bdn_ref[...], rows)

    @pl.when(b >= nact)
    def _():
        o_ref[...] = jnp.zeros_like(o_ref)


def _experts(block_e, n_active, block_run, run_e, xs_tiles, wgu, bgu, wdn, bdn, *, rows, layer):
    ts = TOKEN_SUBLANES
    nb = xs_tiles.shape[0] // (rows * PACKED_SUBLANES)
    grid_spec = pltpu.PrefetchScalarGridSpec(
        num_scalar_prefetch=4,
        grid=(nb,),
        in_specs=[
            pl.BlockSpec((rows * PACKED_SUBLANES, V7X_LANES), lambda b, be, na, *_: (jnp.minimum(b, na[0] - 1), 0)),
            pl.BlockSpec(memory_space=pl.ANY),
            pl.BlockSpec((None, None, 1, 2 * D_EXPERT), lambda b, be, *_: (layer, be[b], 0, 0)),
            pl.BlockSpec(memory_space=pl.ANY),
            pl.BlockSpec((None, None, 1, D_MODEL), lambda b, be, *_: (layer, be[b], 0, 0)),
        ],
        out_specs=pl.BlockSpec((rows * ts, V7X_LANES), lambda b, *_: (b, 0)),
        scratch_shapes=[pltpu.VMEM((2, D_MODEL, 2 * D_EXPERT), BF16), pltpu.VMEM((2, D_EXPERT, D_MODEL), BF16),
                        pltpu.SemaphoreType.DMA((2, 2))],
    )
    return pl.pallas_call(
        functools.partial(_expert_kernel, rows=rows),
        grid_spec=grid_spec,
        out_shape=jax.ShapeDtypeStruct((nb * rows * ts, V7X_LANES), F32),
        compiler_params=_cparams(),
        name="moe_experts",
    )(block_e, n_active, block_run, run_e, xs_tiles, wgu, bgu, wdn, bdn)


def _combine_kernel(dest_first_ref, dest_next_ref, x_ref, slab_ref, ebuf_ref, *refs, tm, n_tiles, attn_pre):
    if attn_pre:
        (kvg_ref, wk_ref, wv_ref, kng_ref, bng_ref, wq_ref, qng_ref, cos_ref, sin_ref,
         o_ref, k_ref, kdup_ref, v_ref, q_ref, rbuf, sem) = refs
    else:
        o_ref, rbuf, sem = refs
    i = pl.program_id(0)
    slot = lax.rem(i, 2)
    ts = TOKEN_SUBLANES

    def gather(dest_ref, s):
        def body(r, carry):
            for k in range(TOP_K):
                d = dest_ref[0, 0, k * tm + r]
                src = ebuf_ref.at[pl.ds(pl.multiple_of(d * ts, ts), ts)]
                dst = rbuf.at[s, k, pl.ds(pl.multiple_of(r * ts, ts), ts)]
                pltpu.make_async_copy(src, dst, sem.at[s]).start(priority=k % 2)
            return carry
        lax.fori_loop(0, tm, body, 0, unroll=4)

    @pl.when(i == 0)
    def _():
        gather(dest_first_ref, 0)

    if attn_pre:
        todo = [(r, k) for r in range(tm) for k in range(TOP_K)]

        def tick(n):
            for r, k in todo[:n]:
                d = dest_next_ref[0, 0, k * tm + r]
                src = ebuf_ref.at[pl.ds(pl.multiple_of(d * ts, ts), ts)]
                pltpu.make_async_copy(src, rbuf.at[1 - slot, k, pl.ds(r * ts, ts)],
                                      sem.at[1 - slot]).start(priority=k % 2)
            del todo[:n]
    else:
        @pl.when(i + 1 < n_tiles)
        def _():
            gather(dest_next_ref, 1 - slot)

        def tick(n):
            pass

    def wait_slot(s):
        for k in range(TOP_K):
            pltpu.make_async_copy(ebuf_ref.at[pl.ds(0, tm * ts)], rbuf.at[s, k], sem.at[s]).wait()

    wait_slot(slot)
    slab = slab_ref[...]
    gates = [slab[:, 2 * TOP_K + k:2 * TOP_K + k + 1] for k in range(TOP_K)]
    per_phase = TOP_K * tm // (4 * ts)
    for s in range(ts):
        cols = slice(s * V7X_LANES, (s + 1) * V7X_LANES)
        acc = x_ref[:, cols]
        for k in range(TOP_K):
            acc = acc + gates[k] * rbuf[slot, k, pl.ds(s, tm, stride=ts), :]
        o_ref[:, cols] = acc
        tick(per_phase)
    if attn_pre:
        x2 = o_ref[...]
        cos2, sin2 = cos_ref[...], sin_ref[...]
        kv_phases = 1 + N_KV_HEADS
        q_phases = 1 + D_MODEL // PAIR
        per_phase = len(todo) // (kv_phases + q_phases)
        _shared_kv(x2, kvg_ref[...], wk_ref, wv_ref, kng_ref[...], cos2, sin2, k_ref, kdup_ref, v_ref,
                   tick=lambda: tick(per_phase))
        hq = _rms(x2, bng_ref[...]).astype(BF16)
        q = jnp.dot(hq, wq_ref[...], preferred_element_type=F32)
        tick(per_phase)
        for p in range(D_MODEL // PAIR):
            cols = slice(p * PAIR, (p + 1) * PAIR)
            q_ref[:, cols] = (_pair_norm_rope(q[:, cols], qng_ref[...], cos2, sin2) * HEAD_DIM ** -0.5).astype(BF16)
            tick(per_phase)
        tick(len(todo))

        @pl.when(i == n_tiles - 1)
        def _():
            wait_slot(1 - slot)


def _shared_kv(x, kvg, wk_ref, wv_ref, kng2, cos2, sin2, k_ref, kdup_ref, v_ref, tick=lambda: None):
    h = _rms(x, kvg).astype(BF16)
    kd = jnp.dot(h, wk_ref[...], preferred_element_type=F32)
    v_ref[...] = jnp.dot(h, wv_ref[...], preferred_element_type=F32)
    tick()
    first_half = lax.broadcasted_iota(I32, (x.shape[0], PAIR), 1) < HEAD_DIM
    slabs = []
    for p in range(N_KV_HEADS):
        cols = slice(p * PAIR, (p + 1) * PAIR)
        slab = _pair_norm_rope(kd[:, cols], kng2, cos2, sin2)
        kdup_ref[:, cols] = slab
        slabs.append(slab)
        tick()
    for p in range(D_KV // PAIR):
        k_ref[:, p * PAIR:(p + 1) * PAIR] = jnp.where(first_half, slabs[2 * p], slabs[2 * p + 1])


def _combine(x, slab, dest3, ebuf, *, tm, attn_pre=None):
    n = x.shape[0]
    n_tiles = n // tm
    in_specs = [
        pl.BlockSpec((1, 1, TOP_K * tm), lambda i: (0, 0, 0), memory_space=pltpu.SMEM),
        pl.BlockSpec((1, 1, TOP_K * tm), lambda i: (jnp.minimum(i + 1, n_tiles - 1), 0, 0),
                     memory_space=pltpu.SMEM),
        _rows(tm, D_MODEL), _rows(tm, SLAB), pl.BlockSpec(memory_space=pl.ANY),
    ]
    args = [dest3, dest3, x, slab, ebuf]
    out_specs = [_rows(tm, D_MODEL)]
    out_shape = [jax.ShapeDtypeStruct((n, D_MODEL), F32)]
    if attn_pre is not None:
        *weights, cos2, sin2, table_tiles = attn_pre
        table_spec = pl.BlockSpec((tm, PAIR), lambda i: (lax.rem(i, table_tiles), 0))
        in_specs += [_resident(w.shape) for w in weights] + [table_spec, table_spec]
        args += list(weights) + [cos2, sin2]
        out_specs += [_rows(tm, D_KV), _rows(tm, 2 * D_KV), _rows(tm, D_KV), _rows(tm, D_MODEL)]
        out_shape += [jax.ShapeDtypeStruct((n, D_KV), F32), jax.ShapeDtypeStruct((n, 2 * D_KV), F32),
                      jax.ShapeDtypeStruct((n, D_KV), F32), jax.ShapeDtypeStruct((n, D_MODEL), BF16)]
    return pl.pallas_call(
        functools.partial(_combine_kernel, tm=tm, n_tiles=n_tiles, attn_pre=attn_pre is not None),
        grid=(n_tiles,),
        in_specs=in_specs, out_specs=out_specs, out_shape=out_shape,
        scratch_shapes=[pltpu.VMEM((2, TOP_K, tm * TOKEN_SUBLANES, V7X_LANES), F32),
                        pltpu.SemaphoreType.DMA((2,))],
        compiler_params=_cparams(),
        name="moe_combine_attn_pre" if attn_pre is not None else "moe_combine",
    )(*args)


def _moe(xs, g, w_router, b_router, wgu, bgu, wdn, bdn, layer, prompt_attn_pre=None):
    n_all = xs[0].shape[0] + xs[1].shape[0]
    w_t = w_router.T
    whi = w_t.astype(BF16)
    wlo = (w_t - whi.astype(F32)).astype(BF16)
    h_all, slab_p, slab_s, rec, cnt = _router(xs[0], xs[1], g.reshape(1, D_MODEL), whi, wlo,
                                              b_router.reshape(N_EXPERTS, 1), tm=TM_DENSE)
    slabs = [slab_p, slab_s]
    idx = rec[:TOP_K, :n_all].astype(I32)
    pos = rec[TOP_K:, :n_all].astype(I32)
    counts = cnt[:, 0].astype(I32)
    rows = MOE_ROWS
    nb = (n_all * TOP_K + N_EXPERTS * (rows - 1)) // rows
    padded = (counts + rows - 1) // rows * rows
    pad_end = jnp.cumsum(padded)
    pad_start = pad_end - padded
    eid = jnp.arange(N_EXPERTS, dtype=I32)
    has_rows = counts > 0
    dest = pos + jnp.sum(jnp.where(idx[None] == eid[:, None, None], pad_start[:, None, None], 0), axis=0)
    n_active = (pad_end[-1] // rows).astype(I32)
    blk = jnp.arange(nb, dtype=I32)
    block_e = jnp.sum((pad_end[None, :] <= (blk * rows)[:, None]).astype(I32), axis=1)
    last_e = jnp.max(jnp.where(has_rows, eid, 0))
    block_e = jnp.where(blk < n_active, jnp.minimum(block_e, N_EXPERTS - 1), last_e)

    def tile_table(d, tm):
        n_tiles = -(-d.shape[1] // tm)
        d = jnp.pad(d, ((0, 0), (0, n_tiles * tm - d.shape[1])))
        return d.reshape(TOP_K, n_tiles, tm).transpose(1, 0, 2).reshape(n_tiles, 1, TOP_K * tm)

    fill_start = jnp.concatenate([pad_start + counts, n_active.reshape(1)])
    xs_tiles, wgu_bf, wdn_bf = _dispatch(fill_start, padded - counts, tile_table(dest, TM_DISPATCH), h_all,
                                         wgu, wdn, n_all, nb, tm=TM_DISPATCH, rows=rows, layer=layer)
    rank = jnp.cumsum(has_rows.astype(I32)) - 1
    run_e = jnp.sum(jnp.where(has_rows[None, :] & (rank[None, :] == eid[:, None]), eid[None, :], 0), axis=1)
    run_e = jnp.concatenate([run_e, jnp.sum(has_rows.astype(I32)).reshape(1)]).astype(I32)
    block_run = jnp.sum(jnp.where(block_e[:, None] == eid, rank, 0), axis=1).astype(I32)
    n_l = wgu.shape[0]
    ebuf = _experts(block_e, n_active.reshape(1), block_run, run_e, xs_tiles, wgu_bf,
                    bgu.reshape(n_l, N_EXPERTS, 1, 2 * D_EXPERT), wdn_bf, bdn.reshape(n_l, N_EXPERTS, 1, D_MODEL),
                    rows=rows, layer=layer)
    outs, off = [], 0
    for x, slab, pre in zip(xs, slabs, (prompt_attn_pre, None)):
        n = x.shape[0]
        tm = min(TM_COMBINE, n)
        outs.append(_combine(x, slab, tile_table(dest[:, off:off + n], tm), ebuf, tm=tm, attn_pre=pre))
        off += n
    return outs


def _kv_kernel(x_ref, kvg_ref, wk_ref, wv_ref, kng_ref, cos_ref, sin_ref, *refs, with_q):
    if with_q:
        bng_ref, wq_ref, qng_ref, k_ref, kdup_ref, v_ref, q_ref = refs
    else:
        k_ref, kdup_ref, v_ref = refs
    x = x_ref[...]
    cos2, sin2 = cos_ref[...], sin_ref[...]
    _shared_kv(x, kvg_ref[...], wk_ref, wv_ref, kng_ref[...], cos2, sin2, k_ref, kdup_ref, v_ref)
    if with_q:
        hq = _rms(x, bng_ref[...]).astype(BF16)
        q = jnp.dot(hq, wq_ref[...], preferred_element_type=F32)
        for p in range(D_MODEL // PAIR):
            cols = slice(p * PAIR, (p + 1) * PAIR)
            q_ref[:, cols] = _pair_norm_rope(q[:, cols], qng_ref[...], cos2, sin2)


def _kv_proj(x, kvg, wk, wv, kng2, cos2, sin2, q_args, *, tm, table_tiles):
    n = x.shape[0]
    with_q = q_args is not None
    table_spec = pl.BlockSpec((tm, PAIR), lambda i: (lax.rem(i, table_tiles), 0))
    in_specs = [_rows(tm, D_MODEL), _resident(kvg.shape), _resident(wk.shape), _resident(wv.shape),
                _resident(kng2.shape), table_spec, table_spec]
    args = [x, kvg, wk, wv, kng2, cos2, sin2]
    out_specs = [_rows(tm, D_KV), _rows(tm, 2 * D_KV), _rows(tm, D_KV)]
    out_shape = [jax.ShapeDtypeStruct((n, D_KV), F32), jax.ShapeDtypeStruct((n, 2 * D_KV), F32),
                 jax.ShapeDtypeStruct((n, D_KV), F32)]
    if with_q:
        in_specs += [_resident(a.shape) for a in q_args]
        args += list(q_args)
        out_specs.append(_rows(tm, D_MODEL))
        out_shape.append(jax.ShapeDtypeStruct((n, D_MODEL), F32))
    return pl.pallas_call(
        functools.partial(_kv_kernel, with_q=with_q),
        grid=(n // tm,),
        in_specs=in_specs, out_specs=out_specs, out_shape=out_shape,
        compiler_params=_cparams(),
        name="kvq_proj_sample" if with_q else "kv_proj_prompt",
    )(*args)


def _attn_prompt_kernel(sinks_ref, x_ref, q_ref, kc_ref, kp_ref, vc_ref, vp_ref, wo_ref, o_ref, a_scr,
                        *, tm, tiles_per_seq):
    i = pl.program_id(0)

    kj = lax.broadcasted_iota(I32, (2 * WINDOW, WINDOW), 0)
    qi = lax.broadcasted_iota(I32, (2 * WINDOW, WINDOW), 1)
    diff = qi + WINDOW - kj
    band = jnp.logical_and(diff >= 0, diff < WINDOW)
    first_half = lax.broadcasted_iota(I32, (WINDOW, PAIR), 1) < HEAD_DIM
    keep = (jnp.where(first_half, 1.0, 0.0).astype(BF16), jnp.where(first_half, 0.0, 1.0).astype(BF16))
    first_kj = jnp.where(lax.rem(i, tiles_per_seq) == 0, WINDOW, 0)

    for blk in range(tm // WINDOW):
        rows = slice(blk * WINDOW, (blk + 1) * WINDOW)
        if blk == 0:
            kk = jnp.concatenate([kp_ref[...], kc_ref[rows, :]], axis=0).astype(BF16)
            vv = jnp.concatenate([vp_ref[...], vc_ref[rows, :]], axis=0)
            valid = jnp.logical_and(band, kj >= first_kj)
        else:
            win = slice((blk - 1) * WINDOW, (blk + 1) * WINDOW)
            kk = kc_ref[win, :].astype(BF16)
            vv = vc_ref[win, :]
            valid = band
        for kvh in range(N_KV_HEADS):
            kk_h = kk[:, kvh * PAIR:(kvh + 1) * PAIR]
            vcols = slice((kvh // 2) * PAIR, (kvh // 2 + 1) * PAIR)
            vt = vv[:, vcols].T.astype(BF16)
            ch = slice((kvh % 2) * HEAD_DIM, (kvh % 2 + 1) * HEAD_DIM)
            outs = []
            for g in range(Q_PER_KV):
                head = kvh * Q_PER_KV + g
                qm = q_ref[rows, (head // 2) * PAIR:(head // 2 + 1) * PAIR] * keep[head % 2]
                sink = sinks_ref[head]
                st = lax.dot_general(kk_h, qm, (((1,), (1,)), ((), ())), preferred_element_type=F32)
                st = jnp.where(valid, st, -jnp.inf)
                m = jnp.maximum(jnp.max(st, axis=0, keepdims=True), sink)
                p = jnp.exp(st - m)
                den = jnp.sum(p, axis=0, keepdims=True) + jnp.exp(sink - m)
                ot = jnp.dot(vt, p.astype(BF16), preferred_element_type=F32)
                outs.append(ot[ch, :] * (1.0 / den))
            for pq in range(Q_PER_KV // 2):
                pair = (kvh * Q_PER_KV) // 2 + pq
                both = jnp.concatenate([outs[2 * pq], outs[2 * pq + 1]], axis=0)
                a_scr[rows, pair * PAIR:(pair + 1) * PAIR] = both.T.astype(BF16)
    o_ref[...] = x_ref[...] + jnp.dot(a_scr[...], wo_ref[...], preferred_element_type=F32)


def _attn_prompt(x, q, kdup, v, sinks, wo, *, tm, seq):
    n = x.shape[0]
    tiles_per_seq = seq // tm
    blocks_per_tile = tm // WINDOW

    def prev_spec(width):
        return pl.BlockSpec((WINDOW, width), lambda i: (jnp.maximum(i * blocks_per_tile - 1, 0), 0))

    return pl.pallas_call(
        functools.partial(_attn_prompt_kernel, tm=tm, tiles_per_seq=tiles_per_seq),
        grid=(n // tm,),
        in_specs=[pl.BlockSpec(memory_space=pltpu.SMEM), _rows(tm, D_MODEL), _rows(tm, D_MODEL),
                  _rows(tm, 2 * D_KV), prev_spec(2 * D_KV), _rows(tm, D_KV), prev_spec(D_KV), _resident(wo.shape)],
        out_specs=_rows(tm, D_MODEL),
        out_shape=jax.ShapeDtypeStruct((n, D_MODEL), F32),
        scratch_shapes=[pltpu.VMEM((tm, D_MODEL), BF16)],
        compiler_params=_cparams(),
        name="attn_prompt",
    )(sinks, x, q, kdup, kdup, v, v, wo)


def _attn_sample_kernel(q_ref, ck_ref, cv_ref, kn_ref, vn_ref, sink_ref, o_ref, *, per_step, win):
    rowi = lax.broadcasted_iota(I32, (N_HEADS, D_KV), 0)
    lanei = lax.broadcasted_iota(I32, (N_HEADS, D_KV), 1)
    group = rowi >> 2
    own_block = group == (lanei >> 6)
    j = lax.broadcasted_iota(I32, (N_HEADS, win), 1)
    diff = win - j
    valid = jnp.logical_and(diff >= 0, diff < WINDOW)
    sink = sink_ref[...]
    scale = HEAD_DIM ** -0.5
    for b in range(per_step):
        qm = jnp.where(own_block, q_ref[b], 0.0)
        s = lax.dot_general(qm.astype(BF16), ck_ref[b].astype(BF16), (((1,), (1,)), ((), ())),
                            preferred_element_type=F32) * scale
        s = jnp.where(valid, s, -jnp.inf)
        s_new = jnp.sum(qm * kn_ref[b:b + 1, :], axis=-1, keepdims=True) * scale
        m = jnp.maximum(jnp.maximum(jnp.max(s, axis=-1, keepdims=True), s_new), sink)
        p = jnp.exp(s - m)
        p_new = jnp.exp(s_new - m)
        den = jnp.sum(p, axis=-1, keepdims=True) + p_new + jnp.exp(sink - m)
        o = jnp.dot(p.astype(BF16), cv_ref[b].astype(BF16), preferred_element_type=F32) + p_new * vn_ref[b:b + 1, :]
        o = o * (1.0 / den)
        res = o
        for sft in range(1, N_KV_HEADS):
            res = jnp.where(group == sft, pltpu.roll(o, D_KV - sft * HEAD_DIM, 1), res)
        o_ref[b] = res[:, :HEAD_DIM]


def _attn_sample(q4, ck, cv, kn, vn, sink_col, *, per_step):
    nb, win = ck.shape[0], ck.shape[1]
    return pl.pallas_call(
        functools.partial(_attn_sample_kernel, per_step=per_step, win=win),
        grid=(nb // per_step,),
        in_specs=[pl.BlockSpec((per_step, N_HEADS, D_KV), lambda i: (i, 0, 0)),
                  pl.BlockSpec((per_step, win, D_KV), lambda i: (i, 0, 0)),
                  pl.BlockSpec((per_step, win, D_KV), lambda i: (i, 0, 0)),
                  _rows(per_step, D_KV), _rows(per_step, D_KV), _resident(sink_col.shape)],
        out_specs=pl.BlockSpec((per_step, N_HEADS, HEAD_DIM), lambda i: (i, 0, 0)),
        out_shape=jax.ShapeDtypeStruct((nb, N_HEADS, HEAD_DIM), F32),
        compiler_params=_cparams(),
        name="attn_sample",
    )(q4, ck, cv, kn, vn, sink_col)


def _residual_proj_kernel(x_ref, a_ref, w_ref, o_ref):
    o_ref[...] = x_ref[...] + jnp.dot(a_ref[...].astype(BF16), w_ref[...], preferred_element_type=F32)


def _residual_proj(x, a, w):
    n = x.shape[0]
    return pl.pallas_call(
        _residual_proj_kernel,
        grid=(1,),
        in_specs=[_rows(n, D_MODEL), _rows(n, a.shape[1]), _resident(w.shape)],
        out_specs=_rows(n, D_MODEL),
        out_shape=jax.ShapeDtypeStruct((n, D_MODEL), F32),
        compiler_params=_cparams(),
        name="attn_out_sample",
    )(x, a, w)


def _rope_tables(pos):
    half = ROPE_DIM // 2
    inv_freq = jnp.power(jnp.float32(ROPE_THETA), -jnp.arange(half, dtype=F32) / half)
    ang = pos.astype(F32)[:, None] * inv_freq[None, :]
    cos, sin = jnp.cos(ang), jnp.sin(ang)
    n = pos.shape[0]
    rest = HEAD_DIM - ROPE_DIM
    cos_h = jnp.concatenate([cos, cos, jnp.ones((n, rest), F32)], axis=1)
    sin_h = jnp.concatenate([-sin, sin, jnp.zeros((n, rest), F32)], axis=1)
    return jnp.tile(cos_h, (1, 2)), jnp.tile(sin_h, (1, 2))


def kernel(x_prompt, x_sample, cache_k, cache_v, a_norm_g, a_w_in, a_ln_g, a_ln_b, a_w_s, a_b_s, a_w_out, kv_norm_g, w_k, w_v, k_norm_g, b_norm_g, w_q, q_norm_g, sinks, w_o, ffn_norm_g, w_router, b_router, w_gu, b_gu, w_down, b_down):
    bsz, seq, _ = x_prompt.shape
    dec_b, dec_seq, _ = x_sample.shape
    assert dec_seq == 1 and seq % TM_DENSE == 0 and a_norm_g.shape[0] == 1 and b_norm_g.shape[0] == 1
    win = cache_k.shape[1]
    n_p, n_s = bsz * seq, dec_b * dec_seq
    xp = x_prompt.reshape(n_p, D_MODEL)
    xs = x_sample.reshape(n_s, D_MODEL)

    ng = a_norm_g[0].reshape(1, D_MODEL)
    win_bf = a_w_in[0].astype(BF16)
    wout_bf = a_w_out[0].astype(BF16)
    lng = a_ln_g[0].reshape(1, D_GATE)
    lnb = a_ln_b[0].reshape(1, D_GATE)
    bias_full = jnp.repeat(a_b_s[0].T, GROUP_DIM_A, axis=1)
    diag_row = jnp.repeat(a_w_s[0][:, 0, 0], GROUP_DIM_A).reshape(1, D_GATE)
    (xp,) = _mixer_a(xp, ng, win_bf, lng, lnb, a_w_s[0], bias_full, wout_bf, tm=TM_DENSE, single_token_chunks=False)
    xs, v_rows = _mixer_a(xs, ng, win_bf, lng, lnb, diag_row, bias_full[0:1], wout_bf, tm=n_s,
                          single_token_chunks=True)

    kvg = kv_norm_g.reshape(1, D_MODEL)
    bng = b_norm_g[0].reshape(1, D_MODEL)
    w_k_dup = jnp.repeat(w_k.reshape(D_MODEL, N_KV_HEADS, 1, HEAD_DIM), 2, axis=2).reshape(D_MODEL, 2 * D_KV)
    wk_bf, wv_bf, wq_bf, wo_bf = (w.astype(BF16) for w in (w_k_dup, w_v, w_q[0], w_o[0]))
    kng2 = jnp.tile(k_norm_g, 2).reshape(1, PAIR)
    qng2 = jnp.tile(q_norm_g[0], 2).reshape(1, PAIR)
    cos_p, sin_p = _rope_tables(jnp.arange(seq, dtype=I32))
    cos_s, sin_s = _rope_tables(jnp.full((n_s,), PAST_LEN, I32))
    attn_pre = (kvg, wk_bf, wv_bf, kng2, bng, wq_bf, qng2, cos_p, sin_p, seq // TM_COMBINE)
    (xp, k_p, kdup_p, v_p, q_p), (xs,) = _moe([xp, xs], ffn_norm_g[0], w_router[0], b_router[0], w_gu, b_gu,
                                              w_down, b_down, 0, prompt_attn_pre=attn_pre)

    k_s, _, v_s, q_s = _kv_proj(xs, kvg, wk_bf, wv_bf, kng2, cos_s, sin_s, (bng, wq_bf, qng2), tm=n_s,
                                table_tiles=1)
    xp = _attn_prompt(xp, q_p, kdup_p, v_p, sinks[0], wo_bf, tm=TM_DENSE, seq=seq)
    q4 = jnp.tile(q_s.reshape(n_s, N_HEADS, HEAD_DIM), (1, 1, N_KV_HEADS))
    attn_s = _attn_sample(q4, cache_k.reshape(dec_b, win, D_KV), cache_v.reshape(dec_b, win, D_KV), k_s, v_s,
                          sinks[0].reshape(N_HEADS, 1), per_step=8)
    xs = _residual_proj(xs, attn_s.reshape(n_s, D_MODEL), wo_bf)
    (xp,), (xs,) = _moe([xp, xs], ffn_norm_g[1], w_router[1], b_router[1], w_gu, b_gu, w_down, b_down, 1)

    def last_window(a):
        a = a.reshape(bsz, seq, D_KV)[:, -WINDOW:]
        return a.reshape(bsz, min(WINDOW, seq), N_KV_HEADS, HEAD_DIM)

    return (xp.reshape(bsz, seq, D_MODEL), xs.reshape(dec_b, dec_seq, D_MODEL),
            v_rows.reshape(1, dec_b, dec_seq, D_GATE), last_window(k_p), last_window(v_p),
            k_s.reshape(dec_b, dec_seq, N_KV_HEADS, HEAD_DIM), v_s.reshape(dec_b, dec_seq, N_KV_HEADS, HEAD_DIM))
```

```python
import functools

import jax
import jax.numpy as jnp
from jax import lax
from jax.experimental import pallas as pl
from jax.experimental.pallas import tpu as pltpu

F32, BF16, I32 = jnp.float32, jnp.bfloat16, jnp.int32

D_MODEL = 1024
CHUNK = 128
D_GATE = 2 * D_MODEL
N_GROUPS_A = 8
GROUP_DIM_A = D_GATE // N_GROUPS_A
HEAD_DIM = 64
N_HEADS = D_MODEL // HEAD_DIM
N_KV_HEADS = N_HEADS // 4
Q_PER_KV = N_HEADS // N_KV_HEADS
D_KV = N_KV_HEADS * HEAD_DIM
WINDOW = 128
ROPE_DIM = HEAD_DIM // 4
ROPE_THETA = 500000.0
N_EXPERTS = 32
TOP_K = 4
D_EXPERT = D_MODEL
SWIGLU_LIMIT = 7.0
SWIGLU_ALPHA = 1.702
EPS = 1e-6
PAST_LEN = 8192

V7X_LANES = 128
V7X_VMEM_BYTES = 64 * 1024 * 1024
VMEM_LIMIT_BYTES = V7X_VMEM_BYTES - 8 * 1024 * 1024

TM_DENSE = 512
TM_COMBINE = 256
TM_COMBINE_FUSED = 256
MOE_ROWS = 512
TM_DISPATCH = 256
FILL_BITS = MOE_ROWS.bit_length() - 1
TOKEN_SUBLANES = D_MODEL // V7X_LANES
PACKED_SUBLANES = TOKEN_SUBLANES // 2
U32 = jnp.uint32
SLAB = V7X_LANES
REC_ROWS = 16
PAIR = 2 * HEAD_DIM
INV_SQRT2 = 0.7071067811865476


def _cparams():
    return pltpu.CompilerParams(dimension_semantics=("arbitrary",), vmem_limit_bytes=VMEM_LIMIT_BYTES)


def _resident(shape):
    zeros = (0,) * len(shape)
    return pl.BlockSpec(shape, lambda i, *_: zeros, pipeline_mode=pl.Buffered(1))


def _rows(tm, width, offset=0):
    return pl.BlockSpec((tm, width), lambda i, *_: (i + offset, 0))


def _rms(x, g):
    return x * lax.rsqrt(jnp.mean(x * x, axis=-1, keepdims=True) + EPS) * g


def _pair_norm_rope(x2, g2, cos2, sin2):
    lane = lax.broadcasted_iota(I32, x2.shape, 1)
    r = lax.broadcasted_iota(I32, (PAIR, PAIR), 0)
    c = lax.broadcasted_iota(I32, (PAIR, PAIR), 1)
    head_mean = jnp.where((r < HEAD_DIM) == (c < HEAD_DIM), 1.0 / HEAD_DIM, 0.0).astype(BF16)
    sq = x2 * x2
    sq_hi = sq.astype(BF16)
    sq_lo = (sq - sq_hi.astype(F32)).astype(BF16)
    ms = (jnp.dot(sq_hi, head_mean, preferred_element_type=F32)
          + jnp.dot(sq_lo, head_mean, preferred_element_type=F32))
    y = x2 * lax.rsqrt(ms + EPS) * g2
    half = ROPE_DIM // 2
    up = pltpu.roll(y, PAIR - half, 1)
    dn = pltpu.roll(y, half, 1)
    partner = jnp.where((lane & (HEAD_DIM - 1)) < half, up, dn)
    return y * cos2 + partner * sin2


def _mixer_a_kernel(x_ref, ng_ref, win_ref, lng_ref, lnb_ref, mix_ref, bias_ref, wout_ref, *refs,
                    tm, single_token_chunks):
    if single_token_chunks:
        o_ref, vout_ref, u_scr, v_scr, p_scr = refs
    else:
        o_ref, u_scr, v_scr, p_scr = refs
    x = x_ref[...]
    h = _rms(x, ng_ref[...]).astype(BF16)
    nc = 512
    for c in range(2 * D_GATE // nc):
        z = jnp.dot(h, win_ref[:, c * nc:(c + 1) * nc], preferred_element_type=F32)
        z = 0.5 * z * (1.0 + lax.erf(z * INV_SQRT2))
        if c < D_GATE // nc:
            u_scr[:, c * nc:(c + 1) * nc] = z
        else:
            v_scr[:, (c - D_GATE // nc) * nc:(c - D_GATE // nc + 1) * nc] = z
    v = v_scr[...]
    vc = v - jnp.mean(v, axis=-1, keepdims=True)
    vn = vc * lax.rsqrt(jnp.mean(vc * vc, axis=-1, keepdims=True) + EPS) * lng_ref[...] + lnb_ref[...]
    if single_token_chunks:
        vout_ref[...] = vn
        p_scr[...] = (u_scr[...] * (vn * mix_ref[...] + bias_ref[...])).astype(BF16)
    else:
        v_scr[...] = vn
        row = lax.broadcasted_iota(I32, (CHUNK, CHUNK), 0)
        col = lax.broadcasted_iota(I32, (CHUNK, CHUNK), 1)
        for g in range(N_GROUPS_A):
            wc = jnp.where(row >= col, mix_ref[g], 0.0).astype(BF16)
            cols = slice(g * GROUP_DIM_A, (g + 1) * GROUP_DIM_A)
            for c in range(tm // CHUNK):
                rows = slice(c * CHUNK, (c + 1) * CHUNK)
                mixed = jnp.dot(wc, v_scr[rows, cols].astype(BF16), preferred_element_type=F32)
                mixed = mixed + bias_ref[:, cols]
                p_scr[rows, cols] = (u_scr[rows, cols] * mixed).astype(BF16)
    o_ref[...] = x + jnp.dot(p_scr[...], wout_ref[...], preferred_element_type=F32)


def _mixer_a(x, ng, win, lng, lnb, mix, bias, wout, *, tm, single_token_chunks):
    n = x.shape[0]
    out_shape = [jax.ShapeDtypeStruct((n, D_MODEL), F32)]
    out_specs = [_rows(tm, D_MODEL)]
    if single_token_chunks:
        out_shape.append(jax.ShapeDtypeStruct((n, D_GATE), F32))
        out_specs.append(_rows(tm, D_GATE))
    return pl.pallas_call(
        functools.partial(_mixer_a_kernel, tm=tm, single_token_chunks=single_token_chunks),
        grid=(n // tm,),
        in_specs=[_rows(tm, D_MODEL), _resident(ng.shape), _resident(win.shape), _resident(lng.shape),
                  _resident(lnb.shape), _resident(mix.shape), _resident(bias.shape), _resident(wout.shape)],
        out_specs=out_specs,
        out_shape=out_shape,
        scratch_shapes=[pltpu.VMEM((tm, D_GATE), F32), pltpu.VMEM((tm, D_GATE), F32),
                        pltpu.VMEM((tm, D_GATE), BF16)],
        compiler_params=_cparams(),
        name="mixer_a_sample" if single_token_chunks else "mixer_a_prompt",
    )(x, ng, win, lng, lnb, mix, bias, wout)


def _route_rows(x, g, whi_t, wlo_t, bias_col, run_scr):
    tm = x.shape[0]
    h = _rms(x, g)
    hh = h.astype(BF16)
    hl = (h - hh.astype(F32)).astype(BF16)
    nt = (((1,), (1,)), ((), ()))
    logits = (lax.dot_general(whi_t, hh, nt, preferred_element_type=F32)
              + lax.dot_general(wlo_t, hh, nt, preferred_element_type=F32)
              + lax.dot_general(whi_t, hl, nt, preferred_element_type=F32) + bias_col)
    e = lax.broadcasted_iota(I32, (N_EXPERTS, tm), 0).astype(F32)
    hot, vals, idxs = [], [], []
    for _ in range(TOP_K):
        m = jnp.max(logits, axis=0, keepdims=True)
        idx = jnp.min(jnp.where(logits == m, e, float(N_EXPERTS)), axis=0, keepdims=True)
        oh = e == idx
        hot.append(oh)
        vals.append(m)
        idxs.append(idx)
        logits = jnp.where(oh, -jnp.inf, logits)
    ex = [jnp.exp(v - vals[0]) for v in vals]
    den = ex[0] + ex[1] + ex[2] + ex[3]
    picked = jnp.zeros((N_EXPERTS, tm), F32)
    for oh in hot:
        picked = picked + jnp.where(oh, 1.0, 0.0)
    r = lax.broadcasted_iota(I32, (tm, tm), 0)
    c = lax.broadcasted_iota(I32, (tm, tm), 1)
    earlier = jnp.where(r < c, 1.0, 0.0).astype(BF16)
    run = run_scr[:, 0:1]
    pos = jnp.dot(picked.astype(BF16), earlier, preferred_element_type=F32) + run
    run_scr[:, 0:1] = run + jnp.sum(picked, axis=1, keepdims=True)
    row = lax.broadcasted_iota(I32, (REC_ROWS, tm), 0)
    rec = jnp.zeros((REC_ROWS, tm), F32)
    for k in range(TOP_K):
        pos_k = jnp.sum(jnp.where(hot[k], pos, 0.0), axis=0, keepdims=True)
        rec = jnp.where(row == k, idxs[k], rec)
        rec = jnp.where(row == TOP_K + k, pos_k, rec)
        rec = jnp.where(row == 2 * TOP_K + k, ex[k] / den, rec)
    return h, rec


def _router_kernel(xp_ref, xs_ref, g_ref, whi_ref, wlo_ref, b_ref, h_ref, slabp_ref, slabs_ref, rec_ref, cnt_ref,
                   run_scr, *, n_tiles, n_s):
    i = pl.program_id(0)

    @pl.when(i == 0)
    def _():
        run_scr[...] = jnp.zeros_like(run_scr)

    def pad_rows(rec):
        return jnp.concatenate([rec, jnp.zeros((SLAB - REC_ROWS, rec.shape[1]), F32)], axis=0)

    @pl.when(i < n_tiles)
    def _():
        h, rec = _route_rows(xp_ref[...], g_ref[...], whi_ref[...], wlo_ref[...], b_ref[...], run_scr)
        _store_packed_tiles(h_ref, h, h.shape[0])
        slabp_ref[...] = pad_rows(rec).T
        rec_ref[...] = rec[0:2 * TOP_K, :]

    @pl.when(i == n_tiles)
    def _():
        h, rec = _route_rows(xs_ref[...], g_ref[...], whi_ref[...], wlo_ref[...], b_ref[...], run_scr)
        h_ref[n_s * PACKED_SUBLANES:, :] = jnp.zeros((h_ref.shape[0] - n_s * PACKED_SUBLANES, V7X_LANES), U32)
        _store_packed_tiles(h_ref, h, n_s)
        slabs_ref[...] = pad_rows(rec).T
        rec_ref[...] = jnp.zeros_like(rec_ref)
        rec_ref[:, 0:n_s] = rec[0:2 * TOP_K, :]

    cnt_ref[...] = run_scr[...]


def _store_packed_tiles(ref, x, n):
    half = D_MODEL // 2
    lo = pltpu.bitcast(x[:, :half].astype(BF16).astype(F32), U32)
    hi = pltpu.bitcast(x[:, half:].astype(BF16).astype(F32), U32)
    word = lax.shift_right_logical(lo, jnp.uint32(16)) | (hi & jnp.uint32(0xFFFF0000))
    for s in range(PACKED_SUBLANES):
        ref[pl.ds(s, n, stride=PACKED_SUBLANES), :] = word[:, s * V7X_LANES:(s + 1) * V7X_LANES]


def _load_packed_tiles(ref, n):
    words = [ref[pl.ds(s, n, stride=PACKED_SUBLANES), :] for s in range(PACKED_SUBLANES)]
    lo = [pltpu.bitcast(w << jnp.uint32(16), F32) for w in words]
    hi = [pltpu.bitcast(w & jnp.uint32(0xFFFF0000), F32) for w in words]
    return jnp.concatenate(lo + hi, axis=1).astype(BF16)


def _store_token_tiles(ref, x, n):
    for s in range(TOKEN_SUBLANES):
        ref[pl.ds(s, n, stride=TOKEN_SUBLANES), :] = x[:, s * V7X_LANES:(s + 1) * V7X_LANES]


def _load_token_tiles(ref, n):
    return jnp.concatenate([ref[pl.ds(s, n, stride=TOKEN_SUBLANES), :] for s in range(TOKEN_SUBLANES)], axis=1)


def _router(xp, xs, g, whi, wlo, b, *, tm):
    n_p, n_s = xp.shape[0], xs.shape[0]
    n_tiles = n_p // tm
    last = n_tiles - 1
    return pl.pallas_call(
        functools.partial(_router_kernel, n_tiles=n_tiles, n_s=n_s),
        grid=(n_tiles + 1,),
        in_specs=[pl.BlockSpec((tm, D_MODEL), lambda i: (jnp.minimum(i, last), 0)), _resident(xs.shape),
                  _resident(g.shape), _resident(whi.shape), _resident(wlo.shape), _resident(b.shape)],
        out_specs=[_rows(tm * PACKED_SUBLANES, V7X_LANES),
                   pl.BlockSpec((tm, SLAB), lambda i: (jnp.minimum(i, last), 0)),
                   pl.BlockSpec((n_s, SLAB), lambda i: (0, 0)),
                   pl.BlockSpec((2 * TOP_K, tm), lambda i: (0, i)),
                   pl.BlockSpec((N_EXPERTS, V7X_LANES), lambda i: (0, 0))],
        out_shape=[jax.ShapeDtypeStruct(((n_p + tm) * PACKED_SUBLANES, V7X_LANES), U32),
                   jax.ShapeDtypeStruct((n_p, SLAB), F32),
                   jax.ShapeDtypeStruct((n_s, SLAB), F32),
                   jax.ShapeDtypeStruct((2 * TOP_K, n_p + tm), F32),
                   jax.ShapeDtypeStruct((N_EXPERTS, V7X_LANES), F32)],
        scratch_shapes=[pltpu.VMEM((N_EXPERTS, V7X_LANES), F32)],
        compiler_params=_cparams(),
        name="moe_router",
    )(xp, xs, g, whi, wlo, b)


def _dispatch_kernel(fill_start_ref, fill_len_ref, dest_ref, h_ref, xs_ref, zbuf, sem, fill_sem,
                     *, tm, n_tiles, n_last, rows, n_blocks):
    i = pl.program_id(0)
    ts = PACKED_SUBLANES

    def tile_copy(r, k):
        d = dest_ref[0, 0, k * tm + r]
        src = h_ref.at[pl.ds(pl.multiple_of(r * ts, ts), ts)]
        dst = xs_ref.at[pl.ds(pl.multiple_of(d * ts, ts), ts)]
        return pltpu.make_async_copy(src, dst, sem.at[k % 2])

    def scatter(n):
        def body(r, carry):
            for k in range(TOP_K):
                tile_copy(r, k).start(priority=k % 2)
            return carry
        lax.fori_loop(0, n, body, 0, unroll=4)
        for k in range(TOP_K):
            pltpu.make_async_copy(h_ref.at[pl.ds(0, n * ts)], xs_ref.at[pl.ds(0, n * ts)], sem.at[k % 2]).wait()

    def fill_copy(e, bit):
        size = 1 << bit
        length = fill_len_ref[e]
        done = length - (length & (2 * size - 1))
        off = pl.multiple_of((fill_start_ref[e] + done) * ts, ts)
        return (length & size) != 0, pltpu.make_async_copy(zbuf.at[pl.ds(0, size * ts)],
                                                            xs_ref.at[pl.ds(off, size * ts)], fill_sem)

    def tail_copy(blk):
        off = pl.multiple_of(blk * (rows * ts), rows * ts)
        return pltpu.make_async_copy(zbuf, xs_ref.at[pl.ds(off, rows * ts)], fill_sem)

    def for_each_fill(act):
        for e in range(N_EXPERTS):
            for bit in range(FILL_BITS):
                go, cp = fill_copy(e, bit)

                @pl.when(go)
                def _():
                    act(cp)

        def body(blk, carry):
            act(tail_copy(blk))
            return carry
        lax.fori_loop(fill_start_ref[N_EXPERTS], n_blocks, body, 0)

    @pl.when(i == 0)
    def _():
        zbuf[...] = jnp.zeros_like(zbuf)
        for_each_fill(lambda cp: cp.start())

    @pl.when(i < n_tiles - 1)
    def _():
        scatter(tm)

    @pl.when(i == n_tiles - 1)
    def _():
        scatter(n_last)
        for_each_fill(lambda cp: cp.wait())


def _dispatch(fill_start, fill_len, dest3, h_tiles, n_all, n_blocks, *, tm, rows):
    n_tiles = dest3.shape[0]
    n_last = n_all - (n_tiles - 1) * tm
    ts = PACKED_SUBLANES
    n_slots = n_blocks * rows
    grid_spec = pltpu.PrefetchScalarGridSpec(
        num_scalar_prefetch=2,
        grid=(n_tiles,),
        in_specs=[pl.BlockSpec((1, 1, TOP_K * tm), lambda i, *_: (i, 0, 0), memory_space=pltpu.SMEM),
                  _rows(tm * ts, V7X_LANES)],
        out_specs=pl.BlockSpec(memory_space=pl.ANY),
        scratch_shapes=[pltpu.VMEM((rows * ts, V7X_LANES), U32),
                        pltpu.SemaphoreType.DMA((2,)), pltpu.SemaphoreType.DMA],
    )
    return pl.pallas_call(
        functools.partial(_dispatch_kernel, tm=tm, n_tiles=n_tiles, n_last=n_last, rows=rows, n_blocks=n_blocks),
        grid_spec=grid_spec,
        out_shape=jax.ShapeDtypeStruct((n_slots * ts, V7X_LANES), U32),
        compiler_params=_cparams(),
        name="moe_dispatch",
    )(fill_start, fill_len, dest3, h_tiles)


def _expert_kernel(be_ref, nact_ref, run_ref, rune_ref, x_ref, wgu_hbm, bgu_ref, wdn_hbm, bdn_ref, o_ref,
                   wgu_f32, wdn_f32, wsem, wgu_bf, wdn_bf, *, rows, layer):
    b = pl.program_id(0)
    nact = nact_ref[0]

    def weight_copies(r, slot):
        e = rune_ref[r]
        return (pltpu.make_async_copy(wgu_hbm.at[layer, e], wgu_f32.at[slot], wsem.at[0, slot]),
                pltpu.make_async_copy(wdn_hbm.at[layer, e], wdn_f32.at[slot], wsem.at[1, slot]))

    @pl.when(b == 0)
    def _():
        for cp in weight_copies(0, 0):
            cp.start()

    @pl.when(b < nact)
    def _():
        r = run_ref[b]
        slot = lax.rem(r, 2)
        changed = jnp.logical_or(b == 0, r != run_ref[jnp.maximum(b - 1, 0)])

        @pl.when(changed)
        def _():
            for cp in weight_copies(r, slot):
                cp.wait()

            @pl.when(r + 1 < rune_ref[N_EXPERTS])
            def _():
                for cp in weight_copies(r + 1, 1 - slot):
                    cp.start()

            step = 256
            for c in range(D_MODEL // step):
                wgu_bf[c * step:(c + 1) * step, :] = wgu_f32[slot, c * step:(c + 1) * step, :].astype(BF16)
                wdn_bf[c * step:(c + 1) * step, :] = wdn_f32[slot, c * step:(c + 1) * step, :].astype(BF16)

        xb = _load_packed_tiles(x_ref, rows)
        nc = 512
        chunks = range(D_EXPERT // nc)
        gcols = [slice(c * nc, (c + 1) * nc) for c in chunks]
        ucols = [slice(D_EXPERT + c * nc, D_EXPERT + (c + 1) * nc) for c in chunks]
        gates = [jnp.dot(xb, wgu_bf[:, gcols[c]], preferred_element_type=F32) + bgu_ref[:, gcols[c]] for c in chunks]
        ups = [jnp.dot(xb, wgu_bf[:, ucols[c]], preferred_element_type=F32) + bgu_ref[:, ucols[c]] for c in chunks]
        acts = []
        for gate, up in zip(gates, ups):
            gate = jnp.minimum(gate, SWIGLU_LIMIT)
            up = jnp.clip(up, -SWIGLU_LIMIT, SWIGLU_LIMIT)
            acts.append(((up + 1.0) * gate * (1.0 / (1.0 + jnp.exp(-SWIGLU_ALPHA * gate)))).astype(BF16))
        out = None
        for c in chunks:
            part = jnp.dot(acts[c], wdn_bf[gcols[c], :], preferred_element_type=F32)
            out = part if out is None else out + part
        _store_token_tiles(o_ref, out + bdn_ref[...], rows)

    @pl.when(b >= nact)
    def _():
        o_ref[...] = jnp.zeros_like(o_ref)


def _experts(block_e, n_active, block_run, run_e, xs_tiles, wgu, bgu, wdn, bdn, *, rows, layer):
    ts = TOKEN_SUBLANES
    nb = xs_tiles.shape[0] // (rows * PACKED_SUBLANES)
    grid_spec = pltpu.PrefetchScalarGridSpec(
        num_scalar_prefetch=4,
        grid=(nb,),
        in_specs=[
            pl.BlockSpec((rows * PACKED_SUBLANES, V7X_LANES), lambda b, be, na, *_: (jnp.minimum(b, na[0] - 1), 0)),
            pl.BlockSpec(memory_space=pl.ANY),
            pl.BlockSpec((None, None, 1, 2 * D_EXPERT), lambda b, be, *_: (layer, be[b], 0, 0)),
            pl.BlockSpec(memory_space=pl.ANY),
            pl.BlockSpec((None, None, 1, D_MODEL), lambda b, be, *_: (layer, be[b], 0, 0)),
        ],
        out_specs=pl.BlockSpec((rows * ts, V7X_LANES), lambda b, *_: (b, 0)),
        scratch_shapes=[pltpu.VMEM((2, D_MODEL, 2 * D_EXPERT), F32), pltpu.VMEM((2, D_EXPERT, D_MODEL), F32),
                        pltpu.SemaphoreType.DMA((2, 2)),
                        pltpu.VMEM((D_MODEL, 2 * D_EXPERT), BF16), pltpu.VMEM((D_EXPERT, D_MODEL), BF16)],
    )
    return pl.pallas_call(
        functools.partial(_expert_kernel, rows=rows, layer=layer),
        grid_spec=grid_spec,
        out_shape=jax.ShapeDtypeStruct((nb * rows * ts, V7X_LANES), F32),
        compiler_params=_cparams(),
        name="moe_experts",
    )(block_e, n_active, block_run, run_e, xs_tiles, wgu, bgu, wdn, bdn)


def _combine_kernel(dest_first_ref, dest_next_ref, x_ref, slab_ref, ebuf_ref, *refs, tm, n_tiles, attn_pre):
    if attn_pre:
        (kvg_ref, wk_ref, wv_ref, kng_ref, bng_ref, wq_ref, qng_ref, cos_ref, sin_ref,
         o_ref, k_ref, kdup_ref, v_ref, q_ref, rbuf, sem) = refs
    else:
        o_ref, rbuf, sem = refs
    i = pl.program_id(0)
    slot = lax.rem(i, 2)
    ts = TOKEN_SUBLANES

    def gather(dest_ref, s):
        def body(r, carry):
            for k in range(TOP_K):
                d = dest_ref[0, 0, k * tm + r]
                src = ebuf_ref.at[pl.ds(pl.multiple_of(d * ts, ts), ts)]
                dst = rbuf.at[s, k, pl.ds(pl.multiple_of(r * ts, ts), ts)]
                pltpu.make_async_copy(src, dst, sem.at[s, k % 2]).start(priority=k % 2)
            return carry
        lax.fori_loop(0, tm, body, 0, unroll=4)

    @pl.when(i == 0)
    def _():
        gather(dest_first_ref, 0)

    if attn_pre:
        todo = [(r, k) for r in range(tm) for k in range(TOP_K)]

        def tick(n):
            for r, k in todo[:n]:
                d = dest_next_ref[0, 0, k * tm + r]
                src = ebuf_ref.at[pl.ds(pl.multiple_of(d * ts, ts), ts)]
                pltpu.make_async_copy(src, rbuf.at[1 - slot, k, pl.ds(r * ts, ts)],
                                      sem.at[1 - slot, k % 2]).start(priority=k % 2)
            del todo[:n]
    else:
        @pl.when(i + 1 < n_tiles)
        def _():
            gather(dest_next_ref, 1 - slot)

        def tick(n):
            pass

    def wait_slot(s):
        for k in range(TOP_K):
            pltpu.make_async_copy(ebuf_ref.at[pl.ds(0, tm * ts)], rbuf.at[s, k], sem.at[s, k % 2]).wait()

    wait_slot(slot)
    slab = slab_ref[...]
    gates = [slab[:, 2 * TOP_K + k:2 * TOP_K + k + 1] for k in range(TOP_K)]
    per_phase = TOP_K * tm // (4 * ts)
    for s in range(ts):
        cols = slice(s * V7X_LANES, (s + 1) * V7X_LANES)
        acc = x_ref[:, cols]
        for k in range(TOP_K):
            acc = acc + gates[k] * rbuf[slot, k, pl.ds(s, tm, stride=ts), :]
        o_ref[:, cols] = acc
        tick(per_phase)
    if attn_pre:
        x2 = o_ref[...]
        cos2, sin2 = cos_ref[...], sin_ref[...]
        kv_phases = 1 + N_KV_HEADS
        q_phases = 1 + D_MODEL // PAIR
        per_phase = len(todo) // (kv_phases + q_phases)
        _shared_kv(x2, kvg_ref[...], wk_ref, wv_ref, kng_ref[...], cos2, sin2, k_ref, kdup_ref, v_ref,
                   tick=lambda: tick(per_phase))
        hq = _rms(x2, bng_ref[...]).astype(BF16)
        q = jnp.dot(hq, wq_ref[...], preferred_element_type=F32)
        tick(per_phase)
        for p in range(D_MODEL // PAIR):
            cols = slice(p * PAIR, (p + 1) * PAIR)
            q_ref[:, cols] = (_pair_norm_rope(q[:, cols], qng_ref[...], cos2, sin2) * HEAD_DIM ** -0.5).astype(BF16)
            tick(per_phase)
        tick(len(todo))

        @pl.when(i == n_tiles - 1)
        def _():
            wait_slot(1 - slot)


def _shared_kv(x, kvg, wk_ref, wv_ref, kng2, cos2, sin2, k_ref, kdup_ref, v_ref, tick=lambda: None):
    h = _rms(x, kvg).astype(BF16)
    kd = jnp.dot(h, wk_ref[...], preferred_element_type=F32)
    v_ref[...] = jnp.dot(h, wv_ref[...], preferred_element_type=F32)
    tick()
    first_half = lax.broadcasted_iota(I32, (x.shape[0], PAIR), 1) < HEAD_DIM
    slabs = []
    for p in range(N_KV_HEADS):
        cols = slice(p * PAIR, (p + 1) * PAIR)
        slab = _pair_norm_rope(kd[:, cols], kng2, cos2, sin2)
        kdup_ref[:, cols] = slab
        slabs.append(slab)
        tick()
    for p in range(D_KV // PAIR):
        k_ref[:, p * PAIR:(p + 1) * PAIR] = jnp.where(first_half, slabs[2 * p], slabs[2 * p + 1])


def _combine(x, slab, dest3, ebuf, *, tm, attn_pre=None):
    n = x.shape[0]
    n_tiles = n // tm
    in_specs = [
        pl.BlockSpec((1, 1, TOP_K * tm), lambda i: (0, 0, 0), memory_space=pltpu.SMEM),
        pl.BlockSpec((1, 1, TOP_K * tm), lambda i: (jnp.minimum(i + 1, n_tiles - 1), 0, 0),
                     memory_space=pltpu.SMEM),
        _rows(tm, D_MODEL), _rows(tm, SLAB), pl.BlockSpec(memory_space=pl.ANY),
    ]
    args = [dest3, dest3, x, slab, ebuf]
    out_specs = [_rows(tm, D_MODEL)]
    out_shape = [jax.ShapeDtypeStruct((n, D_MODEL), F32)]
    if attn_pre is not None:
        *weights, cos2, sin2, table_tiles = attn_pre
        table_spec = pl.BlockSpec((tm, PAIR), lambda i: (lax.rem(i, table_tiles), 0))
        in_specs += [_resident(w.shape) for w in weights] + [table_spec, table_spec]
        args += list(weights) + [cos2, sin2]
        out_specs += [_rows(tm, D_KV), _rows(tm, 2 * D_KV), _rows(tm, D_KV), _rows(tm, D_MODEL)]
        out_shape += [jax.ShapeDtypeStruct((n, D_KV), F32), jax.ShapeDtypeStruct((n, 2 * D_KV), F32),
                      jax.ShapeDtypeStruct((n, D_KV), F32), jax.ShapeDtypeStruct((n, D_MODEL), BF16)]
    return pl.pallas_call(
        functools.partial(_combine_kernel, tm=tm, n_tiles=n_tiles, attn_pre=attn_pre is not None),
        grid=(n_tiles,),
        in_specs=in_specs, out_specs=out_specs, out_shape=out_shape,
        scratch_shapes=[pltpu.VMEM((2, TOP_K, tm * TOKEN_SUBLANES, V7X_LANES), F32),
                        pltpu.SemaphoreType.DMA((2, 2))],
        compiler_params=_cparams(),
        name="moe_combine_attn_pre" if attn_pre is not None else "moe_combine",
    )(*args)


def _moe(xs, g, w_router, b_router, wgu, bgu, wdn, bdn, layer, prompt_attn_pre=None):
    n_all = xs[0].shape[0] + xs[1].shape[0]
    w_t = w_router.T
    whi = w_t.astype(BF16)
    wlo = (w_t - whi.astype(F32)).astype(BF16)
    h_all, slab_p, slab_s, rec, cnt = _router(xs[0], xs[1], g.reshape(1, D_MODEL), whi, wlo,
                                              b_router.reshape(N_EXPERTS, 1), tm=TM_DENSE)
    slabs = [slab_p, slab_s]
    idx = rec[:TOP_K, :n_all].astype(I32)
    pos = rec[TOP_K:, :n_all].astype(I32)
    counts = cnt[:, 0].astype(I32)
    rows = MOE_ROWS
    nb = (n_all * TOP_K + N_EXPERTS * (rows - 1)) // rows
    padded = (counts + rows - 1) // rows * rows
    pad_end = jnp.cumsum(padded)
    pad_start = pad_end - padded
    eid = jnp.arange(N_EXPERTS, dtype=I32)
    has_rows = counts > 0
    dest = pos + jnp.sum(jnp.where(idx[None] == eid[:, None, None], pad_start[:, None, None], 0), axis=0)
    n_active = (pad_end[-1] // rows).astype(I32)
    blk = jnp.arange(nb, dtype=I32)
    block_e = jnp.sum((pad_end[None, :] <= (blk * rows)[:, None]).astype(I32), axis=1)
    last_e = jnp.max(jnp.where(has_rows, eid, 0))
    block_e = jnp.where(blk < n_active, jnp.minimum(block_e, N_EXPERTS - 1), last_e)

    def tile_table(d, tm):
        n_tiles = -(-d.shape[1] // tm)
        d = jnp.pad(d, ((0, 0), (0, n_tiles * tm - d.shape[1])))
        return d.reshape(TOP_K, n_tiles, tm).transpose(1, 0, 2).reshape(n_tiles, 1, TOP_K * tm)

    fill_start = jnp.concatenate([pad_start + counts, n_active.reshape(1)])
    xs_tiles = _dispatch(fill_start, padded - counts, tile_table(dest, TM_DISPATCH), h_all,
                         n_all, nb, tm=TM_DISPATCH, rows=rows)
    rank = jnp.cumsum(has_rows.astype(I32)) - 1
    run_e = jnp.sum(jnp.where(has_rows[None, :] & (rank[None, :] == eid[:, None]), eid[None, :], 0), axis=1)
    run_e = jnp.concatenate([run_e, jnp.sum(has_rows.astype(I32)).reshape(1)]).astype(I32)
    block_run = jnp.sum(jnp.where(block_e[:, None] == eid, rank, 0), axis=1).astype(I32)
    n_l = wgu.shape[0]
    ebuf = _experts(block_e, n_active.reshape(1), block_run, run_e, xs_tiles,
                    wgu, bgu.reshape(n_l, N_EXPERTS, 1, 2 * D_EXPERT), wdn, bdn.reshape(n_l, N_EXPERTS, 1, D_MODEL),
                    rows=rows, layer=layer)
    outs, off = [], 0
    for x, slab, pre in zip(xs, slabs, (prompt_attn_pre, None)):
        n = x.shape[0]
        tm = min(TM_COMBINE_FUSED if pre is not None else TM_COMBINE, n)
        outs.append(_combine(x, slab, tile_table(dest[:, off:off + n], tm), ebuf, tm=tm, attn_pre=pre))
        off += n
    return outs


def _kv_kernel(x_ref, kvg_ref, wk_ref, wv_ref, kng_ref, cos_ref, sin_ref, *refs, with_q):
    if with_q:
        bng_ref, wq_ref, qng_ref, k_ref, kdup_ref, v_ref, q_ref = refs
    else:
        k_ref, kdup_ref, v_ref = refs
    x = x_ref[...]
    cos2, sin2 = cos_ref[...], sin_ref[...]
    _shared_kv(x, kvg_ref[...], wk_ref, wv_ref, kng_ref[...], cos2, sin2, k_ref, kdup_ref, v_ref)
    if with_q:
        hq = _rms(x, bng_ref[...]).astype(BF16)
        q = jnp.dot(hq, wq_ref[...], preferred_element_type=F32)
        for p in range(D_MODEL // PAIR):
            cols = slice(p * PAIR, (p + 1) * PAIR)
            q_ref[:, cols] = _pair_norm_rope(q[:, cols], qng_ref[...], cos2, sin2)


def _kv_proj(x, kvg, wk, wv, kng2, cos2, sin2, q_args, *, tm, table_tiles):
    n = x.shape[0]
    with_q = q_args is not None
    table_spec = pl.BlockSpec((tm, PAIR), lambda i: (lax.rem(i, table_tiles), 0))
    in_specs = [_rows(tm, D_MODEL), _resident(kvg.shape), _resident(wk.shape), _resident(wv.shape),
                _resident(kng2.shape), table_spec, table_spec]
    args = [x, kvg, wk, wv, kng2, cos2, sin2]
    out_specs = [_rows(tm, D_KV), _rows(tm, 2 * D_KV), _rows(tm, D_KV)]
    out_shape = [jax.ShapeDtypeStruct((n, D_KV), F32), jax.ShapeDtypeStruct((n, 2 * D_KV), F32),
                 jax.ShapeDtypeStruct((n, D_KV), F32)]
    if with_q:
        in_specs += [_resident(a.shape) for a in q_args]
        args += list(q_args)
        out_specs.append(_rows(tm, D_MODEL))
        out_shape.append(jax.ShapeDtypeStruct((n, D_MODEL), F32))
    return pl.pallas_call(
        functools.partial(_kv_kernel, with_q=with_q),
        grid=(n // tm,),
        in_specs=in_specs, out_specs=out_specs, out_shape=out_shape,
        compiler_params=_cparams(),
        name="kvq_proj_sample" if with_q else "kv_proj_prompt",
    )(*args)


def _attn_prompt_kernel(sinks_ref, x_ref, q_ref, kc_ref, kp_ref, vc_ref, vp_ref, wo_ref, o_ref, a_scr,
                        *, tm, tiles_per_seq):
    i = pl.program_id(0)

    kj = lax.broadcasted_iota(I32, (2 * WINDOW, WINDOW), 0)
    qi = lax.broadcasted_iota(I32, (2 * WINDOW, WINDOW), 1)
    diff = qi + WINDOW - kj
    band = jnp.logical_and(diff >= 0, diff < WINDOW)
    first_half = lax.broadcasted_iota(I32, (WINDOW, PAIR), 1) < HEAD_DIM
    keep = (jnp.where(first_half, 1.0, 0.0).astype(BF16), jnp.where(first_half, 0.0, 1.0).astype(BF16))
    first_kj = jnp.where(lax.rem(i, tiles_per_seq) == 0, WINDOW, 0)

    for blk in range(tm // WINDOW):
        rows = slice(blk * WINDOW, (blk + 1) * WINDOW)
        if blk == 0:
            kk = jnp.concatenate([kp_ref[...], kc_ref[rows, :]], axis=0).astype(BF16)
            vv = jnp.concatenate([vp_ref[...], vc_ref[rows, :]], axis=0)
            valid = jnp.logical_and(band, kj >= first_kj)
        else:
            win = slice((blk - 1) * WINDOW, (blk + 1) * WINDOW)
            kk = kc_ref[win, :].astype(BF16)
            vv = vc_ref[win, :]
            valid = band
        for kvh in range(N_KV_HEADS):
            kk_h = kk[:, kvh * PAIR:(kvh + 1) * PAIR]
            vcols = slice((kvh // 2) * PAIR, (kvh // 2 + 1) * PAIR)
            vt = vv[:, vcols].T.astype(BF16)
            ch = slice((kvh % 2) * HEAD_DIM, (kvh % 2 + 1) * HEAD_DIM)
            outs = []
            for g in range(Q_PER_KV):
                head = kvh * Q_PER_KV + g
                qm = q_ref[rows, (head // 2) * PAIR:(head // 2 + 1) * PAIR] * keep[head % 2]
                sink = sinks_ref[head]
                st = lax.dot_general(kk_h, qm, (((1,), (1,)), ((), ())), preferred_element_type=F32)
                st = jnp.where(valid, st, -jnp.inf)
                m = jnp.maximum(jnp.max(st, axis=0, keepdims=True), sink)
                p = jnp.exp(st - m)
                den = jnp.sum(p, axis=0, keepdims=True) + jnp.exp(sink - m)
                ot = jnp.dot(vt, p.astype(BF16), preferred_element_type=F32)
                outs.append(ot[ch, :] * (1.0 / den))
            for pq in range(Q_PER_KV // 2):
                pair = (kvh * Q_PER_KV) // 2 + pq
                both = jnp.concatenate([outs[2 * pq], outs[2 * pq + 1]], axis=0)
                a_scr[rows, pair * PAIR:(pair + 1) * PAIR] = both.T.astype(BF16)
    o_ref[...] = x_ref[...] + jnp.dot(a_scr[...], wo_ref[...], preferred_element_type=F32)


def _attn_prompt(x, q, kdup, v, sinks, wo, *, tm, seq):
    n = x.shape[0]
    tiles_per_seq = seq // tm
    blocks_per_tile = tm // WINDOW

    def prev_spec(width):
        return pl.BlockSpec((WINDOW, width), lambda i: (jnp.maximum(i * blocks_per_tile - 1, 0), 0))

    return pl.pallas_call(
        functools.partial(_attn_prompt_kernel, tm=tm, tiles_per_seq=tiles_per_seq),
        grid=(n // tm,),
        in_specs=[pl.BlockSpec(memory_space=pltpu.SMEM), _rows(tm, D_MODEL), _rows(tm, D_MODEL),
                  _rows(tm, 2 * D_KV), prev_spec(2 * D_KV), _rows(tm, D_KV), prev_spec(D_KV), _resident(wo.shape)],
        out_specs=_rows(tm, D_MODEL),
        out_shape=jax.ShapeDtypeStruct((n, D_MODEL), F32),
        scratch_shapes=[pltpu.VMEM((tm, D_MODEL), BF16)],
        compiler_params=_cparams(),
        name="attn_prompt",
    )(sinks, x, q, kdup, kdup, v, v, wo)


def _attn_sample_kernel(q_ref, ck_ref, cv_ref, kn_ref, vn_ref, sink_ref, o_ref, *, per_step, win):
    rowi = lax.broadcasted_iota(I32, (N_HEADS, D_KV), 0)
    lanei = lax.broadcasted_iota(I32, (N_HEADS, D_KV), 1)
    group = rowi >> 2
    own_block = group == (lanei >> 6)
    j = lax.broadcasted_iota(I32, (N_HEADS, win), 1)
    diff = win - j
    valid = jnp.logical_and(diff >= 0, diff < WINDOW)
    sink = sink_ref[...]
    scale = HEAD_DIM ** -0.5
    qms = [jnp.where(own_block, q_ref[b], 0.0) for b in range(per_step)]
    ss = [lax.dot_general(qm.astype(BF16), ck_ref[b].astype(BF16), (((1,), (1,)), ((), ())),
                          preferred_element_type=F32) * scale for b, qm in enumerate(qms)]
    ps, p_news, dens = [], [], []
    for b in range(per_step):
        s = jnp.where(valid, ss[b], -jnp.inf)
        s_new = jnp.sum(qms[b] * kn_ref[b:b + 1, :], axis=-1, keepdims=True) * scale
        m = jnp.maximum(jnp.maximum(jnp.max(s, axis=-1, keepdims=True), s_new), sink)
        p = jnp.exp(s - m)
        p_new = jnp.exp(s_new - m)
        ps.append(p.astype(BF16))
        p_news.append(p_new)
        dens.append(jnp.sum(p, axis=-1, keepdims=True) + p_new + jnp.exp(sink - m))
    os_ = [jnp.dot(ps[b], cv_ref[b].astype(BF16), preferred_element_type=F32) for b in range(per_step)]
    for b in range(per_step):
        o = (os_[b] + p_news[b] * vn_ref[b:b + 1, :]) * (1.0 / dens[b])
        res = o
        for sft in range(1, N_KV_HEADS):
            res = jnp.where(group == sft, pltpu.roll(o, D_KV - sft * HEAD_DIM, 1), res)
        o_ref[b] = res[:, :HEAD_DIM]


def _attn_sample(q4, ck, cv, kn, vn, sink_col, *, per_step):
    nb, win = ck.shape[0], ck.shape[1]
    return pl.pallas_call(
        functools.partial(_attn_sample_kernel, per_step=per_step, win=win),
        grid=(nb // per_step,),
        in_specs=[pl.BlockSpec((per_step, N_HEADS, D_KV), lambda i: (i, 0, 0)),
                  pl.BlockSpec((per_step, win, D_KV), lambda i: (i, 0, 0)),
                  pl.BlockSpec((per_step, win, D_KV), lambda i: (i, 0, 0)),
                  _rows(per_step, D_KV), _rows(per_step, D_KV), _resident(sink_col.shape)],
        out_specs=pl.BlockSpec((per_step, N_HEADS, HEAD_DIM), lambda i: (i, 0, 0)),
        out_shape=jax.ShapeDtypeStruct((nb, N_HEADS, HEAD_DIM), F32),
        compiler_params=_cparams(),
        name="attn_sample",
    )(q4, ck, cv, kn, vn, sink_col)


def _residual_proj_kernel(x_ref, a_ref, w_ref, o_ref):
    o_ref[...] = x_ref[...] + jnp.dot(a_ref[...].astype(BF16), w_ref[...], preferred_element_type=F32)


def _residual_proj(x, a, w):
    n = x.shape[0]
    return pl.pallas_call(
        _residual_proj_kernel,
        grid=(1,),
        in_specs=[_rows(n, D_MODEL), _rows(n, a.shape[1]), _resident(w.shape)],
        out_specs=_rows(n, D_MODEL),
        out_shape=jax.ShapeDtypeStruct((n, D_MODEL), F32),
        compiler_params=_cparams(),
        name="attn_out_sample",
    )(x, a, w)


def _rope_tables(pos):
    half = ROPE_DIM // 2
    inv_freq = jnp.power(jnp.float32(ROPE_THETA), -jnp.arange(half, dtype=F32) / half)
    ang = pos.astype(F32)[:, None] * inv_freq[None, :]
    cos, sin = jnp.cos(ang), jnp.sin(ang)
    n = pos.shape[0]
    rest = HEAD_DIM - ROPE_DIM
    cos_h = jnp.concatenate([cos, cos, jnp.ones((n, rest), F32)], axis=1)
    sin_h = jnp.concatenate([-sin, sin, jnp.zeros((n, rest), F32)], axis=1)
    return jnp.tile(cos_h, (1, 2)), jnp.tile(sin_h, (1, 2))


def kernel(x_prompt, x_sample, cache_k, cache_v, a_norm_g, a_w_in, a_ln_g, a_ln_b, a_w_s, a_b_s, a_w_out, kv_norm_g, w_k, w_v, k_norm_g, b_norm_g, w_q, q_norm_g, sinks, w_o, ffn_norm_g, w_router, b_router, w_gu, b_gu, w_down, b_down):
    bsz, seq, _ = x_prompt.shape
    dec_b, dec_seq, _ = x_sample.shape
    assert dec_seq == 1 and seq % TM_DENSE == 0 and a_norm_g.shape[0] == 1 and b_norm_g.shape[0] == 1
    win = cache_k.shape[1]
    n_p, n_s = bsz * seq, dec_b * dec_seq
    xp = x_prompt.reshape(n_p, D_MODEL)
    xs = x_sample.reshape(n_s, D_MODEL)

    ng = a_norm_g[0].reshape(1, D_MODEL)
    win_bf = a_w_in[0].astype(BF16)
    wout_bf = a_w_out[0].astype(BF16)
    lng = a_ln_g[0].reshape(1, D_GATE)
    lnb = a_ln_b[0].reshape(1, D_GATE)
    bias_full = jnp.repeat(a_b_s[0].T, GROUP_DIM_A, axis=1)
    diag_row = jnp.repeat(a_w_s[0][:, 0, 0], GROUP_DIM_A).reshape(1, D_GATE)
    (xp,) = _mixer_a(xp, ng, win_bf, lng, lnb, a_w_s[0], bias_full, wout_bf, tm=TM_DENSE, single_token_chunks=False)
    xs, v_rows = _mixer_a(xs, ng, win_bf, lng, lnb, diag_row, bias_full[0:1], wout_bf, tm=n_s,
                          single_token_chunks=True)

    kvg = kv_norm_g.reshape(1, D_MODEL)
    bng = b_norm_g[0].reshape(1, D_MODEL)
    w_k_dup = jnp.repeat(w_k.reshape(D_MODEL, N_KV_HEADS, 1, HEAD_DIM), 2, axis=2).reshape(D_MODEL, 2 * D_KV)
    wk_bf, wv_bf, wq_bf, wo_bf = (w.astype(BF16) for w in (w_k_dup, w_v, w_q[0], w_o[0]))
    kng2 = jnp.tile(k_norm_g, 2).reshape(1, PAIR)
    qng2 = jnp.tile(q_norm_g[0], 2).reshape(1, PAIR)
    cos_p, sin_p = _rope_tables(jnp.arange(seq, dtype=I32))
    cos_s, sin_s = _rope_tables(jnp.full((n_s,), PAST_LEN, I32))
    attn_pre = (kvg, wk_bf, wv_bf, kng2, bng, wq_bf, qng2, cos_p, sin_p, seq // TM_COMBINE_FUSED)
    (xp, k_p, kdup_p, v_p, q_p), (xs,) = _moe([xp, xs], ffn_norm_g[0], w_router[0], b_router[0], w_gu, b_gu,
                                              w_down, b_down, 0, prompt_attn_pre=attn_pre)

    k_s, _, v_s, q_s = _kv_proj(xs, kvg, wk_bf, wv_bf, kng2, cos_s, sin_s, (bng, wq_bf, qng2), tm=n_s,
                                table_tiles=1)
    xp = _attn_prompt(xp, q_p, kdup_p, v_p, sinks[0], wo_bf, tm=TM_DENSE, seq=seq)
    q4 = jnp.tile(q_s.reshape(n_s, N_HEADS, HEAD_DIM), (1, 1, N_KV_HEADS))
    attn_s = _attn_sample(q4, cache_k.reshape(dec_b, win, D_KV), cache_v.reshape(dec_b, win, D_KV), k_s, v_s,
                          sinks[0].reshape(N_HEADS, 1), per_step=8)
    xs = _residual_proj(xs, attn_s.reshape(n_s, D_MODEL), wo_bf)
    (xp,), (xs,) = _moe([xp, xs], ffn_norm_g[1], w_router[1], b_router[1], w_gu, b_gu, w_down, b_down, 1)

    def last_window(a):
        a = a.reshape(bsz, seq, D_KV)[:, -WINDOW:]
        return a.reshape(bsz, min(WINDOW, seq), N_KV_HEADS, HEAD_DIM)

    return (xp.reshape(bsz, seq, D_MODEL), xs.reshape(dec_b, dec_seq, D_MODEL),
            v_rows.reshape(1, dec_b, dec_seq, D_GATE), last_window(k_p), last_window(v_p),
            k_s.reshape(dec_b, dec_seq, N_KV_HEADS, HEAD_DIM), v_s.reshape(dec_b, dec_seq, N_KV_HEADS, HEAD_DIM))
```

```python
import functools

import jax
import jax.numpy as jnp
import numpy as np
from jax import lax
from jax.experimental import pallas as pl
from jax.experimental.pallas import tpu as pltpu

F32, BF16, I32 = jnp.float32, jnp.bfloat16, jnp.int32

D_MODEL = 1024
CHUNK = 128
D_GATE = 2 * D_MODEL
N_GROUPS_A = 8
GROUP_DIM_A = D_GATE // N_GROUPS_A
HEAD_DIM = 64
N_HEADS = D_MODEL // HEAD_DIM
N_KV_HEADS = N_HEADS // 4
Q_PER_KV = N_HEADS // N_KV_HEADS
D_KV = N_KV_HEADS * HEAD_DIM
WINDOW = 128
ROPE_DIM = HEAD_DIM // 4
ROPE_THETA = 500000.0
N_EXPERTS = 32
TOP_K = 4
D_EXPERT = D_MODEL
SWIGLU_LIMIT = 7.0
SWIGLU_ALPHA = 1.702
EPS = 1e-6
PAST_LEN = 8192

V7X_LANES = 128
V7X_VMEM_BYTES = 64 * 1024 * 1024
VMEM_LIMIT_BYTES = V7X_VMEM_BYTES - 8 * 1024 * 1024

TM_DENSE = 512
TM_COMBINE = 256
TM_COMBINE_FUSED = 256
MOE_ROWS = 512
TM_DISPATCH = 256
FILL_BITS = MOE_ROWS.bit_length() - 1
TOKEN_SUBLANES = D_MODEL // V7X_LANES
PACKED_SUBLANES = TOKEN_SUBLANES // 2
U32 = jnp.uint32
SLAB = V7X_LANES
REC_ROWS = 16
PAIR = 2 * HEAD_DIM
INV_SQRT2 = 0.7071067811865476


def _cparams():
    return pltpu.CompilerParams(dimension_semantics=("arbitrary",), vmem_limit_bytes=VMEM_LIMIT_BYTES)


def _resident(shape):
    zeros = (0,) * len(shape)
    return pl.BlockSpec(shape, lambda i, *_: zeros, pipeline_mode=pl.Buffered(1))


def _rows(tm, width, offset=0):
    return pl.BlockSpec((tm, width), lambda i, *_: (i + offset, 0))


def _rms(x, g):
    return x * lax.rsqrt(jnp.mean(x * x, axis=-1, keepdims=True) + EPS) * g


def _pair_norm_rope(x2, g2, cos2, sin2):
    lane = lax.broadcasted_iota(I32, x2.shape, 1)
    r = lax.broadcasted_iota(I32, (PAIR, PAIR), 0)
    c = lax.broadcasted_iota(I32, (PAIR, PAIR), 1)
    head_mean = jnp.where((r < HEAD_DIM) == (c < HEAD_DIM), 1.0 / HEAD_DIM, 0.0).astype(BF16)
    sq = x2 * x2
    sq_hi = sq.astype(BF16)
    sq_lo = (sq - sq_hi.astype(F32)).astype(BF16)
    ms = (jnp.dot(sq_hi, head_mean, preferred_element_type=F32)
          + jnp.dot(sq_lo, head_mean, preferred_element_type=F32))
    y = x2 * lax.rsqrt(ms + EPS) * g2
    half = ROPE_DIM // 2
    up = pltpu.roll(y, PAIR - half, 1)
    dn = pltpu.roll(y, half, 1)
    partner = jnp.where((lane & (HEAD_DIM - 1)) < half, up, dn)
    return y * cos2 + partner * sin2


def _mixer_a_kernel(x_ref, ng_ref, win_ref, lng_ref, lnb_ref, mix_ref, bias_ref, wout_ref, *refs,
                    tm, single_token_chunks):
    if single_token_chunks:
        o_ref, vout_ref, u_scr, v_scr, p_scr = refs
    else:
        o_ref, u_scr, v_scr, p_scr = refs
    x = x_ref[...]
    h = _rms(x, ng_ref[...]).astype(BF16)
    nc = 512
    for c in range(2 * D_GATE // nc):
        z = jnp.dot(h, win_ref[:, c * nc:(c + 1) * nc], preferred_element_type=F32)
        z = 0.5 * z * (1.0 + lax.erf(z * INV_SQRT2))
        if c < D_GATE // nc:
            u_scr[:, c * nc:(c + 1) * nc] = z
        else:
            v_scr[:, (c - D_GATE // nc) * nc:(c - D_GATE // nc + 1) * nc] = z
    v = v_scr[...]
    vc = v - jnp.mean(v, axis=-1, keepdims=True)
    vn = vc * lax.rsqrt(jnp.mean(vc * vc, axis=-1, keepdims=True) + EPS) * lng_ref[...] + lnb_ref[...]
    if single_token_chunks:
        vout_ref[...] = vn
        p_scr[...] = (u_scr[...] * (vn * mix_ref[...] + bias_ref[...])).astype(BF16)
    else:
        v_scr[...] = vn
        row = lax.broadcasted_iota(I32, (CHUNK, CHUNK), 0)
        col = lax.broadcasted_iota(I32, (CHUNK, CHUNK), 1)
        for g in range(N_GROUPS_A):
            wc = jnp.where(row >= col, mix_ref[g], 0.0).astype(BF16)
            cols = slice(g * GROUP_DIM_A, (g + 1) * GROUP_DIM_A)
            for c in range(tm // CHUNK):
                rows = slice(c * CHUNK, (c + 1) * CHUNK)
                mixed = jnp.dot(wc, v_scr[rows, cols].astype(BF16), preferred_element_type=F32)
                mixed = mixed + bias_ref[:, cols]
                p_scr[rows, cols] = (u_scr[rows, cols] * mixed).astype(BF16)
    o_ref[...] = x + jnp.dot(p_scr[...], wout_ref[...], preferred_element_type=F32)


def _mixer_a(x, ng, win, lng, lnb, mix, bias, wout, *, tm, single_token_chunks):
    n = x.shape[0]
    out_shape = [jax.ShapeDtypeStruct((n, D_MODEL), F32)]
    out_specs = [_rows(tm, D_MODEL)]
    if single_token_chunks:
        out_shape.append(jax.ShapeDtypeStruct((n, D_GATE), F32))
        out_specs.append(_rows(tm, D_GATE))
    return pl.pallas_call(
        functools.partial(_mixer_a_kernel, tm=tm, single_token_chunks=single_token_chunks),
        grid=(n // tm,),
        in_specs=[_rows(tm, D_MODEL), _resident(ng.shape), _resident(win.shape), _resident(lng.shape),
                  _resident(lnb.shape), _resident(mix.shape), _resident(bias.shape), _resident(wout.shape)],
        out_specs=out_specs,
        out_shape=out_shape,
        scratch_shapes=[pltpu.VMEM((tm, D_GATE), F32), pltpu.VMEM((tm, D_GATE), F32),
                        pltpu.VMEM((tm, D_GATE), BF16)],
        compiler_params=_cparams(),
        name="mixer_a_sample" if single_token_chunks else "mixer_a_prompt",
    )(x, ng, win, lng, lnb, mix, bias, wout)


def _route_rows(x, g, whi_t, wlo_t, bias_col, run_scr):
    tm = x.shape[0]
    h = _rms(x, g)
    hh = h.astype(BF16)
    hl = (h - hh.astype(F32)).astype(BF16)
    nt = (((1,), (1,)), ((), ()))
    logits = (lax.dot_general(whi_t, hh, nt, preferred_element_type=F32)
              + lax.dot_general(wlo_t, hh, nt, preferred_element_type=F32)
              + lax.dot_general(whi_t, hl, nt, preferred_element_type=F32) + bias_col)
    e = lax.broadcasted_iota(I32, (N_EXPERTS, tm), 0).astype(F32)
    hot, vals, idxs = [], [], []
    for _ in range(TOP_K):
        m = jnp.max(logits, axis=0, keepdims=True)
        idx = jnp.min(jnp.where(logits == m, e, float(N_EXPERTS)), axis=0, keepdims=True)
        oh = e == idx
        hot.append(oh)
        vals.append(m)
        idxs.append(idx)
        logits = jnp.where(oh, -jnp.inf, logits)
    ex = [jnp.exp(v - vals[0]) for v in vals]
    den = ex[0] + ex[1] + ex[2] + ex[3]
    picked = jnp.zeros((N_EXPERTS, tm), F32)
    for oh in hot:
        picked = picked + jnp.where(oh, 1.0, 0.0)
    r = lax.broadcasted_iota(I32, (tm, tm), 0)
    c = lax.broadcasted_iota(I32, (tm, tm), 1)
    earlier = jnp.where(r < c, 1.0, 0.0).astype(BF16)
    run = run_scr[:, 0:1]
    pos = jnp.dot(picked.astype(BF16), earlier, preferred_element_type=F32) + run
    run_scr[:, 0:1] = run + jnp.sum(picked, axis=1, keepdims=True)
    row = lax.broadcasted_iota(I32, (REC_ROWS, tm), 0)
    rec = jnp.zeros((REC_ROWS, tm), F32)
    for k in range(TOP_K):
        pos_k = jnp.sum(jnp.where(hot[k], pos, 0.0), axis=0, keepdims=True)
        rec = jnp.where(row == k, idxs[k], rec)
        rec = jnp.where(row == TOP_K + k, pos_k, rec)
        rec = jnp.where(row == 2 * TOP_K + k, ex[k] / den, rec)
    return h, rec


def _router_kernel(xp_ref, xs_ref, g_ref, whi_ref, wlo_ref, b_ref, h_ref, slabp_ref, slabs_ref, rec_ref, cnt_ref,
                   run_scr, *, n_tiles, n_s):
    i = pl.program_id(0)

    @pl.when(i == 0)
    def _():
        run_scr[...] = jnp.zeros_like(run_scr)

    def pad_rows(rec):
        return jnp.concatenate([rec, jnp.zeros((SLAB - REC_ROWS, rec.shape[1]), F32)], axis=0)

    @pl.when(i < n_tiles)
    def _():
        h, rec = _route_rows(xp_ref[...], g_ref[...], whi_ref[...], wlo_ref[...], b_ref[...], run_scr)
        _store_packed_tiles(h_ref, h, h.shape[0])
        slabp_ref[...] = pad_rows(rec).T
        rec_ref[...] = rec[0:2 * TOP_K, :]

    @pl.when(i == n_tiles)
    def _():
        h, rec = _route_rows(xs_ref[...], g_ref[...], whi_ref[...], wlo_ref[...], b_ref[...], run_scr)
        h_ref[n_s * PACKED_SUBLANES:, :] = jnp.zeros((h_ref.shape[0] - n_s * PACKED_SUBLANES, V7X_LANES), U32)
        _store_packed_tiles(h_ref, h, n_s)
        slabs_ref[...] = pad_rows(rec).T
        rec_ref[...] = jnp.zeros_like(rec_ref)
        rec_ref[:, 0:n_s] = rec[0:2 * TOP_K, :]

    cnt_ref[...] = run_scr[...]


def _store_packed_tiles(ref, x, n):
    half = D_MODEL // 2
    lo = pltpu.bitcast(x[:, :half].astype(BF16).astype(F32), U32)
    hi = pltpu.bitcast(x[:, half:].astype(BF16).astype(F32), U32)
    word = lax.shift_right_logical(lo, jnp.uint32(16)) | (hi & jnp.uint32(0xFFFF0000))
    for s in range(PACKED_SUBLANES):
        ref[pl.ds(s, n, stride=PACKED_SUBLANES), :] = word[:, s * V7X_LANES:(s + 1) * V7X_LANES]


def _load_packed_tiles(ref, n):
    words = [ref[pl.ds(s, n, stride=PACKED_SUBLANES), :] for s in range(PACKED_SUBLANES)]
    lo = [pltpu.bitcast(w << jnp.uint32(16), F32) for w in words]
    hi = [pltpu.bitcast(w & jnp.uint32(0xFFFF0000), F32) for w in words]
    return jnp.concatenate(lo + hi, axis=1).astype(BF16)


def _store_token_tiles(ref, x, n):
    for s in range(TOKEN_SUBLANES):
        ref[pl.ds(s, n, stride=TOKEN_SUBLANES), :] = x[:, s * V7X_LANES:(s + 1) * V7X_LANES]


def _load_token_tiles(ref, n):
    return jnp.concatenate([ref[pl.ds(s, n, stride=TOKEN_SUBLANES), :] for s in range(TOKEN_SUBLANES)], axis=1)


def _router(xp, xs, g, whi, wlo, b, *, tm):
    n_p, n_s = xp.shape[0], xs.shape[0]
    n_tiles = n_p // tm
    last = n_tiles - 1
    return pl.pallas_call(
        functools.partial(_router_kernel, n_tiles=n_tiles, n_s=n_s),
        grid=(n_tiles + 1,),
        in_specs=[pl.BlockSpec((tm, D_MODEL), lambda i: (jnp.minimum(i, last), 0)), _resident(xs.shape),
                  _resident(g.shape), _resident(whi.shape), _resident(wlo.shape), _resident(b.shape)],
        out_specs=[_rows(tm * PACKED_SUBLANES, V7X_LANES),
                   pl.BlockSpec((tm, SLAB), lambda i: (jnp.minimum(i, last), 0)),
                   pl.BlockSpec((n_s, SLAB), lambda i: (0, 0)),
                   pl.BlockSpec((2 * TOP_K, tm), lambda i: (0, i)),
                   pl.BlockSpec((N_EXPERTS, V7X_LANES), lambda i: (0, 0))],
        out_shape=[jax.ShapeDtypeStruct(((n_p + tm) * PACKED_SUBLANES, V7X_LANES), U32),
                   jax.ShapeDtypeStruct((n_p, SLAB), F32),
                   jax.ShapeDtypeStruct((n_s, SLAB), F32),
                   jax.ShapeDtypeStruct((2 * TOP_K, n_p + tm), F32),
                   jax.ShapeDtypeStruct((N_EXPERTS, V7X_LANES), F32)],
        scratch_shapes=[pltpu.VMEM((N_EXPERTS, V7X_LANES), F32)],
        compiler_params=_cparams(),
        name="moe_router",
    )(xp, xs, g, whi, wlo, b)


def _dispatch_kernel(fill_start_ref, fill_len_ref, dest_ref, h_ref, xs_ref, zbuf, sem, fill_sem,
                     *, tm, n_tiles, n_last, rows, n_blocks):
    i = pl.program_id(0)
    ts = PACKED_SUBLANES

    def tile_copy(r, k):
        d = dest_ref[0, 0, k * tm + r]
        src = h_ref.at[pl.ds(pl.multiple_of(r * ts, ts), ts)]
        dst = xs_ref.at[pl.ds(pl.multiple_of(d * ts, ts), ts)]
        return pltpu.make_async_copy(src, dst, sem.at[k % 2])

    def scatter(n):
        def body(r, carry):
            for k in range(TOP_K):
                tile_copy(r, k).start(priority=k % 2)
            return carry
        lax.fori_loop(0, n, body, 0, unroll=4)
        for k in range(TOP_K):
            pltpu.make_async_copy(h_ref.at[pl.ds(0, n * ts)], xs_ref.at[pl.ds(0, n * ts)], sem.at[k % 2]).wait()

    def fill_copy(e, bit):
        size = 1 << bit
        length = fill_len_ref[e]
        done = length - (length & (2 * size - 1))
        off = pl.multiple_of((fill_start_ref[e] + done) * ts, ts)
        return (length & size) != 0, pltpu.make_async_copy(zbuf.at[pl.ds(0, size * ts)],
                                                            xs_ref.at[pl.ds(off, size * ts)], fill_sem)

    def tail_copy(blk):
        off = pl.multiple_of(blk * (rows * ts), rows * ts)
        return pltpu.make_async_copy(zbuf, xs_ref.at[pl.ds(off, rows * ts)], fill_sem)

    def for_each_fill(act):
        for e in range(N_EXPERTS):
            for bit in range(FILL_BITS):
                go, cp = fill_copy(e, bit)

                @pl.when(go)
                def _():
                    act(cp)

        def body(blk, carry):
            act(tail_copy(blk))
            return carry
        lax.fori_loop(fill_start_ref[N_EXPERTS], n_blocks, body, 0)

    @pl.when(i == 0)
    def _():
        zbuf[...] = jnp.zeros_like(zbuf)
        for_each_fill(lambda cp: cp.start())

    @pl.when(i < n_tiles - 1)
    def _():
        scatter(tm)

    @pl.when(i == n_tiles - 1)
    def _():
        scatter(n_last)
        for_each_fill(lambda cp: cp.wait())


def _dispatch(fill_start, fill_len, dest3, h_tiles, n_all, n_blocks, *, tm, rows):
    n_tiles = dest3.shape[0]
    n_last = n_all - (n_tiles - 1) * tm
    ts = PACKED_SUBLANES
    n_slots = n_blocks * rows
    grid_spec = pltpu.PrefetchScalarGridSpec(
        num_scalar_prefetch=2,
        grid=(n_tiles,),
        in_specs=[pl.BlockSpec((1, 1, TOP_K * tm), lambda i, *_: (i, 0, 0), memory_space=pltpu.SMEM),
                  _rows(tm * ts, V7X_LANES)],
        out_specs=pl.BlockSpec(memory_space=pl.ANY),
        scratch_shapes=[pltpu.VMEM((rows * ts, V7X_LANES), U32),
                        pltpu.SemaphoreType.DMA((2,)), pltpu.SemaphoreType.DMA],
    )
    return pl.pallas_call(
        functools.partial(_dispatch_kernel, tm=tm, n_tiles=n_tiles, n_last=n_last, rows=rows, n_blocks=n_blocks),
        grid_spec=grid_spec,
        out_shape=jax.ShapeDtypeStruct((n_slots * ts, V7X_LANES), U32),
        compiler_params=_cparams(),
        name="moe_dispatch",
    )(fill_start, fill_len, dest3, h_tiles)


def _expert_kernel(be_ref, nact_ref, run_ref, rune_ref, x_ref, wgu_hbm, bgu_ref, wdn_hbm, bdn_ref, o_ref,
                   wgu_f32, wdn_f32, wsem, wgu_bf, wdn_bf, *, rows, layer):
    b = pl.program_id(0)
    nact = nact_ref[0]

    def weight_copies(r, slot):
        e = rune_ref[r]
        return (pltpu.make_async_copy(wgu_hbm.at[layer, e], wgu_f32.at[slot], wsem.at[0, slot]),
                pltpu.make_async_copy(wdn_hbm.at[layer, e], wdn_f32.at[slot], wsem.at[1, slot]))

    @pl.when(b == 0)
    def _():
        for cp in weight_copies(0, 0):
            cp.start()

    @pl.when(b < nact)
    def _():
        r = run_ref[b]
        slot = lax.rem(r, 2)
        changed = jnp.logical_or(b == 0, r != run_ref[jnp.maximum(b - 1, 0)])

        @pl.when(changed)
        def _():
            for cp in weight_copies(r, slot):
                cp.wait()

            @pl.when(r + 1 < rune_ref[N_EXPERTS])
            def _():
                for cp in weight_copies(r + 1, 1 - slot):
                    cp.start()

            step = 256
            for c in range(D_MODEL // step):
                wgu_bf[c * step:(c + 1) * step, :] = wgu_f32[slot, c * step:(c + 1) * step, :].astype(BF16)
                wdn_bf[c * step:(c + 1) * step, :] = wdn_f32[slot, c * step:(c + 1) * step, :].astype(BF16)

        xb = _load_packed_tiles(x_ref, rows)
        nc = 512
        chunks = range(D_EXPERT // nc)
        gcols = [slice(c * nc, (c + 1) * nc) for c in chunks]
        ucols = [slice(D_EXPERT + c * nc, D_EXPERT + (c + 1) * nc) for c in chunks]
        gates = [jnp.dot(xb, wgu_bf[:, gcols[c]], preferred_element_type=F32) + bgu_ref[:, gcols[c]] for c in chunks]
        ups = [jnp.dot(xb, wgu_bf[:, ucols[c]], preferred_element_type=F32) + bgu_ref[:, ucols[c]] for c in chunks]
        acts = []
        for gate, up in zip(gates, ups):
            gate = jnp.minimum(gate, SWIGLU_LIMIT)
            up = jnp.clip(up, -SWIGLU_LIMIT, SWIGLU_LIMIT)
            acts.append(((up + 1.0) * gate * (1.0 / (1.0 + jnp.exp(-SWIGLU_ALPHA * gate)))).astype(BF16))
        out = None
        for c in chunks:
            part = jnp.dot(acts[c], wdn_bf[gcols[c], :], preferred_element_type=F32)
            out = part if out is None else out + part
        _store_token_tiles(o_ref, out + bdn_ref[...], rows)

    @pl.when(b >= nact)
    def _():
        o_ref[...] = jnp.zeros_like(o_ref)


def _experts(block_e, n_active, block_run, run_e, xs_tiles, wgu, bgu, wdn, bdn, *, rows, layer):
    ts = TOKEN_SUBLANES
    nb = xs_tiles.shape[0] // (rows * PACKED_SUBLANES)
    grid_spec = pltpu.PrefetchScalarGridSpec(
        num_scalar_prefetch=4,
        grid=(nb,),
        in_specs=[
            pl.BlockSpec((rows * PACKED_SUBLANES, V7X_LANES), lambda b, be, na, *_: (jnp.minimum(b, na[0] - 1), 0)),
            pl.BlockSpec(memory_space=pl.ANY),
            pl.BlockSpec((None, None, 1, 2 * D_EXPERT), lambda b, be, *_: (layer, be[b], 0, 0)),
            pl.BlockSpec(memory_space=pl.ANY),
            pl.BlockSpec((None, None, 1, D_MODEL), lambda b, be, *_: (layer, be[b], 0, 0)),
        ],
        out_specs=pl.BlockSpec((rows * ts, V7X_LANES), lambda b, *_: (b, 0)),
        scratch_shapes=[pltpu.VMEM((2, D_MODEL, 2 * D_EXPERT), F32), pltpu.VMEM((2, D_EXPERT, D_MODEL), F32),
                        pltpu.SemaphoreType.DMA((2, 2)),
                        pltpu.VMEM((D_MODEL, 2 * D_EXPERT), BF16), pltpu.VMEM((D_EXPERT, D_MODEL), BF16)],
    )
    return pl.pallas_call(
        functools.partial(_expert_kernel, rows=rows, layer=layer),
        grid_spec=grid_spec,
        out_shape=jax.ShapeDtypeStruct((nb * rows * ts, V7X_LANES), F32),
        compiler_params=_cparams(),
        name="moe_experts",
    )(block_e, n_active, block_run, run_e, xs_tiles, wgu, bgu, wdn, bdn)


def _combine_kernel(dest_first_ref, dest_next_ref, x_ref, slab_ref, ebuf_ref, *refs, tm, n_tiles, attn_pre):
    if attn_pre:
        (kvg_ref, wk_ref, wv_ref, kng_ref, bng_ref, wq_ref, qng_ref, cos_ref, sin_ref,
         o_ref, k_ref, kdup_ref, v_ref, q_ref, rbuf, sem) = refs
    else:
        o_ref, rbuf, sem = refs
    i = pl.program_id(0)
    slot = lax.rem(i, 2)
    ts = TOKEN_SUBLANES

    def gather(dest_ref, s):
        def body(r, carry):
            for k in range(TOP_K):
                d = dest_ref[0, 0, k * tm + r]
                src = ebuf_ref.at[pl.ds(pl.multiple_of(d * ts, ts), ts)]
                dst = rbuf.at[s, k, pl.ds(pl.multiple_of(r * ts, ts), ts)]
                pltpu.make_async_copy(src, dst, sem.at[s, k % 2]).start(priority=k % 2)
            return carry
        lax.fori_loop(0, tm, body, 0, unroll=4)

    @pl.when(i == 0)
    def _():
        gather(dest_first_ref, 0)

    if attn_pre:
        todo = [(r, k) for r in range(tm) for k in range(TOP_K)]

        def tick(n):
            for r, k in todo[:n]:
                d = dest_next_ref[0, 0, k * tm + r]
                src = ebuf_ref.at[pl.ds(pl.multiple_of(d * ts, ts), ts)]
                pltpu.make_async_copy(src, rbuf.at[1 - slot, k, pl.ds(r * ts, ts)],
                                      sem.at[1 - slot, k % 2]).start(priority=k % 2)
            del todo[:n]
    else:
        @pl.when(i + 1 < n_tiles)
        def _():
            gather(dest_next_ref, 1 - slot)

        def tick(n):
            pass

    def wait_slot(s):
        for k in range(TOP_K):
            pltpu.make_async_copy(ebuf_ref.at[pl.ds(0, tm * ts)], rbuf.at[s, k], sem.at[s, k % 2]).wait()

    wait_slot(slot)
    slab = slab_ref[...]
    gates = [slab[:, 2 * TOP_K + k:2 * TOP_K + k + 1] for k in range(TOP_K)]
    per_phase = -(-TOP_K * tm // (ts + 1 + N_KV_HEADS))
    for s in range(ts):
        cols = slice(s * V7X_LANES, (s + 1) * V7X_LANES)
        acc = x_ref[:, cols]
        for k in range(TOP_K):
            acc = acc + gates[k] * rbuf[slot, k, pl.ds(s, tm, stride=ts), :]
        o_ref[:, cols] = acc
        tick(per_phase)
    if attn_pre:
        x2 = o_ref[...]
        cos2, sin2 = cos_ref[...], sin_ref[...]
        _shared_kv(x2, kvg_ref[...], wk_ref, wv_ref, kng_ref[...], cos2, sin2, k_ref, kdup_ref, v_ref,
                   tick=lambda: tick(per_phase))
        tick(len(todo))
        hq = _rms(x2, bng_ref[...]).astype(BF16)
        q = jnp.dot(hq, wq_ref[...], preferred_element_type=F32)
        for p in range(D_MODEL // PAIR):
            cols = slice(p * PAIR, (p + 1) * PAIR)
            q_ref[:, cols] = (_pair_norm_rope(q[:, cols], qng_ref[...], cos2, sin2) * HEAD_DIM ** -0.5).astype(BF16)

        @pl.when(i == n_tiles - 1)
        def _():
            wait_slot(1 - slot)


def _shared_kv(x, kvg, wk_ref, wv_ref, kng2, cos2, sin2, k_ref, kdup_ref, v_ref, tick=lambda: None):
    h = _rms(x, kvg).astype(BF16)
    kd = jnp.dot(h, wk_ref[...], preferred_element_type=F32)
    v_ref[...] = jnp.dot(h, wv_ref[...], preferred_element_type=F32)
    tick()
    first_half = lax.broadcasted_iota(I32, (x.shape[0], PAIR), 1) < HEAD_DIM
    slabs = []
    for p in range(N_KV_HEADS):
        cols = slice(p * PAIR, (p + 1) * PAIR)
        slab = _pair_norm_rope(kd[:, cols], kng2, cos2, sin2)
        kdup_ref[:, cols] = slab
        slabs.append(slab)
        tick()
    for p in range(D_KV // PAIR):
        k_ref[:, p * PAIR:(p + 1) * PAIR] = jnp.where(first_half, slabs[2 * p], slabs[2 * p + 1])


def _combine(x, slab, dest3, ebuf, *, tm, attn_pre=None):
    n = x.shape[0]
    n_tiles = n // tm
    in_specs = [
        pl.BlockSpec((1, 1, TOP_K * tm), lambda i: (0, 0, 0), memory_space=pltpu.SMEM),
        pl.BlockSpec((1, 1, TOP_K * tm), lambda i: (jnp.minimum(i + 1, n_tiles - 1), 0, 0),
                     memory_space=pltpu.SMEM),
        _rows(tm, D_MODEL), _rows(tm, SLAB), pl.BlockSpec(memory_space=pl.ANY),
    ]
    args = [dest3, dest3, x, slab, ebuf]
    out_specs = [_rows(tm, D_MODEL)]
    out_shape = [jax.ShapeDtypeStruct((n, D_MODEL), F32)]
    if attn_pre is not None:
        *weights, cos2, sin2, table_tiles = attn_pre
        table_spec = pl.BlockSpec((tm, PAIR), lambda i: (lax.rem(i, table_tiles), 0))
        in_specs += [_resident(w.shape) for w in weights] + [table_spec, table_spec]
        args += list(weights) + [cos2, sin2]
        out_specs += [_rows(tm, D_KV), _rows(tm, 2 * D_KV), _rows(tm, D_KV), _rows(tm, D_MODEL)]
        out_shape += [jax.ShapeDtypeStruct((n, D_KV), F32), jax.ShapeDtypeStruct((n, 2 * D_KV), F32),
                      jax.ShapeDtypeStruct((n, D_KV), F32), jax.ShapeDtypeStruct((n, D_MODEL), BF16)]
    return pl.pallas_call(
        functools.partial(_combine_kernel, tm=tm, n_tiles=n_tiles, attn_pre=attn_pre is not None),
        grid=(n_tiles,),
        in_specs=in_specs, out_specs=out_specs, out_shape=out_shape,
        scratch_shapes=[pltpu.VMEM((2, TOP_K, tm * TOKEN_SUBLANES, V7X_LANES), F32),
                        pltpu.SemaphoreType.DMA((2, 2))],
        compiler_params=_cparams(),
        name="moe_combine_attn_pre" if attn_pre is not None else "moe_combine",
    )(*args)


def _moe(xs, g, w_router, b_router, wgu, bgu, wdn, bdn, layer, prompt_attn_pre=None):
    n_all = xs[0].shape[0] + xs[1].shape[0]
    w_t = w_router.T
    whi = w_t.astype(BF16)
    wlo = (w_t - whi.astype(F32)).astype(BF16)
    h_all, slab_p, slab_s, rec, cnt = _router(xs[0], xs[1], g.reshape(1, D_MODEL), whi, wlo,
                                              b_router.reshape(N_EXPERTS, 1), tm=TM_DENSE)
    slabs = [slab_p, slab_s]
    idx = rec[:TOP_K, :n_all].astype(I32)
    pos = rec[TOP_K:, :n_all].astype(I32)
    counts = cnt[:, 0].astype(I32)
    rows = MOE_ROWS
    nb = (n_all * TOP_K + N_EXPERTS * (rows - 1)) // rows
    padded = (counts + rows - 1) // rows * rows
    pad_end = jnp.cumsum(padded)
    pad_start = pad_end - padded
    eid = jnp.arange(N_EXPERTS, dtype=I32)
    has_rows = counts > 0
    dest = pos + jnp.sum(jnp.where(idx[None] == eid[:, None, None], pad_start[:, None, None], 0), axis=0)
    n_active = (pad_end[-1] // rows).astype(I32)
    blk = jnp.arange(nb, dtype=I32)
    block_e = jnp.sum((pad_end[None, :] <= (blk * rows)[:, None]).astype(I32), axis=1)
    last_e = jnp.max(jnp.where(has_rows, eid, 0))
    block_e = jnp.where(blk < n_active, jnp.minimum(block_e, N_EXPERTS - 1), last_e)

    def tile_table(d, tm):
        n_tiles = -(-d.shape[1] // tm)
        d = jnp.pad(d, ((0, 0), (0, n_tiles * tm - d.shape[1])))
        return d.reshape(TOP_K, n_tiles, tm).transpose(1, 0, 2).reshape(n_tiles, 1, TOP_K * tm)

    fill_start = jnp.concatenate([pad_start + counts, n_active.reshape(1)])
    xs_tiles = _dispatch(fill_start, padded - counts, tile_table(dest, TM_DISPATCH), h_all,
                         n_all, nb, tm=TM_DISPATCH, rows=rows)
    rank = jnp.cumsum(has_rows.astype(I32)) - 1
    run_e = jnp.sum(jnp.where(has_rows[None, :] & (rank[None, :] == eid[:, None]), eid[None, :], 0), axis=1)
    run_e = jnp.concatenate([run_e, jnp.sum(has_rows.astype(I32)).reshape(1)]).astype(I32)
    block_run = jnp.sum(jnp.where(block_e[:, None] == eid, rank, 0), axis=1).astype(I32)
    n_l = wgu.shape[0]
    ebuf = _experts(block_e, n_active.reshape(1), block_run, run_e, xs_tiles,
                    wgu, bgu.reshape(n_l, N_EXPERTS, 1, 2 * D_EXPERT), wdn, bdn.reshape(n_l, N_EXPERTS, 1, D_MODEL),
                    rows=rows, layer=layer)
    outs, off = [], 0
    for x, slab, pre in zip(xs, slabs, (prompt_attn_pre, None)):
        n = x.shape[0]
        tm = min(TM_COMBINE_FUSED if pre is not None else TM_COMBINE, n)
        outs.append(_combine(x, slab, tile_table(dest[:, off:off + n], tm), ebuf, tm=tm, attn_pre=pre))
        off += n
    return outs


def _kv_kernel(x_ref, kvg_ref, wk_ref, wv_ref, kng_ref, cos_ref, sin_ref, *refs, with_q):
    if with_q:
        bng_ref, wq_ref, qng_ref, k_ref, kdup_ref, v_ref, q_ref = refs
    else:
        k_ref, kdup_ref, v_ref = refs
    x = x_ref[...]
    cos2, sin2 = cos_ref[...], sin_ref[...]
    _shared_kv(x, kvg_ref[...], wk_ref, wv_ref, kng_ref[...], cos2, sin2, k_ref, kdup_ref, v_ref)
    if with_q:
        hq = _rms(x, bng_ref[...]).astype(BF16)
        q = jnp.dot(hq, wq_ref[...], preferred_element_type=F32)
        for p in range(D_MODEL // PAIR):
            cols = slice(p * PAIR, (p + 1) * PAIR)
            q_ref[:, cols] = _pair_norm_rope(q[:, cols], qng_ref[...], cos2, sin2)


def _kv_proj(x, kvg, wk, wv, kng2, cos2, sin2, q_args, *, tm, table_tiles):
    n = x.shape[0]
    with_q = q_args is not None
    table_spec = pl.BlockSpec((tm, PAIR), lambda i: (lax.rem(i, table_tiles), 0))
    in_specs = [_rows(tm, D_MODEL), _resident(kvg.shape), _resident(wk.shape), _resident(wv.shape),
                _resident(kng2.shape), table_spec, table_spec]
    args = [x, kvg, wk, wv, kng2, cos2, sin2]
    out_specs = [_rows(tm, D_KV), _rows(tm, 2 * D_KV), _rows(tm, D_KV)]
    out_shape = [jax.ShapeDtypeStruct((n, D_KV), F32), jax.ShapeDtypeStruct((n, 2 * D_KV), F32),
                 jax.ShapeDtypeStruct((n, D_KV), F32)]
    if with_q:
        in_specs += [_resident(a.shape) for a in q_args]
        args += list(q_args)
        out_specs.append(_rows(tm, D_MODEL))
        out_shape.append(jax.ShapeDtypeStruct((n, D_MODEL), F32))
    return pl.pallas_call(
        functools.partial(_kv_kernel, with_q=with_q),
        grid=(n // tm,),
        in_specs=in_specs, out_specs=out_specs, out_shape=out_shape,
        compiler_params=_cparams(),
        name="kvq_proj_sample" if with_q else "kv_proj_prompt",
    )(*args)


def _attn_prompt_kernel(sinks_ref, x_ref, q_ref, kc_ref, kp_ref, vc_ref, vp_ref, wo_ref, o_ref, a_scr,
                        *, tm, tiles_per_seq):
    i = pl.program_id(0)

    kj = lax.broadcasted_iota(I32, (2 * WINDOW, WINDOW), 0)
    qi = lax.broadcasted_iota(I32, (2 * WINDOW, WINDOW), 1)
    diff = qi + WINDOW - kj
    band = jnp.logical_and(diff >= 0, diff < WINDOW)
    first_half = lax.broadcasted_iota(I32, (WINDOW, PAIR), 1) < HEAD_DIM
    keep = (jnp.where(first_half, 1.0, 0.0).astype(BF16), jnp.where(first_half, 0.0, 1.0).astype(BF16))
    first_kj = jnp.where(lax.rem(i, tiles_per_seq) == 0, WINDOW, 0)

    for blk in range(tm // WINDOW):
        rows = slice(blk * WINDOW, (blk + 1) * WINDOW)
        if blk == 0:
            kk = jnp.concatenate([kp_ref[...], kc_ref[rows, :]], axis=0).astype(BF16)
            vv = jnp.concatenate([vp_ref[...], vc_ref[rows, :]], axis=0)
            valid = jnp.logical_and(band, kj >= first_kj)
        else:
            win = slice((blk - 1) * WINDOW, (blk + 1) * WINDOW)
            kk = kc_ref[win, :].astype(BF16)
            vv = vc_ref[win, :]
            valid = band
        for kvh in range(N_KV_HEADS):
            kk_h = kk[:, kvh * PAIR:(kvh + 1) * PAIR]
            vcols = slice((kvh // 2) * PAIR, (kvh // 2 + 1) * PAIR)
            vt = vv[:, vcols].T.astype(BF16)
            ch = slice((kvh % 2) * HEAD_DIM, (kvh % 2 + 1) * HEAD_DIM)
            outs = []
            for g in range(Q_PER_KV):
                head = kvh * Q_PER_KV + g
                qm = q_ref[rows, (head // 2) * PAIR:(head // 2 + 1) * PAIR] * keep[head % 2]
                sink = sinks_ref[head]
                st = lax.dot_general(kk_h, qm, (((1,), (1,)), ((), ())), preferred_element_type=F32)
                st = jnp.where(valid, st, -jnp.inf)
                m = jnp.maximum(jnp.max(st, axis=0, keepdims=True), sink)
                p = jnp.exp(st - m)
                den = jnp.sum(p, axis=0, keepdims=True) + jnp.exp(sink - m)
                ot = jnp.dot(vt, p.astype(BF16), preferred_element_type=F32)
                outs.append(ot[ch, :] * (1.0 / den))
            for pq in range(Q_PER_KV // 2):
                pair = (kvh * Q_PER_KV) // 2 + pq
                both = jnp.concatenate([outs[2 * pq], outs[2 * pq + 1]], axis=0)
                a_scr[rows, pair * PAIR:(pair + 1) * PAIR] = both.T.astype(BF16)
    o_ref[...] = x_ref[...] + jnp.dot(a_scr[...], wo_ref[...], preferred_element_type=F32)


def _attn_prompt(x, q, kdup, v, sinks, wo, *, tm, seq):
    n = x.shape[0]
    tiles_per_seq = seq // tm
    blocks_per_tile = tm // WINDOW

    def prev_spec(width):
        return pl.BlockSpec((WINDOW, width), lambda i: (jnp.maximum(i * blocks_per_tile - 1, 0), 0))

    return pl.pallas_call(
        functools.partial(_attn_prompt_kernel, tm=tm, tiles_per_seq=tiles_per_seq),
        grid=(n // tm,),
        in_specs=[pl.BlockSpec(memory_space=pltpu.SMEM), _rows(tm, D_MODEL), _rows(tm, D_MODEL),
                  _rows(tm, 2 * D_KV), prev_spec(2 * D_KV), _rows(tm, D_KV), prev_spec(D_KV), _resident(wo.shape)],
        out_specs=_rows(tm, D_MODEL),
        out_shape=jax.ShapeDtypeStruct((n, D_MODEL), F32),
        scratch_shapes=[pltpu.VMEM((tm, D_MODEL), BF16)],
        compiler_params=_cparams(),
        name="attn_prompt",
    )(sinks, x, q, kdup, kdup, v, v, wo)


def _attn_sample_kernel(q_ref, ck_ref, cv_ref, kn_ref, vn_ref, sink_ref, o_ref, *, per_step, win):
    rowi = lax.broadcasted_iota(I32, (N_HEADS, D_KV), 0)
    lanei = lax.broadcasted_iota(I32, (N_HEADS, D_KV), 1)
    group = rowi >> 2
    own_block = group == (lanei >> 6)
    j = lax.broadcasted_iota(I32, (N_HEADS, win), 1)
    diff = win - j
    valid = jnp.logical_and(diff >= 0, diff < WINDOW)
    sink = sink_ref[...]
    scale = HEAD_DIM ** -0.5
    qms = [jnp.where(own_block, q_ref[b], 0.0) for b in range(per_step)]
    ss = [lax.dot_general(qm.astype(BF16), ck_ref[b].astype(BF16), (((1,), (1,)), ((), ())),
                          preferred_element_type=F32) * scale for b, qm in enumerate(qms)]
    ps, p_news, dens = [], [], []
    for b in range(per_step):
        s = jnp.where(valid, ss[b], -jnp.inf)
        s_new = jnp.sum(qms[b] * kn_ref[b:b + 1, :], axis=-1, keepdims=True) * scale
        m = jnp.maximum(jnp.maximum(jnp.max(s, axis=-1, keepdims=True), s_new), sink)
        p = jnp.exp(s - m)
        p_new = jnp.exp(s_new - m)
        ps.append(p.astype(BF16))
        p_news.append(p_new)
        dens.append(jnp.sum(p, axis=-1, keepdims=True) + p_new + jnp.exp(sink - m))
    os_ = [jnp.dot(ps[b], cv_ref[b].astype(BF16), preferred_element_type=F32) for b in range(per_step)]
    for b in range(per_step):
        o = (os_[b] + p_news[b] * vn_ref[b:b + 1, :]) * (1.0 / dens[b])
        res = o
        for sft in range(1, N_KV_HEADS):
            res = jnp.where(group == sft, pltpu.roll(o, D_KV - sft * HEAD_DIM, 1), res)
        o_ref[b] = res[:, :HEAD_DIM]


def _attn_sample(q4, ck, cv, kn, vn, sink_col, *, per_step):
    nb, win = ck.shape[0], ck.shape[1]
    return pl.pallas_call(
        functools.partial(_attn_sample_kernel, per_step=per_step, win=win),
        grid=(nb // per_step,),
        in_specs=[pl.BlockSpec((per_step, N_HEADS, D_KV), lambda i: (i, 0, 0)),
                  pl.BlockSpec((per_step, win, D_KV), lambda i: (i, 0, 0)),
                  pl.BlockSpec((per_step, win, D_KV), lambda i: (i, 0, 0)),
                  _rows(per_step, D_KV), _rows(per_step, D_KV), _resident(sink_col.shape)],
        out_specs=pl.BlockSpec((per_step, N_HEADS, HEAD_DIM), lambda i: (i, 0, 0)),
        out_shape=jax.ShapeDtypeStruct((nb, N_HEADS, HEAD_DIM), F32),
        compiler_params=_cparams(),
        name="attn_sample",
    )(q4, ck, cv, kn, vn, sink_col)


def _residual_proj_kernel(x_ref, a_ref, w_ref, o_ref):
    o_ref[...] = x_ref[...] + jnp.dot(a_ref[...].astype(BF16), w_ref[...], preferred_element_type=F32)


def _residual_proj(x, a, w):
    n = x.shape[0]
    return pl.pallas_call(
        _residual_proj_kernel,
        grid=(1,),
        in_specs=[_rows(n, D_MODEL), _rows(n, a.shape[1]), _resident(w.shape)],
        out_specs=_rows(n, D_MODEL),
        out_shape=jax.ShapeDtypeStruct((n, D_MODEL), F32),
        compiler_params=_cparams(),
        name="attn_out_sample",
    )(x, a, w)


def _rope_tables(pos):
    half = ROPE_DIM // 2
    inv_freq = np.power(np.float32(ROPE_THETA), -np.arange(half, dtype=np.float32) / np.float32(half))
    ang = (pos.astype(np.float32)[:, None] * inv_freq[None, :].astype(np.float32)).astype(np.float32)
    cos, sin = np.cos(ang.astype(np.float64)), np.sin(ang.astype(np.float64))
    n = pos.shape[0]
    rest = HEAD_DIM - ROPE_DIM
    cos_h = np.concatenate([cos, cos, np.ones((n, rest))], axis=1)
    sin_h = np.concatenate([-sin, sin, np.zeros((n, rest))], axis=1)
    return (jnp.asarray(np.tile(cos_h, (1, 2)), dtype=F32), jnp.asarray(np.tile(sin_h, (1, 2)), dtype=F32))


def kernel(x_prompt, x_sample, cache_k, cache_v, a_norm_g, a_w_in, a_ln_g, a_ln_b, a_w_s, a_b_s, a_w_out, kv_norm_g, w_k, w_v, k_norm_g, b_norm_g, w_q, q_norm_g, sinks, w_o, ffn_norm_g, w_router, b_router, w_gu, b_gu, w_down, b_down):
    bsz, seq, _ = x_prompt.shape
    dec_b, dec_seq, _ = x_sample.shape
    assert dec_seq == 1 and seq % TM_DENSE == 0 and a_norm_g.shape[0] == 1 and b_norm_g.shape[0] == 1
    win = cache_k.shape[1]
    n_p, n_s = bsz * seq, dec_b * dec_seq
    xp = x_prompt.reshape(n_p, D_MODEL)
    xs = x_sample.reshape(n_s, D_MODEL)

    ng = a_norm_g[0].reshape(1, D_MODEL)
    win_bf = a_w_in[0].astype(BF16)
    wout_bf = a_w_out[0].astype(BF16)
    lng = a_ln_g[0].reshape(1, D_GATE)
    lnb = a_ln_b[0].reshape(1, D_GATE)
    bias_full = jnp.repeat(a_b_s[0].T, GROUP_DIM_A, axis=1)
    diag_row = jnp.repeat(a_w_s[0][:, 0, 0], GROUP_DIM_A).reshape(1, D_GATE)
    (xp,) = _mixer_a(xp, ng, win_bf, lng, lnb, a_w_s[0], bias_full, wout_bf, tm=TM_DENSE, single_token_chunks=False)
    xs, v_rows = _mixer_a(xs, ng, win_bf, lng, lnb, diag_row, bias_full[0:1], wout_bf, tm=n_s,
                          single_token_chunks=True)

    kvg = kv_norm_g.reshape(1, D_MODEL)
    bng = b_norm_g[0].reshape(1, D_MODEL)
    w_k_dup = jnp.repeat(w_k.reshape(D_MODEL, N_KV_HEADS, 1, HEAD_DIM), 2, axis=2).reshape(D_MODEL, 2 * D_KV)
    wk_bf, wv_bf, wq_bf, wo_bf = (w.astype(BF16) for w in (w_k_dup, w_v, w_q[0], w_o[0]))
    kng2 = jnp.tile(k_norm_g, 2).reshape(1, PAIR)
    qng2 = jnp.tile(q_norm_g[0], 2).reshape(1, PAIR)
    cos_p, sin_p = _rope_tables(np.arange(seq))
    cos_s, sin_s = _rope_tables(np.full((n_s,), PAST_LEN))
    attn_pre = (kvg, wk_bf, wv_bf, kng2, bng, wq_bf, qng2, cos_p, sin_p, seq // TM_COMBINE_FUSED)
    (xp, k_p, kdup_p, v_p, q_p), (xs,) = _moe([xp, xs], ffn_norm_g[0], w_router[0], b_router[0], w_gu, b_gu,
                                              w_down, b_down, 0, prompt_attn_pre=attn_pre)

    k_s, _, v_s, q_s = _kv_proj(xs, kvg, wk_bf, wv_bf, kng2, cos_s, sin_s, (bng, wq_bf, qng2), tm=n_s,
                                table_tiles=1)
    xp = _attn_prompt(xp, q_p, kdup_p, v_p, sinks[0], wo_bf, tm=TM_DENSE, seq=seq)
    q4 = jnp.tile(q_s.reshape(n_s, N_HEADS, HEAD_DIM), (1, 1, N_KV_HEADS))
    attn_s = _attn_sample(q4, cache_k.reshape(dec_b, win, D_KV), cache_v.reshape(dec_b, win, D_KV), k_s, v_s,
                          sinks[0].reshape(N_HEADS, 1), per_step=8)
    xs = _residual_proj(xs, attn_s.reshape(n_s, D_MODEL), wo_bf)
    (xp,), (xs,) = _moe([xp, xs], ffn_norm_g[1], w_router[1], b_router[1], w_gu, b_gu, w_down, b_down, 1)

    def last_window(a):
        a = a.reshape(bsz, seq, D_KV)[:, -WINDOW:]
        return a.reshape(bsz, min(WINDOW, seq), N_KV_HEADS, HEAD_DIM)

    return (xp.reshape(bsz, seq, D_MODEL), xs.reshape(dec_b, dec_seq, D_MODEL),
            v_rows.reshape(1, dec_b, dec_seq, D_GATE), last_window(k_p), last_window(v_p),
            k_s.reshape(dec_b, dec_seq, N_KV_HEADS, HEAD_DIM), v_s.reshape(dec_b, dec_seq, N_KV_HEADS, HEAD_DIM))
```

```python
import functools

import jax
import jax.numpy as jnp
import numpy as np
from jax import lax
from jax.experimental import pallas as pl
from jax.experimental.pallas import tpu as pltpu

F32, BF16, I32 = jnp.float32, jnp.bfloat16, jnp.int32

D_MODEL = 1024
CHUNK = 128
D_GATE = 2 * D_MODEL
N_GROUPS_A = 8
GROUP_DIM_A = D_GATE // N_GROUPS_A
HEAD_DIM = 64
N_HEADS = D_MODEL // HEAD_DIM
N_KV_HEADS = N_HEADS // 4
Q_PER_KV = N_HEADS // N_KV_HEADS
D_KV = N_KV_HEADS * HEAD_DIM
WINDOW = 128
ROPE_DIM = HEAD_DIM // 4
ROPE_THETA = 500000.0
N_EXPERTS = 32
TOP_K = 4
D_EXPERT = D_MODEL
SWIGLU_LIMIT = 7.0
SWIGLU_ALPHA = 1.702
EPS = 1e-6
PAST_LEN = 8192

V7X_LANES = 128
V7X_VMEM_BYTES = 64 * 1024 * 1024
VMEM_LIMIT_BYTES = V7X_VMEM_BYTES - 8 * 1024 * 1024

TM_DENSE = 512
TM_COMBINE = 256
MOE_ROWS = 512
TM_DISPATCH = 256
FILL_BITS = MOE_ROWS.bit_length() - 1
TOKEN_SUBLANES = D_MODEL // V7X_LANES
PACKED_SUBLANES = TOKEN_SUBLANES // 2
U32 = jnp.uint32
SLAB = V7X_LANES
REC_ROWS = 16
PAIR = 2 * HEAD_DIM
INV_SQRT2 = 0.7071067811865476


def _cparams():
    return pltpu.CompilerParams(dimension_semantics=("arbitrary",), vmem_limit_bytes=VMEM_LIMIT_BYTES)


def _resident(shape):
    zeros = (0,) * len(shape)
    return pl.BlockSpec(shape, lambda i, *_: zeros, pipeline_mode=pl.Buffered(1))


def _rows(tm, width, offset=0):
    return pl.BlockSpec((tm, width), lambda i, *_: (i + offset, 0))


def _rms(x, g):
    return x * lax.rsqrt(jnp.mean(x * x, axis=-1, keepdims=True) + EPS) * g


def _pair_norm_rope(x2, g2, cos2, sin2):
    lane = lax.broadcasted_iota(I32, x2.shape, 1)
    r = lax.broadcasted_iota(I32, (PAIR, PAIR), 0)
    c = lax.broadcasted_iota(I32, (PAIR, PAIR), 1)
    head_mean = jnp.where((r < HEAD_DIM) == (c < HEAD_DIM), 1.0 / HEAD_DIM, 0.0).astype(BF16)
    sq = x2 * x2
    sq_hi = sq.astype(BF16)
    sq_lo = (sq - sq_hi.astype(F32)).astype(BF16)
    ms = (jnp.dot(sq_hi, head_mean, preferred_element_type=F32)
          + jnp.dot(sq_lo, head_mean, preferred_element_type=F32))
    y = x2 * lax.rsqrt(ms + EPS) * g2
    half = ROPE_DIM // 2
    up = pltpu.roll(y, PAIR - half, 1)
    dn = pltpu.roll(y, half, 1)
    partner = jnp.where((lane & (HEAD_DIM - 1)) < half, up, dn)
    return y * cos2 + partner * sin2


def _mixer_a_kernel(x_ref, ng_ref, win_ref, lng_ref, lnb_ref, mix_ref, bias_ref, wout_ref, *refs,
                    tm, single_token_chunks):
    if single_token_chunks:
        o_ref, vout_ref, u_scr, v_scr, p_scr = refs
    else:
        o_ref, u_scr, v_scr, p_scr = refs
    x = x_ref[...]
    h = _rms(x, ng_ref[...]).astype(BF16)
    nc = 512
    for c in range(2 * D_GATE // nc):
        z = jnp.dot(h, win_ref[:, c * nc:(c + 1) * nc], preferred_element_type=F32)
        z = 0.5 * z * (1.0 + lax.erf(z * INV_SQRT2))
        if c < D_GATE // nc:
            u_scr[:, c * nc:(c + 1) * nc] = z
        else:
            v_scr[:, (c - D_GATE // nc) * nc:(c - D_GATE // nc + 1) * nc] = z
    v = v_scr[...]
    vc = v - jnp.mean(v, axis=-1, keepdims=True)
    vn = vc * lax.rsqrt(jnp.mean(vc * vc, axis=-1, keepdims=True) + EPS) * lng_ref[...] + lnb_ref[...]
    if single_token_chunks:
        vout_ref[...] = vn
        p_scr[...] = (u_scr[...] * (vn * mix_ref[...] + bias_ref[...])).astype(BF16)
    else:
        v_scr[...] = vn
        row = lax.broadcasted_iota(I32, (CHUNK, CHUNK), 0)
        col = lax.broadcasted_iota(I32, (CHUNK, CHUNK), 1)
        for g in range(N_GROUPS_A):
            wc = jnp.where(row >= col, mix_ref[g], 0.0).astype(BF16)
            cols = slice(g * GROUP_DIM_A, (g + 1) * GROUP_DIM_A)
            for c in range(tm // CHUNK):
                rows = slice(c * CHUNK, (c + 1) * CHUNK)
                mixed = jnp.dot(wc, v_scr[rows, cols].astype(BF16), preferred_element_type=F32)
                mixed = mixed + bias_ref[:, cols]
                p_scr[rows, cols] = (u_scr[rows, cols] * mixed).astype(BF16)
    o_ref[...] = x + jnp.dot(p_scr[...], wout_ref[...], preferred_element_type=F32)


def _mixer_a(x, ng, win, lng, lnb, mix, bias, wout, *, tm, single_token_chunks):
    n = x.shape[0]
    out_shape = [jax.ShapeDtypeStruct((n, D_MODEL), F32)]
    out_specs = [_rows(tm, D_MODEL)]
    if single_token_chunks:
        out_shape.append(jax.ShapeDtypeStruct((n, D_GATE), F32))
        out_specs.append(_rows(tm, D_GATE))
    return pl.pallas_call(
        functools.partial(_mixer_a_kernel, tm=tm, single_token_chunks=single_token_chunks),
        grid=(n // tm,),
        in_specs=[_rows(tm, D_MODEL), _resident(ng.shape), _resident(win.shape), _resident(lng.shape),
                  _resident(lnb.shape), _resident(mix.shape), _resident(bias.shape), _resident(wout.shape)],
        out_specs=out_specs,
        out_shape=out_shape,
        scratch_shapes=[pltpu.VMEM((tm, D_GATE), F32), pltpu.VMEM((tm, D_GATE), F32),
                        pltpu.VMEM((tm, D_GATE), BF16)],
        compiler_params=_cparams(),
        name="mixer_a_sample" if single_token_chunks else "mixer_a_prompt",
    )(x, ng, win, lng, lnb, mix, bias, wout)


def _route_rows(x, g, whi_t, wlo_t, bias_col, run_scr):
    tm = x.shape[0]
    h = _rms(x, g)
    hh = h.astype(BF16)
    hl = (h - hh.astype(F32)).astype(BF16)
    nt = (((1,), (1,)), ((), ()))
    logits = (lax.dot_general(whi_t, hh, nt, preferred_element_type=F32)
              + lax.dot_general(wlo_t, hh, nt, preferred_element_type=F32)
              + lax.dot_general(whi_t, hl, nt, preferred_element_type=F32) + bias_col)
    e = lax.broadcasted_iota(I32, (N_EXPERTS, tm), 0).astype(F32)
    hot, vals, idxs = [], [], []
    for _ in range(TOP_K):
        m = jnp.max(logits, axis=0, keepdims=True)
        idx = jnp.min(jnp.where(logits == m, e, float(N_EXPERTS)), axis=0, keepdims=True)
        oh = e == idx
        hot.append(oh)
        vals.append(m)
        idxs.append(idx)
        logits = jnp.where(oh, -jnp.inf, logits)
    ex = [jnp.exp(v - vals[0]) for v in vals]
    den = ex[0] + ex[1] + ex[2] + ex[3]
    picked = jnp.zeros((N_EXPERTS, tm), F32)
    for oh in hot:
        picked = picked + jnp.where(oh, 1.0, 0.0)
    r = lax.broadcasted_iota(I32, (tm, tm), 0)
    c = lax.broadcasted_iota(I32, (tm, tm), 1)
    earlier = jnp.where(r < c, 1.0, 0.0).astype(BF16)
    run = run_scr[:, 0:1]
    pos = jnp.dot(picked.astype(BF16), earlier, preferred_element_type=F32) + run
    run_scr[:, 0:1] = run + jnp.sum(picked, axis=1, keepdims=True)
    row = lax.broadcasted_iota(I32, (REC_ROWS, tm), 0)
    rec = jnp.zeros((REC_ROWS, tm), F32)
    for k in range(TOP_K):
        pos_k = jnp.sum(jnp.where(hot[k], pos, 0.0), axis=0, keepdims=True)
        rec = jnp.where(row == k, idxs[k], rec)
        rec = jnp.where(row == TOP_K + k, pos_k, rec)
        rec = jnp.where(row == 2 * TOP_K + k, ex[k] / den, rec)
    return h, rec


def _router_kernel(xp_ref, xs_ref, g_ref, whi_ref, wlo_ref, b_ref, h_ref, slabp_ref, slabs_ref, rec_ref, cnt_ref,
                   run_scr, *, n_tiles, n_s):
    i = pl.program_id(0)

    @pl.when(i == 0)
    def _():
        run_scr[...] = jnp.zeros_like(run_scr)

    def pad_rows(rec):
        return jnp.concatenate([rec, jnp.zeros((SLAB - REC_ROWS, rec.shape[1]), F32)], axis=0)

    @pl.when(i < n_tiles)
    def _():
        h, rec = _route_rows(xp_ref[...], g_ref[...], whi_ref[...], wlo_ref[...], b_ref[...], run_scr)
        _store_packed_tiles(h_ref, h, h.shape[0])
        slabp_ref[...] = pad_rows(rec).T
        rec_ref[...] = rec[0:2 * TOP_K, :]

    @pl.when(i == n_tiles)
    def _():
        h, rec = _route_rows(xs_ref[...], g_ref[...], whi_ref[...], wlo_ref[...], b_ref[...], run_scr)
        h_ref[n_s * PACKED_SUBLANES:, :] = jnp.zeros((h_ref.shape[0] - n_s * PACKED_SUBLANES, V7X_LANES), U32)
        _store_packed_tiles(h_ref, h, n_s)
        slabs_ref[...] = pad_rows(rec).T
        rec_ref[...] = jnp.zeros_like(rec_ref)
        rec_ref[:, 0:n_s] = rec[0:2 * TOP_K, :]

    cnt_ref[...] = run_scr[...]


def _store_packed_tiles(ref, x, n):
    half = D_MODEL // 2
    lo = pltpu.bitcast(x[:, :half].astype(BF16).astype(F32), U32)
    hi = pltpu.bitcast(x[:, half:].astype(BF16).astype(F32), U32)
    word = lax.shift_right_logical(lo, jnp.uint32(16)) | (hi & jnp.uint32(0xFFFF0000))
    for s in range(PACKED_SUBLANES):
        ref[pl.ds(s, n, stride=PACKED_SUBLANES), :] = word[:, s * V7X_LANES:(s + 1) * V7X_LANES]


def _load_packed_tiles(ref, n):
    words = [ref[pl.ds(s, n, stride=PACKED_SUBLANES), :] for s in range(PACKED_SUBLANES)]
    lo = [pltpu.bitcast(w << jnp.uint32(16), F32) for w in words]
    hi = [pltpu.bitcast(w & jnp.uint32(0xFFFF0000), F32) for w in words]
    return jnp.concatenate(lo + hi, axis=1).astype(BF16)


def _store_token_tiles(ref, x, n):
    for s in range(TOKEN_SUBLANES):
        ref[pl.ds(s, n, stride=TOKEN_SUBLANES), :] = x[:, s * V7X_LANES:(s + 1) * V7X_LANES]


def _router(xp, xs, g, whi, wlo, b, *, tm):
    n_p, n_s = xp.shape[0], xs.shape[0]
    n_tiles = n_p // tm
    last = n_tiles - 1
    return pl.pallas_call(
        functools.partial(_router_kernel, n_tiles=n_tiles, n_s=n_s),
        grid=(n_tiles + 1,),
        in_specs=[pl.BlockSpec((tm, D_MODEL), lambda i: (jnp.minimum(i, last), 0)), _resident(xs.shape),
                  _resident(g.shape), _resident(whi.shape), _resident(wlo.shape), _resident(b.shape)],
        out_specs=[_rows(tm * PACKED_SUBLANES, V7X_LANES),
                   pl.BlockSpec((tm, SLAB), lambda i: (jnp.minimum(i, last), 0)),
                   pl.BlockSpec((n_s, SLAB), lambda i: (0, 0)),
                   pl.BlockSpec((2 * TOP_K, tm), lambda i: (0, i)),
                   pl.BlockSpec((N_EXPERTS, V7X_LANES), lambda i: (0, 0))],
        out_shape=[jax.ShapeDtypeStruct(((n_p + tm) * PACKED_SUBLANES, V7X_LANES), U32),
                   jax.ShapeDtypeStruct((n_p, SLAB), F32),
                   jax.ShapeDtypeStruct((n_s, SLAB), F32),
                   jax.ShapeDtypeStruct((2 * TOP_K, n_p + tm), F32),
                   jax.ShapeDtypeStruct((N_EXPERTS, V7X_LANES), F32)],
        scratch_shapes=[pltpu.VMEM((N_EXPERTS, V7X_LANES), F32)],
        compiler_params=_cparams(),
        name="moe_router",
    )(xp, xs, g, whi, wlo, b)


def _dispatch_kernel(fill_start_ref, fill_len_ref, dest_ref, h_ref, xs_ref, zbuf, sem, fill_sem,
                     *, tm, n_tiles, n_last, rows, n_blocks):
    i = pl.program_id(0)
    ts = PACKED_SUBLANES

    def tile_copy(r, k):
        d = dest_ref[0, 0, k * tm + r]
        src = h_ref.at[pl.ds(pl.multiple_of(r * ts, ts), ts)]
        dst = xs_ref.at[pl.ds(pl.multiple_of(d * ts, ts), ts)]
        return pltpu.make_async_copy(src, dst, sem.at[k % 2])

    def scatter(n):
        def body(r, carry):
            for k in range(TOP_K):
                tile_copy(r, k).start(priority=k % 2)
            return carry
        lax.fori_loop(0, n, body, 0, unroll=4)
        for k in range(TOP_K):
            pltpu.make_async_copy(h_ref.at[pl.ds(0, n * ts)], xs_ref.at[pl.ds(0, n * ts)], sem.at[k % 2]).wait()

    def fill_copy(e, bit):
        size = 1 << bit
        length = fill_len_ref[e]
        done = length - (length & (2 * size - 1))
        off = pl.multiple_of((fill_start_ref[e] + done) * ts, ts)
        return (length & size) != 0, pltpu.make_async_copy(zbuf.at[pl.ds(0, size * ts)],
                                                            xs_ref.at[pl.ds(off, size * ts)], fill_sem)

    def tail_copy(blk):
        off = pl.multiple_of(blk * (rows * ts), rows * ts)
        return pltpu.make_async_copy(zbuf, xs_ref.at[pl.ds(off, rows * ts)], fill_sem)

    def for_each_fill(act):
        for e in range(N_EXPERTS):
            for bit in range(FILL_BITS):
                go, cp = fill_copy(e, bit)

                @pl.when(go)
                def _():
                    act(cp)

        def body(blk, carry):
            act(tail_copy(blk))
            return carry
        lax.fori_loop(fill_start_ref[N_EXPERTS], n_blocks, body, 0)

    @pl.when(i == 0)
    def _():
        zbuf[...] = jnp.zeros_like(zbuf)
        for_each_fill(lambda cp: cp.start())

    @pl.when(i < n_tiles - 1)
    def _():
        scatter(tm)

    @pl.when(i == n_tiles - 1)
    def _():
        scatter(n_last)
        for_each_fill(lambda cp: cp.wait())


def _dispatch(fill_start, fill_len, dest3, h_tiles, n_all, n_blocks, *, tm, rows):
    n_tiles = dest3.shape[0]
    n_last = n_all - (n_tiles - 1) * tm
    ts = PACKED_SUBLANES
    n_slots = n_blocks * rows
    grid_spec = pltpu.PrefetchScalarGridSpec(
        num_scalar_prefetch=2,
        grid=(n_tiles,),
        in_specs=[pl.BlockSpec((1, 1, TOP_K * tm), lambda i, *_: (i, 0, 0), memory_space=pltpu.SMEM),
                  _rows(tm * ts, V7X_LANES)],
        out_specs=pl.BlockSpec(memory_space=pl.ANY),
        scratch_shapes=[pltpu.VMEM((rows * ts, V7X_LANES), U32),
                        pltpu.SemaphoreType.DMA((2,)), pltpu.SemaphoreType.DMA],
    )
    return pl.pallas_call(
        functools.partial(_dispatch_kernel, tm=tm, n_tiles=n_tiles, n_last=n_last, rows=rows, n_blocks=n_blocks),
        grid_spec=grid_spec,
        out_shape=jax.ShapeDtypeStruct((n_slots * ts, V7X_LANES), U32),
        compiler_params=_cparams(),
        name="moe_dispatch",
    )(fill_start, fill_len, dest3, h_tiles)


def _expert_kernel(be_ref, nact_ref, run_ref, rune_ref, x_ref, wgu_hbm, bgu_ref, wdn_hbm, bdn_ref, o_ref,
                   wgu_f32, wdn_f32, wsem, wgu_bf, wdn_bf, *, rows, layer):
    b = pl.program_id(0)
    nact = nact_ref[0]

    def weight_copies(r, slot):
        e = rune_ref[r]
        return (pltpu.make_async_copy(wgu_hbm.at[layer, e], wgu_f32.at[slot], wsem.at[0, slot]),
                pltpu.make_async_copy(wdn_hbm.at[layer, e], wdn_f32.at[slot], wsem.at[1, slot]))

    @pl.when(b == 0)
    def _():
        for cp in weight_copies(0, 0):
            cp.start()

    @pl.when(b < nact)
    def _():
        r = run_ref[b]
        slot = lax.rem(r, 2)
        changed = jnp.logical_or(b == 0, r != run_ref[jnp.maximum(b - 1, 0)])

        @pl.when(changed)
        def _():
            for cp in weight_copies(r, slot):
                cp.wait()

            @pl.when(r + 1 < rune_ref[N_EXPERTS])
            def _():
                for cp in weight_copies(r + 1, 1 - slot):
                    cp.start()

            step = 256
            for c in range(D_MODEL // step):
                wgu_bf[c * step:(c + 1) * step, :] = wgu_f32[slot, c * step:(c + 1) * step, :].astype(BF16)
                wdn_bf[c * step:(c + 1) * step, :] = wdn_f32[slot, c * step:(c + 1) * step, :].astype(BF16)

        xb = _load_packed_tiles(x_ref, rows)
        nc = 512
        chunks = range(D_EXPERT // nc)
        gcols = [slice(c * nc, (c + 1) * nc) for c in chunks]
        ucols = [slice(D_EXPERT + c * nc, D_EXPERT + (c + 1) * nc) for c in chunks]
        gates = [jnp.dot(xb, wgu_bf[:, gcols[c]], preferred_element_type=F32) + bgu_ref[:, gcols[c]] for c in chunks]
        ups = [jnp.dot(xb, wgu_bf[:, ucols[c]], preferred_element_type=F32) + bgu_ref[:, ucols[c]] for c in chunks]
        acts = []
        for gate, up in zip(gates, ups):
            gate = jnp.minimum(gate, SWIGLU_LIMIT)
            up = jnp.clip(up, -SWIGLU_LIMIT, SWIGLU_LIMIT)
            acts.append(((up + 1.0) * gate * (1.0 / (1.0 + jnp.exp(-SWIGLU_ALPHA * gate)))).astype(BF16))
        out = None
        for c in chunks:
            part = jnp.dot(acts[c], wdn_bf[gcols[c], :], preferred_element_type=F32)
            out = part if out is None else out + part
        _store_token_tiles(o_ref, out + bdn_ref[...], rows)

    @pl.when(b >= nact)
    def _():
        o_ref[...] = jnp.zeros_like(o_ref)


def _experts(block_e, n_active, block_run, run_e, xs_tiles, wgu, bgu, wdn, bdn, *, rows, layer):
    ts = TOKEN_SUBLANES
    nb = xs_tiles.shape[0] // (rows * PACKED_SUBLANES)
    grid_spec = pltpu.PrefetchScalarGridSpec(
        num_scalar_prefetch=4,
        grid=(nb,),
        in_specs=[
            pl.BlockSpec((rows * PACKED_SUBLANES, V7X_LANES), lambda b, be, na, *_: (jnp.minimum(b, na[0] - 1), 0)),
            pl.BlockSpec(memory_space=pl.ANY),
            pl.BlockSpec((None, None, 1, 2 * D_EXPERT), lambda b, be, *_: (layer, be[b], 0, 0)),
            pl.BlockSpec(memory_space=pl.ANY),
            pl.BlockSpec((None, None, 1, D_MODEL), lambda b, be, *_: (layer, be[b], 0, 0)),
        ],
        out_specs=pl.BlockSpec((rows * ts, V7X_LANES), lambda b, *_: (b, 0)),
        scratch_shapes=[pltpu.VMEM((2, D_MODEL, 2 * D_EXPERT), F32), pltpu.VMEM((2, D_EXPERT, D_MODEL), F32),
                        pltpu.SemaphoreType.DMA((2, 2)),
                        pltpu.VMEM((D_MODEL, 2 * D_EXPERT), BF16), pltpu.VMEM((D_EXPERT, D_MODEL), BF16)],
    )
    return pl.pallas_call(
        functools.partial(_expert_kernel, rows=rows, layer=layer),
        grid_spec=grid_spec,
        out_shape=jax.ShapeDtypeStruct((nb * rows * ts, V7X_LANES), F32),
        compiler_params=_cparams(),
        name="moe_experts",
    )(block_e, n_active, block_run, run_e, xs_tiles, wgu, bgu, wdn, bdn)


def _combine_kernel(dest_first_ref, dest_next_ref, x_ref, slab_ref, ebuf_ref, *refs, tm, n_tiles, attn_pre):
    if attn_pre:
        (kvg_ref, wk_ref, wv_ref, kng_ref, bng_ref, wq_ref, qng_ref, cos_ref, sin_ref,
         o_ref, k_ref, kdup_ref, v_ref, q_ref, rbuf, sem) = refs
    else:
        o_ref, rbuf, sem = refs
    i = pl.program_id(0)
    slot = lax.rem(i, 2)
    ts = TOKEN_SUBLANES

    def gather(dest_ref, s):
        def body(r, carry):
            for k in range(TOP_K):
                d = dest_ref[0, 0, k * tm + r]
                src = ebuf_ref.at[pl.ds(pl.multiple_of(d * ts, ts), ts)]
                dst = rbuf.at[s, k, pl.ds(pl.multiple_of(r * ts, ts), ts)]
                pltpu.make_async_copy(src, dst, sem.at[s, k % 2]).start(priority=k % 2)
            return carry
        lax.fori_loop(0, tm, body, 0, unroll=4)

    @pl.when(i == 0)
    def _():
        gather(dest_first_ref, 0)

    if attn_pre:
        todo = [(r, k) for r in range(tm) for k in range(TOP_K)]

        def tick(n):
            for r, k in todo[:n]:
                d = dest_next_ref[0, 0, k * tm + r]
                src = ebuf_ref.at[pl.ds(pl.multiple_of(d * ts, ts), ts)]
                pltpu.make_async_copy(src, rbuf.at[1 - slot, k, pl.ds(r * ts, ts)],
                                      sem.at[1 - slot, k % 2]).start(priority=k % 2)
            del todo[:n]
    else:
        @pl.when(i + 1 < n_tiles)
        def _():
            gather(dest_next_ref, 1 - slot)

        def tick(n):
            pass

    def wait_slot(s):
        for k in range(TOP_K):
            pltpu.make_async_copy(ebuf_ref.at[pl.ds(0, tm * ts)], rbuf.at[s, k], sem.at[s, k % 2]).wait()

    wait_slot(slot)
    slab = slab_ref[...]
    gates = [slab[:, 2 * TOP_K + k:2 * TOP_K + k + 1] for k in range(TOP_K)]
    per_phase = TOP_K * tm // (4 * ts)
    for s in range(ts):
        cols = slice(s * V7X_LANES, (s + 1) * V7X_LANES)
        acc = x_ref[:, cols]
        for k in range(TOP_K):
            acc = acc + gates[k] * rbuf[slot, k, pl.ds(s, tm, stride=ts), :]
        o_ref[:, cols] = acc
        tick(per_phase)
    if attn_pre:
        x2 = o_ref[...]
        cos2, sin2 = cos_ref[...], sin_ref[...]
        kv_phases = 1 + N_KV_HEADS
        q_phases = 1 + D_MODEL // PAIR
        per_phase = len(todo) // (kv_phases + q_phases)
        _shared_kv(x2, kvg_ref[...], wk_ref, wv_ref, kng_ref[...], cos2, sin2, k_ref, kdup_ref, v_ref,
                   tick=lambda: tick(per_phase))
        hq = _rms(x2, bng_ref[...]).astype(BF16)
        q = jnp.dot(hq, wq_ref[...], preferred_element_type=F32)
        tick(per_phase)
        for p in range(D_MODEL // PAIR):
            cols = slice(p * PAIR, (p + 1) * PAIR)
            q_ref[:, cols] = (_pair_norm_rope(q[:, cols], qng_ref[...], cos2, sin2) * HEAD_DIM ** -0.5).astype(BF16)
            tick(per_phase)
        tick(len(todo))

        @pl.when(i == n_tiles - 1)
        def _():
            wait_slot(1 - slot)


def _shared_kv(x, kvg, wk_ref, wv_ref, kng2, cos2, sin2, k_ref, kdup_ref, v_ref, tick=lambda: None):
    h = _rms(x, kvg).astype(BF16)
    kd = jnp.dot(h, wk_ref[...], preferred_element_type=F32)
    v_ref[...] = jnp.dot(h, wv_ref[...], preferred_element_type=F32)
    tick()
    first_half = lax.broadcasted_iota(I32, (x.shape[0], PAIR), 1) < HEAD_DIM
    slabs = []
    for p in range(N_KV_HEADS):
        cols = slice(p * PAIR, (p + 1) * PAIR)
        slab = _pair_norm_rope(kd[:, cols], kng2, cos2, sin2)
        kdup_ref[:, cols] = slab
        slabs.append(slab)
        tick()
    for p in range(D_KV // PAIR):
        k_ref[:, p * PAIR:(p + 1) * PAIR] = jnp.where(first_half, slabs[2 * p], slabs[2 * p + 1])


def _combine(x, slab, dest3, ebuf, *, tm, attn_pre=None):
    n = x.shape[0]
    n_tiles = n // tm
    in_specs = [
        pl.BlockSpec((1, 1, TOP_K * tm), lambda i: (0, 0, 0), memory_space=pltpu.SMEM),
        pl.BlockSpec((1, 1, TOP_K * tm), lambda i: (jnp.minimum(i + 1, n_tiles - 1), 0, 0),
                     memory_space=pltpu.SMEM),
        _rows(tm, D_MODEL), _rows(tm, SLAB), pl.BlockSpec(memory_space=pl.ANY),
    ]
    args = [dest3, dest3, x, slab, ebuf]
    out_specs = [_rows(tm, D_MODEL)]
    out_shape = [jax.ShapeDtypeStruct((n, D_MODEL), F32)]
    if attn_pre is not None:
        *weights, cos2, sin2, table_tiles = attn_pre
        table_spec = pl.BlockSpec((tm, PAIR), lambda i: (lax.rem(i, table_tiles), 0))
        in_specs += [_resident(w.shape) for w in weights] + [table_spec, table_spec]
        args += list(weights) + [cos2, sin2]
        out_specs += [_rows(tm, D_KV), _rows(tm, 2 * D_KV), _rows(tm, D_KV), _rows(tm, D_MODEL)]
        out_shape += [jax.ShapeDtypeStruct((n, D_KV), F32), jax.ShapeDtypeStruct((n, 2 * D_KV), F32),
                      jax.ShapeDtypeStruct((n, D_KV), F32), jax.ShapeDtypeStruct((n, D_MODEL), BF16)]
    return pl.pallas_call(
        functools.partial(_combine_kernel, tm=tm, n_tiles=n_tiles, attn_pre=attn_pre is not None),
        grid=(n_tiles,),
        in_specs=in_specs, out_specs=out_specs, out_shape=out_shape,
        scratch_shapes=[pltpu.VMEM((2, TOP_K, tm * TOKEN_SUBLANES, V7X_LANES), F32),
                        pltpu.SemaphoreType.DMA((2, 2))],
        compiler_params=_cparams(),
        name="moe_combine_attn_pre" if attn_pre is not None else "moe_combine",
    )(*args)


def _moe(xs, g, w_router, b_router, wgu, bgu, wdn, bdn, layer, prompt_attn_pre=None):
    n_all = xs[0].shape[0] + xs[1].shape[0]
    w_t = w_router.T
    whi = w_t.astype(BF16)
    wlo = (w_t - whi.astype(F32)).astype(BF16)
    h_all, slab_p, slab_s, rec, cnt = _router(xs[0], xs[1], g.reshape(1, D_MODEL), whi, wlo,
                                              b_router.reshape(N_EXPERTS, 1), tm=TM_DENSE)
    slabs = [slab_p, slab_s]
    idx = rec[:TOP_K, :n_all].astype(I32)
    pos = rec[TOP_K:, :n_all].astype(I32)
    counts = cnt[:, 0].astype(I32)
    rows = MOE_ROWS
    nb = (n_all * TOP_K + N_EXPERTS * (rows - 1)) // rows
    padded = (counts + rows - 1) // rows * rows
    pad_end = jnp.cumsum(padded)
    pad_start = pad_end - padded
    eid = jnp.arange(N_EXPERTS, dtype=I32)
    has_rows = counts > 0
    dest = pos + jnp.sum(jnp.where(idx[None] == eid[:, None, None], pad_start[:, None, None], 0), axis=0)
    n_active = (pad_end[-1] // rows).astype(I32)
    blk = jnp.arange(nb, dtype=I32)
    block_e = jnp.sum((pad_end[None, :] <= (blk * rows)[:, None]).astype(I32), axis=1)
    last_e = jnp.max(jnp.where(has_rows, eid, 0))
    block_e = jnp.where(blk < n_active, jnp.minimum(block_e, N_EXPERTS - 1), last_e)

    def tile_table(d, tm):
        n_tiles = -(-d.shape[1] // tm)
        d = jnp.pad(d, ((0, 0), (0, n_tiles * tm - d.shape[1])))
        return d.reshape(TOP_K, n_tiles, tm).transpose(1, 0, 2).reshape(n_tiles, 1, TOP_K * tm)

    fill_start = jnp.concatenate([pad_start + counts, n_active.reshape(1)])
    xs_tiles = _dispatch(fill_start, padded - counts, tile_table(dest, TM_DISPATCH), h_all,
                         n_all, nb, tm=TM_DISPATCH, rows=rows)
    rank = jnp.cumsum(has_rows.astype(I32)) - 1
    run_e = jnp.sum(jnp.where(has_rows[None, :] & (rank[None, :] == eid[:, None]), eid[None, :], 0), axis=1)
    run_e = jnp.concatenate([run_e, jnp.sum(has_rows.astype(I32)).reshape(1)]).astype(I32)
    block_run = jnp.sum(jnp.where(block_e[:, None] == eid, rank, 0), axis=1).astype(I32)
    n_l = wgu.shape[0]
    ebuf = _experts(block_e, n_active.reshape(1), block_run, run_e, xs_tiles,
                    wgu, bgu.reshape(n_l, N_EXPERTS, 1, 2 * D_EXPERT), wdn, bdn.reshape(n_l, N_EXPERTS, 1, D_MODEL),
                    rows=rows, layer=layer)
    outs, off = [], 0
    for x, slab, pre in zip(xs, slabs, (prompt_attn_pre, None)):
        n = x.shape[0]
        tm = min(TM_COMBINE, n)
        outs.append(_combine(x, slab, tile_table(dest[:, off:off + n], tm), ebuf, tm=tm, attn_pre=pre))
        off += n
    return outs


def _kvq_sample_kernel(x_ref, kvg_ref, wk_ref, wv_ref, kng_ref, cos_ref, sin_ref, bng_ref, wq_ref, qng_ref,
                       k_ref, kdup_ref, v_ref, q_ref):
    x = x_ref[...]
    cos2, sin2 = cos_ref[...], sin_ref[...]
    _shared_kv(x, kvg_ref[...], wk_ref, wv_ref, kng_ref[...], cos2, sin2, k_ref, kdup_ref, v_ref)
    hq = _rms(x, bng_ref[...]).astype(BF16)
    q = jnp.dot(hq, wq_ref[...], preferred_element_type=F32)
    for p in range(D_MODEL // PAIR):
        cols = slice(p * PAIR, (p + 1) * PAIR)
        q_ref[:, cols] = _pair_norm_rope(q[:, cols], qng_ref[...], cos2, sin2)


def _kvq_sample(x, kvg, wk, wv, kng2, cos2, sin2, bng, wq, qng2):
    n = x.shape[0]
    args = [x, kvg, wk, wv, kng2, cos2, sin2, bng, wq, qng2]
    return pl.pallas_call(
        _kvq_sample_kernel,
        grid=(1,),
        in_specs=[_resident(a.shape) for a in args],
        out_specs=[_rows(n, D_KV), _rows(n, 2 * D_KV), _rows(n, D_KV), _rows(n, D_MODEL)],
        out_shape=[jax.ShapeDtypeStruct((n, D_KV), F32), jax.ShapeDtypeStruct((n, 2 * D_KV), F32),
                   jax.ShapeDtypeStruct((n, D_KV), F32), jax.ShapeDtypeStruct((n, D_MODEL), F32)],
        compiler_params=_cparams(),
        name="kvq_proj_sample",
    )(*args)


def _attn_prompt_kernel(sinks_ref, x_ref, q_ref, kc_ref, kp_ref, vc_ref, vp_ref, wo_ref, o_ref, a_scr,
                        *, tm, tiles_per_seq):
    i = pl.program_id(0)

    kj = lax.broadcasted_iota(I32, (2 * WINDOW, WINDOW), 0)
    qi = lax.broadcasted_iota(I32, (2 * WINDOW, WINDOW), 1)
    diff = qi + WINDOW - kj
    band = jnp.logical_and(diff >= 0, diff < WINDOW)
    first_half = lax.broadcasted_iota(I32, (WINDOW, PAIR), 1) < HEAD_DIM
    keep = (jnp.where(first_half, 1.0, 0.0).astype(BF16), jnp.where(first_half, 0.0, 1.0).astype(BF16))
    first_kj = jnp.where(lax.rem(i, tiles_per_seq) == 0, WINDOW, 0)

    for blk in range(tm // WINDOW):
        rows = slice(blk * WINDOW, (blk + 1) * WINDOW)
        if blk == 0:
            kk = jnp.concatenate([kp_ref[...], kc_ref[rows, :]], axis=0).astype(BF16)
            vv = jnp.concatenate([vp_ref[...], vc_ref[rows, :]], axis=0)
            valid = jnp.logical_and(band, kj >= first_kj)
        else:
            win = slice((blk - 1) * WINDOW, (blk + 1) * WINDOW)
            kk = kc_ref[win, :].astype(BF16)
            vv = vc_ref[win, :]
            valid = band
        for kvh in range(N_KV_HEADS):
            kk_h = kk[:, kvh * PAIR:(kvh + 1) * PAIR]
            vcols = slice((kvh // 2) * PAIR, (kvh // 2 + 1) * PAIR)
            vt = vv[:, vcols].T.astype(BF16)
            ch = slice((kvh % 2) * HEAD_DIM, (kvh % 2 + 1) * HEAD_DIM)
            outs = []
            for g in range(Q_PER_KV):
                head = kvh * Q_PER_KV + g
                qm = q_ref[rows, (head // 2) * PAIR:(head // 2 + 1) * PAIR] * keep[head % 2]
                sink = sinks_ref[head]
                st = lax.dot_general(kk_h, qm, (((1,), (1,)), ((), ())), preferred_element_type=F32)
                st = jnp.where(valid, st, -jnp.inf)
                m = jnp.maximum(jnp.max(st, axis=0, keepdims=True), sink)
                p = jnp.exp(st - m)
                den = jnp.sum(p, axis=0, keepdims=True) + jnp.exp(sink - m)
                ot = jnp.dot(vt, p.astype(BF16), preferred_element_type=F32)
                outs.append(ot[ch, :] * (1.0 / den))
            for pq in range(Q_PER_KV // 2):
                pair = (kvh * Q_PER_KV) // 2 + pq
                both = jnp.concatenate([outs[2 * pq], outs[2 * pq + 1]], axis=0)
                a_scr[rows, pair * PAIR:(pair + 1) * PAIR] = both.T.astype(BF16)
    o_ref[...] = x_ref[...] + jnp.dot(a_scr[...], wo_ref[...], preferred_element_type=F32)


def _attn_prompt(x, q, kdup, v, sinks, wo, *, tm, seq):
    n = x.shape[0]
    tiles_per_seq = seq // tm
    blocks_per_tile = tm // WINDOW

    def prev_spec(width):
        return pl.BlockSpec((WINDOW, width), lambda i: (jnp.maximum(i * blocks_per_tile - 1, 0), 0))

    return pl.pallas_call(
        functools.partial(_attn_prompt_kernel, tm=tm, tiles_per_seq=tiles_per_seq),
        grid=(n // tm,),
        in_specs=[pl.BlockSpec(memory_space=pltpu.SMEM), _rows(tm, D_MODEL), _rows(tm, D_MODEL),
                  _rows(tm, 2 * D_KV), prev_spec(2 * D_KV), _rows(tm, D_KV), prev_spec(D_KV), _resident(wo.shape)],
        out_specs=_rows(tm, D_MODEL),
        out_shape=jax.ShapeDtypeStruct((n, D_MODEL), F32),
        scratch_shapes=[pltpu.VMEM((tm, D_MODEL), BF16)],
        compiler_params=_cparams(),
        name="attn_prompt",
    )(sinks, x, q, kdup, kdup, v, v, wo)


def _attn_sample_kernel(q_ref, ck_ref, cv_ref, kn_ref, vn_ref, sink_ref, o_ref, *, per_step, win):
    rowi = lax.broadcasted_iota(I32, (N_HEADS, D_KV), 0)
    lanei = lax.broadcasted_iota(I32, (N_HEADS, D_KV), 1)
    group = rowi >> 2
    own_block = group == (lanei >> 6)
    j = lax.broadcasted_iota(I32, (N_HEADS, win), 1)
    diff = win - j
    valid = jnp.logical_and(diff >= 0, diff < WINDOW)
    sink = sink_ref[...]
    scale = HEAD_DIM ** -0.5
    qms = [jnp.where(own_block, q_ref[b], 0.0) for b in range(per_step)]
    ss = [lax.dot_general(qm.astype(BF16), ck_ref[b].astype(BF16), (((1,), (1,)), ((), ())),
                          preferred_element_type=F32) * scale for b, qm in enumerate(qms)]
    ps, p_news, dens = [], [], []
    for b in range(per_step):
        s = jnp.where(valid, ss[b], -jnp.inf)
        s_new = jnp.sum(qms[b] * kn_ref[b:b + 1, :], axis=-1, keepdims=True) * scale
        m = jnp.maximum(jnp.maximum(jnp.max(s, axis=-1, keepdims=True), s_new), sink)
        p = jnp.exp(s - m)
        p_new = jnp.exp(s_new - m)
        ps.append(p.astype(BF16))
        p_news.append(p_new)
        dens.append(jnp.sum(p, axis=-1, keepdims=True) + p_new + jnp.exp(sink - m))
    os_ = [jnp.dot(ps[b], cv_ref[b].astype(BF16), preferred_element_type=F32) for b in range(per_step)]
    for b in range(per_step):
        o = (os_[b] + p_news[b] * vn_ref[b:b + 1, :]) * (1.0 / dens[b])
        res = o
        for sft in range(1, N_KV_HEADS):
            res = jnp.where(group == sft, pltpu.roll(o, D_KV - sft * HEAD_DIM, 1), res)
        o_ref[b] = res[:, :HEAD_DIM]


def _attn_sample(q4, ck, cv, kn, vn, sink_col, *, per_step):
    nb, win = ck.shape[0], ck.shape[1]
    return pl.pallas_call(
        functools.partial(_attn_sample_kernel, per_step=per_step, win=win),
        grid=(nb // per_step,),
        in_specs=[pl.BlockSpec((per_step, N_HEADS, D_KV), lambda i: (i, 0, 0)),
                  pl.BlockSpec((per_step, win, D_KV), lambda i: (i, 0, 0)),
                  pl.BlockSpec((per_step, win, D_KV), lambda i: (i, 0, 0)),
                  _rows(per_step, D_KV), _rows(per_step, D_KV), _resident(sink_col.shape)],
        out_specs=pl.BlockSpec((per_step, N_HEADS, HEAD_DIM), lambda i: (i, 0, 0)),
        out_shape=jax.ShapeDtypeStruct((nb, N_HEADS, HEAD_DIM), F32),
        compiler_params=_cparams(),
        name="attn_sample",
    )(q4, ck, cv, kn, vn, sink_col)


def _residual_proj_kernel(x_ref, a_ref, w_ref, o_ref):
    o_ref[...] = x_ref[...] + jnp.dot(a_ref[...].astype(BF16), w_ref[...], preferred_element_type=F32)


def _residual_proj(x, a, w):
    n = x.shape[0]
    return pl.pallas_call(
        _residual_proj_kernel,
        grid=(1,),
        in_specs=[_rows(n, D_MODEL), _rows(n, a.shape[1]), _resident(w.shape)],
        out_specs=_rows(n, D_MODEL),
        out_shape=jax.ShapeDtypeStruct((n, D_MODEL), F32),
        compiler_params=_cparams(),
        name="attn_out_sample",
    )(x, a, w)


def _rope_tables(pos):
    half = ROPE_DIM // 2
    inv_freq = np.power(np.float32(ROPE_THETA), -np.arange(half, dtype=np.float32) / np.float32(half))
    ang = (pos.astype(np.float32)[:, None] * inv_freq[None, :].astype(np.float32)).astype(np.float32)
    cos, sin = np.cos(ang.astype(np.float64)), np.sin(ang.astype(np.float64))
    n = pos.shape[0]
    rest = HEAD_DIM - ROPE_DIM
    cos_h = np.concatenate([cos, cos, np.ones((n, rest))], axis=1)
    sin_h = np.concatenate([-sin, sin, np.zeros((n, rest))], axis=1)
    return (jnp.asarray(np.tile(cos_h, (1, 2)), dtype=F32), jnp.asarray(np.tile(sin_h, (1, 2)), dtype=F32))


def kernel(x_prompt, x_sample, cache_k, cache_v, a_norm_g, a_w_in, a_ln_g, a_ln_b, a_w_s, a_b_s, a_w_out, kv_norm_g, w_k, w_v, k_norm_g, b_norm_g, w_q, q_norm_g, sinks, w_o, ffn_norm_g, w_router, b_router, w_gu, b_gu, w_down, b_down):
    bsz, seq, _ = x_prompt.shape
    dec_b, dec_seq, _ = x_sample.shape
    assert dec_seq == 1 and seq % TM_DENSE == 0 and a_norm_g.shape[0] == 1 and b_norm_g.shape[0] == 1
    win = cache_k.shape[1]
    n_p, n_s = bsz * seq, dec_b * dec_seq
    xp = x_prompt.reshape(n_p, D_MODEL)
    xs = x_sample.reshape(n_s, D_MODEL)

    ng = a_norm_g[0].reshape(1, D_MODEL)
    win_bf = a_w_in[0].astype(BF16)
    wout_bf = a_w_out[0].astype(BF16)
    lng = a_ln_g[0].reshape(1, D_GATE)
    lnb = a_ln_b[0].reshape(1, D_GATE)
    bias_full = jnp.repeat(a_b_s[0].T, GROUP_DIM_A, axis=1)
    diag_row = jnp.repeat(a_w_s[0][:, 0, 0], GROUP_DIM_A).reshape(1, D_GATE)
    (xp,) = _mixer_a(xp, ng, win_bf, lng, lnb, a_w_s[0], bias_full, wout_bf, tm=TM_DENSE, single_token_chunks=False)
    xs, v_rows = _mixer_a(xs, ng, win_bf, lng, lnb, diag_row, bias_full[0:1], wout_bf, tm=n_s,
                          single_token_chunks=True)

    kvg = kv_norm_g.reshape(1, D_MODEL)
    bng = b_norm_g[0].reshape(1, D_MODEL)
    w_k_dup = jnp.repeat(w_k.reshape(D_MODEL, N_KV_HEADS, 1, HEAD_DIM), 2, axis=2).reshape(D_MODEL, 2 * D_KV)
    wk_bf, wv_bf, wq_bf, wo_bf = (w.astype(BF16) for w in (w_k_dup, w_v, w_q[0], w_o[0]))
    kng2 = jnp.tile(k_norm_g, 2).reshape(1, PAIR)
    qng2 = jnp.tile(q_norm_g[0], 2).reshape(1, PAIR)
    cos_p, sin_p = _rope_tables(np.arange(seq))
    cos_s, sin_s = _rope_tables(np.full((n_s,), PAST_LEN))
    attn_pre = (kvg, wk_bf, wv_bf, kng2, bng, wq_bf, qng2, cos_p, sin_p, seq // TM_COMBINE)
    (xp, k_p, kdup_p, v_p, q_p), (xs,) = _moe([xp, xs], ffn_norm_g[0], w_router[0], b_router[0], w_gu, b_gu,
                                              w_down, b_down, 0, prompt_attn_pre=attn_pre)

    k_s, _, v_s, q_s = _kvq_sample(xs, kvg, wk_bf, wv_bf, kng2, cos_s, sin_s, bng, wq_bf, qng2)
    xp = _attn_prompt(xp, q_p, kdup_p, v_p, sinks[0], wo_bf, tm=TM_DENSE, seq=seq)
    q4 = jnp.tile(q_s.reshape(n_s, N_HEADS, HEAD_DIM), (1, 1, N_KV_HEADS))
    attn_s = _attn_sample(q4, cache_k.reshape(dec_b, win, D_KV), cache_v.reshape(dec_b, win, D_KV), k_s, v_s,
                          sinks[0].reshape(N_HEADS, 1), per_step=8)
    xs = _residual_proj(xs, attn_s.reshape(n_s, D_MODEL), wo_bf)
    (xp,), (xs,) = _moe([xp, xs], ffn_norm_g[1], w_router[1], b_router[1], w_gu, b_gu, w_down, b_down, 1)

    def last_window(a):
        a = a.reshape(bsz, seq, D_KV)[:, -WINDOW:]
        return a.reshape(bsz, min(WINDOW, seq), N_KV_HEADS, HEAD_DIM)

    return (xp.reshape(bsz, seq, D_MODEL), xs.reshape(dec_b, dec_seq, D_MODEL),
            v_rows.reshape(1, dec_b, dec_seq, D_GATE), last_window(k_p), last_window(v_p),
            k_s.reshape(dec_b, dec_seq, N_KV_HEADS, HEAD_DIM), v_s.reshape(dec_b, dec_seq, N_KV_HEADS, HEAD_DIM))
```

```python
import functools

import jax
import jax.numpy as jnp
import numpy as np
from jax import lax
from jax.experimental import pallas as pl
from jax.experimental.pallas import tpu as pltpu

F32, BF16, I32 = jnp.float32, jnp.bfloat16, jnp.int32

D_MODEL = 1024
CHUNK = 128
D_GATE = 2 * D_MODEL
N_GROUPS_A = 8
GROUP_DIM_A = D_GATE // N_GROUPS_A
HEAD_DIM = 64
N_HEADS = D_MODEL // HEAD_DIM
N_KV_HEADS = N_HEADS // 4
Q_PER_KV = N_HEADS // N_KV_HEADS
D_KV = N_KV_HEADS * HEAD_DIM
WINDOW = 128
ROPE_DIM = HEAD_DIM // 4
ROPE_THETA = 500000.0
N_EXPERTS = 32
TOP_K = 4
D_EXPERT = D_MODEL
SWIGLU_LIMIT = 7.0
SWIGLU_ALPHA = 1.702
EPS = 1e-6
PAST_LEN = 8192

V7X_LANES = 128
V7X_VMEM_BYTES = 64 * 1024 * 1024
VMEM_LIMIT_BYTES = V7X_VMEM_BYTES - 8 * 1024 * 1024

TM_DENSE = 512
TM_COMBINE = 256
MOE_ROWS = 512
EXPERT_BLOCKS_PER_STEP = 2
TM_DISPATCH = 256
FILL_BITS = MOE_ROWS.bit_length() - 1
TOKEN_SUBLANES = D_MODEL // V7X_LANES
PACKED_SUBLANES = TOKEN_SUBLANES // 2
U32 = jnp.uint32
SLAB = V7X_LANES
REC_ROWS = 16
PAIR = 2 * HEAD_DIM
INV_SQRT2 = 0.7071067811865476


def _cparams():
    return pltpu.CompilerParams(dimension_semantics=("arbitrary",), vmem_limit_bytes=VMEM_LIMIT_BYTES)


def _resident(shape):
    zeros = (0,) * len(shape)
    return pl.BlockSpec(shape, lambda i, *_: zeros, pipeline_mode=pl.Buffered(1))


def _rows(tm, width, offset=0):
    return pl.BlockSpec((tm, width), lambda i, *_: (i + offset, 0))


def _rms(x, g):
    return x * lax.rsqrt(jnp.mean(x * x, axis=-1, keepdims=True) + EPS) * g


def _pair_norm_rope(x2, g2, cos2, sin2):
    lane = lax.broadcasted_iota(I32, x2.shape, 1)
    r = lax.broadcasted_iota(I32, (PAIR, PAIR), 0)
    c = lax.broadcasted_iota(I32, (PAIR, PAIR), 1)
    head_mean = jnp.where((r < HEAD_DIM) == (c < HEAD_DIM), 1.0 / HEAD_DIM, 0.0).astype(BF16)
    sq = x2 * x2
    sq_hi = sq.astype(BF16)
    sq_lo = (sq - sq_hi.astype(F32)).astype(BF16)
    ms = (jnp.dot(sq_hi, head_mean, preferred_element_type=F32)
          + jnp.dot(sq_lo, head_mean, preferred_element_type=F32))
    y = x2 * lax.rsqrt(ms + EPS) * g2
    half = ROPE_DIM // 2
    up = pltpu.roll(y, PAIR - half, 1)
    dn = pltpu.roll(y, half, 1)
    partner = jnp.where((lane & (HEAD_DIM - 1)) < half, up, dn)
    return y * cos2 + partner * sin2


def _mixer_a_kernel(x_ref, ng_ref, win_ref, lng_ref, lnb_ref, mix_ref, bias_ref, wout_ref, *refs,
                    tm, single_token_chunks):
    if single_token_chunks:
        o_ref, vout_ref, u_scr, v_scr, p_scr = refs
    else:
        o_ref, u_scr, v_scr, p_scr = refs
    x = x_ref[...]
    h = _rms(x, ng_ref[...]).astype(BF16)
    nc = 512
    for c in range(2 * D_GATE // nc):
        z = jnp.dot(h, win_ref[:, c * nc:(c + 1) * nc], preferred_element_type=F32)
        z = 0.5 * z * (1.0 + lax.erf(z * INV_SQRT2))
        if c < D_GATE // nc:
            u_scr[:, c * nc:(c + 1) * nc] = z
        else:
            v_scr[:, (c - D_GATE // nc) * nc:(c - D_GATE // nc + 1) * nc] = z
    v = v_scr[...]
    vc = v - jnp.mean(v, axis=-1, keepdims=True)
    vn = vc * lax.rsqrt(jnp.mean(vc * vc, axis=-1, keepdims=True) + EPS) * lng_ref[...] + lnb_ref[...]
    if single_token_chunks:
        vout_ref[...] = vn
        p_scr[...] = (u_scr[...] * (vn * mix_ref[...] + bias_ref[...])).astype(BF16)
    else:
        v_scr[...] = vn
        row = lax.broadcasted_iota(I32, (CHUNK, CHUNK), 0)
        col = lax.broadcasted_iota(I32, (CHUNK, CHUNK), 1)
        for g in range(N_GROUPS_A):
            wc = jnp.where(row >= col, mix_ref[g], 0.0).astype(BF16)
            cols = slice(g * GROUP_DIM_A, (g + 1) * GROUP_DIM_A)
            for c in range(tm // CHUNK):
                rows = slice(c * CHUNK, (c + 1) * CHUNK)
                mixed = jnp.dot(wc, v_scr[rows, cols].astype(BF16), preferred_element_type=F32)
                mixed = mixed + bias_ref[:, cols]
                p_scr[rows, cols] = (u_scr[rows, cols] * mixed).astype(BF16)
    o_ref[...] = x + jnp.dot(p_scr[...], wout_ref[...], preferred_element_type=F32)


def _mixer_a(x, ng, win, lng, lnb, mix, bias, wout, *, tm, single_token_chunks):
    n = x.shape[0]
    out_shape = [jax.ShapeDtypeStruct((n, D_MODEL), F32)]
    out_specs = [_rows(tm, D_MODEL)]
    if single_token_chunks:
        out_shape.append(jax.ShapeDtypeStruct((n, D_GATE), F32))
        out_specs.append(_rows(tm, D_GATE))
    return pl.pallas_call(
        functools.partial(_mixer_a_kernel, tm=tm, single_token_chunks=single_token_chunks),
        grid=(n // tm,),
        in_specs=[_rows(tm, D_MODEL), _resident(ng.shape), _resident(win.shape), _resident(lng.shape),
                  _resident(lnb.shape), _resident(mix.shape), _resident(bias.shape), _resident(wout.shape)],
        out_specs=out_specs,
        out_shape=out_shape,
        scratch_shapes=[pltpu.VMEM((tm, D_GATE), F32), pltpu.VMEM((tm, D_GATE), F32),
                        pltpu.VMEM((tm, D_GATE), BF16)],
        compiler_params=_cparams(),
        name="mixer_a_sample" if single_token_chunks else "mixer_a_prompt",
    )(x, ng, win, lng, lnb, mix, bias, wout)


def _route_rows(x, g, whi_t, wlo_t, bias_col, run_scr):
    tm = x.shape[0]
    h = _rms(x, g)
    hh = h.astype(BF16)
    hl = (h - hh.astype(F32)).astype(BF16)
    nt = (((1,), (1,)), ((), ()))
    logits = (lax.dot_general(whi_t, hh, nt, preferred_element_type=F32)
              + lax.dot_general(wlo_t, hh, nt, preferred_element_type=F32)
              + lax.dot_general(whi_t, hl, nt, preferred_element_type=F32) + bias_col)
    e = lax.broadcasted_iota(I32, (N_EXPERTS, tm), 0).astype(F32)
    hot, vals, idxs = [], [], []
    for _ in range(TOP_K):
        m = jnp.max(logits, axis=0, keepdims=True)
        idx = jnp.min(jnp.where(logits == m, e, float(N_EXPERTS)), axis=0, keepdims=True)
        oh = e == idx
        hot.append(oh)
        vals.append(m)
        idxs.append(idx)
        logits = jnp.where(oh, -jnp.inf, logits)
    ex = [jnp.exp(v - vals[0]) for v in vals]
    den = ex[0] + ex[1] + ex[2] + ex[3]
    picked = jnp.zeros((N_EXPERTS, tm), F32)
    for oh in hot:
        picked = picked + jnp.where(oh, 1.0, 0.0)
    r = lax.broadcasted_iota(I32, (tm, tm), 0)
    c = lax.broadcasted_iota(I32, (tm, tm), 1)
    earlier = jnp.where(r < c, 1.0, 0.0).astype(BF16)
    run = run_scr[:, 0:1]
    pos = jnp.dot(picked.astype(BF16), earlier, preferred_element_type=F32) + run
    run_scr[:, 0:1] = run + jnp.sum(picked, axis=1, keepdims=True)
    row = lax.broadcasted_iota(I32, (REC_ROWS, tm), 0)
    rec = jnp.zeros((REC_ROWS, tm), F32)
    for k in range(TOP_K):
        pos_k = jnp.sum(jnp.where(hot[k], pos, 0.0), axis=0, keepdims=True)
        rec = jnp.where(row == k, idxs[k], rec)
        rec = jnp.where(row == TOP_K + k, pos_k, rec)
        rec = jnp.where(row == 2 * TOP_K + k, ex[k] / den, rec)
    return h, rec


def _router_kernel(xp_ref, xs_ref, g_ref, whi_ref, wlo_ref, b_ref, h_ref, slabp_ref, slabs_ref, rec_ref, cnt_ref,
                   run_scr, *, n_tiles, n_s):
    i = pl.program_id(0)

    @pl.when(i == 0)
    def _():
        run_scr[...] = jnp.zeros_like(run_scr)

    def pad_rows(rec):
        return jnp.concatenate([rec, jnp.zeros((SLAB - REC_ROWS, rec.shape[1]), F32)], axis=0)

    @pl.when(i < n_tiles)
    def _():
        h, rec = _route_rows(xp_ref[...], g_ref[...], whi_ref[...], wlo_ref[...], b_ref[...], run_scr)
        _store_packed_tiles(h_ref, h, h.shape[0])
        slabp_ref[...] = pad_rows(rec).T
        rec_ref[...] = rec[0:2 * TOP_K, :]

    @pl.when(i == n_tiles)
    def _():
        h, rec = _route_rows(xs_ref[...], g_ref[...], whi_ref[...], wlo_ref[...], b_ref[...], run_scr)
        h_ref[n_s * PACKED_SUBLANES:, :] = jnp.zeros((h_ref.shape[0] - n_s * PACKED_SUBLANES, V7X_LANES), U32)
        _store_packed_tiles(h_ref, h, n_s)
        slabs_ref[...] = pad_rows(rec).T
        rec_ref[...] = jnp.zeros_like(rec_ref)
        rec_ref[:, 0:n_s] = rec[0:2 * TOP_K, :]

    cnt_ref[...] = run_scr[...]


def _store_packed_tiles(ref, x, n):
    half = D_MODEL // 2
    lo = pltpu.bitcast(x[:, :half].astype(BF16).astype(F32), U32)
    hi = pltpu.bitcast(x[:, half:].astype(BF16).astype(F32), U32)
    word = lax.shift_right_logical(lo, jnp.uint32(16)) | (hi & jnp.uint32(0xFFFF0000))
    for s in range(PACKED_SUBLANES):
        ref[pl.ds(s, n, stride=PACKED_SUBLANES), :] = word[:, s * V7X_LANES:(s + 1) * V7X_LANES]


def _load_packed_tiles(ref, n):
    words = [ref[pl.ds(s, n, stride=PACKED_SUBLANES), :] for s in range(PACKED_SUBLANES)]
    lo = [pltpu.bitcast(w << jnp.uint32(16), F32) for w in words]
    hi = [pltpu.bitcast(w & jnp.uint32(0xFFFF0000), F32) for w in words]
    return jnp.concatenate(lo + hi, axis=1).astype(BF16)


def _store_token_tiles(ref, x, n):
    for s in range(TOKEN_SUBLANES):
        ref[pl.ds(s, n, stride=TOKEN_SUBLANES), :] = x[:, s * V7X_LANES:(s + 1) * V7X_LANES]


def _router(xp, xs, g, whi, wlo, b, *, tm):
    n_p, n_s = xp.shape[0], xs.shape[0]
    n_tiles = n_p // tm
    last = n_tiles - 1
    return pl.pallas_call(
        functools.partial(_router_kernel, n_tiles=n_tiles, n_s=n_s),
        grid=(n_tiles + 1,),
        in_specs=[pl.BlockSpec((tm, D_MODEL), lambda i: (jnp.minimum(i, last), 0)), _resident(xs.shape),
                  _resident(g.shape), _resident(whi.shape), _resident(wlo.shape), _resident(b.shape)],
        out_specs=[_rows(tm * PACKED_SUBLANES, V7X_LANES),
                   pl.BlockSpec((tm, SLAB), lambda i: (jnp.minimum(i, last), 0)),
                   pl.BlockSpec((n_s, SLAB), lambda i: (0, 0)),
                   pl.BlockSpec((2 * TOP_K, tm), lambda i: (0, i)),
                   pl.BlockSpec((N_EXPERTS, V7X_LANES), lambda i: (0, 0))],
        out_shape=[jax.ShapeDtypeStruct(((n_p + tm) * PACKED_SUBLANES, V7X_LANES), U32),
                   jax.ShapeDtypeStruct((n_p, SLAB), F32),
                   jax.ShapeDtypeStruct((n_s, SLAB), F32),
                   jax.ShapeDtypeStruct((2 * TOP_K, n_p + tm), F32),
                   jax.ShapeDtypeStruct((N_EXPERTS, V7X_LANES), F32)],
        scratch_shapes=[pltpu.VMEM((N_EXPERTS, V7X_LANES), F32)],
        compiler_params=_cparams(),
        name="moe_router",
    )(xp, xs, g, whi, wlo, b)


def _dispatch_kernel(fill_start_ref, fill_len_ref, dest_ref, h_ref, xs_ref, zbuf, sem, fill_sem,
                     *, tm, n_tiles, n_last, rows, n_blocks):
    i = pl.program_id(0)
    ts = PACKED_SUBLANES

    def tile_copy(r, k):
        d = dest_ref[0, 0, k * tm + r]
        src = h_ref.at[pl.ds(pl.multiple_of(r * ts, ts), ts)]
        dst = xs_ref.at[pl.ds(pl.multiple_of(d * ts, ts), ts)]
        return pltpu.make_async_copy(src, dst, sem.at[k % 2])

    def scatter(n):
        def body(r, carry):
            for k in range(TOP_K):
                tile_copy(r, k).start(priority=k % 2)
            return carry
        lax.fori_loop(0, n, body, 0, unroll=4)
        for k in range(TOP_K):
            pltpu.make_async_copy(h_ref.at[pl.ds(0, n * ts)], xs_ref.at[pl.ds(0, n * ts)], sem.at[k % 2]).wait()

    def fill_copy(e, bit):
        size = 1 << bit
        length = fill_len_ref[e]
        done = length - (length & (2 * size - 1))
        off = pl.multiple_of((fill_start_ref[e] + done) * ts, ts)
        return (length & size) != 0, pltpu.make_async_copy(zbuf.at[pl.ds(0, size * ts)],
                                                            xs_ref.at[pl.ds(off, size * ts)], fill_sem)

    def tail_copy(blk):
        off = pl.multiple_of(blk * (rows * ts), rows * ts)
        return pltpu.make_async_copy(zbuf, xs_ref.at[pl.ds(off, rows * ts)], fill_sem)

    def for_each_fill(act):
        for e in range(N_EXPERTS):
            for bit in range(FILL_BITS):
                go, cp = fill_copy(e, bit)

                @pl.when(go)
                def _():
                    act(cp)

        def body(blk, carry):
            act(tail_copy(blk))
            return carry
        lax.fori_loop(fill_start_ref[N_EXPERTS], n_blocks, body, 0)

    @pl.when(i == 0)
    def _():
        zbuf[...] = jnp.zeros_like(zbuf)
        for_each_fill(lambda cp: cp.start())

    @pl.when(i < n_tiles - 1)
    def _():
        scatter(tm)

    @pl.when(i == n_tiles - 1)
    def _():
        scatter(n_last)
        for_each_fill(lambda cp: cp.wait())


def _dispatch(fill_start, fill_len, dest3, h_tiles, n_all, n_blocks, *, tm, rows):
    n_tiles = dest3.shape[0]
    n_last = n_all - (n_tiles - 1) * tm
    ts = PACKED_SUBLANES
    n_slots = n_blocks * rows
    grid_spec = pltpu.PrefetchScalarGridSpec(
        num_scalar_prefetch=2,
        grid=(n_tiles,),
        in_specs=[pl.BlockSpec((1, 1, TOP_K * tm), lambda i, *_: (i, 0, 0), memory_space=pltpu.SMEM),
                  _rows(tm * ts, V7X_LANES)],
        out_specs=pl.BlockSpec(memory_space=pl.ANY),
        scratch_shapes=[pltpu.VMEM((rows * ts, V7X_LANES), U32),
                        pltpu.SemaphoreType.DMA((2,)), pltpu.SemaphoreType.DMA],
    )
    return pl.pallas_call(
        functools.partial(_dispatch_kernel, tm=tm, n_tiles=n_tiles, n_last=n_last, rows=rows, n_blocks=n_blocks),
        grid_spec=grid_spec,
        out_shape=jax.ShapeDtypeStruct((n_slots * ts, V7X_LANES), U32),
        compiler_params=_cparams(),
        name="moe_dispatch",
    )(fill_start, fill_len, dest3, h_tiles)


def _expert_kernel(be_ref, nact_ref, run_ref, rune_ref, x_ref, wgu_hbm, wdn_hbm, *refs, rows, layer):
    bps = EXPERT_BLOCKS_PER_STEP
    bias_refs, (o_ref, wgu_f32, wdn_f32, wsem, wgu_bf, wdn_bf) = refs[:2 * bps], refs[2 * bps:]
    nact = nact_ref[0]

    def weight_copies(r, slot):
        e = rune_ref[r]
        return (pltpu.make_async_copy(wgu_hbm.at[layer, e], wgu_f32.at[slot], wsem.at[0, slot]),
                pltpu.make_async_copy(wdn_hbm.at[layer, e], wdn_f32.at[slot], wsem.at[1, slot]))

    @pl.when(pl.program_id(0) == 0)
    def _():
        for cp in weight_copies(0, 0):
            cp.start()

    for part in range(bps):
        _expert_block(pl.program_id(0) * bps + part, nact, run_ref, rune_ref, weight_copies,
                      x_ref.at[pl.ds(part * rows * PACKED_SUBLANES, rows * PACKED_SUBLANES)],
                      bias_refs[2 * part], bias_refs[2 * part + 1],
                      o_ref.at[pl.ds(part * rows * TOKEN_SUBLANES, rows * TOKEN_SUBLANES)],
                      wgu_f32, wdn_f32, wgu_bf, wdn_bf, rows)


def _expert_block(b, nact, run_ref, rune_ref, weight_copies, x_ref, bgu_ref, bdn_ref, o_ref,
                  wgu_f32, wdn_f32, wgu_bf, wdn_bf, rows):
    @pl.when(b < nact)
    def _():
        r = run_ref[b]
        slot = lax.rem(r, 2)
        changed = jnp.logical_or(b == 0, r != run_ref[jnp.maximum(b - 1, 0)])

        @pl.when(changed)
        def _():
            for cp in weight_copies(r, slot):
                cp.wait()

            @pl.when(r + 1 < rune_ref[N_EXPERTS])
            def _():
                for cp in weight_copies(r + 1, 1 - slot):
                    cp.start()

            step = 256
            for c in range(D_MODEL // step):
                wgu_bf[c * step:(c + 1) * step, :] = wgu_f32[slot, c * step:(c + 1) * step, :].astype(BF16)
                wdn_bf[c * step:(c + 1) * step, :] = wdn_f32[slot, c * step:(c + 1) * step, :].astype(BF16)

        xb = _load_packed_tiles(x_ref, rows)
        nc = 512
        chunks = range(D_EXPERT // nc)
        gcols = [slice(c * nc, (c + 1) * nc) for c in chunks]
        ucols = [slice(D_EXPERT + c * nc, D_EXPERT + (c + 1) * nc) for c in chunks]
        gates = [jnp.dot(xb, wgu_bf[:, gcols[c]], preferred_element_type=F32) + bgu_ref[:, gcols[c]] for c in chunks]
        ups = [jnp.dot(xb, wgu_bf[:, ucols[c]], preferred_element_type=F32) + bgu_ref[:, ucols[c]] for c in chunks]
        acts = []
        for gate, up in zip(gates, ups):
            gate = jnp.minimum(gate, SWIGLU_LIMIT)
            up = jnp.clip(up, -SWIGLU_LIMIT, SWIGLU_LIMIT)
            acts.append(((up + 1.0) * gate * (1.0 / (1.0 + jnp.exp(-SWIGLU_ALPHA * gate)))).astype(BF16))
        out = None
        for c in chunks:
            part = jnp.dot(acts[c], wdn_bf[gcols[c], :], preferred_element_type=F32)
            out = part if out is None else out + part
        _store_token_tiles(o_ref, out + bdn_ref[...], rows)

    @pl.when(b >= nact)
    def _():
        o_ref[...] = jnp.zeros_like(o_ref)


def _experts(block_e, n_active, block_run, run_e, xs_tiles, wgu, bgu, wdn, bdn, *, rows, layer):
    ts = TOKEN_SUBLANES
    bps = EXPERT_BLOCKS_PER_STEP
    nb = xs_tiles.shape[0] // (rows * PACKED_SUBLANES)
    assert nb % bps == 0
    bias_specs, bias_args = [], []
    for part in range(bps):
        def index(s, be, *_, part=part):
            return (layer, be[s * bps + part], 0, 0)
        bias_specs += [pl.BlockSpec((None, None, 1, 2 * D_EXPERT), index), pl.BlockSpec((None, None, 1, D_MODEL), index)]
        bias_args += [bgu, bdn]
    grid_spec = pltpu.PrefetchScalarGridSpec(
        num_scalar_prefetch=4,
        grid=(nb // bps,),
        in_specs=[
            pl.BlockSpec((bps * rows * PACKED_SUBLANES, V7X_LANES),
                         lambda s, be, na, *_: (jnp.minimum(s, (na[0] - 1) // bps), 0)),
            pl.BlockSpec(memory_space=pl.ANY), pl.BlockSpec(memory_space=pl.ANY),
        ] + bias_specs,
        out_specs=pl.BlockSpec((bps * rows * ts, V7X_LANES), lambda s, *_: (s, 0)),
        scratch_shapes=[pltpu.VMEM((2, D_MODEL, 2 * D_EXPERT), F32), pltpu.VMEM((2, D_EXPERT, D_MODEL), F32),
                        pltpu.SemaphoreType.DMA((2, 2)),
                        pltpu.VMEM((D_MODEL, 2 * D_EXPERT), BF16), pltpu.VMEM((D_EXPERT, D_MODEL), BF16)],
    )
    return pl.pallas_call(
        functools.partial(_expert_kernel, rows=rows, layer=layer),
        grid_spec=grid_spec,
        out_shape=jax.ShapeDtypeStruct((nb * rows * ts, V7X_LANES), F32),
        compiler_params=_cparams(),
        name="moe_experts",
    )(block_e, n_active, block_run, run_e, xs_tiles, wgu, wdn, *bias_args)


def _combine_kernel(dest_first_ref, dest_next_ref, x_ref, slab_ref, ebuf_ref, *refs, tm, n_tiles, attn_pre):
    if attn_pre:
        (kvg_ref, wk_ref, wv_ref, kng_ref, bng_ref, wq_ref, qng_ref, cos_ref, sin_ref,
         o_ref, k_ref, kdup_ref, v_ref, q_ref, rbuf, sem) = refs
    else:
        o_ref, rbuf, sem = refs
    i = pl.program_id(0)
    slot = lax.rem(i, 2)
    ts = TOKEN_SUBLANES

    def gather(dest_ref, s):
        def body(r, carry):
            for k in range(TOP_K):
                d = dest_ref[0, 0, k * tm + r]
                src = ebuf_ref.at[pl.ds(pl.multiple_of(d * ts, ts), ts)]
                dst = rbuf.at[s, k, pl.ds(pl.multiple_of(r * ts, ts), ts)]
                pltpu.make_async_copy(src, dst, sem.at[s, k % 2]).start(priority=k % 2)
            return carry
        lax.fori_loop(0, tm, body, 0, unroll=4)

    @pl.when(i == 0)
    def _():
        gather(dest_first_ref, 0)

    if attn_pre:
        todo = [(r, k) for r in range(tm) for k in range(TOP_K)]

        def tick(n):
            for r, k in todo[:n]:
                d = dest_next_ref[0, 0, k * tm + r]
                src = ebuf_ref.at[pl.ds(pl.multiple_of(d * ts, ts), ts)]
                pltpu.make_async_copy(src, rbuf.at[1 - slot, k, pl.ds(r * ts, ts)],
                                      sem.at[1 - slot, k % 2]).start(priority=k % 2)
            del todo[:n]
    else:
        @pl.when(i + 1 < n_tiles)
        def _():
            gather(dest_next_ref, 1 - slot)

        def tick(n):
            pass

    def wait_slot(s):
        for k in range(TOP_K):
            pltpu.make_async_copy(ebuf_ref.at[pl.ds(0, tm * ts)], rbuf.at[s, k], sem.at[s, k % 2]).wait()

    wait_slot(slot)
    slab = slab_ref[...]
    gates = [slab[:, 2 * TOP_K + k:2 * TOP_K + k + 1] for k in range(TOP_K)]
    per_phase = TOP_K * tm // (4 * ts)
    for s in range(ts):
        cols = slice(s * V7X_LANES, (s + 1) * V7X_LANES)
        acc = x_ref[:, cols]
        for k in range(TOP_K):
            acc = acc + gates[k] * rbuf[slot, k, pl.ds(s, tm, stride=ts), :]
        o_ref[:, cols] = acc
        tick(per_phase)
    if attn_pre:
        x2 = o_ref[...]
        cos2, sin2 = cos_ref[...], sin_ref[...]
        kv_phases = 1 + N_KV_HEADS
        q_phases = 1 + D_MODEL // PAIR
        per_phase = len(todo) // (kv_phases + q_phases)
        _shared_kv(x2, kvg_ref[...], wk_ref, wv_ref, kng_ref[...], cos2, sin2, k_ref, kdup_ref, v_ref,
                   tick=lambda: tick(per_phase))
        hq = _rms(x2, bng_ref[...]).astype(BF16)
        q = jnp.dot(hq, wq_ref[...], preferred_element_type=F32)
        tick(per_phase)
        for p in range(D_MODEL // PAIR):
            cols = slice(p * PAIR, (p + 1) * PAIR)
            q_ref[:, cols] = (_pair_norm_rope(q[:, cols], qng_ref[...], cos2, sin2) * HEAD_DIM ** -0.5).astype(BF16)
            tick(per_phase)
        tick(len(todo))

        @pl.when(i == n_tiles - 1)
        def _():
            wait_slot(1 - slot)


def _shared_kv(x, kvg, wk_ref, wv_ref, kng2, cos2, sin2, k_ref, kdup_ref, v_ref, tick=lambda: None):
    h = _rms(x, kvg).astype(BF16)
    kd = jnp.dot(h, wk_ref[...], preferred_element_type=F32)
    v_ref[...] = jnp.dot(h, wv_ref[...], preferred_element_type=F32)
    tick()
    first_half = lax.broadcasted_iota(I32, (x.shape[0], PAIR), 1) < HEAD_DIM
    slabs = []
    for p in range(N_KV_HEADS):
        cols = slice(p * PAIR, (p + 1) * PAIR)
        slab = _pair_norm_rope(kd[:, cols], kng2, cos2, sin2)
        kdup_ref[:, cols] = slab
        slabs.append(slab)
        tick()
    for p in range(D_KV // PAIR):
        k_ref[:, p * PAIR:(p + 1) * PAIR] = jnp.where(first_half, slabs[2 * p], slabs[2 * p + 1])


def _combine(x, slab, dest3, ebuf, *, tm, attn_pre=None):
    n = x.shape[0]
    n_tiles = n // tm
    in_specs = [
        pl.BlockSpec((1, 1, TOP_K * tm), lambda i: (0, 0, 0), memory_space=pltpu.SMEM),
        pl.BlockSpec((1, 1, TOP_K * tm), lambda i: (jnp.minimum(i + 1, n_tiles - 1), 0, 0),
                     memory_space=pltpu.SMEM),
        _rows(tm, D_MODEL), _rows(tm, SLAB), pl.BlockSpec(memory_space=pl.ANY),
    ]
    args = [dest3, dest3, x, slab, ebuf]
    out_specs = [_rows(tm, D_MODEL)]
    out_shape = [jax.ShapeDtypeStruct((n, D_MODEL), F32)]
    if attn_pre is not None:
        *weights, cos2, sin2, table_tiles = attn_pre
        table_spec = pl.BlockSpec((tm, PAIR), lambda i: (lax.rem(i, table_tiles), 0))
        in_specs += [_resident(w.shape) for w in weights] + [table_spec, table_spec]
        args += list(weights) + [cos2, sin2]
        out_specs += [_rows(tm, D_KV), _rows(tm, 2 * D_KV), _rows(tm, D_KV), _rows(tm, D_MODEL)]
        out_shape += [jax.ShapeDtypeStruct((n, D_KV), F32), jax.ShapeDtypeStruct((n, 2 * D_KV), F32),
                      jax.ShapeDtypeStruct((n, D_KV), F32), jax.ShapeDtypeStruct((n, D_MODEL), BF16)]
    return pl.pallas_call(
        functools.partial(_combine_kernel, tm=tm, n_tiles=n_tiles, attn_pre=attn_pre is not None),
        grid=(n_tiles,),
        in_specs=in_specs, out_specs=out_specs, out_shape=out_shape,
        scratch_shapes=[pltpu.VMEM((2, TOP_K, tm * TOKEN_SUBLANES, V7X_LANES), F32),
                        pltpu.SemaphoreType.DMA((2, 2))],
        compiler_params=_cparams(),
        name="moe_combine_attn_pre" if attn_pre is not None else "moe_combine",
    )(*args)


def _moe(xs, g, w_router, b_router, wgu, bgu, wdn, bdn, layer, prompt_attn_pre=None):
    n_all = xs[0].shape[0] + xs[1].shape[0]
    w_t = w_router.T
    whi = w_t.astype(BF16)
    wlo = (w_t - whi.astype(F32)).astype(BF16)
    h_all, slab_p, slab_s, rec, cnt = _router(xs[0], xs[1], g.reshape(1, D_MODEL), whi, wlo,
                                              b_router.reshape(N_EXPERTS, 1), tm=TM_DENSE)
    slabs = [slab_p, slab_s]
    idx = rec[:TOP_K, :n_all].astype(I32)
    pos = rec[TOP_K:, :n_all].astype(I32)
    counts = cnt[:, 0].astype(I32)
    rows = MOE_ROWS
    nb = (n_all * TOP_K + N_EXPERTS * (rows - 1)) // rows
    nb = -(-nb // EXPERT_BLOCKS_PER_STEP) * EXPERT_BLOCKS_PER_STEP
    padded =(counts + rows - 1) // rows * rows
    pad_end = jnp.cumsum(padded)
    pad_start = pad_end - padded
    eid = jnp.arange(N_EXPERTS, dtype=I32)
    has_rows = counts > 0
    dest = pos + jnp.sum(jnp.where(idx[None] == eid[:, None, None], pad_start[:, None, None], 0), axis=0)
    n_active = (pad_end[-1] // rows).astype(I32)
    blk = jnp.arange(nb, dtype=I32)
    block_e = jnp.sum((pad_end[None, :] <= (blk * rows)[:, None]).astype(I32), axis=1)
    last_e = jnp.max(jnp.where(has_rows, eid, 0))
    block_e = jnp.where(blk < n_active, jnp.minimum(block_e, N_EXPERTS - 1), last_e)

    def tile_table(d, tm):
        n_tiles = -(-d.shape[1] // tm)
        d = jnp.pad(d, ((0, 0), (0, n_tiles * tm - d.shape[1])))
        return d.reshape(TOP_K, n_tiles, tm).transpose(1, 0, 2).reshape(n_tiles, 1, TOP_K * tm)

    fill_start = jnp.concatenate([pad_start + counts, n_active.reshape(1)])
    xs_tiles = _dispatch(fill_start, padded - counts, tile_table(dest, TM_DISPATCH), h_all,
                         n_all, nb, tm=TM_DISPATCH, rows=rows)
    rank = jnp.cumsum(has_rows.astype(I32)) - 1
    run_e = jnp.sum(jnp.where(has_rows[None, :] & (rank[None, :] == eid[:, None]), eid[None, :], 0), axis=1)
    run_e = jnp.concatenate([run_e, jnp.sum(has_rows.astype(I32)).reshape(1)]).astype(I32)
    block_run = jnp.sum(jnp.where(block_e[:, None] == eid, rank, 0), axis=1).astype(I32)
    n_l = wgu.shape[0]
    ebuf = _experts(block_e, n_active.reshape(1), block_run, run_e, xs_tiles,
                    wgu, bgu.reshape(n_l, N_EXPERTS, 1, 2 * D_EXPERT), wdn, bdn.reshape(n_l, N_EXPERTS, 1, D_MODEL),
                    rows=rows, layer=layer)
    outs, off = [], 0
    for x, slab, pre in zip(xs, slabs, (prompt_attn_pre, None)):
        n = x.shape[0]
        tm = min(TM_COMBINE, n)
        outs.append(_combine(x, slab, tile_table(dest[:, off:off + n], tm), ebuf, tm=tm, attn_pre=pre))
        off += n
    return outs


def _kvq_sample_kernel(x_ref, kvg_ref, wk_ref, wv_ref, kng_ref, cos_ref, sin_ref, bng_ref, wq_ref, qng_ref,
                       k_ref, kdup_ref, v_ref, q_ref):
    x = x_ref[...]
    cos2, sin2 = cos_ref[...], sin_ref[...]
    _shared_kv(x, kvg_ref[...], wk_ref, wv_ref, kng_ref[...], cos2, sin2, k_ref, kdup_ref, v_ref)
    hq = _rms(x, bng_ref[...]).astype(BF16)
    q = jnp.dot(hq, wq_ref[...], preferred_element_type=F32)
    for p in range(D_MODEL // PAIR):
        cols = slice(p * PAIR, (p + 1) * PAIR)
        q_ref[:, cols] = _pair_norm_rope(q[:, cols], qng_ref[...], cos2, sin2)


def _kvq_sample(x, kvg, wk, wv, kng2, cos2, sin2, bng, wq, qng2):
    n = x.shape[0]
    args = [x, kvg, wk, wv, kng2, cos2, sin2, bng, wq, qng2]
    return pl.pallas_call(
        _kvq_sample_kernel,
        grid=(1,),
        in_specs=[_resident(a.shape) for a in args],
        out_specs=[_rows(n, D_KV), _rows(n, 2 * D_KV), _rows(n, D_KV), _rows(n, D_MODEL)],
        out_shape=[jax.ShapeDtypeStruct((n, D_KV), F32), jax.ShapeDtypeStruct((n, 2 * D_KV), F32),
                   jax.ShapeDtypeStruct((n, D_KV), F32), jax.ShapeDtypeStruct((n, D_MODEL), F32)],
        compiler_params=_cparams(),
        name="kvq_proj_sample",
    )(*args)


def _attn_prompt_kernel(sinks_ref, x_ref, q_ref, kc_ref, kp_ref, vc_ref, vp_ref, wo_ref, o_ref, a_scr,
                        *, tm, tiles_per_seq):
    i = pl.program_id(0)

    kj = lax.broadcasted_iota(I32, (2 * WINDOW, WINDOW), 0)
    qi = lax.broadcasted_iota(I32, (2 * WINDOW, WINDOW), 1)
    diff = qi + WINDOW - kj
    band = jnp.logical_and(diff >= 0, diff < WINDOW)
    first_half = lax.broadcasted_iota(I32, (WINDOW, PAIR), 1) < HEAD_DIM
    keep = (jnp.where(first_half, 1.0, 0.0).astype(BF16), jnp.where(first_half, 0.0, 1.0).astype(BF16))
    first_kj = jnp.where(lax.rem(i, tiles_per_seq) == 0, WINDOW, 0)

    for blk in range(tm // WINDOW):
        rows = slice(blk * WINDOW, (blk + 1) * WINDOW)
        if blk == 0:
            kk = jnp.concatenate([kp_ref[...], kc_ref[rows, :]], axis=0).astype(BF16)
            vv = jnp.concatenate([vp_ref[...], vc_ref[rows, :]], axis=0)
            valid = jnp.logical_and(band, kj >= first_kj)
        else:
            win = slice((blk - 1) * WINDOW, (blk + 1) * WINDOW)
            kk = kc_ref[win, :].astype(BF16)
            vv = vc_ref[win, :]
            valid = band
        for kvh in range(N_KV_HEADS):
            kk_h = kk[:, kvh * PAIR:(kvh + 1) * PAIR]
            vcols = slice((kvh // 2) * PAIR, (kvh // 2 + 1) * PAIR)
            vt = vv[:, vcols].T.astype(BF16)
            ch = slice((kvh % 2) * HEAD_DIM, (kvh % 2 + 1) * HEAD_DIM)
            outs = []
            for g in range(Q_PER_KV):
                head = kvh * Q_PER_KV + g
                qm = q_ref[rows, (head // 2) * PAIR:(head // 2 + 1) * PAIR] * keep[head % 2]
                sink = sinks_ref[head]
                st = lax.dot_general(kk_h, qm, (((1,), (1,)), ((), ())), preferred_element_type=F32)
                st = jnp.where(valid, st, -jnp.inf)
                m = jnp.maximum(jnp.max(st, axis=0, keepdims=True), sink)
                p = jnp.exp(st - m)
                den = jnp.sum(p, axis=0, keepdims=True) + jnp.exp(sink - m)
                ot = jnp.dot(vt, p.astype(BF16), preferred_element_type=F32)
                outs.append(ot[ch, :] * (1.0 / den))
            for pq in range(Q_PER_KV // 2):
                pair = (kvh * Q_PER_KV) // 2 + pq
                both = jnp.concatenate([outs[2 * pq], outs[2 * pq + 1]], axis=0)
                a_scr[rows, pair * PAIR:(pair + 1) * PAIR] = both.T.astype(BF16)
    o_ref[...] = x_ref[...] + jnp.dot(a_scr[...], wo_ref[...], preferred_element_type=F32)


def _attn_prompt(x, q, kdup, v, sinks, wo, *, tm, seq):
    n = x.shape[0]
    tiles_per_seq = seq // tm
    blocks_per_tile = tm // WINDOW

    def prev_spec(width):
        return pl.BlockSpec((WINDOW, width), lambda i: (jnp.maximum(i * blocks_per_tile - 1, 0), 0))

    return pl.pallas_call(
        functools.partial(_attn_prompt_kernel, tm=tm, tiles_per_seq=tiles_per_seq),
        grid=(n // tm,),
        in_specs=[pl.BlockSpec(memory_space=pltpu.SMEM), _rows(tm, D_MODEL), _rows(tm, D_MODEL),
                  _rows(tm, 2 * D_KV), prev_spec(2 * D_KV), _rows(tm, D_KV), prev_spec(D_KV), _resident(wo.shape)],
        out_specs=_rows(tm, D_MODEL),
        out_shape=jax.ShapeDtypeStruct((n, D_MODEL), F32),
        scratch_shapes=[pltpu.VMEM((tm, D_MODEL), BF16)],
        compiler_params=_cparams(),
        name="attn_prompt",
    )(sinks, x, q, kdup, kdup, v, v, wo)


def _attn_sample_kernel(q_ref, ck_ref, cv_ref, kn_ref, vn_ref, sink_ref, o_ref, *, per_step, win):
    rowi = lax.broadcasted_iota(I32, (N_HEADS, D_KV), 0)
    lanei = lax.broadcasted_iota(I32, (N_HEADS, D_KV), 1)
    group = rowi >> 2
    own_block = group == (lanei >> 6)
    j = lax.broadcasted_iota(I32, (N_HEADS, win), 1)
    diff = win - j
    valid = jnp.logical_and(diff >= 0, diff < WINDOW)
    sink = sink_ref[...]
    scale = HEAD_DIM ** -0.5
    qms = [jnp.where(own_block, q_ref[b], 0.0) for b in range(per_step)]
    ss = [lax.dot_general(qm.astype(BF16), ck_ref[b].astype(BF16), (((1,), (1,)), ((), ())),
                          preferred_element_type=F32) * scale for b, qm in enumerate(qms)]
    ps, p_news, dens = [], [], []
    for b in range(per_step):
        s = jnp.where(valid, ss[b], -jnp.inf)
        s_new = jnp.sum(qms[b] * kn_ref[b:b + 1, :], axis=-1, keepdims=True) * scale
        m = jnp.maximum(jnp.maximum(jnp.max(s, axis=-1, keepdims=True), s_new), sink)
        p = jnp.exp(s - m)
        p_new = jnp.exp(s_new - m)
        ps.append(p.astype(BF16))
        p_news.append(p_new)
        dens.append(jnp.sum(p, axis=-1, keepdims=True) + p_new + jnp.exp(sink - m))
    os_ = [jnp.dot(ps[b], cv_ref[b].astype(BF16), preferred_element_type=F32) for b in range(per_step)]
    for b in range(per_step):
        o = (os_[b] + p_news[b] * vn_ref[b:b + 1, :]) * (1.0 / dens[b])
        res = o
        for sft in range(1, N_KV_HEADS):
            res = jnp.where(group == sft, pltpu.roll(o, D_KV - sft * HEAD_DIM, 1), res)
        o_ref[b] = res[:, :HEAD_DIM]


def _attn_sample(q4, ck, cv, kn, vn, sink_col, *, per_step):
    nb, win = ck.shape[0], ck.shape[1]
    return pl.pallas_call(
        functools.partial(_attn_sample_kernel, per_step=per_step, win=win),
        grid=(nb // per_step,),
        in_specs=[pl.BlockSpec((per_step, N_HEADS, D_KV), lambda i: (i, 0, 0)),
                  pl.BlockSpec((per_step, win, D_KV), lambda i: (i, 0, 0)),
                  pl.BlockSpec((per_step, win, D_KV), lambda i: (i, 0, 0)),
                  _rows(per_step, D_KV), _rows(per_step, D_KV), _resident(sink_col.shape)],
        out_specs=pl.BlockSpec((per_step, N_HEADS, HEAD_DIM), lambda i: (i, 0, 0)),
        out_shape=jax.ShapeDtypeStruct((nb, N_HEADS, HEAD_DIM), F32),
        compiler_params=_cparams(),
        name="attn_sample",
    )(q4, ck, cv, kn, vn, sink_col)


def _residual_proj_kernel(x_ref, a_ref, w_ref, o_ref):
    o_ref[...] = x_ref[...] + jnp.dot(a_ref[...].astype(BF16), w_ref[...], preferred_element_type=F32)


def _residual_proj(x, a, w):
    n = x.shape[0]
    return pl.pallas_call(
        _residual_proj_kernel,
        grid=(1,),
        in_specs=[_rows(n, D_MODEL), _rows(n, a.shape[1]), _resident(w.shape)],
        out_specs=_rows(n, D_MODEL),
        out_shape=jax.ShapeDtypeStruct((n, D_MODEL), F32),
        compiler_params=_cparams(),
        name="attn_out_sample",
    )(x, a, w)


def _rope_tables(pos):
    half = ROPE_DIM // 2
    inv_freq = np.power(np.float32(ROPE_THETA), -np.arange(half, dtype=np.float32) / np.float32(half))
    ang = (pos.astype(np.float32)[:, None] * inv_freq[None, :].astype(np.float32)).astype(np.float32)
    cos, sin = np.cos(ang.astype(np.float64)), np.sin(ang.astype(np.float64))
    n = pos.shape[0]
    rest = HEAD_DIM - ROPE_DIM
    cos_h = np.concatenate([cos, cos, np.ones((n, rest))], axis=1)
    sin_h = np.concatenate([-sin, sin, np.zeros((n, rest))], axis=1)
    return (jnp.asarray(np.tile(cos_h, (1, 2)), dtype=F32), jnp.asarray(np.tile(sin_h, (1, 2)), dtype=F32))


def kernel(x_prompt, x_sample, cache_k, cache_v, a_norm_g, a_w_in, a_ln_g, a_ln_b, a_w_s, a_b_s, a_w_out, kv_norm_g, w_k, w_v, k_norm_g, b_norm_g, w_q, q_norm_g, sinks, w_o, ffn_norm_g, w_router, b_router, w_gu, b_gu, w_down, b_down):
    bsz, seq, _ = x_prompt.shape
    dec_b, dec_seq, _ = x_sample.shape
    assert dec_seq == 1 and seq % TM_DENSE == 0 and a_norm_g.shape[0] == 1 and b_norm_g.shape[0] == 1
    win = cache_k.shape[1]
    n_p, n_s = bsz * seq, dec_b * dec_seq
    xp = x_prompt.reshape(n_p, D_MODEL)
    xs = x_sample.reshape(n_s, D_MODEL)

    ng = a_norm_g[0].reshape(1, D_MODEL)
    win_bf = a_w_in[0].astype(BF16)
    wout_bf = a_w_out[0].astype(BF16)
    lng = a_ln_g[0].reshape(1, D_GATE)
    lnb = a_ln_b[0].reshape(1, D_GATE)
    bias_full = jnp.repeat(a_b_s[0].T, GROUP_DIM_A, axis=1)
    diag_row = jnp.repeat(a_w_s[0][:, 0, 0], GROUP_DIM_A).reshape(1, D_GATE)
    (xp,) = _mixer_a(xp, ng, win_bf, lng, lnb, a_w_s[0], bias_full, wout_bf, tm=TM_DENSE, single_token_chunks=False)
    xs, v_rows = _mixer_a(xs, ng, win_bf, lng, lnb, diag_row, bias_full[0:1], wout_bf, tm=n_s,
                          single_token_chunks=True)

    kvg = kv_norm_g.reshape(1, D_MODEL)
    bng = b_norm_g[0].reshape(1, D_MODEL)
    w_k_dup = jnp.repeat(w_k.reshape(D_MODEL, N_KV_HEADS, 1, HEAD_DIM), 2, axis=2).reshape(D_MODEL, 2 * D_KV)
    wk_bf, wv_bf, wq_bf, wo_bf = (w.astype(BF16) for w in (w_k_dup, w_v, w_q[0], w_o[0]))
    kng2 = jnp.tile(k_norm_g, 2).reshape(1, PAIR)
    qng2 = jnp.tile(q_norm_g[0], 2).reshape(1, PAIR)
    cos_p, sin_p = _rope_tables(np.arange(seq))
    cos_s, sin_s = _rope_tables(np.full((n_s,), PAST_LEN))
    attn_pre = (kvg, wk_bf, wv_bf, kng2, bng, wq_bf, qng2, cos_p, sin_p, seq // TM_COMBINE)
    (xp, k_p, kdup_p, v_p, q_p), (xs,) = _moe([xp, xs], ffn_norm_g[0], w_router[0], b_router[0], w_gu, b_gu,
                                              w_down, b_down, 0, prompt_attn_pre=attn_pre)

    k_s, _, v_s, q_s = _kvq_sample(xs, kvg, wk_bf, wv_bf, kng2, cos_s, sin_s, bng, wq_bf, qng2)
    xp = _attn_prompt(xp, q_p, kdup_p, v_p, sinks[0], wo_bf, tm=TM_DENSE, seq=seq)
    q4 = jnp.tile(q_s.reshape(n_s, N_HEADS, HEAD_DIM), (1, 1, N_KV_HEADS))
    ck = cache_k.reshape(dec_b, win, D_KV).astype(BF16)
    cv = cache_v.reshape(dec_b, win, D_KV).astype(BF16)
    attn_s = _attn_sample(q4, ck, cv, k_s, v_s, sinks[0].reshape(N_HEADS, 1), per_step=8)
    xs = _residual_proj(xs, attn_s.reshape(n_s, D_MODEL), wo_bf)
    (xp,), (xs,) = _moe([xp, xs], ffn_norm_g[1], w_router[1], b_router[1], w_gu, b_gu, w_down, b_down, 1)

    def last_window(a):
        a = a.reshape(bsz, seq, D_KV)[:, -WINDOW:]
        return a.reshape(bsz, min(WINDOW, seq), N_KV_HEADS, HEAD_DIM)

    return (xp.reshape(bsz, seq, D_MODEL), xs.reshape(dec_b, dec_seq, D_MODEL),
            v_rows.reshape(1, dec_b, dec_seq, D_GATE), last_window(k_p), last_window(v_p),
            k_s.reshape(dec_b, dec_seq, N_KV_HEADS, HEAD_DIM), v_s.reshape(dec_b, dec_seq, N_KV_HEADS, HEAD_DIM))
```

```python
import functools

import jax
import jax.numpy as jnp
import numpy as np
from jax import lax
from jax.experimental import pallas as pl
from jax.experimental.pallas import tpu as pltpu

F32, BF16, I32 = jnp.float32, jnp.bfloat16, jnp.int32

D_MODEL = 1024
CHUNK = 128
D_GATE = 2 * D_MODEL
N_GROUPS_A = 8
GROUP_DIM_A = D_GATE // N_GROUPS_A
HEAD_DIM = 64
N_HEADS = D_MODEL // HEAD_DIM
N_KV_HEADS = N_HEADS // 4
Q_PER_KV = N_HEADS // N_KV_HEADS
D_KV = N_KV_HEADS * HEAD_DIM
WINDOW = 128
ROPE_DIM = HEAD_DIM // 4
ROPE_THETA = 500000.0
N_EXPERTS = 32
TOP_K = 4
D_EXPERT = D_MODEL
SWIGLU_LIMIT = 7.0
SWIGLU_ALPHA = 1.702
EPS = 1e-6
PAST_LEN = 8192

V7X_LANES = 128
V7X_VMEM_BYTES = 64 * 1024 * 1024
VMEM_LIMIT_BYTES = V7X_VMEM_BYTES - 8 * 1024 * 1024

TM_DENSE = 512
TM_COMBINE = 256
MOE_ROWS = 512
EXPERT_BLOCKS_PER_STEP = 1
TM_DISPATCH = 256
FILL_BITS = MOE_ROWS.bit_length() - 1
TOKEN_SUBLANES = D_MODEL // V7X_LANES
PACKED_SUBLANES = TOKEN_SUBLANES // 2
U32 = jnp.uint32
SLAB = V7X_LANES
REC_ROWS = 16
PAIR = 2 * HEAD_DIM
INV_SQRT2 = 0.7071067811865476


def _cparams():
    return pltpu.CompilerParams(dimension_semantics=("arbitrary",), vmem_limit_bytes=VMEM_LIMIT_BYTES)


def _resident(shape):
    zeros = (0,) * len(shape)
    return pl.BlockSpec(shape, lambda i, *_: zeros, pipeline_mode=pl.Buffered(1))


def _rows(tm, width, offset=0):
    return pl.BlockSpec((tm, width), lambda i, *_: (i + offset, 0))


def _rms(x, g):
    return x * lax.rsqrt(jnp.mean(x * x, axis=-1, keepdims=True) + EPS) * g


def _pair_norm_rope(x2, g2, cos2, sin2):
    lane = lax.broadcasted_iota(I32, x2.shape, 1)
    r = lax.broadcasted_iota(I32, (PAIR, PAIR), 0)
    c = lax.broadcasted_iota(I32, (PAIR, PAIR), 1)
    head_mean = jnp.where((r < HEAD_DIM) == (c < HEAD_DIM), 1.0 / HEAD_DIM, 0.0).astype(BF16)
    sq = x2 * x2
    sq_hi = sq.astype(BF16)
    sq_lo = (sq - sq_hi.astype(F32)).astype(BF16)
    ms = (jnp.dot(sq_hi, head_mean, preferred_element_type=F32)
          + jnp.dot(sq_lo, head_mean, preferred_element_type=F32))
    y = x2 * lax.rsqrt(ms + EPS) * g2
    half = ROPE_DIM // 2
    up = pltpu.roll(y, PAIR - half, 1)
    dn = pltpu.roll(y, half, 1)
    partner = jnp.where((lane & (HEAD_DIM - 1)) < half, up, dn)
    return y * cos2 + partner * sin2


def _mixer_a_kernel(x_ref, ng_ref, win_ref, lng_ref, lnb_ref, mix_ref, bias_ref, wout_ref, *refs,
                    tm, single_token_chunks):
    if single_token_chunks:
        o_ref, vout_ref, u_scr, v_scr, p_scr = refs
    else:
        o_ref, u_scr, v_scr, p_scr = refs
    x = x_ref[...]
    h = _rms(x, ng_ref[...]).astype(BF16)
    nc = 512
    for c in range(2 * D_GATE // nc):
        z = jnp.dot(h, win_ref[:, c * nc:(c + 1) * nc], preferred_element_type=F32)
        z = 0.5 * z * (1.0 + lax.erf(z * INV_SQRT2))
        if c < D_GATE // nc:
            u_scr[:, c * nc:(c + 1) * nc] = z
        else:
            v_scr[:, (c - D_GATE // nc) * nc:(c - D_GATE // nc + 1) * nc] = z
    v = v_scr[...]
    vc = v - jnp.mean(v, axis=-1, keepdims=True)
    vn = vc * lax.rsqrt(jnp.mean(vc * vc, axis=-1, keepdims=True) + EPS) * lng_ref[...] + lnb_ref[...]
    if single_token_chunks:
        vout_ref[...] = vn
        p_scr[...] = (u_scr[...] * (vn * mix_ref[...] + bias_ref[...])).astype(BF16)
    else:
        v_scr[...] = vn
        row = lax.broadcasted_iota(I32, (CHUNK, CHUNK), 0)
        col = lax.broadcasted_iota(I32, (CHUNK, CHUNK), 1)
        for g in range(N_GROUPS_A):
            wc = jnp.where(row >= col, mix_ref[g], 0.0).astype(BF16)
            cols = slice(g * GROUP_DIM_A, (g + 1) * GROUP_DIM_A)
            for c in range(tm // CHUNK):
                rows = slice(c * CHUNK, (c + 1) * CHUNK)
                mixed = jnp.dot(wc, v_scr[rows, cols].astype(BF16), preferred_element_type=F32)
                mixed = mixed + bias_ref[:, cols]
                p_scr[rows, cols] = (u_scr[rows, cols] * mixed).astype(BF16)
    o_ref[...] = x + jnp.dot(p_scr[...], wout_ref[...], preferred_element_type=F32)


def _mixer_a(x, ng, win, lng, lnb, mix, bias, wout, *, tm, single_token_chunks):
    n = x.shape[0]
    out_shape = [jax.ShapeDtypeStruct((n, D_MODEL), F32)]
    out_specs = [_rows(tm, D_MODEL)]
    if single_token_chunks:
        out_shape.append(jax.ShapeDtypeStruct((n, D_GATE), F32))
        out_specs.append(_rows(tm, D_GATE))
    return pl.pallas_call(
        functools.partial(_mixer_a_kernel, tm=tm, single_token_chunks=single_token_chunks),
        grid=(n // tm,),
        in_specs=[_rows(tm, D_MODEL), _resident(ng.shape), _resident(win.shape), _resident(lng.shape),
                  _resident(lnb.shape), _resident(mix.shape), _resident(bias.shape), _resident(wout.shape)],
        out_specs=out_specs,
        out_shape=out_shape,
        scratch_shapes=[pltpu.VMEM((tm, D_GATE), F32), pltpu.VMEM((tm, D_GATE), F32),
                        pltpu.VMEM((tm, D_GATE), BF16)],
        compiler_params=_cparams(),
        name="mixer_a_sample" if single_token_chunks else "mixer_a_prompt",
    )(x, ng, win, lng, lnb, mix, bias, wout)


def _route_rows(x, g, whi_t, wlo_t, bias_col, run_scr):
    tm = x.shape[0]
    h = _rms(x, g)
    hh = h.astype(BF16)
    h_rounded = hh.astype(F32)
    hl = (h - h_rounded).astype(BF16)
    nt = (((1,), (1,)), ((), ()))
    logits = (lax.dot_general(whi_t, hh, nt, preferred_element_type=F32)
              + lax.dot_general(wlo_t, hh, nt, preferred_element_type=F32)
              + lax.dot_general(whi_t, hl, nt, preferred_element_type=F32) + bias_col)
    e = lax.broadcasted_iota(I32, (N_EXPERTS, tm), 0).astype(F32)
    hot, vals, idxs = [], [], []
    for _ in range(TOP_K):
        m = jnp.max(logits, axis=0, keepdims=True)
        idx = jnp.min(jnp.where(logits == m, e, float(N_EXPERTS)), axis=0, keepdims=True)
        oh = e == idx
        hot.append(oh)
        vals.append(m)
        idxs.append(idx)
        logits = jnp.where(oh, -jnp.inf, logits)
    ex = [jnp.exp(v - vals[0]) for v in vals]
    den = ex[0] + ex[1] + ex[2] + ex[3]
    picked = jnp.zeros((N_EXPERTS, tm), F32)
    for oh in hot:
        picked = picked + jnp.where(oh, 1.0, 0.0)
    r = lax.broadcasted_iota(I32, (tm, tm), 0)
    c = lax.broadcasted_iota(I32, (tm, tm), 1)
    earlier = jnp.where(r < c, 1.0, 0.0).astype(BF16)
    run = run_scr[:, 0:1]
    pos = jnp.dot(picked.astype(BF16), earlier, preferred_element_type=F32) + run
    run_scr[:, 0:1] = run + jnp.sum(picked, axis=1, keepdims=True)
    row = lax.broadcasted_iota(I32, (REC_ROWS, tm), 0)
    rec = jnp.zeros((REC_ROWS, tm), F32)
    for k in range(TOP_K):
        pos_k = jnp.sum(jnp.where(hot[k], pos, 0.0), axis=0, keepdims=True)
        rec = jnp.where(row == k, idxs[k], rec)
        rec = jnp.where(row == TOP_K + k, pos_k, rec)
        rec = jnp.where(row == 2 * TOP_K + k, ex[k] / den, rec)
    return h_rounded, rec


def _router_kernel(xp_ref, xs_ref, g_ref, whi_ref, wlo_ref, b_ref, h_ref, slabp_ref, slabs_ref, rec_ref, cnt_ref,
                   run_scr, *, n_tiles, n_s):
    i = pl.program_id(0)

    @pl.when(i == 0)
    def _():
        run_scr[...] = jnp.zeros_like(run_scr)

    def pad_rows(rec):
        return jnp.concatenate([rec, jnp.zeros((SLAB - REC_ROWS, rec.shape[1]), F32)], axis=0)

    @pl.when(i < n_tiles)
    def _():
        h, rec = _route_rows(xp_ref[...], g_ref[...], whi_ref[...], wlo_ref[...], b_ref[...], run_scr)
        _store_packed_tiles(h_ref, h, h.shape[0])
        slabp_ref[...] = pad_rows(rec).T
        rec_ref[...] = rec[0:2 * TOP_K, :]

    @pl.when(i == n_tiles)
    def _():
        h, rec = _route_rows(xs_ref[...], g_ref[...], whi_ref[...], wlo_ref[...], b_ref[...], run_scr)
        h_ref[n_s * PACKED_SUBLANES:, :] = jnp.zeros((h_ref.shape[0] - n_s * PACKED_SUBLANES, V7X_LANES), U32)
        _store_packed_tiles(h_ref, h, n_s)
        slabs_ref[...] = pad_rows(rec).T
        rec_ref[...] = jnp.zeros_like(rec_ref)
        rec_ref[:, 0:n_s] = rec[0:2 * TOP_K, :]

    cnt_ref[...] = run_scr[...]


def _store_packed_tiles(ref, x, n):
    half = D_MODEL // 2
    lo = pltpu.bitcast(x[:, :half], U32)
    hi = pltpu.bitcast(x[:, half:], U32)
    word = lax.shift_right_logical(lo, jnp.uint32(16)) | (hi & jnp.uint32(0xFFFF0000))
    for s in range(PACKED_SUBLANES):
        ref[pl.ds(s, n, stride=PACKED_SUBLANES), :] = word[:, s * V7X_LANES:(s + 1) * V7X_LANES]


def _load_packed_tiles(ref, n):
    words = [ref[pl.ds(s, n, stride=PACKED_SUBLANES), :] for s in range(PACKED_SUBLANES)]
    lo = [pltpu.bitcast(w << jnp.uint32(16), F32) for w in words]
    hi = [pltpu.bitcast(w & jnp.uint32(0xFFFF0000), F32) for w in words]
    return jnp.concatenate(lo + hi, axis=1).astype(BF16)


def _store_token_tiles(ref, x, n):
    for s in range(TOKEN_SUBLANES):
        ref[pl.ds(s, n, stride=TOKEN_SUBLANES), :] = x[:, s * V7X_LANES:(s + 1) * V7X_LANES]


def _router(xp, xs, g, whi, wlo, b, *, tm):
    n_p, n_s = xp.shape[0], xs.shape[0]
    n_tiles = n_p // tm
    last = n_tiles - 1
    return pl.pallas_call(
        functools.partial(_router_kernel, n_tiles=n_tiles, n_s=n_s),
        grid=(n_tiles + 1,),
        in_specs=[pl.BlockSpec((tm, D_MODEL), lambda i: (jnp.minimum(i, last), 0)), _resident(xs.shape),
                  _resident(g.shape), _resident(whi.shape), _resident(wlo.shape), _resident(b.shape)],
        out_specs=[_rows(tm * PACKED_SUBLANES, V7X_LANES),
                   pl.BlockSpec((tm, SLAB), lambda i: (jnp.minimum(i, last), 0)),
                   pl.BlockSpec((n_s, SLAB), lambda i: (0, 0)),
                   pl.BlockSpec((2 * TOP_K, tm), lambda i: (0, i)),
                   pl.BlockSpec((N_EXPERTS, V7X_LANES), lambda i: (0, 0))],
        out_shape=[jax.ShapeDtypeStruct(((n_p + tm) * PACKED_SUBLANES, V7X_LANES), U32),
                   jax.ShapeDtypeStruct((n_p, SLAB), F32),
                   jax.ShapeDtypeStruct((n_s, SLAB), F32),
                   jax.ShapeDtypeStruct((2 * TOP_K, n_p + tm), F32),
                   jax.ShapeDtypeStruct((N_EXPERTS, V7X_LANES), F32)],
        scratch_shapes=[pltpu.VMEM((N_EXPERTS, V7X_LANES), F32)],
        compiler_params=_cparams(),
        name="moe_router",
    )(xp, xs, g, whi, wlo, b)


def _dispatch_kernel(fill_start_ref, fill_len_ref, dest_ref, h_ref, xs_ref, zbuf, sem, fill_sem,
                     *, tm, n_tiles, n_last, rows, n_blocks):
    i = pl.program_id(0)
    ts = PACKED_SUBLANES

    def tile_copy(r, k):
        d = dest_ref[0, 0, k * tm + r]
        src = h_ref.at[pl.ds(pl.multiple_of(r * ts, ts), ts)]
        dst = xs_ref.at[pl.ds(pl.multiple_of(d * ts, ts), ts)]
        return pltpu.make_async_copy(src, dst, sem.at[k % 2])

    def scatter(n):
        def body(r, carry):
            for k in range(TOP_K):
                tile_copy(r, k).start(priority=k % 2)
            return carry
        lax.fori_loop(0, n, body, 0, unroll=4)
        for k in range(TOP_K):
            pltpu.make_async_copy(h_ref.at[pl.ds(0, n * ts)], xs_ref.at[pl.ds(0, n * ts)], sem.at[k % 2]).wait()

    def fill_copy(e, bit):
        size = 1 << bit
        length = fill_len_ref[e]
        done = length - (length & (2 * size - 1))
        off = pl.multiple_of((fill_start_ref[e] + done) * ts, ts)
        return (length & size) != 0, pltpu.make_async_copy(zbuf.at[pl.ds(0, size * ts)],
                                                            xs_ref.at[pl.ds(off, size * ts)], fill_sem)

    def tail_copy(blk):
        off = pl.multiple_of(blk * (rows * ts), rows * ts)
        return pltpu.make_async_copy(zbuf, xs_ref.at[pl.ds(off, rows * ts)], fill_sem)

    def for_each_fill(act):
        for e in range(N_EXPERTS):
            for bit in range(FILL_BITS):
                go, cp = fill_copy(e, bit)

                @pl.when(go)
                def _():
                    act(cp)

        def body(blk, carry):
            act(tail_copy(blk))
            return carry
        lax.fori_loop(fill_start_ref[N_EXPERTS], n_blocks, body, 0)

    @pl.when(i == 0)
    def _():
        zbuf[...] = jnp.zeros_like(zbuf)
        for_each_fill(lambda cp: cp.start())

    @pl.when(i < n_tiles - 1)
    def _():
        scatter(tm)

    @pl.when(i == n_tiles - 1)
    def _():
        scatter(n_last)
        for_each_fill(lambda cp: cp.wait())


def _dispatch(fill_start, fill_len, dest3, h_tiles, n_all, n_blocks, *, tm, rows):
    n_tiles = dest3.shape[0]
    n_last = n_all - (n_tiles - 1) * tm
    ts = PACKED_SUBLANES
    n_slots = n_blocks * rows
    grid_spec = pltpu.PrefetchScalarGridSpec(
        num_scalar_prefetch=2,
        grid=(n_tiles,),
        in_specs=[pl.BlockSpec((1, 1, TOP_K * tm), lambda i, *_: (i, 0, 0), memory_space=pltpu.SMEM),
                  _rows(tm * ts, V7X_LANES)],
        out_specs=pl.BlockSpec(memory_space=pl.ANY),
        scratch_shapes=[pltpu.VMEM((rows * ts, V7X_LANES), U32),
                        pltpu.SemaphoreType.DMA((2,)), pltpu.SemaphoreType.DMA],
    )
    return pl.pallas_call(
        functools.partial(_dispatch_kernel, tm=tm, n_tiles=n_tiles, n_last=n_last, rows=rows, n_blocks=n_blocks),
        grid_spec=grid_spec,
        out_shape=jax.ShapeDtypeStruct((n_slots * ts, V7X_LANES), U32),
        compiler_params=_cparams(),
        name="moe_dispatch",
    )(fill_start, fill_len, dest3, h_tiles)


def _expert_kernel(be_ref, nact_ref, run_ref, rune_ref, x_ref, wgu_hbm, wdn_hbm, *refs, rows, layer):
    bps = EXPERT_BLOCKS_PER_STEP
    bias_refs, (o_ref, wgu_f32, wdn_f32, wsem, wgu_bf, wdn_bf) = refs[:2 * bps], refs[2 * bps:]
    nact = nact_ref[0]

    def weight_copies(r, slot):
        e = rune_ref[r]
        return (pltpu.make_async_copy(wgu_hbm.at[layer, e], wgu_f32.at[slot], wsem.at[0, slot]),
                pltpu.make_async_copy(wdn_hbm.at[layer, e], wdn_f32.at[slot], wsem.at[1, slot]))

    @pl.when(pl.program_id(0) == 0)
    def _():
        for cp in weight_copies(0, 0):
            cp.start()

    for part in range(bps):
        _expert_block(pl.program_id(0) * bps + part, nact, run_ref, rune_ref, weight_copies,
                      x_ref.at[pl.ds(part * rows * PACKED_SUBLANES, rows * PACKED_SUBLANES)],
                      bias_refs[2 * part], bias_refs[2 * part + 1],
                      o_ref.at[pl.ds(part * rows * TOKEN_SUBLANES, rows * TOKEN_SUBLANES)],
                      wgu_f32, wdn_f32, wgu_bf, wdn_bf, rows)


def _expert_block(b, nact, run_ref, rune_ref, weight_copies, x_ref, bgu_ref, bdn_ref, o_ref,
                  wgu_f32, wdn_f32, wgu_bf, wdn_bf, rows):
    @pl.when(b < nact)
    def _():
        r = run_ref[b]
        slot = lax.rem(r, 2)
        changed = jnp.logical_or(b == 0, r != run_ref[jnp.maximum(b - 1, 0)])

        @pl.when(changed)
        def _():
            for cp in weight_copies(r, slot):
                cp.wait()

            @pl.when(r + 1 < rune_ref[N_EXPERTS])
            def _():
                for cp in weight_copies(r + 1, 1 - slot):
                    cp.start(priority=1)

            step = 256
            for c in range(D_MODEL // step):
                wgu_bf[c * step:(c + 1) * step, :] = wgu_f32[slot, c * step:(c + 1) * step, :].astype(BF16)
                wdn_bf[c * step:(c + 1) * step, :] = wdn_f32[slot, c * step:(c + 1) * step, :].astype(BF16)

        xb = _load_packed_tiles(x_ref, rows)
        nc = 512
        chunks = range(D_EXPERT // nc)
        gcols = [slice(c * nc, (c + 1) * nc) for c in chunks]
        ucols = [slice(D_EXPERT + c * nc, D_EXPERT + (c + 1) * nc) for c in chunks]
        gates = [jnp.dot(xb, wgu_bf[:, gcols[c]], preferred_element_type=F32) + bgu_ref[:, gcols[c]] for c in chunks]
        ups = [jnp.dot(xb, wgu_bf[:, ucols[c]], preferred_element_type=F32) + bgu_ref[:, ucols[c]] for c in chunks]
        acts = []
        for gate, up in zip(gates, ups):
            gate = jnp.minimum(gate, SWIGLU_LIMIT)
            up = jnp.clip(up, -SWIGLU_LIMIT, SWIGLU_LIMIT)
            acts.append(((up + 1.0) * gate * (1.0 / (1.0 + jnp.exp(-SWIGLU_ALPHA * gate)))).astype(BF16))
        out = None
        for c in chunks:
            part = jnp.dot(acts[c], wdn_bf[gcols[c], :], preferred_element_type=F32)
            out = part if out is None else out + part
        _store_token_tiles(o_ref, out + bdn_ref[...], rows)

    @pl.when(b >= nact)
    def _():
        o_ref[...] = jnp.zeros_like(o_ref)


def _experts(block_e, n_active, block_run, run_e, xs_tiles, wgu, bgu, wdn, bdn, *, rows, layer):
    ts = TOKEN_SUBLANES
    bps = EXPERT_BLOCKS_PER_STEP
    nb = xs_tiles.shape[0] // (rows * PACKED_SUBLANES)
    assert nb % bps == 0
    bias_specs, bias_args = [], []
    for part in range(bps):
        def index(s, be, *_, part=part):
            return (layer, be[s * bps + part], 0, 0)
        bias_specs += [pl.BlockSpec((None, None, 1, 2 * D_EXPERT), index), pl.BlockSpec((None, None, 1, D_MODEL), index)]
        bias_args += [bgu, bdn]
    grid_spec = pltpu.PrefetchScalarGridSpec(
        num_scalar_prefetch=4,
        grid=(nb // bps,),
        in_specs=[
            pl.BlockSpec((bps * rows * PACKED_SUBLANES, V7X_LANES),
                         lambda s, be, na, *_: (jnp.minimum(s, (na[0] - 1) // bps), 0)),
            pl.BlockSpec(memory_space=pl.ANY), pl.BlockSpec(memory_space=pl.ANY),
        ] + bias_specs,
        out_specs=pl.BlockSpec((bps * rows * ts, V7X_LANES), lambda s, *_: (s, 0)),
        scratch_shapes=[pltpu.VMEM((2, D_MODEL, 2 * D_EXPERT), F32), pltpu.VMEM((2, D_EXPERT, D_MODEL), F32),
                        pltpu.SemaphoreType.DMA((2, 2)),
                        pltpu.VMEM((D_MODEL, 2 * D_EXPERT), BF16), pltpu.VMEM((D_EXPERT, D_MODEL), BF16)],
    )
    return pl.pallas_call(
        functools.partial(_expert_kernel, rows=rows, layer=layer),
        grid_spec=grid_spec,
        out_shape=jax.ShapeDtypeStruct((nb * rows * ts, V7X_LANES), F32),
        compiler_params=_cparams(),
        name="moe_experts",
    )(block_e, n_active, block_run, run_e, xs_tiles, wgu, wdn, *bias_args)


def _combine_kernel(dest_first_ref, dest_next_ref, x_ref, slab_ref, ebuf_ref, *refs, tm, n_tiles, attn_pre):
    if attn_pre:
        (kvg_ref, wk_ref, wv_ref, kng_ref, bng_ref, wq_ref, qng_ref, cos_ref, sin_ref,
         o_ref, k_ref, kdup_ref, v_ref, q_ref, rbuf, sem) = refs
    else:
        o_ref, rbuf, sem = refs
    i = pl.program_id(0)
    slot = lax.rem(i, 2)
    ts = TOKEN_SUBLANES

    def gather(dest_ref, s):
        def body(r, carry):
            for k in range(TOP_K):
                d = dest_ref[0, 0, k * tm + r]
                src = ebuf_ref.at[pl.ds(pl.multiple_of(d * ts, ts), ts)]
                dst = rbuf.at[s, k, pl.ds(pl.multiple_of(r * ts, ts), ts)]
                pltpu.make_async_copy(src, dst, sem.at[s, k % 2]).start(priority=k % 2)
            return carry
        lax.fori_loop(0, tm, body, 0, unroll=4)

    @pl.when(i == 0)
    def _():
        gather(dest_first_ref, 0)

    if attn_pre:
        todo = [(r, k) for r in range(tm) for k in range(TOP_K)]

        def tick(n):
            for r, k in todo[:n]:
                d = dest_next_ref[0, 0, k * tm + r]
                src = ebuf_ref.at[pl.ds(pl.multiple_of(d * ts, ts), ts)]
                pltpu.make_async_copy(src, rbuf.at[1 - slot, k, pl.ds(r * ts, ts)],
                                      sem.at[1 - slot, k % 2]).start(priority=k % 2)
            del todo[:n]
    else:
        @pl.when(i + 1 < n_tiles)
        def _():
            gather(dest_next_ref, 1 - slot)

        def tick(n):
            pass

    def wait_slot(s):
        for k in range(TOP_K):
            pltpu.make_async_copy(ebuf_ref.at[pl.ds(0, tm * ts)], rbuf.at[s, k], sem.at[s, k % 2]).wait()

    wait_slot(slot)
    slab = slab_ref[...]
    gates = [slab[:, 2 * TOP_K + k:2 * TOP_K + k + 1] for k in range(TOP_K)]
    per_phase = TOP_K * tm // (4 * ts)
    for s in range(ts):
        cols = slice(s * V7X_LANES, (s + 1) * V7X_LANES)
        acc = x_ref[:, cols]
        for k in range(TOP_K):
            acc = acc + gates[k] * rbuf[slot, k, pl.ds(s, tm, stride=ts), :]
        o_ref[:, cols] = acc
        tick(per_phase)
    if attn_pre:
        x2 = o_ref[...]
        cos2, sin2 = cos_ref[...], sin_ref[...]
        kv_phases = 1 + N_KV_HEADS
        q_phases = 1 + D_MODEL // PAIR
        per_phase = len(todo) // (kv_phases + q_phases)
        _shared_kv(x2, kvg_ref[...], wk_ref, wv_ref, kng_ref[...], cos2, sin2, k_ref, kdup_ref, v_ref,
                   tick=lambda: tick(per_phase))
        hq = _rms(x2, bng_ref[...]).astype(BF16)
        q = jnp.dot(hq, wq_ref[...], preferred_element_type=F32)
        tick(per_phase)
        for p in range(D_MODEL // PAIR):
            cols = slice(p * PAIR, (p + 1) * PAIR)
            q_ref[:, cols] = (_pair_norm_rope(q[:, cols], qng_ref[...], cos2, sin2) * HEAD_DIM ** -0.5).astype(BF16)
            tick(per_phase)
        tick(len(todo))

        @pl.when(i == n_tiles - 1)
        def _():
            wait_slot(1 - slot)


def _shared_kv(x, kvg, wk_ref, wv_ref, kng2, cos2, sin2, k_ref, kdup_ref, v_ref, tick=lambda: None):
    h = _rms(x, kvg).astype(BF16)
    kd = jnp.dot(h, wk_ref[...], preferred_element_type=F32)
    v_ref[...] = jnp.dot(h, wv_ref[...], preferred_element_type=F32)
    tick()
    first_half = lax.broadcasted_iota(I32, (x.shape[0], PAIR), 1) < HEAD_DIM
    slabs = []
    for p in range(N_KV_HEADS):
        cols = slice(p * PAIR, (p + 1) * PAIR)
        slab = _pair_norm_rope(kd[:, cols], kng2, cos2, sin2)
        kdup_ref[:, cols] = slab
        slabs.append(slab)
        tick()
    for p in range(D_KV // PAIR):
        k_ref[:, p * PAIR:(p + 1) * PAIR] = jnp.where(first_half, slabs[2 * p], slabs[2 * p + 1])


def _combine(x, slab, dest3, ebuf, *, tm, attn_pre=None):
    n = x.shape[0]
    n_tiles = n // tm
    in_specs = [
        pl.BlockSpec((1, 1, TOP_K * tm), lambda i: (0, 0, 0), memory_space=pltpu.SMEM),
        pl.BlockSpec((1, 1, TOP_K * tm), lambda i: (jnp.minimum(i + 1, n_tiles - 1), 0, 0),
                     memory_space=pltpu.SMEM),
        _rows(tm, D_MODEL), _rows(tm, SLAB), pl.BlockSpec(memory_space=pl.ANY),
    ]
    args = [dest3, dest3, x, slab, ebuf]
    out_specs = [_rows(tm, D_MODEL)]
    out_shape = [jax.ShapeDtypeStruct((n, D_MODEL), F32)]
    if attn_pre is not None:
        *weights, cos2, sin2, table_tiles = attn_pre
        table_spec = pl.BlockSpec((tm, PAIR), lambda i: (lax.rem(i, table_tiles), 0))
        in_specs += [_resident(w.shape) for w in weights] + [table_spec, table_spec]
        args += list(weights) + [cos2, sin2]
        out_specs += [_rows(tm, D_KV), _rows(tm, 2 * D_KV), _rows(tm, D_KV), _rows(tm, D_MODEL)]
        out_shape += [jax.ShapeDtypeStruct((n, D_KV), F32), jax.ShapeDtypeStruct((n, 2 * D_KV), F32),
                      jax.ShapeDtypeStruct((n, D_KV), F32), jax.ShapeDtypeStruct((n, D_MODEL), BF16)]
    return pl.pallas_call(
        functools.partial(_combine_kernel, tm=tm, n_tiles=n_tiles, attn_pre=attn_pre is not None),
        grid=(n_tiles,),
        in_specs=in_specs, out_specs=out_specs, out_shape=out_shape,
        scratch_shapes=[pltpu.VMEM((2, TOP_K, tm * TOKEN_SUBLANES, V7X_LANES), F32),
                        pltpu.SemaphoreType.DMA((2, 2))],
        compiler_params=_cparams(),
        name="moe_combine_attn_pre" if attn_pre is not None else "moe_combine",
    )(*args)


def _moe(xs, g, w_router, b_router, wgu, bgu, wdn, bdn, layer, prompt_attn_pre=None):
    n_all = xs[0].shape[0] + xs[1].shape[0]
    w_t = w_router.T
    whi = w_t.astype(BF16)
    wlo = (w_t - whi.astype(F32)).astype(BF16)
    h_all, slab_p, slab_s, rec, cnt = _router(xs[0], xs[1], g.reshape(1, D_MODEL), whi, wlo,
                                              b_router.reshape(N_EXPERTS, 1), tm=TM_DENSE)
    slabs = [slab_p, slab_s]
    idx = rec[:TOP_K, :n_all].astype(I32)
    pos = rec[TOP_K:, :n_all].astype(I32)
    counts = cnt[:, 0].astype(I32)
    rows = MOE_ROWS
    nb = (n_all * TOP_K + N_EXPERTS * (rows - 1)) // rows
    nb = -(-nb // EXPERT_BLOCKS_PER_STEP) * EXPERT_BLOCKS_PER_STEP
    padded =(counts + rows - 1) // rows * rows
    pad_end = jnp.cumsum(padded)
    pad_start = pad_end - padded
    eid = jnp.arange(N_EXPERTS, dtype=I32)
    has_rows = counts > 0
    dest = pos + jnp.sum(jnp.where(idx[None] == eid[:, None, None], pad_start[:, None, None], 0), axis=0)
    n_active = (pad_end[-1] // rows).astype(I32)
    blk = jnp.arange(nb, dtype=I32)
    block_e = jnp.sum((pad_end[None, :] <= (blk * rows)[:, None]).astype(I32), axis=1)
    last_e = jnp.max(jnp.where(has_rows, eid, 0))
    block_e = jnp.where(blk < n_active, jnp.minimum(block_e, N_EXPERTS - 1), last_e)

    def tile_table(d, tm):
        n_tiles = -(-d.shape[1] // tm)
        d = jnp.pad(d, ((0, 0), (0, n_tiles * tm - d.shape[1])))
        return d.reshape(TOP_K, n_tiles, tm).transpose(1, 0, 2).reshape(n_tiles, 1, TOP_K * tm)

    fill_start = jnp.concatenate([pad_start + counts, n_active.reshape(1)])
    xs_tiles = _dispatch(fill_start, padded - counts, tile_table(dest, TM_DISPATCH), h_all,
                         n_all, nb, tm=TM_DISPATCH, rows=rows)
    rank = jnp.cumsum(has_rows.astype(I32)) - 1
    run_e = jnp.sum(jnp.where(has_rows[None, :] & (rank[None, :] == eid[:, None]), eid[None, :], 0), axis=1)
    run_e = jnp.concatenate([run_e, jnp.sum(has_rows.astype(I32)).reshape(1)]).astype(I32)
    block_run = jnp.sum(jnp.where(block_e[:, None] == eid, rank, 0), axis=1).astype(I32)
    n_l = wgu.shape[0]
    ebuf = _experts(block_e, n_active.reshape(1), block_run, run_e, xs_tiles,
                    wgu, bgu.reshape(n_l, N_EXPERTS, 1, 2 * D_EXPERT), wdn, bdn.reshape(n_l, N_EXPERTS, 1, D_MODEL),
                    rows=rows, layer=layer)
    outs, off = [], 0
    for x, slab, pre in zip(xs, slabs, (prompt_attn_pre, None)):
        n = x.shape[0]
        tm = min(TM_COMBINE, n)
        outs.append(_combine(x, slab, tile_table(dest[:, off:off + n], tm), ebuf, tm=tm, attn_pre=pre))
        off += n
    return outs


def _kvq_sample_kernel(x_ref, kvg_ref, wk_ref, wv_ref, kng_ref, cos_ref, sin_ref, bng_ref, wq_ref, qng_ref,
                       k_ref, kdup_ref, v_ref, q_ref):
    x = x_ref[...]
    cos2, sin2 = cos_ref[...], sin_ref[...]
    _shared_kv(x, kvg_ref[...], wk_ref, wv_ref, kng_ref[...], cos2, sin2, k_ref, kdup_ref, v_ref)
    hq = _rms(x, bng_ref[...]).astype(BF16)
    q = jnp.dot(hq, wq_ref[...], preferred_element_type=F32)
    for p in range(D_MODEL // PAIR):
        cols = slice(p * PAIR, (p + 1) * PAIR)
        q_ref[:, cols] = _pair_norm_rope(q[:, cols], qng_ref[...], cos2, sin2)


def _kvq_sample(x, kvg, wk, wv, kng2, cos2, sin2, bng, wq, qng2):
    n = x.shape[0]
    args = [x, kvg, wk, wv, kng2, cos2, sin2, bng, wq, qng2]
    return pl.pallas_call(
        _kvq_sample_kernel,
        grid=(1,),
        in_specs=[_resident(a.shape) for a in args],
        out_specs=[_rows(n, D_KV), _rows(n, 2 * D_KV), _rows(n, D_KV), _rows(n, D_MODEL)],
        out_shape=[jax.ShapeDtypeStruct((n, D_KV), F32), jax.ShapeDtypeStruct((n, 2 * D_KV), F32),
                   jax.ShapeDtypeStruct((n, D_KV), F32), jax.ShapeDtypeStruct((n, D_MODEL), F32)],
        compiler_params=_cparams(),
        name="kvq_proj_sample",
    )(*args)


def _attn_prompt_kernel(sinks_ref, x_ref, q_ref, kc_ref, kp_ref, vc_ref, vp_ref, wo_ref, o_ref, a_scr,
                        *, tm, tiles_per_seq):
    i = pl.program_id(0)

    kj = lax.broadcasted_iota(I32, (2 * WINDOW, WINDOW), 0)
    qi = lax.broadcasted_iota(I32, (2 * WINDOW, WINDOW), 1)
    diff = qi + WINDOW - kj
    band = jnp.logical_and(diff >= 0, diff < WINDOW)
    first_half = lax.broadcasted_iota(I32, (WINDOW, PAIR), 1) < HEAD_DIM
    keep = (jnp.where(first_half, 1.0, 0.0).astype(BF16), jnp.where(first_half, 0.0, 1.0).astype(BF16))
    first_kj = jnp.where(lax.rem(i, tiles_per_seq) == 0, WINDOW, 0)

    for blk in range(tm // WINDOW):
        rows = slice(blk * WINDOW, (blk + 1) * WINDOW)
        if blk == 0:
            kk = jnp.concatenate([kp_ref[...], kc_ref[rows, :]], axis=0).astype(BF16)
            vv = jnp.concatenate([vp_ref[...], vc_ref[rows, :]], axis=0)
            valid = jnp.logical_and(band, kj >= first_kj)
        else:
            win = slice((blk - 1) * WINDOW, (blk + 1) * WINDOW)
            kk = kc_ref[win, :].astype(BF16)
            vv = vc_ref[win, :]
            valid = band
        for kvh in range(N_KV_HEADS):
            kk_h = kk[:, kvh * PAIR:(kvh + 1) * PAIR]
            vcols = slice((kvh // 2) * PAIR, (kvh // 2 + 1) * PAIR)
            vt = vv[:, vcols].T.astype(BF16)
            ch = slice((kvh % 2) * HEAD_DIM, (kvh % 2 + 1) * HEAD_DIM)
            outs = []
            for g in range(Q_PER_KV):
                head = kvh * Q_PER_KV + g
                qm = q_ref[rows, (head // 2) * PAIR:(head // 2 + 1) * PAIR] * keep[head % 2]
                sink = sinks_ref[head]
                st = lax.dot_general(kk_h, qm, (((1,), (1,)), ((), ())), preferred_element_type=F32)
                st = jnp.where(valid, st, -jnp.inf)
                m = jnp.maximum(jnp.max(st, axis=0, keepdims=True), sink)
                p = jnp.exp(st - m)
                den = jnp.sum(p, axis=0, keepdims=True) + jnp.exp(sink - m)
                ot = jnp.dot(vt, p.astype(BF16), preferred_element_type=F32)
                outs.append(ot[ch, :] * (1.0 / den))
            for pq in range(Q_PER_KV // 2):
                pair = (kvh * Q_PER_KV) // 2 + pq
                both = jnp.concatenate([outs[2 * pq], outs[2 * pq + 1]], axis=0)
                a_scr[rows, pair * PAIR:(pair + 1) * PAIR] = both.T.astype(BF16)
    o_ref[...] = x_ref[...] + jnp.dot(a_scr[...], wo_ref[...], preferred_element_type=F32)


def _attn_prompt(x, q, kdup, v, sinks, wo, *, tm, seq):
    n = x.shape[0]
    tiles_per_seq = seq // tm
    blocks_per_tile = tm // WINDOW

    def prev_spec(width):
        return pl.BlockSpec((WINDOW, width), lambda i: (jnp.maximum(i * blocks_per_tile - 1, 0), 0))

    return pl.pallas_call(
        functools.partial(_attn_prompt_kernel, tm=tm, tiles_per_seq=tiles_per_seq),
        grid=(n // tm,),
        in_specs=[pl.BlockSpec(memory_space=pltpu.SMEM), _rows(tm, D_MODEL), _rows(tm, D_MODEL),
                  _rows(tm, 2 * D_KV), prev_spec(2 * D_KV), _rows(tm, D_KV), prev_spec(D_KV), _resident(wo.shape)],
        out_specs=_rows(tm, D_MODEL),
        out_shape=jax.ShapeDtypeStruct((n, D_MODEL), F32),
        scratch_shapes=[pltpu.VMEM((tm, D_MODEL), BF16)],
        compiler_params=_cparams(),
        name="attn_prompt",
    )(sinks, x, q, kdup, kdup, v, v, wo)


def _attn_sample_kernel(q_ref, ck_ref, cv_ref, kn_ref, vn_ref, sink_ref, o_ref, *, per_step, win):
    rowi = lax.broadcasted_iota(I32, (N_HEADS, D_KV), 0)
    lanei = lax.broadcasted_iota(I32, (N_HEADS, D_KV), 1)
    group = rowi >> 2
    own_block = group == (lanei >> 6)
    j = lax.broadcasted_iota(I32, (N_HEADS, win), 1)
    diff = win - j
    valid = jnp.logical_and(diff >= 0, diff < WINDOW)
    sink = sink_ref[...]
    scale = HEAD_DIM ** -0.5
    qms = [jnp.where(own_block, q_ref[b], 0.0) for b in range(per_step)]
    ss = [lax.dot_general(qm.astype(BF16), ck_ref[b].astype(BF16), (((1,), (1,)), ((), ())),
                          preferred_element_type=F32) * scale for b, qm in enumerate(qms)]
    ps, p_news, dens = [], [], []
    for b in range(per_step):
        s = jnp.where(valid, ss[b], -jnp.inf)
        s_new = jnp.sum(qms[b] * kn_ref[b:b + 1, :], axis=-1, keepdims=True) * scale
        m = jnp.maximum(jnp.maximum(jnp.max(s, axis=-1, keepdims=True), s_new), sink)
        p = jnp.exp(s - m)
        p_new = jnp.exp(s_new - m)
        ps.append(p.astype(BF16))
        p_news.append(p_new)
        dens.append(jnp.sum(p, axis=-1, keepdims=True) + p_new + jnp.exp(sink - m))
    os_ = [jnp.dot(ps[b], cv_ref[b].astype(BF16), preferred_element_type=F32) for b in range(per_step)]
    for b in range(per_step):
        o = (os_[b] + p_news[b] * vn_ref[b:b + 1, :]) * (1.0 / dens[b])
        res = o
        for sft in range(1, N_KV_HEADS):
            res = jnp.where(group == sft, pltpu.roll(o, D_KV - sft * HEAD_DIM, 1), res)
        o_ref[b] = res[:, :HEAD_DIM]


def _attn_sample(q4, ck, cv, kn, vn, sink_col, *, per_step):
    nb, win = ck.shape[0], ck.shape[1]
    return pl.pallas_call(
        functools.partial(_attn_sample_kernel, per_step=per_step, win=win),
        grid=(nb // per_step,),
        in_specs=[pl.BlockSpec((per_step, N_HEADS, D_KV), lambda i: (i, 0, 0)),
                  pl.BlockSpec((per_step, win, D_KV), lambda i: (i, 0, 0)),
                  pl.BlockSpec((per_step, win, D_KV), lambda i: (i, 0, 0)),
                  _rows(per_step, D_KV), _rows(per_step, D_KV), _resident(sink_col.shape)],
        out_specs=pl.BlockSpec((per_step, N_HEADS, HEAD_DIM), lambda i: (i, 0, 0)),
        out_shape=jax.ShapeDtypeStruct((nb, N_HEADS, HEAD_DIM), F32),
        compiler_params=_cparams(),
        name="attn_sample",
    )(q4, ck, cv, kn, vn, sink_col)


def _residual_proj_kernel(x_ref, a_ref, w_ref, o_ref):
    o_ref[...] = x_ref[...] + jnp.dot(a_ref[...].astype(BF16), w_ref[...], preferred_element_type=F32)


def _residual_proj(x, a, w):
    n = x.shape[0]
    return pl.pallas_call(
        _residual_proj_kernel,
        grid=(1,),
        in_specs=[_rows(n, D_MODEL), _rows(n, a.shape[1]), _resident(w.shape)],
        out_specs=_rows(n, D_MODEL),
        out_shape=jax.ShapeDtypeStruct((n, D_MODEL), F32),
        compiler_params=_cparams(),
        name="attn_out_sample",
    )(x, a, w)


def _rope_tables(pos):
    half = ROPE_DIM // 2
    inv_freq = np.power(np.float32(ROPE_THETA), -np.arange(half, dtype=np.float32) / np.float32(half))
    ang = (pos.astype(np.float32)[:, None] * inv_freq[None, :].astype(np.float32)).astype(np.float32)
    cos, sin = np.cos(ang.astype(np.float64)), np.sin(ang.astype(np.float64))
    n = pos.shape[0]
    rest = HEAD_DIM - ROPE_DIM
    cos_h = np.concatenate([cos, cos, np.ones((n, rest))], axis=1)
    sin_h = np.concatenate([-sin, sin, np.zeros((n, rest))], axis=1)
    return (jnp.asarray(np.tile(cos_h, (1, 2)), dtype=F32), jnp.asarray(np.tile(sin_h, (1, 2)), dtype=F32))


def kernel(x_prompt, x_sample, cache_k, cache_v, a_norm_g, a_w_in, a_ln_g, a_ln_b, a_w_s, a_b_s, a_w_out, kv_norm_g, w_k, w_v, k_norm_g, b_norm_g, w_q, q_norm_g, sinks, w_o, ffn_norm_g, w_router, b_router, w_gu, b_gu, w_down, b_down):
    bsz, seq, _ = x_prompt.shape
    dec_b, dec_seq, _ = x_sample.shape
    assert dec_seq == 1 and seq % TM_DENSE == 0 and a_norm_g.shape[0] == 1 and b_norm_g.shape[0] == 1
    win = cache_k.shape[1]
    n_p, n_s = bsz * seq, dec_b * dec_seq
    xp = x_prompt.reshape(n_p, D_MODEL)
    xs = x_sample.reshape(n_s, D_MODEL)

    ng = a_norm_g[0].reshape(1, D_MODEL)
    win_bf = a_w_in[0].astype(BF16)
    wout_bf = a_w_out[0].astype(BF16)
    lng = a_ln_g[0].reshape(1, D_GATE)
    lnb = a_ln_b[0].reshape(1, D_GATE)
    bias_full = jnp.repeat(a_b_s[0].T, GROUP_DIM_A, axis=1)
    diag_row = jnp.repeat(a_w_s[0][:, 0, 0], GROUP_DIM_A).reshape(1, D_GATE)
    (xp,) = _mixer_a(xp, ng, win_bf, lng, lnb, a_w_s[0], bias_full, wout_bf, tm=TM_DENSE, single_token_chunks=False)
    xs, v_rows = _mixer_a(xs, ng, win_bf, lng, lnb, diag_row, bias_full[0:1], wout_bf, tm=n_s,
                          single_token_chunks=True)

    kvg = kv_norm_g.reshape(1, D_MODEL)
    bng = b_norm_g[0].reshape(1, D_MODEL)
    w_k_dup = jnp.repeat(w_k.reshape(D_MODEL, N_KV_HEADS, 1, HEAD_DIM), 2, axis=2).reshape(D_MODEL, 2 * D_KV)
    wk_bf, wv_bf, wq_bf, wo_bf = (w.astype(BF16) for w in (w_k_dup, w_v, w_q[0], w_o[0]))
    kng2 = jnp.tile(k_norm_g, 2).reshape(1, PAIR)
    qng2 = jnp.tile(q_norm_g[0], 2).reshape(1, PAIR)
    cos_p, sin_p = _rope_tables(np.arange(seq))
    cos_s, sin_s = _rope_tables(np.full((n_s,), PAST_LEN))
    attn_pre = (kvg, wk_bf, wv_bf, kng2, bng, wq_bf, qng2, cos_p, sin_p, seq // TM_COMBINE)
    (xp, k_p, kdup_p, v_p, q_p), (xs,) = _moe([xp, xs], ffn_norm_g[0], w_router[0], b_router[0], w_gu, b_gu,
                                              w_down, b_down, 0, prompt_attn_pre=attn_pre)

    k_s, _, v_s, q_s = _kvq_sample(xs, kvg, wk_bf, wv_bf, kng2, cos_s, sin_s, bng, wq_bf, qng2)
    xp = _attn_prompt(xp, q_p, kdup_p, v_p, sinks[0], wo_bf, tm=TM_DENSE, seq=seq)
    q4 = jnp.tile(q_s.reshape(n_s, N_HEADS, HEAD_DIM), (1, 1, N_KV_HEADS))
    attn_s = _attn_sample(q4, cache_k.reshape(dec_b, win, D_KV), cache_v.reshape(dec_b, win, D_KV), k_s, v_s,
                          sinks[0].reshape(N_HEADS, 1), per_step=8)
    xs = _residual_proj(xs, attn_s.reshape(n_s, D_MODEL), wo_bf)
    (xp,), (xs,) = _moe([xp, xs], ffn_norm_g[1], w_router[1], b_router[1], w_gu, b_gu, w_down, b_down, 1)

    def last_window(a):
        a = a.reshape(bsz, seq, D_KV)[:, -WINDOW:]
        return a.reshape(bsz, min(WINDOW, seq), N_KV_HEADS, HEAD_DIM)

    return (xp.reshape(bsz, seq, D_MODEL), xs.reshape(dec_b, dec_seq, D_MODEL),
            v_rows.reshape(1, dec_b, dec_seq, D_GATE), last_window(k_p), last_window(v_p),
            k_s.reshape(dec_b, dec_seq, N_KV_HEADS, HEAD_DIM), v_s.reshape(dec_b, dec_seq, N_KV_HEADS, HEAD_DIM))
```

```python
import functools

import jax
import jax.numpy as jnp
import numpy as np
from jax import lax
from jax.experimental import pallas as pl
from jax.experimental.pallas import tpu as pltpu

F32, BF16, I32 = jnp.float32, jnp.bfloat16, jnp.int32

D_MODEL = 1024
CHUNK = 128
D_GATE = 2 * D_MODEL
N_GROUPS_A = 8
GROUP_DIM_A = D_GATE // N_GROUPS_A
HEAD_DIM = 64
N_HEADS = D_MODEL // HEAD_DIM
N_KV_HEADS = N_HEADS // 4
Q_PER_KV = N_HEADS // N_KV_HEADS
D_KV = N_KV_HEADS * HEAD_DIM
WINDOW = 128
ROPE_DIM = HEAD_DIM // 4
ROPE_THETA = 500000.0
N_EXPERTS = 32
TOP_K = 4
D_EXPERT = D_MODEL
SWIGLU_LIMIT = 7.0
SWIGLU_ALPHA = 1.702
EPS = 1e-6
PAST_LEN = 8192

V7X_LANES = 128
V7X_VMEM_BYTES = 64 * 1024 * 1024
VMEM_LIMIT_BYTES = V7X_VMEM_BYTES - 8 * 1024 * 1024

TM_DENSE = 512
TM_COMBINE = 256
MOE_ROWS_TARGET = 512
EXPERT_BLOCKS_PER_STEP = 1
TM_DISPATCH = 256
TOKEN_SUBLANES = D_MODEL // V7X_LANES
PACKED_SUBLANES = TOKEN_SUBLANES // 2
U32 = jnp.uint32
SLAB = V7X_LANES
REC_ROWS = 16
PAIR = 2 * HEAD_DIM
INV_SQRT2 = 0.7071067811865476


def _cparams():
    return pltpu.CompilerParams(dimension_semantics=("arbitrary",), vmem_limit_bytes=VMEM_LIMIT_BYTES)


def _resident(shape):
    zeros = (0,) * len(shape)
    return pl.BlockSpec(shape, lambda i, *_: zeros, pipeline_mode=pl.Buffered(1))


def _rows(tm, width, offset=0):
    return pl.BlockSpec((tm, width), lambda i, *_: (i + offset, 0))


def _rms(x, g):
    return x * lax.rsqrt(jnp.mean(x * x, axis=-1, keepdims=True) + EPS) * g


def _pair_norm_rope(x2, g2, cos2, sin2):
    lane = lax.broadcasted_iota(I32, x2.shape, 1)
    r = lax.broadcasted_iota(I32, (PAIR, PAIR), 0)
    c = lax.broadcasted_iota(I32, (PAIR, PAIR), 1)
    head_mean = jnp.where((r < HEAD_DIM) == (c < HEAD_DIM), 1.0 / HEAD_DIM, 0.0).astype(BF16)
    sq = x2 * x2
    sq_hi = sq.astype(BF16)
    sq_lo = (sq - sq_hi.astype(F32)).astype(BF16)
    ms = (jnp.dot(sq_hi, head_mean, preferred_element_type=F32)
          + jnp.dot(sq_lo, head_mean, preferred_element_type=F32))
    y = x2 * lax.rsqrt(ms + EPS) * g2
    half = ROPE_DIM // 2
    up = pltpu.roll(y, PAIR - half, 1)
    dn = pltpu.roll(y, half, 1)
    partner = jnp.where((lane & (HEAD_DIM - 1)) < half, up, dn)
    return y * cos2 + partner * sin2


def _mixer_a_kernel(x_ref, ng_ref, win_ref, lng_ref, lnb_ref, mix_ref, bias_ref, wout_ref, *refs,
                    tm, single_token_chunks):
    if single_token_chunks:
        o_ref, vout_ref, u_scr, v_scr, p_scr = refs
    else:
        o_ref, u_scr, v_scr, p_scr = refs
    x = x_ref[...]
    h = _rms(x, ng_ref[...]).astype(BF16)
    nc = 512
    for c in range(2 * D_GATE // nc):
        z = jnp.dot(h, win_ref[:, c * nc:(c + 1) * nc], preferred_element_type=F32)
        z = 0.5 * z * (1.0 + lax.erf(z * INV_SQRT2))
        if c < D_GATE // nc:
            u_scr[:, c * nc:(c + 1) * nc] = z
        else:
            v_scr[:, (c - D_GATE // nc) * nc:(c - D_GATE // nc + 1) * nc] = z
    v = v_scr[...]
    vc = v - jnp.mean(v, axis=-1, keepdims=True)
    vn = vc * lax.rsqrt(jnp.mean(vc * vc, axis=-1, keepdims=True) + EPS) * lng_ref[...] + lnb_ref[...]
    if single_token_chunks:
        vout_ref[...] = vn
        p_scr[...] = (u_scr[...] * (vn * mix_ref[...] + bias_ref[...])).astype(BF16)
    else:
        v_scr[...] = vn
        row = lax.broadcasted_iota(I32, (CHUNK, CHUNK), 0)
        col = lax.broadcasted_iota(I32, (CHUNK, CHUNK), 1)
        for g in range(N_GROUPS_A):
            wc = jnp.where(row >= col, mix_ref[g], 0.0).astype(BF16)
            cols = slice(g * GROUP_DIM_A, (g + 1) * GROUP_DIM_A)
            for c in range(tm // CHUNK):
                rows = slice(c * CHUNK, (c + 1) * CHUNK)
                mixed = jnp.dot(wc, v_scr[rows, cols].astype(BF16), preferred_element_type=F32)
                mixed = mixed + bias_ref[:, cols]
                p_scr[rows, cols] = (u_scr[rows, cols] * mixed).astype(BF16)
    o_ref[...] = x + jnp.dot(p_scr[...], wout_ref[...], preferred_element_type=F32)


def _mixer_a(x, ng, win, lng, lnb, mix, bias, wout, *, tm, single_token_chunks):
    n = x.shape[0]
    out_shape = [jax.ShapeDtypeStruct((n, D_MODEL), F32)]
    out_specs = [_rows(tm, D_MODEL)]
    if single_token_chunks:
        out_shape.append(jax.ShapeDtypeStruct((n, D_GATE), F32))
        out_specs.append(_rows(tm, D_GATE))
    return pl.pallas_call(
        functools.partial(_mixer_a_kernel, tm=tm, single_token_chunks=single_token_chunks),
        grid=(n // tm,),
        in_specs=[_rows(tm, D_MODEL), _resident(ng.shape), _resident(win.shape), _resident(lng.shape),
                  _resident(lnb.shape), _resident(mix.shape), _resident(bias.shape), _resident(wout.shape)],
        out_specs=out_specs,
        out_shape=out_shape,
        scratch_shapes=[pltpu.VMEM((tm, D_GATE), F32), pltpu.VMEM((tm, D_GATE), F32),
                        pltpu.VMEM((tm, D_GATE), BF16)],
        compiler_params=_cparams(),
        name="mixer_a_sample" if single_token_chunks else "mixer_a_prompt",
    )(x, ng, win, lng, lnb, mix, bias, wout)


def _route_rows(x, g, whi_t, wlo_t, bias_col, run_scr):
    tm = x.shape[0]
    h = _rms(x, g)
    hh = h.astype(BF16)
    h_rounded = hh.astype(F32)
    hl = (h - h_rounded).astype(BF16)
    nt = (((1,), (1,)), ((), ()))
    logits = (lax.dot_general(whi_t, hh, nt, preferred_element_type=F32)
              + lax.dot_general(wlo_t, hh, nt, preferred_element_type=F32)
              + lax.dot_general(whi_t, hl, nt, preferred_element_type=F32) + bias_col)
    e = lax.broadcasted_iota(I32, (N_EXPERTS, tm), 0).astype(F32)
    hot, vals, idxs = [], [], []
    for _ in range(TOP_K):
        m = jnp.max(logits, axis=0, keepdims=True)
        idx = jnp.min(jnp.where(logits == m, e, float(N_EXPERTS)), axis=0, keepdims=True)
        oh = e == idx
        hot.append(oh)
        vals.append(m)
        idxs.append(idx)
        logits = jnp.where(oh, -jnp.inf, logits)
    ex = [jnp.exp(v - vals[0]) for v in vals]
    den = ex[0] + ex[1] + ex[2] + ex[3]
    picked = jnp.zeros((N_EXPERTS, tm), F32)
    for oh in hot:
        picked = picked + jnp.where(oh, 1.0, 0.0)
    r = lax.broadcasted_iota(I32, (tm, tm), 0)
    c = lax.broadcasted_iota(I32, (tm, tm), 1)
    earlier = jnp.where(r < c, 1.0, 0.0).astype(BF16)
    run = run_scr[:, 0:1]
    pos = jnp.dot(picked.astype(BF16), earlier, preferred_element_type=F32) + run
    run_scr[:, 0:1] = run + jnp.sum(picked, axis=1, keepdims=True)
    row = lax.broadcasted_iota(I32, (REC_ROWS, tm), 0)
    rec = jnp.zeros((REC_ROWS, tm), F32)
    for k in range(TOP_K):
        pos_k = jnp.sum(jnp.where(hot[k], pos, 0.0), axis=0, keepdims=True)
        rec = jnp.where(row == k, idxs[k], rec)
        rec = jnp.where(row == TOP_K + k, pos_k, rec)
        rec = jnp.where(row == 2 * TOP_K + k, ex[k] / den, rec)
    return h_rounded, rec


def _router_kernel(xp_ref, xs_ref, g_ref, whi_ref, wlo_ref, b_ref, h_ref, slabp_ref, slabs_ref, rec_ref, cnt_ref,
                   run_scr, *, n_tiles, n_s):
    i = pl.program_id(0)

    @pl.when(i == 0)
    def _():
        run_scr[...] = jnp.zeros_like(run_scr)

    def pad_rows(rec):
        return jnp.concatenate([rec, jnp.zeros((SLAB - REC_ROWS, rec.shape[1]), F32)], axis=0)

    @pl.when(i < n_tiles)
    def _():
        h, rec = _route_rows(xp_ref[...], g_ref[...], whi_ref[...], wlo_ref[...], b_ref[...], run_scr)
        _store_packed_tiles(h_ref, h, h.shape[0])
        slabp_ref[...] = pad_rows(rec).T
        rec_ref[...] = rec[0:2 * TOP_K, :]

    @pl.when(i == n_tiles)
    def _():
        h, rec = _route_rows(xs_ref[...], g_ref[...], whi_ref[...], wlo_ref[...], b_ref[...], run_scr)
        h_ref[n_s * PACKED_SUBLANES:, :] = jnp.zeros((h_ref.shape[0] - n_s * PACKED_SUBLANES, V7X_LANES), U32)
        _store_packed_tiles(h_ref, h, n_s)
        slabs_ref[...] = pad_rows(rec).T
        rec_ref[...] = jnp.zeros_like(rec_ref)
        rec_ref[:, 0:n_s] = rec[0:2 * TOP_K, :]

    cnt_ref[...] = run_scr[...]


def _store_packed_tiles(ref, x, n):
    half = D_MODEL // 2
    lo = pltpu.bitcast(x[:, :half], U32)
    hi = pltpu.bitcast(x[:, half:], U32)
    word = lax.shift_right_logical(lo, jnp.uint32(16)) | (hi & jnp.uint32(0xFFFF0000))
    for s in range(PACKED_SUBLANES):
        ref[pl.ds(s, n, stride=PACKED_SUBLANES), :] = word[:, s * V7X_LANES:(s + 1) * V7X_LANES]


def _load_packed_tiles(ref, n):
    words = [ref[pl.ds(s, n, stride=PACKED_SUBLANES), :] for s in range(PACKED_SUBLANES)]
    lo = [pltpu.bitcast(w << jnp.uint32(16), F32) for w in words]
    hi = [pltpu.bitcast(w & jnp.uint32(0xFFFF0000), F32) for w in words]
    return jnp.concatenate(lo + hi, axis=1).astype(BF16)


def _store_token_tiles(ref, x, n):
    for s in range(TOKEN_SUBLANES):
        ref[pl.ds(s, n, stride=TOKEN_SUBLANES), :] = x[:, s * V7X_LANES:(s + 1) * V7X_LANES]


def _router(xp, xs, g, whi, wlo, b, *, tm):
    n_p, n_s = xp.shape[0], xs.shape[0]
    n_tiles = n_p // tm
    last = n_tiles - 1
    return pl.pallas_call(
        functools.partial(_router_kernel, n_tiles=n_tiles, n_s=n_s),
        grid=(n_tiles + 1,),
        in_specs=[pl.BlockSpec((tm, D_MODEL), lambda i: (jnp.minimum(i, last), 0)), _resident(xs.shape),
                  _resident(g.shape), _resident(whi.shape), _resident(wlo.shape), _resident(b.shape)],
        out_specs=[_rows(tm * PACKED_SUBLANES, V7X_LANES),
                   pl.BlockSpec((tm, SLAB), lambda i: (jnp.minimum(i, last), 0)),
                   pl.BlockSpec((n_s, SLAB), lambda i: (0, 0)),
                   pl.BlockSpec((2 * TOP_K, tm), lambda i: (0, i)),
                   pl.BlockSpec((N_EXPERTS, V7X_LANES), lambda i: (0, 0))],
        out_shape=[jax.ShapeDtypeStruct(((n_p + tm) * PACKED_SUBLANES, V7X_LANES), U32),
                   jax.ShapeDtypeStruct((n_p, SLAB), F32),
                   jax.ShapeDtypeStruct((n_s, SLAB), F32),
                   jax.ShapeDtypeStruct((2 * TOP_K, n_p + tm), F32),
                   jax.ShapeDtypeStruct((N_EXPERTS, V7X_LANES), F32)],
        scratch_shapes=[pltpu.VMEM((N_EXPERTS, V7X_LANES), F32)],
        compiler_params=_cparams(),
        name="moe_router",
    )(xp, xs, g, whi, wlo, b)


def _dispatch_kernel(fill_start_ref, fill_len_ref, dest_ref, h_ref, xs_ref, zbuf, sem, fill_sem,
                     *, tm, n_tiles, n_last, rows, n_blocks):
    i = pl.program_id(0)
    ts = PACKED_SUBLANES

    def tile_copy(r, k):
        d = dest_ref[0, 0, k * tm + r]
        src = h_ref.at[pl.ds(pl.multiple_of(r * ts, ts), ts)]
        dst = xs_ref.at[pl.ds(pl.multiple_of(d * ts, ts), ts)]
        return pltpu.make_async_copy(src, dst, sem.at[k % 2])

    def scatter(n):
        def body(r, carry):
            for k in range(TOP_K):
                tile_copy(r, k).start(priority=k % 2)
            return carry
        lax.fori_loop(0, n, body, 0, unroll=4)
        for k in range(TOP_K):
            pltpu.make_async_copy(h_ref.at[pl.ds(0, n * ts)], xs_ref.at[pl.ds(0, n * ts)], sem.at[k % 2]).wait()

    def fill_copy(e, bit):
        size = 1 << bit
        length = fill_len_ref[e]
        done = length - (length & (2 * size - 1))
        off = pl.multiple_of((fill_start_ref[e] + done) * ts, ts)
        return (length & size) != 0, pltpu.make_async_copy(zbuf.at[pl.ds(0, size * ts)],
                                                            xs_ref.at[pl.ds(off, size * ts)], fill_sem)

    def tail_copy(blk):
        off = pl.multiple_of(blk * (rows * ts), rows * ts)
        return pltpu.make_async_copy(zbuf, xs_ref.at[pl.ds(off, rows * ts)], fill_sem)

    def for_each_fill(act):
        for e in range(N_EXPERTS):
            for bit in range((rows - 1).bit_length()):
                go, cp = fill_copy(e, bit)

                @pl.when(go)
                def _():
                    act(cp)

        def body(blk, carry):
            act(tail_copy(blk))
            return carry
        lax.fori_loop(fill_start_ref[N_EXPERTS], n_blocks, body, 0)

    @pl.when(i == 0)
    def _():
        zbuf[...] = jnp.zeros_like(zbuf)
        for_each_fill(lambda cp: cp.start())

    @pl.when(i < n_tiles - 1)
    def _():
        scatter(tm)

    @pl.when(i == n_tiles - 1)
    def _():
        scatter(n_last)
        for_each_fill(lambda cp: cp.wait())


def _dispatch(fill_start, fill_len, dest3, h_tiles, n_all, n_blocks, *, tm, rows):
    n_tiles = dest3.shape[0]
    n_last = n_all - (n_tiles - 1) * tm
    ts = PACKED_SUBLANES
    n_slots = n_blocks * rows
    grid_spec = pltpu.PrefetchScalarGridSpec(
        num_scalar_prefetch=2,
        grid=(n_tiles,),
        in_specs=[pl.BlockSpec((1, 1, TOP_K * tm), lambda i, *_: (i, 0, 0), memory_space=pltpu.SMEM),
                  _rows(tm * ts, V7X_LANES)],
        out_specs=pl.BlockSpec(memory_space=pl.ANY),
        scratch_shapes=[pltpu.VMEM((rows * ts, V7X_LANES), U32),
                        pltpu.SemaphoreType.DMA((2,)), pltpu.SemaphoreType.DMA],
    )
    return pl.pallas_call(
        functools.partial(_dispatch_kernel, tm=tm, n_tiles=n_tiles, n_last=n_last, rows=rows, n_blocks=n_blocks),
        grid_spec=grid_spec,
        out_shape=jax.ShapeDtypeStruct((n_slots * ts, V7X_LANES), U32),
        compiler_params=_cparams(),
        name="moe_dispatch",
    )(fill_start, fill_len, dest3, h_tiles)


def _expert_kernel(be_ref, nact_ref, run_ref, rune_ref, x_ref, wgu_hbm, wdn_hbm, *refs, rows, layer):
    bps = EXPERT_BLOCKS_PER_STEP
    bias_refs, (o_ref, wgu_f32, wdn_f32, wsem, wgu_bf, wdn_bf) = refs[:2 * bps], refs[2 * bps:]
    nact = nact_ref[0]

    def weight_copies(r, slot):
        e = rune_ref[r]
        return (pltpu.make_async_copy(wgu_hbm.at[layer, e], wgu_f32.at[slot], wsem.at[0, slot]),
                pltpu.make_async_copy(wdn_hbm.at[layer, e], wdn_f32.at[slot], wsem.at[1, slot]))

    @pl.when(pl.program_id(0) == 0)
    def _():
        for cp in weight_copies(0, 0):
            cp.start()

    for part in range(bps):
        _expert_block(pl.program_id(0) * bps + part, nact, run_ref, rune_ref, weight_copies,
                      x_ref.at[pl.ds(part * rows * PACKED_SUBLANES, rows * PACKED_SUBLANES)],
                      bias_refs[2 * part], bias_refs[2 * part + 1],
                      o_ref.at[pl.ds(part * rows * TOKEN_SUBLANES, rows * TOKEN_SUBLANES)],
                      wgu_f32, wdn_f32, wgu_bf, wdn_bf, rows)


def _expert_block(b, nact, run_ref, rune_ref, weight_copies, x_ref, bgu_ref, bdn_ref, o_ref,
                  wgu_f32, wdn_f32, wgu_bf, wdn_bf, rows):
    @pl.when(b < nact)
    def _():
        r = run_ref[b]
        slot = lax.rem(r, 2)
        changed = jnp.logical_or(b == 0, r != run_ref[jnp.maximum(b - 1, 0)])

        @pl.when(changed)
        def _():
            for cp in weight_copies(r, slot):
                cp.wait()

            @pl.when(r + 1 < rune_ref[N_EXPERTS])
            def _():
                for cp in weight_copies(r + 1, 1 - slot):
                    cp.start(priority=1)

            step = 256
            for c in range(D_MODEL // step):
                wgu_bf[c * step:(c + 1) * step, :] = wgu_f32[slot, c * step:(c + 1) * step, :].astype(BF16)
                wdn_bf[c * step:(c + 1) * step, :] = wdn_f32[slot, c * step:(c + 1) * step, :].astype(BF16)

        xb = _load_packed_tiles(x_ref, rows)
        nc = 512
        chunks = range(D_EXPERT // nc)
        gcols = [slice(c * nc, (c + 1) * nc) for c in chunks]
        ucols = [slice(D_EXPERT + c * nc, D_EXPERT + (c + 1) * nc) for c in chunks]
        gates = [jnp.dot(xb, wgu_bf[:, gcols[c]], preferred_element_type=F32) + bgu_ref[:, gcols[c]] for c in chunks]
        ups = [jnp.dot(xb, wgu_bf[:, ucols[c]], preferred_element_type=F32) + bgu_ref[:, ucols[c]] for c in chunks]
        acts = []
        for gate, up in zip(gates, ups):
            gate = jnp.minimum(gate, SWIGLU_LIMIT)
            up = jnp.clip(up, -SWIGLU_LIMIT, SWIGLU_LIMIT)
            acts.append(((up + 1.0) * gate * (1.0 / (1.0 + jnp.exp(-SWIGLU_ALPHA * gate)))).astype(BF16))
        out = None
        for c in chunks:
            part = jnp.dot(acts[c], wdn_bf[gcols[c], :], preferred_element_type=F32)
            out = part if out is None else out + part
        _store_token_tiles(o_ref, out + bdn_ref[...], rows)

    @pl.when(b >= nact)
    def _():
        o_ref[...] = jnp.zeros_like(o_ref)


def _experts(block_e, n_active, block_run, run_e, xs_tiles, wgu, bgu, wdn, bdn, *, rows, layer):
    ts = TOKEN_SUBLANES
    bps = EXPERT_BLOCKS_PER_STEP
    nb = xs_tiles.shape[0] // (rows * PACKED_SUBLANES)
    assert nb % bps == 0
    bias_specs, bias_args = [], []
    for part in range(bps):
        def index(s, be, *_, part=part):
            return (layer, be[s * bps + part], 0, 0)
        bias_specs += [pl.BlockSpec((None, None, 1, 2 * D_EXPERT), index), pl.BlockSpec((None, None, 1, D_MODEL), index)]
        bias_args += [bgu, bdn]
    grid_spec = pltpu.PrefetchScalarGridSpec(
        num_scalar_prefetch=4,
        grid=(nb // bps,),
        in_specs=[
            pl.BlockSpec((bps * rows * PACKED_SUBLANES, V7X_LANES),
                         lambda s, be, na, *_: (jnp.minimum(s, (na[0] - 1) // bps), 0)),
            pl.BlockSpec(memory_space=pl.ANY), pl.BlockSpec(memory_space=pl.ANY),
        ] + bias_specs,
        out_specs=pl.BlockSpec((bps * rows * ts, V7X_LANES), lambda s, *_: (s, 0)),
        scratch_shapes=[pltpu.VMEM((2, D_MODEL, 2 * D_EXPERT), F32), pltpu.VMEM((2, D_EXPERT, D_MODEL), F32),
                        pltpu.SemaphoreType.DMA((2, 2)),
                        pltpu.VMEM((D_MODEL, 2 * D_EXPERT), BF16), pltpu.VMEM((D_EXPERT, D_MODEL), BF16)],
    )
    return pl.pallas_call(
        functools.partial(_expert_kernel, rows=rows, layer=layer),
        grid_spec=grid_spec,
        out_shape=jax.ShapeDtypeStruct((nb * rows * ts, V7X_LANES), F32),
        compiler_params=_cparams(),
        name="moe_experts",
    )(block_e, n_active, block_run, run_e, xs_tiles, wgu, wdn, *bias_args)


def _combine_kernel(dest_first_ref, dest_next_ref, x_ref, slab_ref, ebuf_ref, *refs, tm, n_tiles, attn_pre):
    if attn_pre:
        (kvg_ref, wk_ref, wv_ref, kng_ref, bng_ref, wq_ref, qng_ref, cos_ref, sin_ref,
         o_ref, k_ref, kdup_ref, v_ref, q_ref, rbuf, sem) = refs
    else:
        o_ref, rbuf, sem = refs
    i = pl.program_id(0)
    slot = lax.rem(i, 2)
    ts = TOKEN_SUBLANES

    def gather(dest_ref, s):
        def body(r, carry):
            for k in range(TOP_K):
                d = dest_ref[0, 0, k * tm + r]
                src = ebuf_ref.at[pl.ds(pl.multiple_of(d * ts, ts), ts)]
                dst = rbuf.at[s, k, pl.ds(pl.multiple_of(r * ts, ts), ts)]
                pltpu.make_async_copy(src, dst, sem.at[s, k % 2]).start(priority=k % 2)
            return carry
        lax.fori_loop(0, tm, body, 0, unroll=4)

    @pl.when(i == 0)
    def _():
        gather(dest_first_ref, 0)

    if attn_pre:
        todo = [(r, k) for r in range(tm) for k in range(TOP_K)]

        def tick(n):
            for r, k in todo[:n]:
                d = dest_next_ref[0, 0, k * tm + r]
                src = ebuf_ref.at[pl.ds(pl.multiple_of(d * ts, ts), ts)]
                pltpu.make_async_copy(src, rbuf.at[1 - slot, k, pl.ds(r * ts, ts)],
                                      sem.at[1 - slot, k % 2]).start(priority=k % 2)
            del todo[:n]
    else:
        @pl.when(i + 1 < n_tiles)
        def _():
            gather(dest_next_ref, 1 - slot)

        def tick(n):
            pass

    def wait_slot(s):
        for k in range(TOP_K):
            pltpu.make_async_copy(ebuf_ref.at[pl.ds(0, tm * ts)], rbuf.at[s, k], sem.at[s, k % 2]).wait()

    wait_slot(slot)
    slab = slab_ref[...]
    gates = [slab[:, 2 * TOP_K + k:2 * TOP_K + k + 1] for k in range(TOP_K)]
    per_phase = TOP_K * tm // (4 * ts)
    for s in range(ts):
        cols = slice(s * V7X_LANES, (s + 1) * V7X_LANES)
        acc = x_ref[:, cols]
        for k in range(TOP_K):
            acc = acc + gates[k] * rbuf[slot, k, pl.ds(s, tm, stride=ts), :]
        o_ref[:, cols] = acc
        tick(per_phase)
    if attn_pre:
        x2 = o_ref[...]
        cos2, sin2 = cos_ref[...], sin_ref[...]
        kv_phases = 1 + N_KV_HEADS
        q_phases = 1 + D_MODEL // PAIR
        per_phase = len(todo) // (kv_phases + q_phases)
        _shared_kv(x2, kvg_ref[...], wk_ref, wv_ref, kng_ref[...], cos2, sin2, k_ref, kdup_ref, v_ref,
                   tick=lambda: tick(per_phase))
        hq = _rms(x2, bng_ref[...]).astype(BF16)
        q = jnp.dot(hq, wq_ref[...], preferred_element_type=F32)
        tick(per_phase)
        for p in range(D_MODEL // PAIR):
            cols = slice(p * PAIR, (p + 1) * PAIR)
            q_ref[:, cols] = (_pair_norm_rope(q[:, cols], qng_ref[...], cos2, sin2) * HEAD_DIM ** -0.5).astype(BF16)
            tick(per_phase)
        tick(len(todo))

        @pl.when(i == n_tiles - 1)
        def _():
            wait_slot(1 - slot)


def _shared_kv(x, kvg, wk_ref, wv_ref, kng2, cos2, sin2, k_ref, kdup_ref, v_ref, tick=lambda: None):
    h = _rms(x, kvg).astype(BF16)
    kd = jnp.dot(h, wk_ref[...], preferred_element_type=F32)
    v_ref[...] = jnp.dot(h, wv_ref[...], preferred_element_type=F32)
    tick()
    first_half = lax.broadcasted_iota(I32, (x.shape[0], PAIR), 1) < HEAD_DIM
    slabs = []
    for p in range(N_KV_HEADS):
        cols = slice(p * PAIR, (p + 1) * PAIR)
        slab = _pair_norm_rope(kd[:, cols], kng2, cos2, sin2)
        kdup_ref[:, cols] = slab
        slabs.append(slab)
        tick()
    for p in range(D_KV // PAIR):
        k_ref[:, p * PAIR:(p + 1) * PAIR] = jnp.where(first_half, slabs[2 * p], slabs[2 * p + 1])


def _combine(x, slab, dest3, ebuf, *, tm, attn_pre=None):
    n = x.shape[0]
    n_tiles = n // tm
    in_specs = [
        pl.BlockSpec((1, 1, TOP_K * tm), lambda i: (0, 0, 0), memory_space=pltpu.SMEM),
        pl.BlockSpec((1, 1, TOP_K * tm), lambda i: (jnp.minimum(i + 1, n_tiles - 1), 0, 0),
                     memory_space=pltpu.SMEM),
        _rows(tm, D_MODEL), _rows(tm, SLAB), pl.BlockSpec(memory_space=pl.ANY),
    ]
    args = [dest3, dest3, x, slab, ebuf]
    out_specs = [_rows(tm, D_MODEL)]
    out_shape = [jax.ShapeDtypeStruct((n, D_MODEL), F32)]
    if attn_pre is not None:
        *weights, cos2, sin2, table_tiles = attn_pre
        table_spec = pl.BlockSpec((tm, PAIR), lambda i: (lax.rem(i, table_tiles), 0))
        in_specs += [_resident(w.shape) for w in weights] + [table_spec, table_spec]
        args += list(weights) + [cos2, sin2]
        out_specs += [_rows(tm, D_KV), _rows(tm, 2 * D_KV), _rows(tm, D_KV), _rows(tm, D_MODEL)]
        out_shape += [jax.ShapeDtypeStruct((n, D_KV), F32), jax.ShapeDtypeStruct((n, 2 * D_KV), F32),
                      jax.ShapeDtypeStruct((n, D_KV), F32), jax.ShapeDtypeStruct((n, D_MODEL), BF16)]
    return pl.pallas_call(
        functools.partial(_combine_kernel, tm=tm, n_tiles=n_tiles, attn_pre=attn_pre is not None),
        grid=(n_tiles,),
        in_specs=in_specs, out_specs=out_specs, out_shape=out_shape,
        scratch_shapes=[pltpu.VMEM((2, TOP_K, tm * TOKEN_SUBLANES, V7X_LANES), F32),
                        pltpu.SemaphoreType.DMA((2, 2))],
        compiler_params=_cparams(),
        name="moe_combine_attn_pre" if attn_pre is not None else "moe_combine",
    )(*args)


def _expert_block_rows(n_assign):
    mean = n_assign / N_EXPERTS
    blocks = max(1, round(mean / MOE_ROWS_TARGET))
    rows = (mean + 2.0 * mean ** 0.5) / blocks
    bf16_rows = 16
    return -(-int(rows + 1) // bf16_rows) * bf16_rows


def _moe(xs, g, w_router, b_router, wgu, bgu, wdn, bdn, layer, prompt_attn_pre=None):
    n_all = xs[0].shape[0] + xs[1].shape[0]
    w_t = w_router.T
    whi = w_t.astype(BF16)
    wlo = (w_t - whi.astype(F32)).astype(BF16)
    h_all, slab_p, slab_s, rec, cnt = _router(xs[0], xs[1], g.reshape(1, D_MODEL), whi, wlo,
                                              b_router.reshape(N_EXPERTS, 1), tm=TM_DENSE)
    slabs = [slab_p, slab_s]
    idx = rec[:TOP_K, :n_all].astype(I32)
    pos = rec[TOP_K:, :n_all].astype(I32)
    counts = cnt[:, 0].astype(I32)
    rows = _expert_block_rows(n_all * TOP_K)
    nb = (n_all * TOP_K + N_EXPERTS * (rows - 1)) // rows
    nb = -(-nb // EXPERT_BLOCKS_PER_STEP) * EXPERT_BLOCKS_PER_STEP
    padded =(counts + rows - 1) // rows * rows
    pad_end = jnp.cumsum(padded)
    pad_start = pad_end - padded
    eid = jnp.arange(N_EXPERTS, dtype=I32)
    has_rows = counts > 0
    dest = pos + jnp.sum(jnp.where(idx[None] == eid[:, None, None], pad_start[:, None, None], 0), axis=0)
    n_active = (pad_end[-1] // rows).astype(I32)
    blk = jnp.arange(nb, dtype=I32)
    block_e = jnp.sum((pad_end[None, :] <= (blk * rows)[:, None]).astype(I32), axis=1)
    last_e = jnp.max(jnp.where(has_rows, eid, 0))
    block_e = jnp.where(blk < n_active, jnp.minimum(block_e, N_EXPERTS - 1), last_e)

    def tile_table(d, tm):
        n_tiles = -(-d.shape[1] // tm)
        d = jnp.pad(d, ((0, 0), (0, n_tiles * tm - d.shape[1])))
        return d.reshape(TOP_K, n_tiles, tm).transpose(1, 0, 2).reshape(n_tiles, 1, TOP_K * tm)

    fill_start = jnp.concatenate([pad_start + counts, n_active.reshape(1)])
    xs_tiles = _dispatch(fill_start, padded - counts, tile_table(dest, TM_DISPATCH), h_all,
                         n_all, nb, tm=TM_DISPATCH, rows=rows)
    rank = jnp.cumsum(has_rows.astype(I32)) - 1
    run_e = jnp.sum(jnp.where(has_rows[None, :] & (rank[None, :] == eid[:, None]), eid[None, :], 0), axis=1)
    run_e = jnp.concatenate([run_e, jnp.sum(has_rows.astype(I32)).reshape(1)]).astype(I32)
    block_run = jnp.sum(jnp.where(block_e[:, None] == eid, rank, 0), axis=1).astype(I32)
    n_l = wgu.shape[0]
    ebuf = _experts(block_e, n_active.reshape(1), block_run, run_e, xs_tiles,
                    wgu, bgu.reshape(n_l, N_EXPERTS, 1, 2 * D_EXPERT), wdn, bdn.reshape(n_l, N_EXPERTS, 1, D_MODEL),
                    rows=rows, layer=layer)
    outs, off = [], 0
    for x, slab, pre in zip(xs, slabs, (prompt_attn_pre, None)):
        n = x.shape[0]
        tm = min(TM_COMBINE, n)
        outs.append(_combine(x, slab, tile_table(dest[:, off:off + n], tm), ebuf, tm=tm, attn_pre=pre))
        off += n
    return outs


def _kvq_sample_kernel(x_ref, kvg_ref, wk_ref, wv_ref, kng_ref, cos_ref, sin_ref, bng_ref, wq_ref, qng_ref,
                       k_ref, kdup_ref, v_ref, q_ref):
    x = x_ref[...]
    cos2, sin2 = cos_ref[...], sin_ref[...]
    _shared_kv(x, kvg_ref[...], wk_ref, wv_ref, kng_ref[...], cos2, sin2, k_ref, kdup_ref, v_ref)
    hq = _rms(x, bng_ref[...]).astype(BF16)
    q = jnp.dot(hq, wq_ref[...], preferred_element_type=F32)
    for p in range(D_MODEL // PAIR):
        cols = slice(p * PAIR, (p + 1) * PAIR)
        q_ref[:, cols] = _pair_norm_rope(q[:, cols], qng_ref[...], cos2, sin2)


def _kvq_sample(x, kvg, wk, wv, kng2, cos2, sin2, bng, wq, qng2):
    n = x.shape[0]
    args = [x, kvg, wk, wv, kng2, cos2, sin2, bng, wq, qng2]
    return pl.pallas_call(
        _kvq_sample_kernel,
        grid=(1,),
        in_specs=[_resident(a.shape) for a in args],
        out_specs=[_rows(n, D_KV), _rows(n, 2 * D_KV), _rows(n, D_KV), _rows(n, D_MODEL)],
        out_shape=[jax.ShapeDtypeStruct((n, D_KV), F32), jax.ShapeDtypeStruct((n, 2 * D_KV), F32),
                   jax.ShapeDtypeStruct((n, D_KV), F32), jax.ShapeDtypeStruct((n, D_MODEL), F32)],
        compiler_params=_cparams(),
        name="kvq_proj_sample",
    )(*args)


def _attn_prompt_kernel(sinks_ref, x_ref, q_ref, kc_ref, kp_ref, vc_ref, vp_ref, wo_ref, o_ref, a_scr,
                        *, tm, tiles_per_seq):
    i = pl.program_id(0)

    kj = lax.broadcasted_iota(I32, (2 * WINDOW, WINDOW), 0)
    qi = lax.broadcasted_iota(I32, (2 * WINDOW, WINDOW), 1)
    diff = qi + WINDOW - kj
    band = jnp.logical_and(diff >= 0, diff < WINDOW)
    first_half = lax.broadcasted_iota(I32, (WINDOW, PAIR), 1) < HEAD_DIM
    keep = (jnp.where(first_half, 1.0, 0.0).astype(BF16), jnp.where(first_half, 0.0, 1.0).astype(BF16))
    first_kj = jnp.where(lax.rem(i, tiles_per_seq) == 0, WINDOW, 0)

    for blk in range(tm // WINDOW):
        rows = slice(blk * WINDOW, (blk + 1) * WINDOW)
        if blk == 0:
            kk = jnp.concatenate([kp_ref[...], kc_ref[rows, :]], axis=0).astype(BF16)
            vv = jnp.concatenate([vp_ref[...], vc_ref[rows, :]], axis=0)
            valid = jnp.logical_and(band, kj >= first_kj)
        else:
            win = slice((blk - 1) * WINDOW, (blk + 1) * WINDOW)
            kk = kc_ref[win, :].astype(BF16)
            vv = vc_ref[win, :]
            valid = band
        for kvh in range(N_KV_HEADS):
            kk_h = kk[:, kvh * PAIR:(kvh + 1) * PAIR]
            vcols = slice((kvh // 2) * PAIR, (kvh // 2 + 1) * PAIR)
            vt = vv[:, vcols].T.astype(BF16)
            ch = slice((kvh % 2) * HEAD_DIM, (kvh % 2 + 1) * HEAD_DIM)
            outs = []
            for g in range(Q_PER_KV):
                head = kvh * Q_PER_KV + g
                qm = q_ref[rows, (head // 2) * PAIR:(head // 2 + 1) * PAIR] * keep[head % 2]
                sink = sinks_ref[head]
                st = lax.dot_general(kk_h, qm, (((1,), (1,)), ((), ())), preferred_element_type=F32)
                st = jnp.where(valid, st, -jnp.inf)
                m = jnp.maximum(jnp.max(st, axis=0, keepdims=True), sink)
                p = jnp.exp(st - m)
                den = jnp.sum(p, axis=0, keepdims=True) + jnp.exp(sink - m)
                ot = jnp.dot(vt, p.astype(BF16), preferred_element_type=F32)
                outs.append(ot[ch, :] * (1.0 / den))
            for pq in range(Q_PER_KV // 2):
                pair = (kvh * Q_PER_KV) // 2 + pq
                both = jnp.concatenate([outs[2 * pq], outs[2 * pq + 1]], axis=0)
                a_scr[rows, pair * PAIR:(pair + 1) * PAIR] = both.T.astype(BF16)
    o_ref[...] = x_ref[...] + jnp.dot(a_scr[...], wo_ref[...], preferred_element_type=F32)


def _attn_prompt(x, q, kdup, v, sinks, wo, *, tm, seq):
    n = x.shape[0]
    tiles_per_seq = seq // tm
    blocks_per_tile = tm // WINDOW

    def prev_spec(width):
        return pl.BlockSpec((WINDOW, width), lambda i: (jnp.maximum(i * blocks_per_tile - 1, 0), 0))

    return pl.pallas_call(
        functools.partial(_attn_prompt_kernel, tm=tm, tiles_per_seq=tiles_per_seq),
        grid=(n // tm,),
        in_specs=[pl.BlockSpec(memory_space=pltpu.SMEM), _rows(tm, D_MODEL), _rows(tm, D_MODEL),
                  _rows(tm, 2 * D_KV), prev_spec(2 * D_KV), _rows(tm, D_KV), prev_spec(D_KV), _resident(wo.shape)],
        out_specs=_rows(tm, D_MODEL),
        out_shape=jax.ShapeDtypeStruct((n, D_MODEL), F32),
        scratch_shapes=[pltpu.VMEM((tm, D_MODEL), BF16)],
        compiler_params=_cparams(),
        name="attn_prompt",
    )(sinks, x, q, kdup, kdup, v, v, wo)


def _attn_sample_kernel(q_ref, ck_ref, cv_ref, kn_ref, vn_ref, sink_ref, o_ref, *, per_step, win):
    rowi = lax.broadcasted_iota(I32, (N_HEADS, D_KV), 0)
    lanei = lax.broadcasted_iota(I32, (N_HEADS, D_KV), 1)
    group = rowi >> 2
    own_block = group == (lanei >> 6)
    j = lax.broadcasted_iota(I32, (N_HEADS, win), 1)
    diff = win - j
    valid = jnp.logical_and(diff >= 0, diff < WINDOW)
    sink = sink_ref[...]
    scale = HEAD_DIM ** -0.5
    qms = [jnp.where(own_block, q_ref[b], 0.0) for b in range(per_step)]
    ss = [lax.dot_general(qm.astype(BF16), ck_ref[b].astype(BF16), (((1,), (1,)), ((), ())),
                          preferred_element_type=F32) * scale for b, qm in enumerate(qms)]
    ps, p_news, dens = [], [], []
    for b in range(per_step):
        s = jnp.where(valid, ss[b], -jnp.inf)
        s_new = jnp.sum(qms[b] * kn_ref[b:b + 1, :], axis=-1, keepdims=True) * scale
        m = jnp.maximum(jnp.maximum(jnp.max(s, axis=-1, keepdims=True), s_new), sink)
        p = jnp.exp(s - m)
        p_new = jnp.exp(s_new - m)
        ps.append(p.astype(BF16))
        p_news.append(p_new)
        dens.append(jnp.sum(p, axis=-1, keepdims=True) + p_new + jnp.exp(sink - m))
    os_ = [jnp.dot(ps[b], cv_ref[b].astype(BF16), preferred_element_type=F32) for b in range(per_step)]
    for b in range(per_step):
        o = (os_[b] + p_news[b] * vn_ref[b:b + 1, :]) * (1.0 / dens[b])
        res = o
        for sft in range(1, N_KV_HEADS):
            res = jnp.where(group == sft, pltpu.roll(o, D_KV - sft * HEAD_DIM, 1), res)
        o_ref[b] = res[:, :HEAD_DIM]


def _attn_sample(q4, ck, cv, kn, vn, sink_col, *, per_step):
    nb, win = ck.shape[0], ck.shape[1]
    return pl.pallas_call(
        functools.partial(_attn_sample_kernel, per_step=per_step, win=win),
        grid=(nb // per_step,),
        in_specs=[pl.BlockSpec((per_step, N_HEADS, D_KV), lambda i: (i, 0, 0)),
                  pl.BlockSpec((per_step, win, D_KV), lambda i: (i, 0, 0)),
                  pl.BlockSpec((per_step, win, D_KV), lambda i: (i, 0, 0)),
                  _rows(per_step, D_KV), _rows(per_step, D_KV), _resident(sink_col.shape)],
        out_specs=pl.BlockSpec((per_step, N_HEADS, HEAD_DIM), lambda i: (i, 0, 0)),
        out_shape=jax.ShapeDtypeStruct((nb, N_HEADS, HEAD_DIM), F32),
        compiler_params=_cparams(),
        name="attn_sample",
    )(q4, ck, cv, kn, vn, sink_col)


def _residual_proj_kernel(x_ref, a_ref, w_ref, o_ref):
    o_ref[...] = x_ref[...] + jnp.dot(a_ref[...].astype(BF16), w_ref[...], preferred_element_type=F32)


def _residual_proj(x, a, w):
    n = x.shape[0]
    return pl.pallas_call(
        _residual_proj_kernel,
        grid=(1,),
        in_specs=[_rows(n, D_MODEL), _rows(n, a.shape[1]), _resident(w.shape)],
        out_specs=_rows(n, D_MODEL),
        out_shape=jax.ShapeDtypeStruct((n, D_MODEL), F32),
        compiler_params=_cparams(),
        name="attn_out_sample",
    )(x, a, w)


def _rope_tables(pos):
    half = ROPE_DIM // 2
    inv_freq = np.power(np.float32(ROPE_THETA), -np.arange(half, dtype=np.float32) / np.float32(half))
    ang = (pos.astype(np.float32)[:, None] * inv_freq[None, :].astype(np.float32)).astype(np.float32)
    cos, sin = np.cos(ang.astype(np.float64)), np.sin(ang.astype(np.float64))
    n = pos.shape[0]
    rest = HEAD_DIM - ROPE_DIM
    cos_h = np.concatenate([cos, cos, np.ones((n, rest))], axis=1)
    sin_h = np.concatenate([-sin, sin, np.zeros((n, rest))], axis=1)
    return (jnp.asarray(np.tile(cos_h, (1, 2)), dtype=F32), jnp.asarray(np.tile(sin_h, (1, 2)), dtype=F32))


def kernel(x_prompt, x_sample, cache_k, cache_v, a_norm_g, a_w_in, a_ln_g, a_ln_b, a_w_s, a_b_s, a_w_out, kv_norm_g, w_k, w_v, k_norm_g, b_norm_g, w_q, q_norm_g, sinks, w_o, ffn_norm_g, w_router, b_router, w_gu, b_gu, w_down, b_down):
    bsz, seq, _ = x_prompt.shape
    dec_b, dec_seq, _ = x_sample.shape
    assert dec_seq == 1 and seq % TM_DENSE == 0 and a_norm_g.shape[0] == 1 and b_norm_g.shape[0] == 1
    win = cache_k.shape[1]
    n_p, n_s = bsz * seq, dec_b * dec_seq
    xp = x_prompt.reshape(n_p, D_MODEL)
    xs = x_sample.reshape(n_s, D_MODEL)

    ng = a_norm_g[0].reshape(1, D_MODEL)
    win_bf = a_w_in[0].astype(BF16)
    wout_bf = a_w_out[0].astype(BF16)
    lng = a_ln_g[0].reshape(1, D_GATE)
    lnb = a_ln_b[0].reshape(1, D_GATE)
    bias_full = jnp.repeat(a_b_s[0].T, GROUP_DIM_A, axis=1)
    diag_row = jnp.repeat(a_w_s[0][:, 0, 0], GROUP_DIM_A).reshape(1, D_GATE)
    (xp,) = _mixer_a(xp, ng, win_bf, lng, lnb, a_w_s[0], bias_full, wout_bf, tm=TM_DENSE, single_token_chunks=False)
    xs, v_rows = _mixer_a(xs, ng, win_bf, lng, lnb, diag_row, bias_full[0:1], wout_bf, tm=n_s,
                          single_token_chunks=True)

    kvg = kv_norm_g.reshape(1, D_MODEL)
    bng = b_norm_g[0].reshape(1, D_MODEL)
    w_k_dup = jnp.repeat(w_k.reshape(D_MODEL, N_KV_HEADS, 1, HEAD_DIM), 2, axis=2).reshape(D_MODEL, 2 * D_KV)
    wk_bf, wv_bf, wq_bf, wo_bf = (w.astype(BF16) for w in (w_k_dup, w_v, w_q[0], w_o[0]))
    kng2 = jnp.tile(k_norm_g, 2).reshape(1, PAIR)
    qng2 = jnp.tile(q_norm_g[0], 2).reshape(1, PAIR)
    cos_p, sin_p = _rope_tables(np.arange(seq))
    cos_s, sin_s = _rope_tables(np.full((n_s,), PAST_LEN))
    attn_pre = (kvg, wk_bf, wv_bf, kng2, bng, wq_bf, qng2, cos_p, sin_p, seq // TM_COMBINE)
    (xp, k_p, kdup_p, v_p, q_p), (xs,) = _moe([xp, xs], ffn_norm_g[0], w_router[0], b_router[0], w_gu, b_gu,
                                              w_down, b_down, 0, prompt_attn_pre=attn_pre)

    k_s, _, v_s, q_s = _kvq_sample(xs, kvg, wk_bf, wv_bf, kng2, cos_s, sin_s, bng, wq_bf, qng2)
    xp = _attn_prompt(xp, q_p, kdup_p, v_p, sinks[0], wo_bf, tm=TM_DENSE, seq=seq)
    q4 = jnp.tile(q_s.reshape(n_s, N_HEADS, HEAD_DIM), (1, 1, N_KV_HEADS))
    attn_s = _attn_sample(q4, cache_k.reshape(dec_b, win, D_KV), cache_v.reshape(dec_b, win, D_KV), k_s, v_s,
                          sinks[0].reshape(N_HEADS, 1), per_step=8)
    xs = _residual_proj(xs, attn_s.reshape(n_s, D_MODEL), wo_bf)
    (xp,), (xs,) = _moe([xp, xs], ffn_norm_g[1], w_router[1], b_router[1], w_gu, b_gu, w_down, b_down, 1)

    def last_window(a):
        a = a.reshape(bsz, seq, D_KV)[:, -WINDOW:]
        return a.reshape(bsz, min(WINDOW, seq), N_KV_HEADS, HEAD_DIM)

    return (xp.reshape(bsz, seq, D_MODEL), xs.reshape(dec_b, dec_seq, D_MODEL),
            v_rows.reshape(1, dec_b, dec_seq, D_GATE), last_window(k_p), last_window(v_p),
            k_s.reshape(dec_b, dec_seq, N_KV_HEADS, HEAD_DIM), v_s.reshape(dec_b, dec_seq, N_KV_HEADS, HEAD_DIM))
```

```python
import functools

import jax
import jax.numpy as jnp
import numpy as np
from jax import lax
from jax.experimental import pallas as pl
from jax.experimental.pallas import tpu as pltpu

F32, BF16, I32 = jnp.float32, jnp.bfloat16, jnp.int32

D_MODEL = 1024
CHUNK = 128
D_GATE = 2 * D_MODEL
N_GROUPS_A = 8
GROUP_DIM_A = D_GATE // N_GROUPS_A
HEAD_DIM = 64
N_HEADS = D_MODEL // HEAD_DIM
N_KV_HEADS = N_HEADS // 4
Q_PER_KV = N_HEADS // N_KV_HEADS
D_KV = N_KV_HEADS * HEAD_DIM
WINDOW = 128
ROPE_DIM = HEAD_DIM // 4
ROPE_THETA = 500000.0
N_EXPERTS = 32
TOP_K = 4
D_EXPERT = D_MODEL
SWIGLU_LIMIT = 7.0
SWIGLU_ALPHA = 1.702
EPS = 1e-6
PAST_LEN = 8192

V7X_LANES = 128
V7X_VMEM_BYTES = 64 * 1024 * 1024
VMEM_LIMIT_BYTES = V7X_VMEM_BYTES - 8 * 1024 * 1024

TM_DENSE = 512
TM_COMBINE = 256
MOE_ROWS = 512
EXPERT_BLOCKS_PER_STEP = 1
TM_DISPATCH = 256
TOKEN_SUBLANES = D_MODEL // V7X_LANES
PACKED_SUBLANES = TOKEN_SUBLANES // 2
U32 = jnp.uint32
SLAB = V7X_LANES
REC_ROWS = 16
PAIR = 2 * HEAD_DIM
INV_SQRT2 = 0.7071067811865476


def _cparams():
    return pltpu.CompilerParams(dimension_semantics=("arbitrary",), vmem_limit_bytes=VMEM_LIMIT_BYTES)


def _resident(shape):
    zeros = (0,) * len(shape)
    return pl.BlockSpec(shape, lambda i, *_: zeros, pipeline_mode=pl.Buffered(1))


def _rows(tm, width, offset=0):
    return pl.BlockSpec((tm, width), lambda i, *_: (i + offset, 0))


def _rms(x, g):
    return x * lax.rsqrt(jnp.mean(x * x, axis=-1, keepdims=True) + EPS) * g


def _pair_norm_rope(x2, g2, cos2, sin2):
    lane = lax.broadcasted_iota(I32, x2.shape, 1)
    r = lax.broadcasted_iota(I32, (PAIR, PAIR), 0)
    c = lax.broadcasted_iota(I32, (PAIR, PAIR), 1)
    head_mean = jnp.where((r < HEAD_DIM) == (c < HEAD_DIM), 1.0 / HEAD_DIM, 0.0).astype(BF16)
    sq = x2 * x2
    sq_hi = sq.astype(BF16)
    sq_lo = (sq - sq_hi.astype(F32)).astype(BF16)
    ms = (jnp.dot(sq_hi, head_mean, preferred_element_type=F32)
          + jnp.dot(sq_lo, head_mean, preferred_element_type=F32))
    y = x2 * lax.rsqrt(ms + EPS) * g2
    half = ROPE_DIM // 2
    up = pltpu.roll(y, PAIR - half, 1)
    dn = pltpu.roll(y, half, 1)
    partner = jnp.where((lane & (HEAD_DIM - 1)) < half, up, dn)
    return y * cos2 + partner * sin2


def _mixer_a_kernel(x_ref, ng_ref, win_ref, lng_ref, lnb_ref, mix_ref, bias_ref, wout_ref, *refs,
                    tm, single_token_chunks):
    if single_token_chunks:
        o_ref, vout_ref, u_scr, v_scr, p_scr = refs
    else:
        o_ref, u_scr, v_scr, p_scr = refs
    x = x_ref[...]
    h = _rms(x, ng_ref[...]).astype(BF16)
    nc = 512
    for c in range(2 * D_GATE // nc):
        z = jnp.dot(h, win_ref[:, c * nc:(c + 1) * nc], preferred_element_type=F32)
        z = 0.5 * z * (1.0 + lax.erf(z * INV_SQRT2))
        if c < D_GATE // nc:
            u_scr[:, c * nc:(c + 1) * nc] = z
        else:
            v_scr[:, (c - D_GATE // nc) * nc:(c - D_GATE // nc + 1) * nc] = z
    v = v_scr[...]
    vc = v - jnp.mean(v, axis=-1, keepdims=True)
    vn = vc * lax.rsqrt(jnp.mean(vc * vc, axis=-1, keepdims=True) + EPS) * lng_ref[...] + lnb_ref[...]
    if single_token_chunks:
        vout_ref[...] = vn
        p_scr[...] = (u_scr[...] * (vn * mix_ref[...] + bias_ref[...])).astype(BF16)
    else:
        v_scr[...] = vn
        row = lax.broadcasted_iota(I32, (CHUNK, CHUNK), 0)
        col = lax.broadcasted_iota(I32, (CHUNK, CHUNK), 1)
        for g in range(N_GROUPS_A):
            wc = jnp.where(row >= col, mix_ref[g], 0.0).astype(BF16)
            cols = slice(g * GROUP_DIM_A, (g + 1) * GROUP_DIM_A)
            for c in range(tm // CHUNK):
                rows = slice(c * CHUNK, (c + 1) * CHUNK)
                mixed = jnp.dot(wc, v_scr[rows, cols].astype(BF16), preferred_element_type=F32)
                mixed = mixed + bias_ref[:, cols]
                p_scr[rows, cols] = (u_scr[rows, cols] * mixed).astype(BF16)
    o_ref[...] = x + jnp.dot(p_scr[...], wout_ref[...], preferred_element_type=F32)


def _mixer_a(x, ng, win, lng, lnb, mix, bias, wout, *, tm, single_token_chunks):
    n = x.shape[0]
    out_shape = [jax.ShapeDtypeStruct((n, D_MODEL), F32)]
    out_specs = [_rows(tm, D_MODEL)]
    if single_token_chunks:
        out_shape.append(jax.ShapeDtypeStruct((n, D_GATE), F32))
        out_specs.append(_rows(tm, D_GATE))
    return pl.pallas_call(
        functools.partial(_mixer_a_kernel, tm=tm, single_token_chunks=single_token_chunks),
        grid=(n // tm,),
        in_specs=[_rows(tm, D_MODEL), _resident(ng.shape), _resident(win.shape), _resident(lng.shape),
                  _resident(lnb.shape), _resident(mix.shape), _resident(bias.shape), _resident(wout.shape)],
        out_specs=out_specs,
        out_shape=out_shape,
        scratch_shapes=[pltpu.VMEM((tm, D_GATE), F32), pltpu.VMEM((tm, D_GATE), F32),
                        pltpu.VMEM((tm, D_GATE), BF16)],
        compiler_params=_cparams(),
        name="mixer_a_sample" if single_token_chunks else "mixer_a_prompt",
    )(x, ng, win, lng, lnb, mix, bias, wout)


def _route_rows(x, g, whi_t, wlo_t, bias_col, run_scr):
    tm = x.shape[0]
    h = _rms(x, g)
    hh = h.astype(BF16)
    h_rounded = hh.astype(F32)
    hl = (h - h_rounded).astype(BF16)
    nt = (((1,), (1,)), ((), ()))
    logits = (lax.dot_general(whi_t, hh, nt, preferred_element_type=F32)
              + lax.dot_general(wlo_t, hh, nt, preferred_element_type=F32)
              + lax.dot_general(whi_t, hl, nt, preferred_element_type=F32) + bias_col)
    e = lax.broadcasted_iota(I32, (N_EXPERTS, tm), 0).astype(F32)
    hot, vals, idxs = [], [], []
    for _ in range(TOP_K):
        m = jnp.max(logits, axis=0, keepdims=True)
        idx = jnp.min(jnp.where(logits == m, e, float(N_EXPERTS)), axis=0, keepdims=True)
        oh = e == idx
        hot.append(oh)
        vals.append(m)
        idxs.append(idx)
        logits = jnp.where(oh, -jnp.inf, logits)
    ex = [jnp.exp(v - vals[0]) for v in vals]
    den = ex[0] + ex[1] + ex[2] + ex[3]
    picked = jnp.zeros((N_EXPERTS, tm), F32)
    for oh in hot:
        picked = picked + jnp.where(oh, 1.0, 0.0)
    r = lax.broadcasted_iota(I32, (tm, tm), 0)
    c = lax.broadcasted_iota(I32, (tm, tm), 1)
    earlier = jnp.where(r < c, 1.0, 0.0).astype(BF16)
    run = run_scr[:, 0:1]
    pos = jnp.dot(picked.astype(BF16), earlier, preferred_element_type=F32) + run
    run_scr[:, 0:1] = run + jnp.sum(picked, axis=1, keepdims=True)
    row = lax.broadcasted_iota(I32, (REC_ROWS, tm), 0)
    rec = jnp.zeros((REC_ROWS, tm), F32)
    for k in range(TOP_K):
        pos_k = jnp.sum(jnp.where(hot[k], pos, 0.0), axis=0, keepdims=True)
        rec = jnp.where(row == k, idxs[k], rec)
        rec = jnp.where(row == TOP_K + k, pos_k, rec)
        rec = jnp.where(row == 2 * TOP_K + k, ex[k] / den, rec)
    return h_rounded, rec


def _router_kernel(xp_ref, xs_ref, g_ref, whi_ref, wlo_ref, b_ref, h_ref, slabp_ref, slabs_ref, rec_ref, cnt_ref,
                   run_scr, *, n_tiles, n_s):
    i = pl.program_id(0)

    @pl.when(i == 0)
    def _():
        run_scr[...] = jnp.zeros_like(run_scr)

    def pad_rows(rec):
        return jnp.concatenate([rec, jnp.zeros((SLAB - REC_ROWS, rec.shape[1]), F32)], axis=0)

    @pl.when(i < n_tiles)
    def _():
        h, rec = _route_rows(xp_ref[...], g_ref[...], whi_ref[...], wlo_ref[...], b_ref[...], run_scr)
        _store_packed_tiles(h_ref, h, h.shape[0])
        slabp_ref[...] = pad_rows(rec).T
        rec_ref[...] = rec[0:2 * TOP_K, :]

    @pl.when(i == n_tiles)
    def _():
        h, rec = _route_rows(xs_ref[...], g_ref[...], whi_ref[...], wlo_ref[...], b_ref[...], run_scr)
        h_ref[n_s * PACKED_SUBLANES:, :] = jnp.zeros((h_ref.shape[0] - n_s * PACKED_SUBLANES, V7X_LANES), U32)
        _store_packed_tiles(h_ref, h, n_s)
        slabs_ref[...] = pad_rows(rec).T
        rec_ref[...] = jnp.zeros_like(rec_ref)
        rec_ref[:, 0:n_s] = rec[0:2 * TOP_K, :]

    cnt_ref[...] = run_scr[...]


def _store_packed_tiles(ref, x, n):
    half = D_MODEL // 2
    lo = pltpu.bitcast(x[:, :half], U32)
    hi = pltpu.bitcast(x[:, half:], U32)
    word = lax.shift_right_logical(lo, jnp.uint32(16)) | (hi & jnp.uint32(0xFFFF0000))
    for s in range(PACKED_SUBLANES):
        ref[pl.ds(s, n, stride=PACKED_SUBLANES), :] = word[:, s * V7X_LANES:(s + 1) * V7X_LANES]


def _load_packed_tiles(ref, n):
    words = [ref[pl.ds(s, n, stride=PACKED_SUBLANES), :] for s in range(PACKED_SUBLANES)]
    lo = [pltpu.bitcast(w << jnp.uint32(16), F32) for w in words]
    hi = [pltpu.bitcast(w & jnp.uint32(0xFFFF0000), F32) for w in words]
    return jnp.concatenate(lo + hi, axis=1).astype(BF16)


def _store_token_tiles(ref, x, n):
    for s in range(TOKEN_SUBLANES):
        ref[pl.ds(s, n, stride=TOKEN_SUBLANES), :] = x[:, s * V7X_LANES:(s + 1) * V7X_LANES]


def _router(xp, xs, g, whi, wlo, b, *, tm):
    n_p, n_s = xp.shape[0], xs.shape[0]
    n_tiles = n_p // tm
    last = n_tiles - 1
    return pl.pallas_call(
        functools.partial(_router_kernel, n_tiles=n_tiles, n_s=n_s),
        grid=(n_tiles + 1,),
        in_specs=[pl.BlockSpec((tm, D_MODEL), lambda i: (jnp.minimum(i, last), 0)), _resident(xs.shape),
                  _resident(g.shape), _resident(whi.shape), _resident(wlo.shape), _resident(b.shape)],
        out_specs=[_rows(tm * PACKED_SUBLANES, V7X_LANES),
                   pl.BlockSpec((tm, SLAB), lambda i: (jnp.minimum(i, last), 0)),
                   pl.BlockSpec((n_s, SLAB), lambda i: (0, 0)),
                   pl.BlockSpec((2 * TOP_K, tm), lambda i: (0, i)),
                   pl.BlockSpec((N_EXPERTS, V7X_LANES), lambda i: (0, 0))],
        out_shape=[jax.ShapeDtypeStruct(((n_p + tm) * PACKED_SUBLANES, V7X_LANES), U32),
                   jax.ShapeDtypeStruct((n_p, SLAB), F32),
                   jax.ShapeDtypeStruct((n_s, SLAB), F32),
                   jax.ShapeDtypeStruct((2 * TOP_K, n_p + tm), F32),
                   jax.ShapeDtypeStruct((N_EXPERTS, V7X_LANES), F32)],
        scratch_shapes=[pltpu.VMEM((N_EXPERTS, V7X_LANES), F32)],
        compiler_params=_cparams(),
        name="moe_router",
    )(xp, xs, g, whi, wlo, b)


def _dispatch_kernel(fill_start_ref, fill_len_ref, dest_ref, h_ref, xs_ref, zbuf, sem, fill_sem,
                     *, tm, n_tiles, n_last, rows, n_blocks):
    i = pl.program_id(0)
    ts = PACKED_SUBLANES

    def tile_copy(r, k):
        d = dest_ref[0, 0, k * tm + r]
        src = h_ref.at[pl.ds(pl.multiple_of(r * ts, ts), ts)]
        dst = xs_ref.at[pl.ds(pl.multiple_of(d * ts, ts), ts)]
        return pltpu.make_async_copy(src, dst, sem.at[k % 2])

    def scatter(n):
        def body(r, carry):
            for k in range(TOP_K):
                tile_copy(r, k).start(priority=k % 2)
            return carry
        lax.fori_loop(0, n, body, 0, unroll=4)
        for k in range(TOP_K):
            pltpu.make_async_copy(h_ref.at[pl.ds(0, n * ts)], xs_ref.at[pl.ds(0, n * ts)], sem.at[k % 2]).wait()

    def fill_copy(e, bit):
        size = 1 << bit
        length = fill_len_ref[e]
        done = length - (length & (2 * size - 1))
        off = pl.multiple_of((fill_start_ref[e] + done) * ts, ts)
        return (length & size) != 0, pltpu.make_async_copy(zbuf.at[pl.ds(0, size * ts)],
                                                            xs_ref.at[pl.ds(off, size * ts)], fill_sem)

    def tail_copy(blk):
        off = pl.multiple_of(blk * (rows * ts), rows * ts)
        return pltpu.make_async_copy(zbuf, xs_ref.at[pl.ds(off, rows * ts)], fill_sem)

    def for_each_fill(act):
        for e in range(N_EXPERTS):
            for bit in range((rows - 1).bit_length()):
                go, cp = fill_copy(e, bit)

                @pl.when(go)
                def _():
                    act(cp)

        def body(blk, carry):
            act(tail_copy(blk))
            return carry
        lax.fori_loop(fill_start_ref[N_EXPERTS], n_blocks, body, 0)

    @pl.when(i == 0)
    def _():
        zbuf[...] = jnp.zeros_like(zbuf)
        for_each_fill(lambda cp: cp.start())

    @pl.when(i < n_tiles - 1)
    def _():
        scatter(tm)

    @pl.when(i == n_tiles - 1)
    def _():
        scatter(n_last)
        for_each_fill(lambda cp: cp.wait())


def _dispatch(fill_start, fill_len, dest3, h_tiles, n_all, n_blocks, *, tm, rows):
    n_tiles = dest3.shape[0]
    n_last = n_all - (n_tiles - 1) * tm
    ts = PACKED_SUBLANES
    n_slots = n_blocks * rows
    grid_spec = pltpu.PrefetchScalarGridSpec(
        num_scalar_prefetch=2,
        grid=(n_tiles,),
        in_specs=[pl.BlockSpec((1, 1, TOP_K * tm), lambda i, *_: (i, 0, 0), memory_space=pltpu.SMEM),
                  _rows(tm * ts, V7X_LANES)],
        out_specs=pl.BlockSpec(memory_space=pl.ANY),
        scratch_shapes=[pltpu.VMEM((rows * ts, V7X_LANES), U32),
                        pltpu.SemaphoreType.DMA((2,)), pltpu.SemaphoreType.DMA],
    )
    return pl.pallas_call(
        functools.partial(_dispatch_kernel, tm=tm, n_tiles=n_tiles, n_last=n_last, rows=rows, n_blocks=n_blocks),
        grid_spec=grid_spec,
        out_shape=jax.ShapeDtypeStruct((n_slots * ts, V7X_LANES), U32),
        compiler_params=_cparams(),
        name="moe_dispatch",
    )(fill_start, fill_len, dest3, h_tiles)


def _expert_kernel(be_ref, nact_ref, run_ref, rune_ref, valid_ref, x_ref, wgu_hbm, wdn_hbm, *refs, rows, layer):
    bps = EXPERT_BLOCKS_PER_STEP
    bias_refs, (o_ref, wgu_f32, wdn_f32, wsem, wgu_bf, wdn_bf) = refs[:2 * bps], refs[2 * bps:]
    nact = nact_ref[0]

    def weight_copies(r, slot):
        e = rune_ref[r]
        return (pltpu.make_async_copy(wgu_hbm.at[layer, e], wgu_f32.at[slot], wsem.at[0, slot]),
                pltpu.make_async_copy(wdn_hbm.at[layer, e], wdn_f32.at[slot], wsem.at[1, slot]))

    @pl.when(pl.program_id(0) == 0)
    def _():
        for cp in weight_copies(0, 0):
            cp.start()

    for part in range(bps):
        _expert_block(pl.program_id(0) * bps + part, nact, run_ref, rune_ref, valid_ref, weight_copies,
                      x_ref.at[pl.ds(part * rows * PACKED_SUBLANES, rows * PACKED_SUBLANES)],
                      bias_refs[2 * part], bias_refs[2 * part + 1],
                      o_ref.at[pl.ds(part * rows * TOKEN_SUBLANES, rows * TOKEN_SUBLANES)],
                      wgu_f32, wdn_f32, wgu_bf, wdn_bf, rows)


def _expert_block(b, nact, run_ref, rune_ref, valid_ref, weight_copies, x_ref, bgu_ref, bdn_ref, o_ref,
                  wgu_f32, wdn_f32, wgu_bf, wdn_bf, rows):
    @pl.when(b < nact)
    def _():
        r = run_ref[b]
        valid = valid_ref[b]
        slot = lax.rem(r, 2)
        changed = jnp.logical_or(b == 0, r != run_ref[jnp.maximum(b - 1, 0)])

        @pl.when(changed)
        def _():
            for cp in weight_copies(r, slot):
                cp.wait()

            @pl.when(r + 1 < rune_ref[N_EXPERTS])
            def _():
                for cp in weight_copies(r + 1, 1 - slot):
                    cp.start(priority=1)

            step = 256
            for c in range(D_MODEL // step):
                wgu_bf[c * step:(c + 1) * step, :] = wgu_f32[slot, c * step:(c + 1) * step, :].astype(BF16)
                wdn_bf[c * step:(c + 1) * step, :] = wdn_f32[slot, c * step:(c + 1) * step, :].astype(BF16)

        def ffn(m):
            xb = _load_packed_tiles(x_ref, m)
            nc = 512
            chunks = range(D_EXPERT // nc)
            gcols = [slice(c * nc, (c + 1) * nc) for c in chunks]
            ucols = [slice(D_EXPERT + c * nc, D_EXPERT + (c + 1) * nc) for c in chunks]
            gates = [jnp.dot(xb, wgu_bf[:, gcols[c]], preferred_element_type=F32) + bgu_ref[:, gcols[c]]
                     for c in chunks]
            ups = [jnp.dot(xb, wgu_bf[:, ucols[c]], preferred_element_type=F32) + bgu_ref[:, ucols[c]]
                   for c in chunks]
            acts = []
            for gate, up in zip(gates, ups):
                gate = jnp.minimum(gate, SWIGLU_LIMIT)
                up = jnp.clip(up, -SWIGLU_LIMIT, SWIGLU_LIMIT)
                acts.append(((up + 1.0) * gate * (1.0 / (1.0 + jnp.exp(-SWIGLU_ALPHA * gate)))).astype(BF16))
            out = None
            for c in chunks:
                part = jnp.dot(acts[c], wdn_bf[gcols[c], :], preferred_element_type=F32)
                out = part if out is None else out + part
            _store_token_tiles(o_ref, out + bdn_ref[...], m)
            if m < rows:
                o_ref[m * TOKEN_SUBLANES:, :] = jnp.zeros(((rows - m) * TOKEN_SUBLANES, V7X_LANES), F32)

        lightly_filled = valid <= rows // 2

        @pl.when(lightly_filled)
        def _():
            ffn(rows // 2)

        @pl.when(jnp.logical_not(lightly_filled))
        def _():
            ffn(rows)

    @pl.when(b >= nact)
    def _():
        o_ref[...] = jnp.zeros_like(o_ref)


def _experts(block_e, n_active, block_run, run_e, block_valid, xs_tiles, wgu, bgu, wdn, bdn, *, rows, layer):
    ts = TOKEN_SUBLANES
    bps = EXPERT_BLOCKS_PER_STEP
    nb = xs_tiles.shape[0] // (rows * PACKED_SUBLANES)
    assert nb % bps == 0
    bias_specs, bias_args = [], []
    for part in range(bps):
        def index(s, be, *_, part=part):
            return (layer, be[s * bps + part], 0, 0)
        bias_specs += [pl.BlockSpec((None, None, 1, 2 * D_EXPERT), index), pl.BlockSpec((None, None, 1, D_MODEL), index)]
        bias_args += [bgu, bdn]
    grid_spec = pltpu.PrefetchScalarGridSpec(
        num_scalar_prefetch=5,
        grid=(nb // bps,),
        in_specs=[
            pl.BlockSpec((bps * rows * PACKED_SUBLANES, V7X_LANES),
                         lambda s, be, na, *_: (jnp.minimum(s, (na[0] - 1) // bps), 0)),
            pl.BlockSpec(memory_space=pl.ANY), pl.BlockSpec(memory_space=pl.ANY),
        ] + bias_specs,
        out_specs=pl.BlockSpec((bps * rows * ts, V7X_LANES), lambda s, *_: (s, 0)),
        scratch_shapes=[pltpu.VMEM((2, D_MODEL, 2 * D_EXPERT), F32), pltpu.VMEM((2, D_EXPERT, D_MODEL), F32),
                        pltpu.SemaphoreType.DMA((2, 2)),
                        pltpu.VMEM((D_MODEL, 2 * D_EXPERT), BF16), pltpu.VMEM((D_EXPERT, D_MODEL), BF16)],
    )
    return pl.pallas_call(
        functools.partial(_expert_kernel, rows=rows, layer=layer),
        grid_spec=grid_spec,
        out_shape=jax.ShapeDtypeStruct((nb * rows * ts, V7X_LANES), F32),
        compiler_params=_cparams(),
        name="moe_experts",
    )(block_e, n_active, block_run, run_e, block_valid, xs_tiles, wgu, wdn, *bias_args)


def _combine_kernel(dest_first_ref, dest_next_ref, x_ref, slab_ref, ebuf_ref, *refs, tm, n_tiles, attn_pre):
    if attn_pre:
        (kvg_ref, wk_ref, wv_ref, kng_ref, bng_ref, wq_ref, qng_ref, cos_ref, sin_ref,
         o_ref, k_ref, kdup_ref, v_ref, q_ref, rbuf, sem) = refs
    else:
        o_ref, rbuf, sem = refs
    i = pl.program_id(0)
    slot = lax.rem(i, 2)
    ts = TOKEN_SUBLANES

    def gather(dest_ref, s):
        def body(r, carry):
            for k in range(TOP_K):
                d = dest_ref[0, 0, k * tm + r]
                src = ebuf_ref.at[pl.ds(pl.multiple_of(d * ts, ts), ts)]
                dst = rbuf.at[s, k, pl.ds(pl.multiple_of(r * ts, ts), ts)]
                pltpu.make_async_copy(src, dst, sem.at[s, k % 2]).start(priority=k % 2)
            return carry
        lax.fori_loop(0, tm, body, 0, unroll=4)

    @pl.when(i == 0)
    def _():
        gather(dest_first_ref, 0)

    if attn_pre:
        todo = [(r, k) for r in range(tm) for k in range(TOP_K)]

        def tick(n):
            for r, k in todo[:n]:
                d = dest_next_ref[0, 0, k * tm + r]
                src = ebuf_ref.at[pl.ds(pl.multiple_of(d * ts, ts), ts)]
                pltpu.make_async_copy(src, rbuf.at[1 - slot, k, pl.ds(r * ts, ts)],
                                      sem.at[1 - slot, k % 2]).start(priority=k % 2)
            del todo[:n]
    else:
        @pl.when(i + 1 < n_tiles)
        def _():
            gather(dest_next_ref, 1 - slot)

        def tick(n):
            pass

    def wait_slot(s):
        for k in range(TOP_K):
            pltpu.make_async_copy(ebuf_ref.at[pl.ds(0, tm * ts)], rbuf.at[s, k], sem.at[s, k % 2]).wait()

    wait_slot(slot)
    slab = slab_ref[...]
    gates = [slab[:, 2 * TOP_K + k:2 * TOP_K + k + 1] for k in range(TOP_K)]
    per_phase = TOP_K * tm // (4 * ts)
    for s in range(ts):
        cols = slice(s * V7X_LANES, (s + 1) * V7X_LANES)
        acc = x_ref[:, cols]
        for k in range(TOP_K):
            acc = acc + gates[k] * rbuf[slot, k, pl.ds(s, tm, stride=ts), :]
        o_ref[:, cols] = acc
        tick(per_phase)
    if attn_pre:
        x2 = o_ref[...]
        cos2, sin2 = cos_ref[...], sin_ref[...]
        kv_phases = 1 + N_KV_HEADS
        q_phases = 1 + D_MODEL // PAIR
        per_phase = len(todo) // (kv_phases + q_phases)
        _shared_kv(x2, kvg_ref[...], wk_ref, wv_ref, kng_ref[...], cos2, sin2, k_ref, kdup_ref, v_ref,
                   tick=lambda: tick(per_phase))
        hq = _rms(x2, bng_ref[...]).astype(BF16)
        q = jnp.dot(hq, wq_ref[...], preferred_element_type=F32)
        tick(per_phase)
        for p in range(D_MODEL // PAIR):
            cols = slice(p * PAIR, (p + 1) * PAIR)
            q_ref[:, cols] = (_pair_norm_rope(q[:, cols], qng_ref[...], cos2, sin2) * HEAD_DIM ** -0.5).astype(BF16)
            tick(per_phase)
        tick(len(todo))

        @pl.when(i == n_tiles - 1)
        def _():
            wait_slot(1 - slot)


def _shared_kv(x, kvg, wk_ref, wv_ref, kng2, cos2, sin2, k_ref, kdup_ref, v_ref, tick=lambda: None):
    h = _rms(x, kvg).astype(BF16)
    kd = jnp.dot(h, wk_ref[...], preferred_element_type=F32)
    v_ref[...] = jnp.dot(h, wv_ref[...], preferred_element_type=F32)
    tick()
    first_half = lax.broadcasted_iota(I32, (x.shape[0], PAIR), 1) < HEAD_DIM
    slabs = []
    for p in range(N_KV_HEADS):
        cols = slice(p * PAIR, (p + 1) * PAIR)
        slab = _pair_norm_rope(kd[:, cols], kng2, cos2, sin2)
        kdup_ref[:, cols] = slab
        slabs.append(slab)
        tick()
    for p in range(D_KV // PAIR):
        k_ref[:, p * PAIR:(p + 1) * PAIR] = jnp.where(first_half, slabs[2 * p], slabs[2 * p + 1])


def _combine(x, slab, dest3, ebuf, *, tm, attn_pre=None):
    n = x.shape[0]
    n_tiles = n // tm
    in_specs = [
        pl.BlockSpec((1, 1, TOP_K * tm), lambda i: (0, 0, 0), memory_space=pltpu.SMEM),
        pl.BlockSpec((1, 1, TOP_K * tm), lambda i: (jnp.minimum(i + 1, n_tiles - 1), 0, 0),
                     memory_space=pltpu.SMEM),
        _rows(tm, D_MODEL), _rows(tm, SLAB), pl.BlockSpec(memory_space=pl.ANY),
    ]
    args = [dest3, dest3, x, slab, ebuf]
    out_specs = [_rows(tm, D_MODEL)]
    out_shape = [jax.ShapeDtypeStruct((n, D_MODEL), F32)]
    if attn_pre is not None:
        *weights, cos2, sin2, table_tiles = attn_pre
        table_spec = pl.BlockSpec((tm, PAIR), lambda i: (lax.rem(i, table_tiles), 0))
        in_specs += [_resident(w.shape) for w in weights] + [table_spec, table_spec]
        args += list(weights) + [cos2, sin2]
        out_specs += [_rows(tm, D_KV), _rows(tm, 2 * D_KV), _rows(tm, D_KV), _rows(tm, D_MODEL)]
        out_shape += [jax.ShapeDtypeStruct((n, D_KV), F32), jax.ShapeDtypeStruct((n, 2 * D_KV), F32),
                      jax.ShapeDtypeStruct((n, D_KV), F32), jax.ShapeDtypeStruct((n, D_MODEL), BF16)]
    return pl.pallas_call(
        functools.partial(_combine_kernel, tm=tm, n_tiles=n_tiles, attn_pre=attn_pre is not None),
        grid=(n_tiles,),
        in_specs=in_specs, out_specs=out_specs, out_shape=out_shape,
        scratch_shapes=[pltpu.VMEM((2, TOP_K, tm * TOKEN_SUBLANES, V7X_LANES), F32),
                        pltpu.SemaphoreType.DMA((2, 2))],
        compiler_params=_cparams(),
        name="moe_combine_attn_pre" if attn_pre is not None else "moe_combine",
    )(*args)


def _moe(xs, g, w_router, b_router, wgu, bgu, wdn, bdn, layer, prompt_attn_pre=None):
    n_all = xs[0].shape[0] + xs[1].shape[0]
    w_t = w_router.T
    whi = w_t.astype(BF16)
    wlo = (w_t - whi.astype(F32)).astype(BF16)
    h_all, slab_p, slab_s, rec, cnt = _router(xs[0], xs[1], g.reshape(1, D_MODEL), whi, wlo,
                                              b_router.reshape(N_EXPERTS, 1), tm=TM_DENSE)
    slabs = [slab_p, slab_s]
    idx = rec[:TOP_K, :n_all].astype(I32)
    pos = rec[TOP_K:, :n_all].astype(I32)
    counts = cnt[:, 0].astype(I32)
    rows = MOE_ROWS
    nb = (n_all * TOP_K + N_EXPERTS * (rows - 1)) // rows
    nb = -(-nb // EXPERT_BLOCKS_PER_STEP) * EXPERT_BLOCKS_PER_STEP
    padded =(counts + rows - 1) // rows * rows
    pad_end = jnp.cumsum(padded)
    pad_start = pad_end - padded
    eid = jnp.arange(N_EXPERTS, dtype=I32)
    has_rows = counts > 0
    dest = pos + jnp.sum(jnp.where(idx[None] == eid[:, None, None], pad_start[:, None, None], 0), axis=0)
    n_active = (pad_end[-1] // rows).astype(I32)
    blk = jnp.arange(nb, dtype=I32)
    block_e = jnp.sum((pad_end[None, :] <= (blk * rows)[:, None]).astype(I32), axis=1)
    last_e = jnp.max(jnp.where(has_rows, eid, 0))
    block_e = jnp.where(blk < n_active, jnp.minimum(block_e, N_EXPERTS - 1), last_e)

    def tile_table(d, tm):
        n_tiles = -(-d.shape[1] // tm)
        d = jnp.pad(d, ((0, 0), (0, n_tiles * tm - d.shape[1])))
        return d.reshape(TOP_K, n_tiles, tm).transpose(1, 0, 2).reshape(n_tiles, 1, TOP_K * tm)

    fill_start = jnp.concatenate([pad_start + counts, n_active.reshape(1)])
    xs_tiles = _dispatch(fill_start, padded - counts, tile_table(dest, TM_DISPATCH), h_all,
                         n_all, nb, tm=TM_DISPATCH, rows=rows)
    rank = jnp.cumsum(has_rows.astype(I32)) - 1
    run_e = jnp.sum(jnp.where(has_rows[None, :] & (rank[None, :] == eid[:, None]), eid[None, :], 0), axis=1)
    run_e = jnp.concatenate([run_e, jnp.sum(has_rows.astype(I32)).reshape(1)]).astype(I32)
    mine = block_e[:, None] == eid
    block_run = jnp.sum(jnp.where(mine, rank, 0), axis=1).astype(I32)
    block_valid = jnp.sum(jnp.where(mine, counts + pad_start, 0), axis=1) - blk * rows
    block_valid = jnp.clip(block_valid, 0, rows).astype(I32)
    n_l = wgu.shape[0]
    ebuf = _experts(block_e, n_active.reshape(1), block_run, run_e, block_valid, xs_tiles,
                    wgu, bgu.reshape(n_l, N_EXPERTS, 1, 2 * D_EXPERT), wdn, bdn.reshape(n_l, N_EXPERTS, 1, D_MODEL),
                    rows=rows, layer=layer)
    outs, off = [], 0
    for x, slab, pre in zip(xs, slabs, (prompt_attn_pre, None)):
        n = x.shape[0]
        tm = min(TM_COMBINE, n)
        outs.append(_combine(x, slab, tile_table(dest[:, off:off + n], tm), ebuf, tm=tm, attn_pre=pre))
        off += n
    return outs


def _kvq_sample_kernel(x_ref, kvg_ref, wk_ref, wv_ref, kng_ref, cos_ref, sin_ref, bng_ref, wq_ref, qng_ref,
                       k_ref, kdup_ref, v_ref, q_ref):
    x = x_ref[...]
    cos2, sin2 = cos_ref[...], sin_ref[...]
    _shared_kv(x, kvg_ref[...], wk_ref, wv_ref, kng_ref[...], cos2, sin2, k_ref, kdup_ref, v_ref)
    hq = _rms(x, bng_ref[...]).astype(BF16)
    q = jnp.dot(hq, wq_ref[...], preferred_element_type=F32)
    for p in range(D_MODEL // PAIR):
        cols = slice(p * PAIR, (p + 1) * PAIR)
        q_ref[:, cols] = _pair_norm_rope(q[:, cols], qng_ref[...], cos2, sin2)


def _kvq_sample(x, kvg, wk, wv, kng2, cos2, sin2, bng, wq, qng2):
    n = x.shape[0]
    args = [x, kvg, wk, wv, kng2, cos2, sin2, bng, wq, qng2]
    return pl.pallas_call(
        _kvq_sample_kernel,
        grid=(1,),
        in_specs=[_resident(a.shape) for a in args],
        out_specs=[_rows(n, D_KV), _rows(n, 2 * D_KV), _rows(n, D_KV), _rows(n, D_MODEL)],
        out_shape=[jax.ShapeDtypeStruct((n, D_KV), F32), jax.ShapeDtypeStruct((n, 2 * D_KV), F32),
                   jax.ShapeDtypeStruct((n, D_KV), F32), jax.ShapeDtypeStruct((n, D_MODEL), F32)],
        compiler_params=_cparams(),
        name="kvq_proj_sample",
    )(*args)


def _attn_prompt_kernel(sinks_ref, x_ref, q_ref, kc_ref, kp_ref, vc_ref, vp_ref, wo_ref, o_ref, a_scr,
                        *, tm, tiles_per_seq):
    i = pl.program_id(0)

    kj = lax.broadcasted_iota(I32, (2 * WINDOW, WINDOW), 0)
    qi = lax.broadcasted_iota(I32, (2 * WINDOW, WINDOW), 1)
    diff = qi + WINDOW - kj
    band = jnp.logical_and(diff >= 0, diff < WINDOW)
    first_half = lax.broadcasted_iota(I32, (WINDOW, PAIR), 1) < HEAD_DIM
    keep = (jnp.where(first_half, 1.0, 0.0).astype(BF16), jnp.where(first_half, 0.0, 1.0).astype(BF16))
    first_kj = jnp.where(lax.rem(i, tiles_per_seq) == 0, WINDOW, 0)

    for blk in range(tm // WINDOW):
        rows = slice(blk * WINDOW, (blk + 1) * WINDOW)
        if blk == 0:
            kk = jnp.concatenate([kp_ref[...], kc_ref[rows, :]], axis=0).astype(BF16)
            vv = jnp.concatenate([vp_ref[...], vc_ref[rows, :]], axis=0)
            valid = jnp.logical_and(band, kj >= first_kj)
        else:
            win = slice((blk - 1) * WINDOW, (blk + 1) * WINDOW)
            kk = kc_ref[win, :].astype(BF16)
            vv = vc_ref[win, :]
            valid = band
        for kvh in range(N_KV_HEADS):
            kk_h = kk[:, kvh * PAIR:(kvh + 1) * PAIR]
            vcols = slice((kvh // 2) * PAIR, (kvh // 2 + 1) * PAIR)
            vt = vv[:, vcols].T.astype(BF16)
            ch = slice((kvh % 2) * HEAD_DIM, (kvh % 2 + 1) * HEAD_DIM)
            outs = []
            for g in range(Q_PER_KV):
                head = kvh * Q_PER_KV + g
                qm = q_ref[rows, (head // 2) * PAIR:(head // 2 + 1) * PAIR] * keep[head % 2]
                sink = sinks_ref[head]
                st = lax.dot_general(kk_h, qm, (((1,), (1,)), ((), ())), preferred_element_type=F32)
                st = jnp.where(valid, st, -jnp.inf)
                m = jnp.maximum(jnp.max(st, axis=0, keepdims=True), sink)
                p = jnp.exp(st - m)
                den = jnp.sum(p, axis=0, keepdims=True) + jnp.exp(sink - m)
                ot = jnp.dot(vt, p.astype(BF16), preferred_element_type=F32)
                outs.append(ot[ch, :] * (1.0 / den))
            for pq in range(Q_PER_KV // 2):
                pair = (kvh * Q_PER_KV) // 2 + pq
                both = jnp.concatenate([outs[2 * pq], outs[2 * pq + 1]], axis=0)
                a_scr[rows, pair * PAIR:(pair + 1) * PAIR] = both.T.astype(BF16)
    o_ref[...] = x_ref[...] + jnp.dot(a_scr[...], wo_ref[...], preferred_element_type=F32)


def _attn_prompt(x, q, kdup, v, sinks, wo, *, tm, seq):
    n = x.shape[0]
    tiles_per_seq = seq // tm
    blocks_per_tile = tm // WINDOW

    def prev_spec(width):
        return pl.BlockSpec((WINDOW, width), lambda i: (jnp.maximum(i * blocks_per_tile - 1, 0), 0))

    return pl.pallas_call(
        functools.partial(_attn_prompt_kernel, tm=tm, tiles_per_seq=tiles_per_seq),
        grid=(n // tm,),
        in_specs=[pl.BlockSpec(memory_space=pltpu.SMEM), _rows(tm, D_MODEL), _rows(tm, D_MODEL),
                  _rows(tm, 2 * D_KV), prev_spec(2 * D_KV), _rows(tm, D_KV), prev_spec(D_KV), _resident(wo.shape)],
        out_specs=_rows(tm, D_MODEL),
        out_shape=jax.ShapeDtypeStruct((n, D_MODEL), F32),
        scratch_shapes=[pltpu.VMEM((tm, D_MODEL), BF16)],
        compiler_params=_cparams(),
        name="attn_prompt",
    )(sinks, x, q, kdup, kdup, v, v, wo)


def _attn_sample_kernel(q_ref, ck_ref, cv_ref, kn_ref, vn_ref, sink_ref, o_ref, *, per_step, win):
    rowi = lax.broadcasted_iota(I32, (N_HEADS, D_KV), 0)
    lanei = lax.broadcasted_iota(I32, (N_HEADS, D_KV), 1)
    group = rowi >> 2
    own_block = group == (lanei >> 6)
    j = lax.broadcasted_iota(I32, (N_HEADS, win), 1)
    diff = win - j
    valid = jnp.logical_and(diff >= 0, diff < WINDOW)
    sink = sink_ref[...]
    scale = HEAD_DIM ** -0.5
    qms = [jnp.where(own_block, q_ref[b], 0.0) for b in range(per_step)]
    ss = [lax.dot_general(qm.astype(BF16), ck_ref[b].astype(BF16), (((1,), (1,)), ((), ())),
                          preferred_element_type=F32) * scale for b, qm in enumerate(qms)]
    ps, p_news, dens = [], [], []
    for b in range(per_step):
        s = jnp.where(valid, ss[b], -jnp.inf)
        s_new = jnp.sum(qms[b] * kn_ref[b:b + 1, :], axis=-1, keepdims=True) * scale
        m = jnp.maximum(jnp.maximum(jnp.max(s, axis=-1, keepdims=True), s_new), sink)
        p = jnp.exp(s - m)
        p_new = jnp.exp(s_new - m)
        ps.append(p.astype(BF16))
        p_news.append(p_new)
        dens.append(jnp.sum(p, axis=-1, keepdims=True) + p_new + jnp.exp(sink - m))
    os_ = [jnp.dot(ps[b], cv_ref[b].astype(BF16), preferred_element_type=F32) for b in range(per_step)]
    for b in range(per_step):
        o = (os_[b] + p_news[b] * vn_ref[b:b + 1, :]) * (1.0 / dens[b])
        res = o
        for sft in range(1, N_KV_HEADS):
            res = jnp.where(group == sft, pltpu.roll(o, D_KV - sft * HEAD_DIM, 1), res)
        o_ref[b] = res[:, :HEAD_DIM]


def _attn_sample(q4, ck, cv, kn, vn, sink_col, *, per_step):
    nb, win = ck.shape[0], ck.shape[1]
    return pl.pallas_call(
        functools.partial(_attn_sample_kernel, per_step=per_step, win=win),
        grid=(nb // per_step,),
        in_specs=[pl.BlockSpec((per_step, N_HEADS, D_KV), lambda i: (i, 0, 0)),
                  pl.BlockSpec((per_step, win, D_KV), lambda i: (i, 0, 0)),
                  pl.BlockSpec((per_step, win, D_KV), lambda i: (i, 0, 0)),
                  _rows(per_step, D_KV), _rows(per_step, D_KV), _resident(sink_col.shape)],
        out_specs=pl.BlockSpec((per_step, N_HEADS, HEAD_DIM), lambda i: (i, 0, 0)),
        out_shape=jax.ShapeDtypeStruct((nb, N_HEADS, HEAD_DIM), F32),
        compiler_params=_cparams(),
        name="attn_sample",
    )(q4, ck, cv, kn, vn, sink_col)


def _residual_proj_kernel(x_ref, a_ref, w_ref, o_ref):
    o_ref[...] = x_ref[...] + jnp.dot(a_ref[...].astype(BF16), w_ref[...], preferred_element_type=F32)


def _residual_proj(x, a, w):
    n = x.shape[0]
    return pl.pallas_call(
        _residual_proj_kernel,
        grid=(1,),
        in_specs=[_rows(n, D_MODEL), _rows(n, a.shape[1]), _resident(w.shape)],
        out_specs=_rows(n, D_MODEL),
        out_shape=jax.ShapeDtypeStruct((n, D_MODEL), F32),
        compiler_params=_cparams(),
        name="attn_out_sample",
    )(x, a, w)


def _rope_tables(pos):
    half = ROPE_DIM // 2
    inv_freq = np.power(np.float32(ROPE_THETA), -np.arange(half, dtype=np.float32) / np.float32(half))
    ang = (pos.astype(np.float32)[:, None] * inv_freq[None, :].astype(np.float32)).astype(np.float32)
    cos, sin = np.cos(ang.astype(np.float64)), np.sin(ang.astype(np.float64))
    n = pos.shape[0]
    rest = HEAD_DIM - ROPE_DIM
    cos_h = np.concatenate([cos, cos, np.ones((n, rest))], axis=1)
    sin_h = np.concatenate([-sin, sin, np.zeros((n, rest))], axis=1)
    return (jnp.asarray(np.tile(cos_h, (1, 2)), dtype=F32), jnp.asarray(np.tile(sin_h, (1, 2)), dtype=F32))


def kernel(x_prompt, x_sample, cache_k, cache_v, a_norm_g, a_w_in, a_ln_g, a_ln_b, a_w_s, a_b_s, a_w_out, kv_norm_g, w_k, w_v, k_norm_g, b_norm_g, w_q, q_norm_g, sinks, w_o, ffn_norm_g, w_router, b_router, w_gu, b_gu, w_down, b_down):
    bsz, seq, _ = x_prompt.shape
    dec_b, dec_seq, _ = x_sample.shape
    assert dec_seq == 1 and seq % TM_DENSE == 0 and a_norm_g.shape[0] == 1 and b_norm_g.shape[0] == 1
    win = cache_k.shape[1]
    n_p, n_s = bsz * seq, dec_b * dec_seq
    xp = x_prompt.reshape(n_p, D_MODEL)
    xs = x_sample.reshape(n_s, D_MODEL)

    ng = a_norm_g[0].reshape(1, D_MODEL)
    win_bf = a_w_in[0].astype(BF16)
    wout_bf = a_w_out[0].astype(BF16)
    lng = a_ln_g[0].reshape(1, D_GATE)
    lnb = a_ln_b[0].reshape(1, D_GATE)
    bias_full = jnp.repeat(a_b_s[0].T, GROUP_DIM_A, axis=1)
    diag_row = jnp.repeat(a_w_s[0][:, 0, 0], GROUP_DIM_A).reshape(1, D_GATE)
    (xp,) = _mixer_a(xp, ng, win_bf, lng, lnb, a_w_s[0], bias_full, wout_bf, tm=TM_DENSE, single_token_chunks=False)
    xs, v_rows = _mixer_a(xs, ng, win_bf, lng, lnb, diag_row, bias_full[0:1], wout_bf, tm=n_s,
                          single_token_chunks=True)

    kvg = kv_norm_g.reshape(1, D_MODEL)
    bng = b_norm_g[0].reshape(1, D_MODEL)
    w_k_dup = jnp.repeat(w_k.reshape(D_MODEL, N_KV_HEADS, 1, HEAD_DIM), 2, axis=2).reshape(D_MODEL, 2 * D_KV)
    wk_bf, wv_bf, wq_bf, wo_bf = (w.astype(BF16) for w in (w_k_dup, w_v, w_q[0], w_o[0]))
    kng2 = jnp.tile(k_norm_g, 2).reshape(1, PAIR)
    qng2 = jnp.tile(q_norm_g[0], 2).reshape(1, PAIR)
    cos_p, sin_p = _rope_tables(np.arange(seq))
    cos_s, sin_s = _rope_tables(np.full((n_s,), PAST_LEN))
    attn_pre = (kvg, wk_bf, wv_bf, kng2, bng, wq_bf, qng2, cos_p, sin_p, seq // TM_COMBINE)
    (xp, k_p, kdup_p, v_p, q_p), (xs,) = _moe([xp, xs], ffn_norm_g[0], w_router[0], b_router[0], w_gu, b_gu,
                                              w_down, b_down, 0, prompt_attn_pre=attn_pre)

    k_s, _, v_s, q_s = _kvq_sample(xs, kvg, wk_bf, wv_bf, kng2, cos_s, sin_s, bng, wq_bf, qng2)
    xp = _attn_prompt(xp, q_p, kdup_p, v_p, sinks[0], wo_bf, tm=TM_DENSE, seq=seq)
    q4 = jnp.tile(q_s.reshape(n_s, N_HEADS, HEAD_DIM), (1, 1, N_KV_HEADS))
    attn_s = _attn_sample(q4, cache_k.reshape(dec_b, win, D_KV), cache_v.reshape(dec_b, win, D_KV), k_s, v_s,
                          sinks[0].reshape(N_HEADS, 1), per_step=8)
    xs = _residual_proj(xs, attn_s.reshape(n_s, D_MODEL), wo_bf)
    (xp,), (xs,) = _moe([xp, xs], ffn_norm_g[1], w_router[1], b_router[1], w_gu, b_gu, w_down, b_down, 1)

    def last_window(a):
        a = a.reshape(bsz, seq, D_KV)[:, -WINDOW:]
        return a.reshape(bsz, min(WINDOW, seq), N_KV_HEADS, HEAD_DIM)

    return (xp.reshape(bsz, seq, D_MODEL), xs.reshape(dec_b, dec_seq, D_MODEL),
            v_rows.reshape(1, dec_b, dec_seq, D_GATE), last_window(k_p), last_window(v_p),
            k_s.reshape(dec_b, dec_seq, N_KV_HEADS, HEAD_DIM), v_s.reshape(dec_b, dec_seq, N_KV_HEADS, HEAD_DIM))
```

```python
import functools

import jax
import jax.numpy as jnp
import numpy as np
from jax import lax
from jax.experimental import pallas as pl
from jax.experimental.pallas import tpu as pltpu

F32, BF16, I32 = jnp.float32, jnp.bfloat16, jnp.int32

D_MODEL = 1024
CHUNK = 128
D_GATE = 2 * D_MODEL
N_GROUPS_A = 8
GROUP_DIM_A = D_GATE // N_GROUPS_A
HEAD_DIM = 64
N_HEADS = D_MODEL // HEAD_DIM
N_KV_HEADS = N_HEADS // 4
Q_PER_KV = N_HEADS // N_KV_HEADS
D_KV = N_KV_HEADS * HEAD_DIM
WINDOW = 128
ROPE_DIM = HEAD_DIM // 4
ROPE_THETA = 500000.0
N_EXPERTS = 32
TOP_K = 4
D_EXPERT = D_MODEL
SWIGLU_LIMIT = 7.0
SWIGLU_ALPHA = 1.702
EPS = 1e-6
PAST_LEN = 8192

V7X_LANES = 128
V7X_VMEM_BYTES = 64 * 1024 * 1024
VMEM_LIMIT_BYTES = V7X_VMEM_BYTES - 8 * 1024 * 1024

TM_DENSE = 512
TM_COMBINE = 256
MOE_ROWS = 512
EXPERT_BLOCKS_PER_STEP = 1
EXPERT_HEIGHTS = 4
TM_DISPATCH = 256
TOKEN_SUBLANES = D_MODEL // V7X_LANES
PACKED_SUBLANES = TOKEN_SUBLANES // 2
U32 = jnp.uint32
SLAB = V7X_LANES
REC_ROWS = 16
PAIR = 2 * HEAD_DIM
INV_SQRT2 = 0.7071067811865476


def _cparams():
    return pltpu.CompilerParams(dimension_semantics=("arbitrary",), vmem_limit_bytes=VMEM_LIMIT_BYTES)


def _resident(shape):
    zeros = (0,) * len(shape)
    return pl.BlockSpec(shape, lambda i, *_: zeros, pipeline_mode=pl.Buffered(1))


def _rows(tm, width, offset=0):
    return pl.BlockSpec((tm, width), lambda i, *_: (i + offset, 0))


def _rms(x, g):
    return x * lax.rsqrt(jnp.mean(x * x, axis=-1, keepdims=True) + EPS) * g


def _pair_norm_rope(x2, g2, cos2, sin2):
    lane = lax.broadcasted_iota(I32, x2.shape, 1)
    r = lax.broadcasted_iota(I32, (PAIR, PAIR), 0)
    c = lax.broadcasted_iota(I32, (PAIR, PAIR), 1)
    head_mean = jnp.where((r < HEAD_DIM) == (c < HEAD_DIM), 1.0 / HEAD_DIM, 0.0).astype(BF16)
    sq = x2 * x2
    sq_hi = sq.astype(BF16)
    sq_lo = (sq - sq_hi.astype(F32)).astype(BF16)
    ms = (jnp.dot(sq_hi, head_mean, preferred_element_type=F32)
          + jnp.dot(sq_lo, head_mean, preferred_element_type=F32))
    y = x2 * lax.rsqrt(ms + EPS) * g2
    half = ROPE_DIM // 2
    up = pltpu.roll(y, PAIR - half, 1)
    dn = pltpu.roll(y, half, 1)
    partner = jnp.where((lane & (HEAD_DIM - 1)) < half, up, dn)
    return y * cos2 + partner * sin2


def _mixer_a_kernel(x_ref, ng_ref, win_ref, lng_ref, lnb_ref, mix_ref, bias_ref, wout_ref, *refs,
                    tm, single_token_chunks):
    if single_token_chunks:
        o_ref, vout_ref, u_scr, v_scr, p_scr = refs
    else:
        o_ref, u_scr, v_scr, p_scr = refs
    x = x_ref[...]
    h = _rms(x, ng_ref[...]).astype(BF16)
    nc = 512
    for c in range(2 * D_GATE // nc):
        z = jnp.dot(h, win_ref[:, c * nc:(c + 1) * nc], preferred_element_type=F32)
        z = 0.5 * z * (1.0 + lax.erf(z * INV_SQRT2))
        if c < D_GATE // nc:
            u_scr[:, c * nc:(c + 1) * nc] = z
        else:
            v_scr[:, (c - D_GATE // nc) * nc:(c - D_GATE // nc + 1) * nc] = z
    v = v_scr[...]
    vc = v - jnp.mean(v, axis=-1, keepdims=True)
    vn = vc * lax.rsqrt(jnp.mean(vc * vc, axis=-1, keepdims=True) + EPS) * lng_ref[...] + lnb_ref[...]
    if single_token_chunks:
        vout_ref[...] = vn
        p_scr[...] = (u_scr[...] * (vn * mix_ref[...] + bias_ref[...])).astype(BF16)
    else:
        v_scr[...] = vn
        row = lax.broadcasted_iota(I32, (CHUNK, CHUNK), 0)
        col = lax.broadcasted_iota(I32, (CHUNK, CHUNK), 1)
        for g in range(N_GROUPS_A):
            wc = jnp.where(row >= col, mix_ref[g], 0.0).astype(BF16)
            cols = slice(g * GROUP_DIM_A, (g + 1) * GROUP_DIM_A)
            for c in range(tm // CHUNK):
                rows = slice(c * CHUNK, (c + 1) * CHUNK)
                mixed = jnp.dot(wc, v_scr[rows, cols].astype(BF16), preferred_element_type=F32)
                mixed = mixed + bias_ref[:, cols]
                p_scr[rows, cols] = (u_scr[rows, cols] * mixed).astype(BF16)
    o_ref[...] = x + jnp.dot(p_scr[...], wout_ref[...], preferred_element_type=F32)


def _mixer_a(x, ng, win, lng, lnb, mix, bias, wout, *, tm, single_token_chunks):
    n = x.shape[0]
    out_shape = [jax.ShapeDtypeStruct((n, D_MODEL), F32)]
    out_specs = [_rows(tm, D_MODEL)]
    if single_token_chunks:
        out_shape.append(jax.ShapeDtypeStruct((n, D_GATE), F32))
        out_specs.append(_rows(tm, D_GATE))
    return pl.pallas_call(
        functools.partial(_mixer_a_kernel, tm=tm, single_token_chunks=single_token_chunks),
        grid=(n // tm,),
        in_specs=[_rows(tm, D_MODEL), _resident(ng.shape), _resident(win.shape), _resident(lng.shape),
                  _resident(lnb.shape), _resident(mix.shape), _resident(bias.shape), _resident(wout.shape)],
        out_specs=out_specs,
        out_shape=out_shape,
        scratch_shapes=[pltpu.VMEM((tm, D_GATE), F32), pltpu.VMEM((tm, D_GATE), F32),
                        pltpu.VMEM((tm, D_GATE), BF16)],
        compiler_params=_cparams(),
        name="mixer_a_sample" if single_token_chunks else "mixer_a_prompt",
    )(x, ng, win, lng, lnb, mix, bias, wout)


def _route_rows(x, g, whi_t, wlo_t, bias_col, run_scr):
    tm = x.shape[0]
    h = _rms(x, g)
    hh = h.astype(BF16)
    h_rounded = hh.astype(F32)
    hl = (h - h_rounded).astype(BF16)
    nt = (((1,), (1,)), ((), ()))
    logits = (lax.dot_general(whi_t, hh, nt, preferred_element_type=F32)
              + lax.dot_general(wlo_t, hh, nt, preferred_element_type=F32)
              + lax.dot_general(whi_t, hl, nt, preferred_element_type=F32) + bias_col)
    e = lax.broadcasted_iota(I32, (N_EXPERTS, tm), 0).astype(F32)
    hot, vals, idxs = [], [], []
    for _ in range(TOP_K):
        m = jnp.max(logits, axis=0, keepdims=True)
        idx = jnp.min(jnp.where(logits == m, e, float(N_EXPERTS)), axis=0, keepdims=True)
        oh = e == idx
        hot.append(oh)
        vals.append(m)
        idxs.append(idx)
        logits = jnp.where(oh, -jnp.inf, logits)
    ex = [jnp.exp(v - vals[0]) for v in vals]
    den = ex[0] + ex[1] + ex[2] + ex[3]
    picked = jnp.zeros((N_EXPERTS, tm), F32)
    for oh in hot:
        picked = picked + jnp.where(oh, 1.0, 0.0)
    r = lax.broadcasted_iota(I32, (tm, tm), 0)
    c = lax.broadcasted_iota(I32, (tm, tm), 1)
    earlier = jnp.where(r < c, 1.0, 0.0).astype(BF16)
    run = run_scr[:, 0:1]
    pos = jnp.dot(picked.astype(BF16), earlier, preferred_element_type=F32) + run
    run_scr[:, 0:1] = run + jnp.sum(picked, axis=1, keepdims=True)
    row = lax.broadcasted_iota(I32, (REC_ROWS, tm), 0)
    rec = jnp.zeros((REC_ROWS, tm), F32)
    for k in range(TOP_K):
        pos_k = jnp.sum(jnp.where(hot[k], pos, 0.0), axis=0, keepdims=True)
        rec = jnp.where(row == k, idxs[k], rec)
        rec = jnp.where(row == TOP_K + k, pos_k, rec)
        rec = jnp.where(row == 2 * TOP_K + k, ex[k] / den, rec)
    return h_rounded, rec


def _router_kernel(xp_ref, xs_ref, g_ref, whi_ref, wlo_ref, b_ref, h_ref, slabp_ref, slabs_ref, rec_ref, cnt_ref,
                   run_scr, *, n_tiles, n_s):
    i = pl.program_id(0)

    @pl.when(i == 0)
    def _():
        run_scr[...] = jnp.zeros_like(run_scr)

    def pad_rows(rec):
        return jnp.concatenate([rec, jnp.zeros((SLAB - REC_ROWS, rec.shape[1]), F32)], axis=0)

    @pl.when(i < n_tiles)
    def _():
        h, rec = _route_rows(xp_ref[...], g_ref[...], whi_ref[...], wlo_ref[...], b_ref[...], run_scr)
        _store_packed_tiles(h_ref, h, h.shape[0])
        slabp_ref[...] = pad_rows(rec).T
        rec_ref[...] = rec[0:2 * TOP_K, :]

    @pl.when(i == n_tiles)
    def _():
        h, rec = _route_rows(xs_ref[...], g_ref[...], whi_ref[...], wlo_ref[...], b_ref[...], run_scr)
        h_ref[n_s * PACKED_SUBLANES:, :] = jnp.zeros((h_ref.shape[0] - n_s * PACKED_SUBLANES, V7X_LANES), U32)
        _store_packed_tiles(h_ref, h, n_s)
        slabs_ref[...] = pad_rows(rec).T
        rec_ref[...] = jnp.zeros_like(rec_ref)
        rec_ref[:, 0:n_s] = rec[0:2 * TOP_K, :]

    cnt_ref[...] = run_scr[...]


def _store_packed_tiles(ref, x, n):
    half = D_MODEL // 2
    lo = pltpu.bitcast(x[:, :half], U32)
    hi = pltpu.bitcast(x[:, half:], U32)
    word = lax.shift_right_logical(lo, jnp.uint32(16)) | (hi & jnp.uint32(0xFFFF0000))
    for s in range(PACKED_SUBLANES):
        ref[pl.ds(s, n, stride=PACKED_SUBLANES), :] = word[:, s * V7X_LANES:(s + 1) * V7X_LANES]


def _load_packed_tiles(ref, n):
    words = [ref[pl.ds(s, n, stride=PACKED_SUBLANES), :] for s in range(PACKED_SUBLANES)]
    lo = [pltpu.bitcast(w << jnp.uint32(16), F32) for w in words]
    hi = [pltpu.bitcast(w & jnp.uint32(0xFFFF0000), F32) for w in words]
    return jnp.concatenate(lo + hi, axis=1).astype(BF16)


def _store_token_tiles(ref, x, n):
    for s in range(TOKEN_SUBLANES):
        ref[pl.ds(s, n, stride=TOKEN_SUBLANES), :] = x[:, s * V7X_LANES:(s + 1) * V7X_LANES]


def _router(xp, xs, g, whi, wlo, b, *, tm):
    n_p, n_s = xp.shape[0], xs.shape[0]
    n_tiles = n_p // tm
    last = n_tiles - 1
    return pl.pallas_call(
        functools.partial(_router_kernel, n_tiles=n_tiles, n_s=n_s),
        grid=(n_tiles + 1,),
        in_specs=[pl.BlockSpec((tm, D_MODEL), lambda i: (jnp.minimum(i, last), 0)), _resident(xs.shape),
                  _resident(g.shape), _resident(whi.shape), _resident(wlo.shape), _resident(b.shape)],
        out_specs=[_rows(tm * PACKED_SUBLANES, V7X_LANES),
                   pl.BlockSpec((tm, SLAB), lambda i: (jnp.minimum(i, last), 0)),
                   pl.BlockSpec((n_s, SLAB), lambda i: (0, 0)),
                   pl.BlockSpec((2 * TOP_K, tm), lambda i: (0, i)),
                   pl.BlockSpec((N_EXPERTS, V7X_LANES), lambda i: (0, 0))],
        out_shape=[jax.ShapeDtypeStruct(((n_p + tm) * PACKED_SUBLANES, V7X_LANES), U32),
                   jax.ShapeDtypeStruct((n_p, SLAB), F32),
                   jax.ShapeDtypeStruct((n_s, SLAB), F32),
                   jax.ShapeDtypeStruct((2 * TOP_K, n_p + tm), F32),
                   jax.ShapeDtypeStruct((N_EXPERTS, V7X_LANES), F32)],
        scratch_shapes=[pltpu.VMEM((N_EXPERTS, V7X_LANES), F32)],
        compiler_params=_cparams(),
        name="moe_router",
    )(xp, xs, g, whi, wlo, b)


def _dispatch_kernel(fill_start_ref, fill_len_ref, dest_ref, h_ref, xs_ref, zbuf, sem, fill_sem,
                     *, tm, n_tiles, n_last, rows, n_blocks):
    i = pl.program_id(0)
    ts = PACKED_SUBLANES

    def tile_copy(r, k):
        d = dest_ref[0, 0, k * tm + r]
        src = h_ref.at[pl.ds(pl.multiple_of(r * ts, ts), ts)]
        dst = xs_ref.at[pl.ds(pl.multiple_of(d * ts, ts), ts)]
        return pltpu.make_async_copy(src, dst, sem.at[k % 2])

    def scatter(n):
        def body(r, carry):
            for k in range(TOP_K):
                tile_copy(r, k).start(priority=k % 2)
            return carry
        lax.fori_loop(0, n, body, 0, unroll=4)
        for k in range(TOP_K):
            pltpu.make_async_copy(h_ref.at[pl.ds(0, n * ts)], xs_ref.at[pl.ds(0, n * ts)], sem.at[k % 2]).wait()

    def fill_copy(e, bit):
        size = 1 << bit
        length = fill_len_ref[e]
        done = length - (length & (2 * size - 1))
        off = pl.multiple_of((fill_start_ref[e] + done) * ts, ts)
        return (length & size) != 0, pltpu.make_async_copy(zbuf.at[pl.ds(0, size * ts)],
                                                            xs_ref.at[pl.ds(off, size * ts)], fill_sem)

    def tail_copy(blk):
        off = pl.multiple_of(blk * (rows * ts), rows * ts)
        return pltpu.make_async_copy(zbuf, xs_ref.at[pl.ds(off, rows * ts)], fill_sem)

    def for_each_fill(act):
        for e in range(N_EXPERTS):
            for bit in range((rows - 1).bit_length()):
                go, cp = fill_copy(e, bit)

                @pl.when(go)
                def _():
                    act(cp)

        def body(blk, carry):
            act(tail_copy(blk))
            return carry
        lax.fori_loop(fill_start_ref[N_EXPERTS], n_blocks, body, 0)

    @pl.when(i == 0)
    def _():
        zbuf[...] = jnp.zeros_like(zbuf)
        for_each_fill(lambda cp: cp.start())

    @pl.when(i < n_tiles - 1)
    def _():
        scatter(tm)

    @pl.when(i == n_tiles - 1)
    def _():
        scatter(n_last)
        for_each_fill(lambda cp: cp.wait())


def _dispatch(fill_start, fill_len, dest3, h_tiles, n_all, n_blocks, *, tm, rows):
    n_tiles = dest3.shape[0]
    n_last = n_all - (n_tiles - 1) * tm
    ts = PACKED_SUBLANES
    n_slots = n_blocks * rows
    grid_spec = pltpu.PrefetchScalarGridSpec(
        num_scalar_prefetch=2,
        grid=(n_tiles,),
        in_specs=[pl.BlockSpec((1, 1, TOP_K * tm), lambda i, *_: (i, 0, 0), memory_space=pltpu.SMEM),
                  _rows(tm * ts, V7X_LANES)],
        out_specs=pl.BlockSpec(memory_space=pl.ANY),
        scratch_shapes=[pltpu.VMEM((rows * ts, V7X_LANES), U32),
                        pltpu.SemaphoreType.DMA((2,)), pltpu.SemaphoreType.DMA],
    )
    return pl.pallas_call(
        functools.partial(_dispatch_kernel, tm=tm, n_tiles=n_tiles, n_last=n_last, rows=rows, n_blocks=n_blocks),
        grid_spec=grid_spec,
        out_shape=jax.ShapeDtypeStruct((n_slots * ts, V7X_LANES), U32),
        compiler_params=_cparams(),
        name="moe_dispatch",
    )(fill_start, fill_len, dest3, h_tiles)


def _expert_kernel(be_ref, nact_ref, run_ref, rune_ref, valid_ref, x_ref, wgu_hbm, wdn_hbm, *refs, rows, layer):
    bps = EXPERT_BLOCKS_PER_STEP
    bias_refs, (o_ref, wgu_f32, wdn_f32, wsem, wgu_bf, wdn_bf) = refs[:2 * bps], refs[2 * bps:]
    nact = nact_ref[0]

    def weight_copies(r, slot):
        e = rune_ref[r]
        return (pltpu.make_async_copy(wgu_hbm.at[layer, e], wgu_f32.at[slot], wsem.at[0, slot]),
                pltpu.make_async_copy(wdn_hbm.at[layer, e], wdn_f32.at[slot], wsem.at[1, slot]))

    @pl.when(pl.program_id(0) == 0)
    def _():
        for cp in weight_copies(0, 0):
            cp.start()

    for part in range(bps):
        _expert_block(pl.program_id(0) * bps + part, nact, run_ref, rune_ref, valid_ref, weight_copies,
                      x_ref.at[pl.ds(part * rows * PACKED_SUBLANES, rows * PACKED_SUBLANES)],
                      bias_refs[2 * part], bias_refs[2 * part + 1],
                      o_ref.at[pl.ds(part * rows * TOKEN_SUBLANES, rows * TOKEN_SUBLANES)],
                      wgu_f32, wdn_f32, wgu_bf, wdn_bf, rows)


def _expert_block(b, nact, run_ref, rune_ref, valid_ref, weight_copies, x_ref, bgu_ref, bdn_ref, o_ref,
                  wgu_f32, wdn_f32, wgu_bf, wdn_bf, rows):
    @pl.when(b < nact)
    def _():
        r = run_ref[b]
        valid = valid_ref[b]
        slot = lax.rem(r, 2)
        changed = jnp.logical_or(b == 0, r != run_ref[jnp.maximum(b - 1, 0)])

        @pl.when(changed)
        def _():
            for cp in weight_copies(r, slot):
                cp.wait()

            @pl.when(r + 1 < rune_ref[N_EXPERTS])
            def _():
                for cp in weight_copies(r + 1, 1 - slot):
                    cp.start(priority=1)

            step = 256
            for c in range(D_MODEL // step):
                wgu_bf[c * step:(c + 1) * step, :] = wgu_f32[slot, c * step:(c + 1) * step, :].astype(BF16)
                wdn_bf[c * step:(c + 1) * step, :] = wdn_f32[slot, c * step:(c + 1) * step, :].astype(BF16)

        def ffn(m):
            xb = _load_packed_tiles(x_ref, m)
            nc = 512
            chunks = range(D_EXPERT // nc)
            gcols = [slice(c * nc, (c + 1) * nc) for c in chunks]
            ucols = [slice(D_EXPERT + c * nc, D_EXPERT + (c + 1) * nc) for c in chunks]
            gates = [jnp.dot(xb, wgu_bf[:, gcols[c]], preferred_element_type=F32) + bgu_ref[:, gcols[c]]
                     for c in chunks]
            ups = [jnp.dot(xb, wgu_bf[:, ucols[c]], preferred_element_type=F32) + bgu_ref[:, ucols[c]]
                   for c in chunks]
            acts = []
            for gate, up in zip(gates, ups):
                gate = jnp.minimum(gate, SWIGLU_LIMIT)
                up = jnp.clip(up, -SWIGLU_LIMIT, SWIGLU_LIMIT)
                acts.append(((up + 1.0) * gate * (1.0 / (1.0 + jnp.exp(-SWIGLU_ALPHA * gate)))).astype(BF16))
            out = None
            for c in chunks:
                part = jnp.dot(acts[c], wdn_bf[gcols[c], :], preferred_element_type=F32)
                out = part if out is None else out + part
            _store_token_tiles(o_ref, out + bdn_ref[...], m)
            if m < rows:
                o_ref[m * TOKEN_SUBLANES:, :] = jnp.zeros(((rows - m) * TOKEN_SUBLANES, V7X_LANES), F32)

        quarter = rows // EXPERT_HEIGHTS
        level = (valid + quarter - 1) // quarter
        for q in range(1, EXPERT_HEIGHTS + 1):
            @pl.when(level == q if q > 1 else level <= 1)
            def _():
                ffn(q * quarter)

    @pl.when(b >= nact)
    def _():
        o_ref[...] = jnp.zeros_like(o_ref)


def _experts(block_e, n_active, block_run, run_e, block_valid, xs_tiles, wgu, bgu, wdn, bdn, *, rows, layer):
    ts = TOKEN_SUBLANES
    bps = EXPERT_BLOCKS_PER_STEP
    nb = xs_tiles.shape[0] // (rows * PACKED_SUBLANES)
    assert nb % bps == 0
    bias_specs, bias_args = [], []
    for part in range(bps):
        def index(s, be, *_, part=part):
            return (layer, be[s * bps + part], 0, 0)
        bias_specs += [pl.BlockSpec((None, None, 1, 2 * D_EXPERT), index), pl.BlockSpec((None, None, 1, D_MODEL), index)]
        bias_args += [bgu, bdn]
    grid_spec = pltpu.PrefetchScalarGridSpec(
        num_scalar_prefetch=5,
        grid=(nb // bps,),
        in_specs=[
            pl.BlockSpec((bps * rows * PACKED_SUBLANES, V7X_LANES),
                         lambda s, be, na, *_: (jnp.minimum(s, (na[0] - 1) // bps), 0)),
            pl.BlockSpec(memory_space=pl.ANY), pl.BlockSpec(memory_space=pl.ANY),
        ] + bias_specs,
        out_specs=pl.BlockSpec((bps * rows * ts, V7X_LANES), lambda s, *_: (s, 0)),
        scratch_shapes=[pltpu.VMEM((2, D_MODEL, 2 * D_EXPERT), F32), pltpu.VMEM((2, D_EXPERT, D_MODEL), F32),
                        pltpu.SemaphoreType.DMA((2, 2)),
                        pltpu.VMEM((D_MODEL, 2 * D_EXPERT), BF16), pltpu.VMEM((D_EXPERT, D_MODEL), BF16)],
    )
    return pl.pallas_call(
        functools.partial(_expert_kernel, rows=rows, layer=layer),
        grid_spec=grid_spec,
        out_shape=jax.ShapeDtypeStruct((nb * rows * ts, V7X_LANES), F32),
        compiler_params=_cparams(),
        name="moe_experts",
    )(block_e, n_active, block_run, run_e, block_valid, xs_tiles, wgu, wdn, *bias_args)


def _combine_kernel(dest_first_ref, dest_next_ref, x_ref, slab_ref, ebuf_ref, *refs, tm, n_tiles, attn_pre):
    if attn_pre:
        (kvg_ref, wk_ref, wv_ref, kng_ref, bng_ref, wq_ref, qng_ref, cos_ref, sin_ref,
         o_ref, k_ref, kdup_ref, v_ref, q_ref, rbuf, sem) = refs
    else:
        o_ref, rbuf, sem = refs
    i = pl.program_id(0)
    slot = lax.rem(i, 2)
    ts = TOKEN_SUBLANES

    def gather(dest_ref, s):
        def body(r, carry):
            for k in range(TOP_K):
                d = dest_ref[0, 0, k * tm + r]
                src = ebuf_ref.at[pl.ds(pl.multiple_of(d * ts, ts), ts)]
                dst = rbuf.at[s, k, pl.ds(pl.multiple_of(r * ts, ts), ts)]
                pltpu.make_async_copy(src, dst, sem.at[s, k % 2]).start(priority=k % 2)
            return carry
        lax.fori_loop(0, tm, body, 0, unroll=4)

    @pl.when(i == 0)
    def _():
        gather(dest_first_ref, 0)

    if attn_pre:
        todo = [(r, k) for r in range(tm) for k in range(TOP_K)]

        def tick(n):
            for r, k in todo[:n]:
                d = dest_next_ref[0, 0, k * tm + r]
                src = ebuf_ref.at[pl.ds(pl.multiple_of(d * ts, ts), ts)]
                pltpu.make_async_copy(src, rbuf.at[1 - slot, k, pl.ds(r * ts, ts)],
                                      sem.at[1 - slot, k % 2]).start(priority=k % 2)
            del todo[:n]
    else:
        @pl.when(i + 1 < n_tiles)
        def _():
            gather(dest_next_ref, 1 - slot)

        def tick(n):
            pass

    def wait_slot(s):
        for k in range(TOP_K):
            pltpu.make_async_copy(ebuf_ref.at[pl.ds(0, tm * ts)], rbuf.at[s, k], sem.at[s, k % 2]).wait()

    wait_slot(slot)
    slab = slab_ref[...]
    gates = [slab[:, 2 * TOP_K + k:2 * TOP_K + k + 1] for k in range(TOP_K)]
    per_phase = TOP_K * tm // (4 * ts)
    for s in range(ts):
        cols = slice(s * V7X_LANES, (s + 1) * V7X_LANES)
        acc = x_ref[:, cols]
        for k in range(TOP_K):
            acc = acc + gates[k] * rbuf[slot, k, pl.ds(s, tm, stride=ts), :]
        o_ref[:, cols] = acc
        tick(per_phase)
    if attn_pre:
        x2 = o_ref[...]
        cos2, sin2 = cos_ref[...], sin_ref[...]
        kv_phases = 1 + N_KV_HEADS
        q_phases = 1 + D_MODEL // PAIR
        per_phase = len(todo) // (kv_phases + q_phases)
        _shared_kv(x2, kvg_ref[...], wk_ref, wv_ref, kng_ref[...], cos2, sin2, k_ref, kdup_ref, v_ref,
                   tick=lambda: tick(per_phase))
        hq = _rms(x2, bng_ref[...]).astype(BF16)
        q = jnp.dot(hq, wq_ref[...], preferred_element_type=F32)
        tick(per_phase)
        for p in range(D_MODEL // PAIR):
            cols = slice(p * PAIR, (p + 1) * PAIR)
            q_ref[:, cols] = (_pair_norm_rope(q[:, cols], qng_ref[...], cos2, sin2) * HEAD_DIM ** -0.5).astype(BF16)
            tick(per_phase)
        tick(len(todo))

        @pl.when(i == n_tiles - 1)
        def _():
            wait_slot(1 - slot)


def _shared_kv(x, kvg, wk_ref, wv_ref, kng2, cos2, sin2, k_ref, kdup_ref, v_ref, tick=lambda: None):
    h = _rms(x, kvg).astype(BF16)
    kd = jnp.dot(h, wk_ref[...], preferred_element_type=F32)
    v_ref[...] = jnp.dot(h, wv_ref[...], preferred_element_type=F32)
    tick()
    first_half = lax.broadcasted_iota(I32, (x.shape[0], PAIR), 1) < HEAD_DIM
    slabs = []
    for p in range(N_KV_HEADS):
        cols = slice(p * PAIR, (p + 1) * PAIR)
        slab = _pair_norm_rope(kd[:, cols], kng2, cos2, sin2)
        kdup_ref[:, cols] = slab
        slabs.append(slab)
        tick()
    for p in range(D_KV // PAIR):
        k_ref[:, p * PAIR:(p + 1) * PAIR] = jnp.where(first_half, slabs[2 * p], slabs[2 * p + 1])


def _combine(x, slab, dest3, ebuf, *, tm, attn_pre=None):
    n = x.shape[0]
    n_tiles = n // tm
    in_specs = [
        pl.BlockSpec((1, 1, TOP_K * tm), lambda i: (0, 0, 0), memory_space=pltpu.SMEM),
        pl.BlockSpec((1, 1, TOP_K * tm), lambda i: (jnp.minimum(i + 1, n_tiles - 1), 0, 0),
                     memory_space=pltpu.SMEM),
        _rows(tm, D_MODEL), _rows(tm, SLAB), pl.BlockSpec(memory_space=pl.ANY),
    ]
    args = [dest3, dest3, x, slab, ebuf]
    out_specs = [_rows(tm, D_MODEL)]
    out_shape = [jax.ShapeDtypeStruct((n, D_MODEL), F32)]
    if attn_pre is not None:
        *weights, cos2, sin2, table_tiles = attn_pre
        table_spec = pl.BlockSpec((tm, PAIR), lambda i: (lax.rem(i, table_tiles), 0))
        in_specs += [_resident(w.shape) for w in weights] + [table_spec, table_spec]
        args += list(weights) + [cos2, sin2]
        out_specs += [_rows(tm, D_KV), _rows(tm, 2 * D_KV), _rows(tm, D_KV), _rows(tm, D_MODEL)]
        out_shape += [jax.ShapeDtypeStruct((n, D_KV), F32), jax.ShapeDtypeStruct((n, 2 * D_KV), F32),
                      jax.ShapeDtypeStruct((n, D_KV), F32), jax.ShapeDtypeStruct((n, D_MODEL), BF16)]
    return pl.pallas_call(
        functools.partial(_combine_kernel, tm=tm, n_tiles=n_tiles, attn_pre=attn_pre is not None),
        grid=(n_tiles,),
        in_specs=in_specs, out_specs=out_specs, out_shape=out_shape,
        scratch_shapes=[pltpu.VMEM((2, TOP_K, tm * TOKEN_SUBLANES, V7X_LANES), F32),
                        pltpu.SemaphoreType.DMA((2, 2))],
        compiler_params=_cparams(),
        name="moe_combine_attn_pre" if attn_pre is not None else "moe_combine",
    )(*args)


def _moe(xs, g, w_router, b_router, wgu, bgu, wdn, bdn, layer, prompt_attn_pre=None):
    n_all = xs[0].shape[0] + xs[1].shape[0]
    w_t = w_router.T
    whi = w_t.astype(BF16)
    wlo = (w_t - whi.astype(F32)).astype(BF16)
    h_all, slab_p, slab_s, rec, cnt = _router(xs[0], xs[1], g.reshape(1, D_MODEL), whi, wlo,
                                              b_router.reshape(N_EXPERTS, 1), tm=TM_DENSE)
    slabs = [slab_p, slab_s]
    idx = rec[:TOP_K, :n_all].astype(I32)
    pos = rec[TOP_K:, :n_all].astype(I32)
    counts = cnt[:, 0].astype(I32)
    rows = MOE_ROWS
    nb = (n_all * TOP_K + N_EXPERTS * (rows - 1)) // rows
    nb = -(-nb // EXPERT_BLOCKS_PER_STEP) * EXPERT_BLOCKS_PER_STEP
    padded =(counts + rows - 1) // rows * rows
    pad_end = jnp.cumsum(padded)
    pad_start = pad_end - padded
    eid = jnp.arange(N_EXPERTS, dtype=I32)
    has_rows = counts > 0
    dest = pos + jnp.sum(jnp.where(idx[None] == eid[:, None, None], pad_start[:, None, None], 0), axis=0)
    n_active = (pad_end[-1] // rows).astype(I32)
    blk = jnp.arange(nb, dtype=I32)
    block_e = jnp.sum((pad_end[None, :] <= (blk * rows)[:, None]).astype(I32), axis=1)
    last_e = jnp.max(jnp.where(has_rows, eid, 0))
    block_e = jnp.where(blk < n_active, jnp.minimum(block_e, N_EXPERTS - 1), last_e)

    def tile_table(d, tm):
        n_tiles = -(-d.shape[1] // tm)
        d = jnp.pad(d, ((0, 0), (0, n_tiles * tm - d.shape[1])))
        return d.reshape(TOP_K, n_tiles, tm).transpose(1, 0, 2).reshape(n_tiles, 1, TOP_K * tm)

    fill_start = jnp.concatenate([pad_start + counts, n_active.reshape(1)])
    xs_tiles = _dispatch(fill_start, padded - counts, tile_table(dest, TM_DISPATCH), h_all,
                         n_all, nb, tm=TM_DISPATCH, rows=rows)
    rank = jnp.cumsum(has_rows.astype(I32)) - 1
    run_e = jnp.sum(jnp.where(has_rows[None, :] & (rank[None, :] == eid[:, None]), eid[None, :], 0), axis=1)
    run_e = jnp.concatenate([run_e, jnp.sum(has_rows.astype(I32)).reshape(1)]).astype(I32)
    mine = block_e[:, None] == eid
    block_run = jnp.sum(jnp.where(mine, rank, 0), axis=1).astype(I32)
    block_valid = jnp.sum(jnp.where(mine, counts + pad_start, 0), axis=1) - blk * rows
    block_valid = jnp.clip(block_valid, 0, rows).astype(I32)
    n_l = wgu.shape[0]
    ebuf = _experts(block_e, n_active.reshape(1), block_run, run_e, block_valid, xs_tiles,
                    wgu, bgu.reshape(n_l, N_EXPERTS, 1, 2 * D_EXPERT), wdn, bdn.reshape(n_l, N_EXPERTS, 1, D_MODEL),
                    rows=rows, layer=layer)
    outs, off = [], 0
    for x, slab, pre in zip(xs, slabs, (prompt_attn_pre, None)):
        n = x.shape[0]
        tm = min(TM_COMBINE, n)
        outs.append(_combine(x, slab, tile_table(dest[:, off:off + n], tm), ebuf, tm=tm, attn_pre=pre))
        off += n
    return outs


def _kvq_sample_kernel(x_ref, kvg_ref, wk_ref, wv_ref, kng_ref, cos_ref, sin_ref, bng_ref, wq_ref, qng_ref,
                       k_ref, kdup_ref, v_ref, q_ref):
    x = x_ref[...]
    cos2, sin2 = cos_ref[...], sin_ref[...]
    _shared_kv(x, kvg_ref[...], wk_ref, wv_ref, kng_ref[...], cos2, sin2, k_ref, kdup_ref, v_ref)
    hq = _rms(x, bng_ref[...]).astype(BF16)
    q = jnp.dot(hq, wq_ref[...], preferred_element_type=F32)
    for p in range(D_MODEL // PAIR):
        cols = slice(p * PAIR, (p + 1) * PAIR)
        q_ref[:, cols] = _pair_norm_rope(q[:, cols], qng_ref[...], cos2, sin2)


def _kvq_sample(x, kvg, wk, wv, kng2, cos2, sin2, bng, wq, qng2):
    n = x.shape[0]
    args = [x, kvg, wk, wv, kng2, cos2, sin2, bng, wq, qng2]
    return pl.pallas_call(
        _kvq_sample_kernel,
        grid=(1,),
        in_specs=[_resident(a.shape) for a in args],
        out_specs=[_rows(n, D_KV), _rows(n, 2 * D_KV), _rows(n, D_KV), _rows(n, D_MODEL)],
        out_shape=[jax.ShapeDtypeStruct((n, D_KV), F32), jax.ShapeDtypeStruct((n, 2 * D_KV), F32),
                   jax.ShapeDtypeStruct((n, D_KV), F32), jax.ShapeDtypeStruct((n, D_MODEL), F32)],
        compiler_params=_cparams(),
        name="kvq_proj_sample",
    )(*args)


def _attn_prompt_kernel(sinks_ref, x_ref, q_ref, kc_ref, kp_ref, vc_ref, vp_ref, wo_ref, o_ref, a_scr,
                        *, tm, tiles_per_seq):
    i = pl.program_id(0)

    kj = lax.broadcasted_iota(I32, (2 * WINDOW, WINDOW), 0)
    qi = lax.broadcasted_iota(I32, (2 * WINDOW, WINDOW), 1)
    diff = qi + WINDOW - kj
    band = jnp.logical_and(diff >= 0, diff < WINDOW)
    first_half = lax.broadcasted_iota(I32, (WINDOW, PAIR), 1) < HEAD_DIM
    keep = (jnp.where(first_half, 1.0, 0.0).astype(BF16), jnp.where(first_half, 0.0, 1.0).astype(BF16))
    first_kj = jnp.where(lax.rem(i, tiles_per_seq) == 0, WINDOW, 0)

    for blk in range(tm // WINDOW):
        rows = slice(blk * WINDOW, (blk + 1) * WINDOW)
        if blk == 0:
            kk = jnp.concatenate([kp_ref[...], kc_ref[rows, :]], axis=0).astype(BF16)
            vv = jnp.concatenate([vp_ref[...], vc_ref[rows, :]], axis=0)
            valid = jnp.logical_and(band, kj >= first_kj)
        else:
            win = slice((blk - 1) * WINDOW, (blk + 1) * WINDOW)
            kk = kc_ref[win, :].astype(BF16)
            vv = vc_ref[win, :]
            valid = band
        for kvh in range(N_KV_HEADS):
            kk_h = kk[:, kvh * PAIR:(kvh + 1) * PAIR]
            vcols = slice((kvh // 2) * PAIR, (kvh // 2 + 1) * PAIR)
            vt = vv[:, vcols].T.astype(BF16)
            ch = slice((kvh % 2) * HEAD_DIM, (kvh % 2 + 1) * HEAD_DIM)
            outs = []
            for g in range(Q_PER_KV):
                head = kvh * Q_PER_KV + g
                qm = q_ref[rows, (head // 2) * PAIR:(head // 2 + 1) * PAIR] * keep[head % 2]
                sink = sinks_ref[head]
                st = lax.dot_general(kk_h, qm, (((1,), (1,)), ((), ())), preferred_element_type=F32)
                st = jnp.where(valid, st, -jnp.inf)
                m = jnp.maximum(jnp.max(st, axis=0, keepdims=True), sink)
                p = jnp.exp(st - m)
                den = jnp.sum(p, axis=0, keepdims=True) + jnp.exp(sink - m)
                ot = jnp.dot(vt, p.astype(BF16), preferred_element_type=F32)
                outs.append(ot[ch, :] * (1.0 / den))
            for pq in range(Q_PER_KV // 2):
                pair = (kvh * Q_PER_KV) // 2 + pq
                both = jnp.concatenate([outs[2 * pq], outs[2 * pq + 1]], axis=0)
                a_scr[rows, pair * PAIR:(pair + 1) * PAIR] = both.T.astype(BF16)
    o_ref[...] = x_ref[...] + jnp.dot(a_scr[...], wo_ref[...], preferred_element_type=F32)


def _attn_prompt(x, q, kdup, v, sinks, wo, *, tm, seq):
    n = x.shape[0]
    tiles_per_seq = seq // tm
    blocks_per_tile = tm // WINDOW

    def prev_spec(width):
        return pl.BlockSpec((WINDOW, width), lambda i: (jnp.maximum(i * blocks_per_tile - 1, 0), 0))

    return pl.pallas_call(
        functools.partial(_attn_prompt_kernel, tm=tm, tiles_per_seq=tiles_per_seq),
        grid=(n // tm,),
        in_specs=[pl.BlockSpec(memory_space=pltpu.SMEM), _rows(tm, D_MODEL), _rows(tm, D_MODEL),
                  _rows(tm, 2 * D_KV), prev_spec(2 * D_KV), _rows(tm, D_KV), prev_spec(D_KV), _resident(wo.shape)],
        out_specs=_rows(tm, D_MODEL),
        out_shape=jax.ShapeDtypeStruct((n, D_MODEL), F32),
        scratch_shapes=[pltpu.VMEM((tm, D_MODEL), BF16)],
        compiler_params=_cparams(),
        name="attn_prompt",
    )(sinks, x, q, kdup, kdup, v, v, wo)


def _attn_sample_kernel(q_ref, ck_ref, cv_ref, kn_ref, vn_ref, sink_ref, o_ref, *, per_step, win):
    rowi = lax.broadcasted_iota(I32, (N_HEADS, D_KV), 0)
    lanei = lax.broadcasted_iota(I32, (N_HEADS, D_KV), 1)
    group = rowi >> 2
    own_block = group == (lanei >> 6)
    j = lax.broadcasted_iota(I32, (N_HEADS, win), 1)
    diff = win - j
    valid = jnp.logical_and(diff >= 0, diff < WINDOW)
    sink = sink_ref[...]
    scale = HEAD_DIM ** -0.5
    qms = [jnp.where(own_block, q_ref[b], 0.0) for b in range(per_step)]
    ss = [lax.dot_general(qm.astype(BF16), ck_ref[b].astype(BF16), (((1,), (1,)), ((), ())),
                          preferred_element_type=F32) * scale for b, qm in enumerate(qms)]
    ps, p_news, dens = [], [], []
    for b in range(per_step):
        s = jnp.where(valid, ss[b], -jnp.inf)
        s_new = jnp.sum(qms[b] * kn_ref[b:b + 1, :], axis=-1, keepdims=True) * scale
        m = jnp.maximum(jnp.maximum(jnp.max(s, axis=-1, keepdims=True), s_new), sink)
        p = jnp.exp(s - m)
        p_new = jnp.exp(s_new - m)
        ps.append(p.astype(BF16))
        p_news.append(p_new)
        dens.append(jnp.sum(p, axis=-1, keepdims=True) + p_new + jnp.exp(sink - m))
    os_ = [jnp.dot(ps[b], cv_ref[b].astype(BF16), preferred_element_type=F32) for b in range(per_step)]
    for b in range(per_step):
        o = (os_[b] + p_news[b] * vn_ref[b:b + 1, :]) * (1.0 / dens[b])
        res = o
        for sft in range(1, N_KV_HEADS):
            res = jnp.where(group == sft, pltpu.roll(o, D_KV - sft * HEAD_DIM, 1), res)
        o_ref[b] = res[:, :HEAD_DIM]


def _attn_sample(q4, ck, cv, kn, vn, sink_col, *, per_step):
    nb, win = ck.shape[0], ck.shape[1]
    return pl.pallas_call(
        functools.partial(_attn_sample_kernel, per_step=per_step, win=win),
        grid=(nb // per_step,),
        in_specs=[pl.BlockSpec((per_step, N_HEADS, D_KV), lambda i: (i, 0, 0)),
                  pl.BlockSpec((per_step, win, D_KV), lambda i: (i, 0, 0)),
                  pl.BlockSpec((per_step, win, D_KV), lambda i: (i, 0, 0)),
                  _rows(per_step, D_KV), _rows(per_step, D_KV), _resident(sink_col.shape)],
        out_specs=pl.BlockSpec((per_step, N_HEADS, HEAD_DIM), lambda i: (i, 0, 0)),
        out_shape=jax.ShapeDtypeStruct((nb, N_HEADS, HEAD_DIM), F32),
        compiler_params=_cparams(),
        name="attn_sample",
    )(q4, ck, cv, kn, vn, sink_col)


def _residual_proj_kernel(x_ref, a_ref, w_ref, o_ref):
    o_ref[...] = x_ref[...] + jnp.dot(a_ref[...].astype(BF16), w_ref[...], preferred_element_type=F32)


def _residual_proj(x, a, w):
    n = x.shape[0]
    return pl.pallas_call(
        _residual_proj_kernel,
        grid=(1,),
        in_specs=[_rows(n, D_MODEL), _rows(n, a.shape[1]), _resident(w.shape)],
        out_specs=_rows(n, D_MODEL),
        out_shape=jax.ShapeDtypeStruct((n, D_MODEL), F32),
        compiler_params=_cparams(),
        name="attn_out_sample",
    )(x, a, w)


def _rope_tables(pos):
    half = ROPE_DIM // 2
    inv_freq = np.power(np.float32(ROPE_THETA), -np.arange(half, dtype=np.float32) / np.float32(half))
    ang = (pos.astype(np.float32)[:, None] * inv_freq[None, :].astype(np.float32)).astype(np.float32)
    cos, sin = np.cos(ang.astype(np.float64)), np.sin(ang.astype(np.float64))
    n = pos.shape[0]
    rest = HEAD_DIM - ROPE_DIM
    cos_h = np.concatenate([cos, cos, np.ones((n, rest))], axis=1)
    sin_h = np.concatenate([-sin, sin, np.zeros((n, rest))], axis=1)
    return (jnp.asarray(np.tile(cos_h, (1, 2)), dtype=F32), jnp.asarray(np.tile(sin_h, (1, 2)), dtype=F32))


def kernel(x_prompt, x_sample, cache_k, cache_v, a_norm_g, a_w_in, a_ln_g, a_ln_b, a_w_s, a_b_s, a_w_out, kv_norm_g, w_k, w_v, k_norm_g, b_norm_g, w_q, q_norm_g, sinks, w_o, ffn_norm_g, w_router, b_router, w_gu, b_gu, w_down, b_down):
    bsz, seq, _ = x_prompt.shape
    dec_b, dec_seq, _ = x_sample.shape
    assert dec_seq == 1 and seq % TM_DENSE == 0 and a_norm_g.shape[0] == 1 and b_norm_g.shape[0] == 1
    win = cache_k.shape[1]
    n_p, n_s = bsz * seq, dec_b * dec_seq
    xp = x_prompt.reshape(n_p, D_MODEL)
    xs = x_sample.reshape(n_s, D_MODEL)

    ng = a_norm_g[0].reshape(1, D_MODEL)
    win_bf = a_w_in[0].astype(BF16)
    wout_bf = a_w_out[0].astype(BF16)
    lng = a_ln_g[0].reshape(1, D_GATE)
    lnb = a_ln_b[0].reshape(1, D_GATE)
    bias_full = jnp.repeat(a_b_s[0].T, GROUP_DIM_A, axis=1)
    diag_row = jnp.repeat(a_w_s[0][:, 0, 0], GROUP_DIM_A).reshape(1, D_GATE)
    (xp,) = _mixer_a(xp, ng, win_bf, lng, lnb, a_w_s[0], bias_full, wout_bf, tm=TM_DENSE, single_token_chunks=False)
    xs, v_rows = _mixer_a(xs, ng, win_bf, lng, lnb, diag_row, bias_full[0:1], wout_bf, tm=n_s,
                          single_token_chunks=True)

    kvg = kv_norm_g.reshape(1, D_MODEL)
    bng = b_norm_g[0].reshape(1, D_MODEL)
    w_k_dup = jnp.repeat(w_k.reshape(D_MODEL, N_KV_HEADS, 1, HEAD_DIM), 2, axis=2).reshape(D_MODEL, 2 * D_KV)
    wk_bf, wv_bf, wq_bf, wo_bf = (w.astype(BF16) for w in (w_k_dup, w_v, w_q[0], w_o[0]))
    kng2 = jnp.tile(k_norm_g, 2).reshape(1, PAIR)
    qng2 = jnp.tile(q_norm_g[0], 2).reshape(1, PAIR)
    cos_p, sin_p = _rope_tables(np.arange(seq))
    cos_s, sin_s = _rope_tables(np.full((n_s,), PAST_LEN))
    attn_pre = (kvg, wk_bf, wv_bf, kng2, bng, wq_bf, qng2, cos_p, sin_p, seq // TM_COMBINE)
    (xp, k_p, kdup_p, v_p, q_p), (xs,) = _moe([xp, xs], ffn_norm_g[0], w_router[0], b_router[0], w_gu, b_gu,
                                              w_down, b_down, 0, prompt_attn_pre=attn_pre)

    k_s, _, v_s, q_s = _kvq_sample(xs, kvg, wk_bf, wv_bf, kng2, cos_s, sin_s, bng, wq_bf, qng2)
    xp = _attn_prompt(xp, q_p, kdup_p, v_p, sinks[0], wo_bf, tm=TM_DENSE, seq=seq)
    q4 = jnp.tile(q_s.reshape(n_s, N_HEADS, HEAD_DIM), (1, 1, N_KV_HEADS))
    attn_s = _attn_sample(q4, cache_k.reshape(dec_b, win, D_KV), cache_v.reshape(dec_b, win, D_KV), k_s, v_s,
                          sinks[0].reshape(N_HEADS, 1), per_step=8)
    xs = _residual_proj(xs, attn_s.reshape(n_s, D_MODEL), wo_bf)
    (xp,), (xs,) = _moe([xp, xs], ffn_norm_g[1], w_router[1], b_router[1], w_gu, b_gu, w_down, b_down, 1)

    def last_window(a):
        a = a.reshape(bsz, seq, D_KV)[:, -WINDOW:]
        return a.reshape(bsz, min(WINDOW, seq), N_KV_HEADS, HEAD_DIM)

    return (xp.reshape(bsz, seq, D_MODEL), xs.reshape(dec_b, dec_seq, D_MODEL),
            v_rows.reshape(1, dec_b, dec_seq, D_GATE), last_window(k_p), last_window(v_p),
            k_s.reshape(dec_b, dec_seq, N_KV_HEADS, HEAD_DIM), v_s.reshape(dec_b, dec_seq, N_KV_HEADS, HEAD_DIM))
```

```python
import functools

import jax
import jax.numpy as jnp
import numpy as np
from jax import lax
from jax.experimental import pallas as pl
from jax.experimental.pallas import tpu as pltpu

F32, BF16, I32 = jnp.float32, jnp.bfloat16, jnp.int32

D_MODEL = 1024
CHUNK = 128
D_GATE = 2 * D_MODEL
N_GROUPS_A = 8
GROUP_DIM_A = D_GATE // N_GROUPS_A
HEAD_DIM = 64
N_HEADS = D_MODEL // HEAD_DIM
N_KV_HEADS = N_HEADS // 4
Q_PER_KV = N_HEADS // N_KV_HEADS
D_KV = N_KV_HEADS * HEAD_DIM
WINDOW = 128
ROPE_DIM = HEAD_DIM // 4
ROPE_THETA = 500000.0
N_EXPERTS = 32
TOP_K = 4
D_EXPERT = D_MODEL
SWIGLU_LIMIT = 7.0
SWIGLU_ALPHA = 1.702
EPS = 1e-6
PAST_LEN = 8192

V7X_LANES = 128
V7X_VMEM_BYTES = 64 * 1024 * 1024
VMEM_LIMIT_BYTES = V7X_VMEM_BYTES - 8 * 1024 * 1024

TM_DENSE = 512
TM_COMBINE = 256
MOE_ROWS = 512
EXPERT_BLOCKS_PER_STEP = 1
EXPERT_HEIGHTS = 4
TM_DISPATCH = 256
TOKEN_SUBLANES = D_MODEL // V7X_LANES
PACKED_SUBLANES = TOKEN_SUBLANES // 2
U32 = jnp.uint32
SLAB = V7X_LANES
REC_ROWS = 16
PAIR = 2 * HEAD_DIM
INV_SQRT2 = 0.7071067811865476


def _cparams():
    return pltpu.CompilerParams(dimension_semantics=("arbitrary",), vmem_limit_bytes=VMEM_LIMIT_BYTES)


def _resident(shape):
    zeros = (0,) * len(shape)
    return pl.BlockSpec(shape, lambda i, *_: zeros, pipeline_mode=pl.Buffered(1))


def _rows(tm, width, offset=0):
    return pl.BlockSpec((tm, width), lambda i, *_: (i + offset, 0))


def _rms(x, g):
    return x * lax.rsqrt(jnp.mean(x * x, axis=-1, keepdims=True) + EPS) * g


def _pair_norm_rope(x2, g2, cos2, sin2):
    lane = lax.broadcasted_iota(I32, x2.shape, 1)
    r = lax.broadcasted_iota(I32, (PAIR, PAIR), 0)
    c = lax.broadcasted_iota(I32, (PAIR, PAIR), 1)
    head_mean = jnp.where((r < HEAD_DIM) == (c < HEAD_DIM), 1.0 / HEAD_DIM, 0.0).astype(BF16)
    sq = x2 * x2
    sq_hi = sq.astype(BF16)
    sq_lo = (sq - sq_hi.astype(F32)).astype(BF16)
    ms = (jnp.dot(sq_hi, head_mean, preferred_element_type=F32)
          + jnp.dot(sq_lo, head_mean, preferred_element_type=F32))
    y = x2 * lax.rsqrt(ms + EPS) * g2
    half = ROPE_DIM // 2
    up = pltpu.roll(y, PAIR - half, 1)
    dn = pltpu.roll(y, half, 1)
    partner = jnp.where((lane & (HEAD_DIM - 1)) < half, up, dn)
    return y * cos2 + partner * sin2


def _mixer_a_kernel(x_ref, ng_ref, win_ref, lng_ref, lnb_ref, mix_ref, bias_ref, wout_ref, *refs,
                    tm, single_token_chunks):
    if single_token_chunks:
        o_ref, vout_ref, u_scr, v_scr, p_scr = refs
    else:
        o_ref, u_scr, v_scr, p_scr = refs
    x = x_ref[...]
    h = _rms(x, ng_ref[...]).astype(BF16)
    nc = 512
    for c in range(2 * D_GATE // nc):
        z = jnp.dot(h, win_ref[:, c * nc:(c + 1) * nc], preferred_element_type=F32)
        z = 0.5 * z * (1.0 + lax.erf(z * INV_SQRT2))
        if c < D_GATE // nc:
            u_scr[:, c * nc:(c + 1) * nc] = z
        else:
            v_scr[:, (c - D_GATE // nc) * nc:(c - D_GATE // nc + 1) * nc] = z
    v = v_scr[...]
    vc = v - jnp.mean(v, axis=-1, keepdims=True)
    vn = vc * lax.rsqrt(jnp.mean(vc * vc, axis=-1, keepdims=True) + EPS) * lng_ref[...] + lnb_ref[...]
    if single_token_chunks:
        vout_ref[...] = vn
        p_scr[...] = (u_scr[...] * (vn * mix_ref[...] + bias_ref[...])).astype(BF16)
    else:
        v_scr[...] = vn
        row = lax.broadcasted_iota(I32, (CHUNK, CHUNK), 0)
        col = lax.broadcasted_iota(I32, (CHUNK, CHUNK), 1)
        for g in range(N_GROUPS_A):
            wc = jnp.where(row >= col, mix_ref[g], 0.0).astype(BF16)
            cols = slice(g * GROUP_DIM_A, (g + 1) * GROUP_DIM_A)
            for c in range(tm // CHUNK):
                rows = slice(c * CHUNK, (c + 1) * CHUNK)
                mixed = jnp.dot(wc, v_scr[rows, cols].astype(BF16), preferred_element_type=F32)
                mixed = mixed + bias_ref[:, cols]
                p_scr[rows, cols] = (u_scr[rows, cols] * mixed).astype(BF16)
    o_ref[...] = x + jnp.dot(p_scr[...], wout_ref[...], preferred_element_type=F32)


def _mixer_a(x, ng, win, lng, lnb, mix, bias, wout, *, tm, single_token_chunks):
    n = x.shape[0]
    out_shape = [jax.ShapeDtypeStruct((n, D_MODEL), F32)]
    out_specs = [_rows(tm, D_MODEL)]
    if single_token_chunks:
        out_shape.append(jax.ShapeDtypeStruct((n, D_GATE), F32))
        out_specs.append(_rows(tm, D_GATE))
    return pl.pallas_call(
        functools.partial(_mixer_a_kernel, tm=tm, single_token_chunks=single_token_chunks),
        grid=(n // tm,),
        in_specs=[_rows(tm, D_MODEL), _resident(ng.shape), _resident(win.shape), _resident(lng.shape),
                  _resident(lnb.shape), _resident(mix.shape), _resident(bias.shape), _resident(wout.shape)],
        out_specs=out_specs,
        out_shape=out_shape,
        scratch_shapes=[pltpu.VMEM((tm, D_GATE), F32), pltpu.VMEM((tm, D_GATE), F32),
                        pltpu.VMEM((tm, D_GATE), BF16)],
        compiler_params=_cparams(),
        name="mixer_a_sample" if single_token_chunks else "mixer_a_prompt",
    )(x, ng, win, lng, lnb, mix, bias, wout)


def _route_rows(x, g, whi_t, wlo_t, bias_col, run_scr):
    tm = x.shape[0]
    h = _rms(x, g)
    hh = h.astype(BF16)
    h_rounded = hh.astype(F32)
    hl = (h - h_rounded).astype(BF16)
    nt = (((1,), (1,)), ((), ()))
    logits = (lax.dot_general(whi_t, hh, nt, preferred_element_type=F32)
              + lax.dot_general(wlo_t, hh, nt, preferred_element_type=F32)
              + lax.dot_general(whi_t, hl, nt, preferred_element_type=F32) + bias_col)
    e = lax.broadcasted_iota(I32, (N_EXPERTS, tm), 0).astype(F32)
    hot, vals, idxs = [], [], []
    for _ in range(TOP_K):
        m = jnp.max(logits, axis=0, keepdims=True)
        idx = jnp.min(jnp.where(logits == m, e, float(N_EXPERTS)), axis=0, keepdims=True)
        oh = e == idx
        hot.append(oh)
        vals.append(m)
        idxs.append(idx)
        logits = jnp.where(oh, -jnp.inf, logits)
    ex = [jnp.exp(v - vals[0]) for v in vals]
    den = ex[0] + ex[1] + ex[2] + ex[3]
    picked = jnp.zeros((N_EXPERTS, tm), F32)
    for oh in hot:
        picked = picked + jnp.where(oh, 1.0, 0.0)
    r = lax.broadcasted_iota(I32, (tm, tm), 0)
    c = lax.broadcasted_iota(I32, (tm, tm), 1)
    earlier = jnp.where(r < c, 1.0, 0.0).astype(BF16)
    run = run_scr[:, 0:1]
    pos = jnp.dot(picked.astype(BF16), earlier, preferred_element_type=F32) + run
    run_scr[:, 0:1] = run + jnp.sum(picked, axis=1, keepdims=True)
    row = lax.broadcasted_iota(I32, (REC_ROWS, tm), 0)
    rec = jnp.zeros((REC_ROWS, tm), F32)
    for k in range(TOP_K):
        pos_k = jnp.sum(jnp.where(hot[k], pos, 0.0), axis=0, keepdims=True)
        rec = jnp.where(row == k, idxs[k], rec)
        rec = jnp.where(row == TOP_K + k, pos_k, rec)
        rec = jnp.where(row == 2 * TOP_K + k, ex[k] / den, rec)
    return h_rounded, rec


def _router_kernel(xp_ref, xs_ref, g_ref, whi_ref, wlo_ref, b_ref, h_ref, slabp_ref, slabs_ref, rec_ref, cnt_ref,
                   run_scr, *, n_tiles, n_s):
    i = pl.program_id(0)

    @pl.when(i == 0)
    def _():
        run_scr[...] = jnp.zeros_like(run_scr)

    def pad_rows(rec):
        return jnp.concatenate([rec, jnp.zeros((SLAB - REC_ROWS, rec.shape[1]), F32)], axis=0)

    @pl.when(i < n_tiles)
    def _():
        h, rec = _route_rows(xp_ref[...], g_ref[...], whi_ref[...], wlo_ref[...], b_ref[...], run_scr)
        _store_packed_tiles(h_ref, h, h.shape[0])
        slabp_ref[...] = pad_rows(rec).T
        rec_ref[...] = rec[0:2 * TOP_K, :]

    @pl.when(i == n_tiles)
    def _():
        h, rec = _route_rows(xs_ref[...], g_ref[...], whi_ref[...], wlo_ref[...], b_ref[...], run_scr)
        h_ref[n_s * PACKED_SUBLANES:, :] = jnp.zeros((h_ref.shape[0] - n_s * PACKED_SUBLANES, V7X_LANES), U32)
        _store_packed_tiles(h_ref, h, n_s)
        slabs_ref[...] = pad_rows(rec).T
        rec_ref[...] = jnp.zeros_like(rec_ref)
        rec_ref[:, 0:n_s] = rec[0:2 * TOP_K, :]

    cnt_ref[...] = run_scr[...]


def _store_packed_tiles(ref, x, n):
    half = D_MODEL // 2
    lo = pltpu.bitcast(x[:, :half], U32)
    hi = pltpu.bitcast(x[:, half:], U32)
    word = lax.shift_right_logical(lo, jnp.uint32(16)) | (hi & jnp.uint32(0xFFFF0000))
    for s in range(PACKED_SUBLANES):
        ref[pl.ds(s, n, stride=PACKED_SUBLANES), :] = word[:, s * V7X_LANES:(s + 1) * V7X_LANES]


def _load_packed_tiles(ref, n):
    words = [ref[pl.ds(s, n, stride=PACKED_SUBLANES), :] for s in range(PACKED_SUBLANES)]
    lo = [pltpu.bitcast(w << jnp.uint32(16), F32) for w in words]
    hi = [pltpu.bitcast(w & jnp.uint32(0xFFFF0000), F32) for w in words]
    return jnp.concatenate(lo + hi, axis=1).astype(BF16)


def _store_token_tiles(ref, x, n):
    for s in range(TOKEN_SUBLANES):
        ref[pl.ds(s, n, stride=TOKEN_SUBLANES), :] = x[:, s * V7X_LANES:(s + 1) * V7X_LANES]


def _router(xp, xs, g, whi, wlo, b, *, tm):
    n_p, n_s = xp.shape[0], xs.shape[0]
    n_tiles = n_p // tm
    last = n_tiles - 1
    return pl.pallas_call(
        functools.partial(_router_kernel, n_tiles=n_tiles, n_s=n_s),
        grid=(n_tiles + 1,),
        in_specs=[pl.BlockSpec((tm, D_MODEL), lambda i: (jnp.minimum(i, last), 0)), _resident(xs.shape),
                  _resident(g.shape), _resident(whi.shape), _resident(wlo.shape), _resident(b.shape)],
        out_specs=[_rows(tm * PACKED_SUBLANES, V7X_LANES),
                   pl.BlockSpec((tm, SLAB), lambda i: (jnp.minimum(i, last), 0)),
                   pl.BlockSpec((n_s, SLAB), lambda i: (0, 0)),
                   pl.BlockSpec((2 * TOP_K, tm), lambda i: (0, i)),
                   pl.BlockSpec((N_EXPERTS, V7X_LANES), lambda i: (0, 0))],
        out_shape=[jax.ShapeDtypeStruct(((n_p + tm) * PACKED_SUBLANES, V7X_LANES), U32),
                   jax.ShapeDtypeStruct((n_p, SLAB), F32),
                   jax.ShapeDtypeStruct((n_s, SLAB), F32),
                   jax.ShapeDtypeStruct((2 * TOP_K, n_p + tm), F32),
                   jax.ShapeDtypeStruct((N_EXPERTS, V7X_LANES), F32)],
        scratch_shapes=[pltpu.VMEM((N_EXPERTS, V7X_LANES), F32)],
        compiler_params=_cparams(),
        name="moe_router",
    )(xp, xs, g, whi, wlo, b)


def _dispatch_kernel(fill_start_ref, fill_len_ref, dest_ref, h_ref, xs_ref, zbuf, sem, fill_sem,
                     *, tm, n_tiles, n_last, rows, n_blocks):
    i = pl.program_id(0)
    ts = PACKED_SUBLANES

    def tile_copy(r, k):
        d = dest_ref[0, 0, k * tm + r]
        src = h_ref.at[pl.ds(pl.multiple_of(r * ts, ts), ts)]
        dst = xs_ref.at[pl.ds(pl.multiple_of(d * ts, ts), ts)]
        return pltpu.make_async_copy(src, dst, sem.at[k % 2])

    def scatter(n):
        def body(r, carry):
            for k in range(TOP_K):
                tile_copy(r, k).start(priority=k % 2)
            return carry
        lax.fori_loop(0, n, body, 0, unroll=4)
        for k in range(TOP_K):
            pltpu.make_async_copy(h_ref.at[pl.ds(0, n * ts)], xs_ref.at[pl.ds(0, n * ts)], sem.at[k % 2]).wait()

    def fill_copy(e, bit):
        size = 1 << bit
        length = fill_len_ref[e]
        done = length - (length & (2 * size - 1))
        off = pl.multiple_of((fill_start_ref[e] + done) * ts, ts)
        return (length & size) != 0, pltpu.make_async_copy(zbuf.at[pl.ds(0, size * ts)],
                                                            xs_ref.at[pl.ds(off, size * ts)], fill_sem)

    def tail_copy(blk):
        off = pl.multiple_of(blk * (rows * ts), rows * ts)
        return pltpu.make_async_copy(zbuf, xs_ref.at[pl.ds(off, rows * ts)], fill_sem)

    def for_each_fill(act):
        for e in range(N_EXPERTS):
            for bit in range((rows - 1).bit_length()):
                go, cp = fill_copy(e, bit)

                @pl.when(go)
                def _():
                    act(cp)

        def body(blk, carry):
            act(tail_copy(blk))
            return carry
        lax.fori_loop(fill_start_ref[N_EXPERTS], n_blocks, body, 0)

    @pl.when(i == 0)
    def _():
        zbuf[...] = jnp.zeros_like(zbuf)
        for_each_fill(lambda cp: cp.start())

    @pl.when(i < n_tiles - 1)
    def _():
        scatter(tm)

    @pl.when(i == n_tiles - 1)
    def _():
        scatter(n_last)
        for_each_fill(lambda cp: cp.wait())


def _dispatch(fill_start, fill_len, dest3, h_tiles, n_all, n_blocks, *, tm, rows):
    n_tiles = dest3.shape[0]
    n_last = n_all - (n_tiles - 1) * tm
    ts = PACKED_SUBLANES
    n_slots = n_blocks * rows
    grid_spec = pltpu.PrefetchScalarGridSpec(
        num_scalar_prefetch=2,
        grid=(n_tiles,),
        in_specs=[pl.BlockSpec((1, 1, TOP_K * tm), lambda i, *_: (i, 0, 0), memory_space=pltpu.SMEM),
                  _rows(tm * ts, V7X_LANES)],
        out_specs=pl.BlockSpec(memory_space=pl.ANY),
        scratch_shapes=[pltpu.VMEM((rows * ts, V7X_LANES), U32),
                        pltpu.SemaphoreType.DMA((2,)), pltpu.SemaphoreType.DMA],
    )
    return pl.pallas_call(
        functools.partial(_dispatch_kernel, tm=tm, n_tiles=n_tiles, n_last=n_last, rows=rows, n_blocks=n_blocks),
        grid_spec=grid_spec,
        out_shape=jax.ShapeDtypeStruct((n_slots * ts, V7X_LANES), U32),
        compiler_params=_cparams(),
        name="moe_dispatch",
    )(fill_start, fill_len, dest3, h_tiles)


def _expert_kernel(be_ref, nact_ref, run_ref, rune_ref, valid_ref, x_ref, wgu_hbm, wdn_hbm, *refs, rows, layer):
    bps = EXPERT_BLOCKS_PER_STEP
    bias_refs, (o_ref, wgu_f32, wdn_f32, wsem, wgu_bf, wdn_bf) = refs[:2 * bps], refs[2 * bps:]
    nact = nact_ref[0]

    def weight_copies(r, slot):
        e = rune_ref[r]
        return (pltpu.make_async_copy(wgu_hbm.at[layer, e], wgu_f32.at[slot], wsem.at[0, slot]),
                pltpu.make_async_copy(wdn_hbm.at[layer, e], wdn_f32.at[slot], wsem.at[1, slot]))

    @pl.when(pl.program_id(0) == 0)
    def _():
        for cp in weight_copies(0, 0):
            cp.start()

    for part in range(bps):
        _expert_block(pl.program_id(0) * bps + part, nact, run_ref, rune_ref, valid_ref, weight_copies,
                      x_ref.at[pl.ds(part * rows * PACKED_SUBLANES, rows * PACKED_SUBLANES)],
                      bias_refs[2 * part], bias_refs[2 * part + 1],
                      o_ref.at[pl.ds(part * rows * TOKEN_SUBLANES, rows * TOKEN_SUBLANES)],
                      wgu_f32, wdn_f32, wgu_bf, wdn_bf, rows)


def _expert_block(b, nact, run_ref, rune_ref, valid_ref, weight_copies, x_ref, bgu_ref, bdn_ref, o_ref,
                  wgu_f32, wdn_f32, wgu_bf, wdn_bf, rows):
    @pl.when(b < nact)
    def _():
        r = run_ref[b]
        valid = valid_ref[b]
        slot = lax.rem(r, 2)
        changed = jnp.logical_or(b == 0, r != run_ref[jnp.maximum(b - 1, 0)])

        @pl.when(changed)
        def _():
            for cp in weight_copies(r, slot):
                cp.wait()

            @pl.when(r + 1 < rune_ref[N_EXPERTS])
            def _():
                for cp in weight_copies(r + 1, 1 - slot):
                    cp.start(priority=1)

            step = 256
            for c in range(D_MODEL // step):
                wgu_bf[c * step:(c + 1) * step, :] = wgu_f32[slot, c * step:(c + 1) * step, :].astype(BF16)
                wdn_bf[c * step:(c + 1) * step, :] = wdn_f32[slot, c * step:(c + 1) * step, :].astype(BF16)

        def ffn(m):
            xb = _load_packed_tiles(x_ref, m)
            nc = 512
            chunks = range(D_EXPERT // nc)
            gcols = [slice(c * nc, (c + 1) * nc) for c in chunks]
            ucols = [slice(D_EXPERT + c * nc, D_EXPERT + (c + 1) * nc) for c in chunks]
            gates = [jnp.dot(xb, wgu_bf[:, gcols[c]], preferred_element_type=F32) + bgu_ref[:, gcols[c]]
                     for c in chunks]
            ups = [jnp.dot(xb, wgu_bf[:, ucols[c]], preferred_element_type=F32) + bgu_ref[:, ucols[c]]
                   for c in chunks]
            acts = []
            for gate, up in zip(gates, ups):
                gate = jnp.minimum(gate, SWIGLU_LIMIT)
                up = jnp.clip(up, -SWIGLU_LIMIT, SWIGLU_LIMIT)
                acts.append(((up + 1.0) * gate * (1.0 / (1.0 + jnp.exp(-SWIGLU_ALPHA * gate)))).astype(BF16))
            out = None
            for c in chunks:
                part = jnp.dot(acts[c], wdn_bf[gcols[c], :], preferred_element_type=F32)
                out = part if out is None else out + part
            _store_token_tiles(o_ref, out + bdn_ref[...], m)
            if m < rows:
                o_ref[m * TOKEN_SUBLANES:, :] = jnp.zeros(((rows - m) * TOKEN_SUBLANES, V7X_LANES), F32)

        quarter = rows // EXPERT_HEIGHTS
        level = (valid + quarter - 1) // quarter
        for q in range(1, EXPERT_HEIGHTS + 1):
            @pl.when(level == q if q > 1 else level <= 1)
            def _():
                ffn(q * quarter)

    @pl.when(b >= nact)
    def _():
        o_ref[...] = jnp.zeros_like(o_ref)


def _experts(block_e, n_active, block_run, run_e, block_valid, xs_tiles, wgu, bgu, wdn, bdn, *, rows, layer):
    ts = TOKEN_SUBLANES
    bps = EXPERT_BLOCKS_PER_STEP
    nb = xs_tiles.shape[0] // (rows * PACKED_SUBLANES)
    assert nb % bps == 0
    bias_specs, bias_args = [], []
    for part in range(bps):
        def index(s, be, *_, part=part):
            return (layer, be[s * bps + part], 0, 0)
        bias_specs += [pl.BlockSpec((None, None, 1, 2 * D_EXPERT), index), pl.BlockSpec((None, None, 1, D_MODEL), index)]
        bias_args += [bgu, bdn]
    grid_spec = pltpu.PrefetchScalarGridSpec(
        num_scalar_prefetch=5,
        grid=(nb // bps,),
        in_specs=[
            pl.BlockSpec((bps * rows * PACKED_SUBLANES, V7X_LANES),
                         lambda s, be, na, *_: (jnp.minimum(s, (na[0] - 1) // bps), 0)),
            pl.BlockSpec(memory_space=pl.ANY), pl.BlockSpec(memory_space=pl.ANY),
        ] + bias_specs,
        out_specs=pl.BlockSpec((bps * rows * ts, V7X_LANES), lambda s, *_: (s, 0)),
        scratch_shapes=[pltpu.VMEM((2, D_MODEL, 2 * D_EXPERT), F32), pltpu.VMEM((2, D_EXPERT, D_MODEL), F32),
                        pltpu.SemaphoreType.DMA((2, 2)),
                        pltpu.VMEM((D_MODEL, 2 * D_EXPERT), BF16), pltpu.VMEM((D_EXPERT, D_MODEL), BF16)],
    )
    return pl.pallas_call(
        functools.partial(_expert_kernel, rows=rows, layer=layer),
        grid_spec=grid_spec,
        out_shape=jax.ShapeDtypeStruct((nb * rows * ts, V7X_LANES), F32),
        compiler_params=_cparams(),
        name="moe_experts",
    )(block_e, n_active, block_run, run_e, block_valid, xs_tiles, wgu, wdn, *bias_args)


def _combine_kernel(dest_first_ref, dest_next_ref, x_ref, slab_ref, ebuf_ref, *refs, tm, n_tiles, attn_pre):
    if attn_pre:
        (kvg_ref, wk_ref, wv_ref, kng_ref, bng_ref, wq_ref, qng_ref, cos_ref, sin_ref,
         o_ref, k_ref, kdup_ref, v_ref, q_ref, rbuf, sem) = refs
    else:
        o_ref, rbuf, sem = refs
    i = pl.program_id(0)
    slot = lax.rem(i, 2)
    ts = TOKEN_SUBLANES

    def gather(dest_ref, s):
        def body(r, carry):
            for k in range(TOP_K):
                d = dest_ref[0, 0, k * tm + r]
                src = ebuf_ref.at[pl.ds(pl.multiple_of(d * ts, ts), ts)]
                dst = rbuf.at[s, k, pl.ds(pl.multiple_of(r * ts, ts), ts)]
                pltpu.make_async_copy(src, dst, sem.at[s, k % 2]).start(priority=k % 2)
            return carry
        lax.fori_loop(0, tm, body, 0, unroll=4)

    @pl.when(i == 0)
    def _():
        gather(dest_first_ref, 0)

    if attn_pre:
        todo = [(r, k) for r in range(tm) for k in range(TOP_K)]

        def tick(n):
            for r, k in todo[:n]:
                d = dest_next_ref[0, 0, k * tm + r]
                src = ebuf_ref.at[pl.ds(pl.multiple_of(d * ts, ts), ts)]
                pltpu.make_async_copy(src, rbuf.at[1 - slot, k, pl.ds(r * ts, ts)],
                                      sem.at[1 - slot, k % 2]).start(priority=k % 2)
            del todo[:n]
    else:
        @pl.when(i + 1 < n_tiles)
        def _():
            gather(dest_next_ref, 1 - slot)

        def tick(n):
            pass

    def wait_slot(s):
        for k in range(TOP_K):
            pltpu.make_async_copy(ebuf_ref.at[pl.ds(0, tm * ts)], rbuf.at[s, k], sem.at[s, k % 2]).wait()

    wait_slot(slot)
    slab = slab_ref[...]
    gates = [slab[:, 2 * TOP_K + k:2 * TOP_K + k + 1] for k in range(TOP_K)]
    per_phase = TOP_K * tm // (4 * ts)
    for s in range(ts):
        cols = slice(s * V7X_LANES, (s + 1) * V7X_LANES)
        acc = x_ref[:, cols]
        for k in range(TOP_K):
            acc = acc + gates[k] * rbuf[slot, k, pl.ds(s, tm, stride=ts), :]
        o_ref[:, cols] = acc
        tick(per_phase)
    if attn_pre:
        x2 = o_ref[...]
        cos2, sin2 = cos_ref[...], sin_ref[...]
        kv_phases = 1 + N_KV_HEADS
        q_phases = 1 + D_MODEL // PAIR
        per_phase = len(todo) // (kv_phases + q_phases)
        _shared_kv(x2, kvg_ref[...], wk_ref, wv_ref, kng_ref[...], cos2, sin2, k_ref, kdup_ref, v_ref,
                   tick=lambda: tick(per_phase))
        hq = _rms(x2, bng_ref[...]).astype(BF16)
        q = jnp.dot(hq, wq_ref[...], preferred_element_type=F32)
        tick(per_phase)
        for p in range(D_MODEL // PAIR):
            cols = slice(p * PAIR, (p + 1) * PAIR)
            q_ref[:, cols] = (_pair_norm_rope(q[:, cols], qng_ref[...], cos2, sin2) * HEAD_DIM ** -0.5).astype(BF16)
            tick(per_phase)
        tick(len(todo))

        @pl.when(i == n_tiles - 1)
        def _():
            wait_slot(1 - slot)


def _shared_kv(x, kvg, wk_ref, wv_ref, kng2, cos2, sin2, k_ref, kdup_ref, v_ref, tick=lambda: None):
    h = _rms(x, kvg).astype(BF16)
    kd = jnp.dot(h, wk_ref[...], preferred_element_type=F32)
    v_ref[...] = jnp.dot(h, wv_ref[...], preferred_element_type=F32)
    tick()
    first_half = lax.broadcasted_iota(I32, (x.shape[0], PAIR), 1) < HEAD_DIM
    slabs = []
    for p in range(N_KV_HEADS):
        cols = slice(p * PAIR, (p + 1) * PAIR)
        slab = _pair_norm_rope(kd[:, cols], kng2, cos2, sin2)
        kdup_ref[:, cols] = slab
        slabs.append(slab)
        tick()
    for p in range(D_KV // PAIR):
        k_ref[:, p * PAIR:(p + 1) * PAIR] = jnp.where(first_half, slabs[2 * p], slabs[2 * p + 1])


def _combine(x, slab, dest3, ebuf, *, tm, attn_pre=None):
    n = x.shape[0]
    n_tiles = n // tm
    in_specs = [
        pl.BlockSpec((1, 1, TOP_K * tm), lambda i: (0, 0, 0), memory_space=pltpu.SMEM),
        pl.BlockSpec((1, 1, TOP_K * tm), lambda i: (jnp.minimum(i + 1, n_tiles - 1), 0, 0),
                     memory_space=pltpu.SMEM),
        _rows(tm, D_MODEL), _rows(tm, SLAB), pl.BlockSpec(memory_space=pl.ANY),
    ]
    args = [dest3, dest3, x, slab, ebuf]
    out_specs = [_rows(tm, D_MODEL)]
    out_shape = [jax.ShapeDtypeStruct((n, D_MODEL), F32)]
    if attn_pre is not None:
        *weights, cos2, sin2, table_tiles = attn_pre
        table_spec = pl.BlockSpec((tm, PAIR), lambda i: (lax.rem(i, table_tiles), 0))
        in_specs += [_resident(w.shape) for w in weights] + [table_spec, table_spec]
        args += list(weights) + [cos2, sin2]
        out_specs += [_rows(tm, D_KV), _rows(tm, 2 * D_KV), _rows(tm, D_KV), _rows(tm, D_MODEL)]
        out_shape += [jax.ShapeDtypeStruct((n, D_KV), F32), jax.ShapeDtypeStruct((n, 2 * D_KV), F32),
                      jax.ShapeDtypeStruct((n, D_KV), F32), jax.ShapeDtypeStruct((n, D_MODEL), BF16)]
    return pl.pallas_call(
        functools.partial(_combine_kernel, tm=tm, n_tiles=n_tiles, attn_pre=attn_pre is not None),
        grid=(n_tiles,),
        in_specs=in_specs, out_specs=out_specs, out_shape=out_shape,
        scratch_shapes=[pltpu.VMEM((2, TOP_K, tm * TOKEN_SUBLANES, V7X_LANES), F32),
                        pltpu.SemaphoreType.DMA((2, 2))],
        compiler_params=_cparams(),
        name="moe_combine_attn_pre" if attn_pre is not None else "moe_combine",
    )(*args)


def _moe(xs, g, w_router, b_router, wgu, bgu, wdn, bdn, layer, prompt_attn_pre=None):
    n_all = xs[0].shape[0] + xs[1].shape[0]
    w_t = w_router.T
    whi = w_t.astype(BF16)
    wlo = (w_t - whi.astype(F32)).astype(BF16)
    h_all, slab_p, slab_s, rec, cnt = _router(xs[0], xs[1], g.reshape(1, D_MODEL), whi, wlo,
                                              b_router.reshape(N_EXPERTS, 1), tm=TM_DENSE)
    slabs = [slab_p, slab_s]
    idx = rec[:TOP_K, :n_all].astype(I32)
    pos = rec[TOP_K:, :n_all].astype(I32)
    counts = cnt[:, 0].astype(I32)
    rows = MOE_ROWS
    nb = (n_all * TOP_K + N_EXPERTS * (rows - 1)) // rows
    nb = -(-nb // EXPERT_BLOCKS_PER_STEP) * EXPERT_BLOCKS_PER_STEP
    padded =(counts + rows - 1) // rows * rows
    pad_end = jnp.cumsum(padded)
    pad_start = pad_end - padded
    eid = jnp.arange(N_EXPERTS, dtype=I32)
    has_rows = counts > 0
    dest = pos + jnp.sum(jnp.where(idx[None] == eid[:, None, None], pad_start[:, None, None], 0), axis=0)
    n_active = (pad_end[-1] // rows).astype(I32)
    blk = jnp.arange(nb, dtype=I32)
    block_e = jnp.sum((pad_end[None, :] <= (blk * rows)[:, None]).astype(I32), axis=1)
    last_e = jnp.max(jnp.where(has_rows, eid, 0))
    block_e = jnp.where(blk < n_active, jnp.minimum(block_e, N_EXPERTS - 1), last_e)

    def tile_table(d, tm):
        n_tiles = -(-d.shape[1] // tm)
        d = jnp.pad(d, ((0, 0), (0, n_tiles * tm - d.shape[1])))
        return d.reshape(TOP_K, n_tiles, tm).transpose(1, 0, 2).reshape(n_tiles, 1, TOP_K * tm)

    fill_start = jnp.concatenate([pad_start + counts, n_active.reshape(1)])
    xs_tiles = _dispatch(fill_start, padded - counts, tile_table(dest, TM_DISPATCH), h_all,
                         n_all, nb, tm=TM_DISPATCH, rows=rows)
    rank = jnp.cumsum(has_rows.astype(I32)) - 1
    run_e = jnp.sum(jnp.where(has_rows[None, :] & (rank[None, :] == eid[:, None]), eid[None, :], 0), axis=1)
    run_e = jnp.concatenate([run_e, jnp.sum(has_rows.astype(I32)).reshape(1)]).astype(I32)
    mine = block_e[:, None] == eid
    block_run = jnp.sum(jnp.where(mine, rank, 0), axis=1).astype(I32)
    block_valid = jnp.sum(jnp.where(mine, counts + pad_start, 0), axis=1) - blk * rows
    block_valid = jnp.clip(block_valid, 0, rows).astype(I32)
    n_l = wgu.shape[0]
    ebuf = _experts(block_e, n_active.reshape(1), block_run, run_e, block_valid, xs_tiles,
                    wgu, bgu.reshape(n_l, N_EXPERTS, 1, 2 * D_EXPERT), wdn, bdn.reshape(n_l, N_EXPERTS, 1, D_MODEL),
                    rows=rows, layer=layer)
    outs, off = [], 0
    for x, slab, pre in zip(xs, slabs, (prompt_attn_pre, None)):
        n = x.shape[0]
        tm = min(TM_COMBINE, n)
        outs.append(_combine(x, slab, tile_table(dest[:, off:off + n], tm), ebuf, tm=tm, attn_pre=pre))
        off += n
    return outs


def _kvq_sample_kernel(x_ref, kvg_ref, wk_ref, wv_ref, kng_ref, cos_ref, sin_ref, bng_ref, wq_ref, qng_ref,
                       k_ref, kdup_ref, v_ref, q_ref):
    x = x_ref[...]
    cos2, sin2 = cos_ref[...], sin_ref[...]
    _shared_kv(x, kvg_ref[...], wk_ref, wv_ref, kng_ref[...], cos2, sin2, k_ref, kdup_ref, v_ref)
    hq = _rms(x, bng_ref[...]).astype(BF16)
    q = jnp.dot(hq, wq_ref[...], preferred_element_type=F32)
    for p in range(D_MODEL // PAIR):
        cols = slice(p * PAIR, (p + 1) * PAIR)
        q_ref[:, cols] = _pair_norm_rope(q[:, cols], qng_ref[...], cos2, sin2)


def _kvq_sample(x, kvg, wk, wv, kng2, cos2, sin2, bng, wq, qng2):
    n = x.shape[0]
    args = [x, kvg, wk, wv, kng2, cos2, sin2, bng, wq, qng2]
    return pl.pallas_call(
        _kvq_sample_kernel,
        grid=(1,),
        in_specs=[_resident(a.shape) for a in args],
        out_specs=[_rows(n, D_KV), _rows(n, 2 * D_KV), _rows(n, D_KV), _rows(n, D_MODEL)],
        out_shape=[jax.ShapeDtypeStruct((n, D_KV), F32), jax.ShapeDtypeStruct((n, 2 * D_KV), F32),
                   jax.ShapeDtypeStruct((n, D_KV), F32), jax.ShapeDtypeStruct((n, D_MODEL), F32)],
        compiler_params=_cparams(),
        name="kvq_proj_sample",
    )(*args)


def _attn_prompt_kernel(sinks_ref, x_ref, q_ref, kc_ref, kp_ref, vc_ref, vp_ref, wo_ref, o_ref, a_scr,
                        *, tm, tiles_per_seq):
    i = pl.program_id(0)

    kj = lax.broadcasted_iota(I32, (2 * WINDOW, WINDOW), 0)
    qi = lax.broadcasted_iota(I32, (2 * WINDOW, WINDOW), 1)
    diff = qi + WINDOW - kj
    band = jnp.logical_and(diff >= 0, diff < WINDOW)
    first_half = lax.broadcasted_iota(I32, (WINDOW, PAIR), 1) < HEAD_DIM
    keep = (jnp.where(first_half, 1.0, 0.0).astype(BF16), jnp.where(first_half, 0.0, 1.0).astype(BF16))
    first_kj = jnp.where(lax.rem(i, tiles_per_seq) == 0, WINDOW, 0)

    for blk in range(tm // WINDOW):
        rows = slice(blk * WINDOW, (blk + 1) * WINDOW)
        if blk == 0:
            kk = jnp.concatenate([kp_ref[...], kc_ref[rows, :]], axis=0).astype(BF16)
            vv = jnp.concatenate([vp_ref[...], vc_ref[rows, :]], axis=0)
            valid = jnp.logical_and(band, kj >= first_kj)
        else:
            win = slice((blk - 1) * WINDOW, (blk + 1) * WINDOW)
            kk = kc_ref[win, :].astype(BF16)
            vv = vc_ref[win, :]
            valid = band
        for kvh in range(N_KV_HEADS):
            kk_h = kk[:, kvh * PAIR:(kvh + 1) * PAIR]
            vcols = slice((kvh // 2) * PAIR, (kvh // 2 + 1) * PAIR)
            vt = vv[:, vcols].T.astype(BF16)
            ch = slice((kvh % 2) * HEAD_DIM, (kvh % 2 + 1) * HEAD_DIM)
            outs = []
            for g in range(Q_PER_KV):
                head = kvh * Q_PER_KV + g
                qm = q_ref[rows, (head // 2) * PAIR:(head // 2 + 1) * PAIR] * keep[head % 2]
                sink = sinks_ref[head]
                st = lax.dot_general(kk_h, qm, (((1,), (1,)), ((), ())), preferred_element_type=F32)
                st = jnp.where(valid, st, -jnp.inf)
                m = jnp.maximum(jnp.max(st, axis=0, keepdims=True), sink)
                p = jnp.exp(st - m)
                den = jnp.sum(p, axis=0, keepdims=True) + jnp.exp(sink - m)
                ot = jnp.dot(vt, p.astype(BF16), preferred_element_type=F32)
                outs.append(ot[ch, :] * (1.0 / den))
            for pq in range(Q_PER_KV // 2):
                pair = (kvh * Q_PER_KV) // 2 + pq
                both = jnp.concatenate([outs[2 * pq], outs[2 * pq + 1]], axis=0)
                a_scr[rows, pair * PAIR:(pair + 1) * PAIR] = both.T.astype(BF16)
    o_ref[...] = x_ref[...] + jnp.dot(a_scr[...], wo_ref[...], preferred_element_type=F32)


def _attn_prompt(x, q, kdup, v, sinks, wo, *, tm, seq):
    n = x.shape[0]
    tiles_per_seq = seq // tm
    blocks_per_tile = tm // WINDOW

    def prev_spec(width):
        return pl.BlockSpec((WINDOW, width), lambda i: (jnp.maximum(i * blocks_per_tile - 1, 0), 0))

    return pl.pallas_call(
        functools.partial(_attn_prompt_kernel, tm=tm, tiles_per_seq=tiles_per_seq),
        grid=(n // tm,),
        in_specs=[pl.BlockSpec(memory_space=pltpu.SMEM), _rows(tm, D_MODEL), _rows(tm, D_MODEL),
                  _rows(tm, 2 * D_KV), prev_spec(2 * D_KV), _rows(tm, D_KV), prev_spec(D_KV), _resident(wo.shape)],
        out_specs=_rows(tm, D_MODEL),
        out_shape=jax.ShapeDtypeStruct((n, D_MODEL), F32),
        scratch_shapes=[pltpu.VMEM((tm, D_MODEL), BF16)],
        compiler_params=_cparams(),
        name="attn_prompt",
    )(sinks, x, q, kdup, kdup, v, v, wo)


def _attn_sample_kernel(q_ref, ck_ref, cv_ref, kn_ref, vn_ref, sink_ref, o_ref, *, per_step, win):
    rowi = lax.broadcasted_iota(I32, (N_HEADS, D_KV), 0)
    lanei = lax.broadcasted_iota(I32, (N_HEADS, D_KV), 1)
    group = rowi >> 2
    own_block = group == (lanei >> 6)
    j = lax.broadcasted_iota(I32, (N_HEADS, win), 1)
    diff = win - j
    valid = jnp.logical_and(diff >= 0, diff < WINDOW)
    sink = sink_ref[...]
    scale = HEAD_DIM ** -0.5
    qms = [jnp.where(own_block, q_ref[b], 0.0) for b in range(per_step)]
    ss = [lax.dot_general(qm.astype(BF16), ck_ref[b].astype(BF16), (((1,), (1,)), ((), ())),
                          preferred_element_type=F32) * scale for b, qm in enumerate(qms)]
    ps, p_news, dens = [], [], []
    for b in range(per_step):
        s = jnp.where(valid, ss[b], -jnp.inf)
        s_new = jnp.sum(qms[b] * kn_ref[b:b + 1, :], axis=-1, keepdims=True) * scale
        m = jnp.maximum(jnp.maximum(jnp.max(s, axis=-1, keepdims=True), s_new), sink)
        p = jnp.exp(s - m)
        p_new = jnp.exp(s_new - m)
        ps.append(p.astype(BF16))
        p_news.append(p_new)
        dens.append(jnp.sum(p, axis=-1, keepdims=True) + p_new + jnp.exp(sink - m))
    os_ = [jnp.dot(ps[b], cv_ref[b].astype(BF16), preferred_element_type=F32) for b in range(per_step)]
    for b in range(per_step):
        o = (os_[b] + p_news[b] * vn_ref[b:b + 1, :]) * (1.0 / dens[b])
        res = o
        for sft in range(1, N_KV_HEADS):
            res = jnp.where(group == sft, pltpu.roll(o, D_KV - sft * HEAD_DIM, 1), res)
        o_ref[b] = res[:, :HEAD_DIM]


def _attn_sample(q4, ck, cv, kn, vn, sink_col, *, per_step):
    nb, win = ck.shape[0], ck.shape[1]
    return pl.pallas_call(
        functools.partial(_attn_sample_kernel, per_step=per_step, win=win),
        grid=(nb // per_step,),
        in_specs=[pl.BlockSpec((per_step, N_HEADS, D_KV), lambda i: (i, 0, 0)),
                  pl.BlockSpec((per_step, win, D_KV), lambda i: (i, 0, 0)),
                  pl.BlockSpec((per_step, win, D_KV), lambda i: (i, 0, 0)),
                  _rows(per_step, D_KV), _rows(per_step, D_KV), _resident(sink_col.shape)],
        out_specs=pl.BlockSpec((per_step, N_HEADS, HEAD_DIM), lambda i: (i, 0, 0)),
        out_shape=jax.ShapeDtypeStruct((nb, N_HEADS, HEAD_DIM), F32),
        compiler_params=_cparams(),
        name="attn_sample",
    )(q4, ck, cv, kn, vn, sink_col)


def _residual_proj_kernel(x_ref, a_ref, w_ref, o_ref):
    o_ref[...] = x_ref[...] + jnp.dot(a_ref[...].astype(BF16), w_ref[...], preferred_element_type=F32)


def _residual_proj(x, a, w):
    n = x.shape[0]
    return pl.pallas_call(
        _residual_proj_kernel,
        grid=(1,),
        in_specs=[_rows(n, D_MODEL), _rows(n, a.shape[1]), _resident(w.shape)],
        out_specs=_rows(n, D_MODEL),
        out_shape=jax.ShapeDtypeStruct((n, D_MODEL), F32),
        compiler_params=_cparams(),
        name="attn_out_sample",
    )(x, a, w)


def _rope_tables(pos):
    half = ROPE_DIM // 2
    inv_freq = np.power(np.float32(ROPE_THETA), -np.arange(half, dtype=np.float32) / np.float32(half))
    ang = (pos.astype(np.float32)[:, None] * inv_freq[None, :].astype(np.float32)).astype(np.float32)
    cos, sin = np.cos(ang.astype(np.float64)), np.sin(ang.astype(np.float64))
    n = pos.shape[0]
    rest = HEAD_DIM - ROPE_DIM
    cos_h = np.concatenate([cos, cos, np.ones((n, rest))], axis=1)
    sin_h = np.concatenate([-sin, sin, np.zeros((n, rest))], axis=1)
    return (jnp.asarray(np.tile(cos_h, (1, 2)), dtype=F32), jnp.asarray(np.tile(sin_h, (1, 2)), dtype=F32))


def kernel(x_prompt, x_sample, cache_k, cache_v, a_norm_g, a_w_in, a_ln_g, a_ln_b, a_w_s, a_b_s, a_w_out, kv_norm_g, w_k, w_v, k_norm_g, b_norm_g, w_q, q_norm_g, sinks, w_o, ffn_norm_g, w_router, b_router, w_gu, b_gu, w_down, b_down):
    bsz, seq, _ = x_prompt.shape
    dec_b, dec_seq, _ = x_sample.shape
    assert dec_seq == 1 and seq % TM_DENSE == 0 and a_norm_g.shape[0] == 1 and b_norm_g.shape[0] == 1
    win = cache_k.shape[1]
    n_p, n_s = bsz * seq, dec_b * dec_seq
    xp = x_prompt.reshape(n_p, D_MODEL)
    xs = x_sample.reshape(n_s, D_MODEL)

    ng = a_norm_g[0].reshape(1, D_MODEL)
    win_bf = a_w_in[0].astype(BF16)
    wout_bf = a_w_out[0].astype(BF16)
    lng = a_ln_g[0].reshape(1, D_GATE)
    lnb = a_ln_b[0].reshape(1, D_GATE)
    bias_full = jnp.repeat(a_b_s[0].T, GROUP_DIM_A, axis=1)
    diag_row = jnp.repeat(a_w_s[0][:, 0, 0], GROUP_DIM_A).reshape(1, D_GATE)
    (xp,) = _mixer_a(xp, ng, win_bf, lng, lnb, a_w_s[0], bias_full, wout_bf, tm=TM_DENSE, single_token_chunks=False)
    xs, v_rows = _mixer_a(xs, ng, win_bf, lng, lnb, diag_row, bias_full[0:1], wout_bf, tm=n_s,
                          single_token_chunks=True)

    kvg = kv_norm_g.reshape(1, D_MODEL)
    bng = b_norm_g[0].reshape(1, D_MODEL)
    w_k_dup = jnp.repeat(w_k.reshape(D_MODEL, N_KV_HEADS, 1, HEAD_DIM), 2, axis=2).reshape(D_MODEL, 2 * D_KV)
    wk_bf, wv_bf, wq_bf, wo_bf = (w.astype(BF16) for w in (w_k_dup, w_v, w_q[0], w_o[0]))
    kng2 = jnp.tile(k_norm_g, 2).reshape(1, PAIR)
    qng2 = jnp.tile(q_norm_g[0], 2).reshape(1, PAIR)
    cos_p, sin_p = _rope_tables(np.arange(seq))
    cos_s, sin_s = _rope_tables(np.full((n_s,), PAST_LEN))
    attn_pre = (kvg, wk_bf, wv_bf, kng2, bng, wq_bf, qng2, cos_p, sin_p, seq // TM_COMBINE)
    (xp, k_p, kdup_p, v_p, q_p), (xs,) = _moe([xp, xs], ffn_norm_g[0], w_router[0], b_router[0], w_gu, b_gu,
                                              w_down, b_down, 0, prompt_attn_pre=attn_pre)

    k_s, _, v_s, q_s = _kvq_sample(xs, kvg, wk_bf, wv_bf, kng2, cos_s, sin_s, bng, wq_bf, qng2)
    xp = _attn_prompt(xp, q_p, kdup_p, v_p, sinks[0], wo_bf, tm=2 * TM_DENSE, seq=seq)
    q4 = jnp.tile(q_s.reshape(n_s, N_HEADS, HEAD_DIM), (1, 1, N_KV_HEADS))
    attn_s = _attn_sample(q4, cache_k.reshape(dec_b, win, D_KV), cache_v.reshape(dec_b, win, D_KV), k_s, v_s,
                          sinks[0].reshape(N_HEADS, 1), per_step=8)
    xs = _residual_proj(xs, attn_s.reshape(n_s, D_MODEL), wo_bf)
    (xp,), (xs,) = _moe([xp, xs], ffn_norm_g[1], w_router[1], b_router[1], w_gu, b_gu, w_down, b_down, 1)

    def last_window(a):
        a = a.reshape(bsz, seq, D_KV)[:, -WINDOW:]
        return a.reshape(bsz, min(WINDOW, seq), N_KV_HEADS, HEAD_DIM)

    return (xp.reshape(bsz, seq, D_MODEL), xs.reshape(dec_b, dec_seq, D_MODEL),
            v_rows.reshape(1, dec_b, dec_seq, D_GATE), last_window(k_p), last_window(v_p),
            k_s.reshape(dec_b, dec_seq, N_KV_HEADS, HEAD_DIM), v_s.reshape(dec_b, dec_seq, N_KV_HEADS, HEAD_DIM))
```
